```python
import jax, jax.numpy as jnp
from jax import lax
import numpy as np

D_MODEL = 1024
BATCH = 4
SEQ = 4096
DEPTH = 1

CHUNK = 64
LRU_WIDTH = 512
LRU_HEADS = 8
LRU_HEAD_DIM = LRU_WIDTH // LRU_HEADS
LRU_CONV_WIDTH = 4
LRU_C = 8.0
GMLP_WIDTH = 512
GMLP_GROUPS = 4
GMLP_GROUP_DIM = GMLP_WIDTH // GMLP_GROUPS
GMLP_BLOCK = 128
MIX_WIDTH = LRU_WIDTH + GMLP_WIDTH
IN_COLS = 2 * LRU_WIDTH + 2 * GMLP_WIDTH
D_FF = 3 * D_MODEL
FFN_CONV_WIDTH = 3
N_MOD = 6
EPS = 1e-6

kernel_name = "hybrid_rglru_gmlp_convffn_block"


def rmsnorm(x, g):
    xf = x.astype(jnp.float32)
    y = xf * lax.rsqrt(jnp.mean(xf * xf, axis=-1, keepdims=True) + EPS)
    return (y * g.astype(jnp.float32)).astype(x.dtype)


def layernorm(x, g, b):
    xf = x.astype(jnp.float32)
    mu = jnp.mean(xf, axis=-1, keepdims=True)
    var = jnp.mean(jnp.square(xf - mu), axis=-1, keepdims=True)
    y = (xf - mu) * lax.rsqrt(var + EPS)
    return (y * g.astype(jnp.float32) + b.astype(jnp.float32)).astype(x.dtype)


def causal_depthwise_conv(x, w, b):
    k_width = w.shape[0]
    s = x.shape[1]
    xp = jnp.pad(x, ((0, 0), (k_width - 1, 0), (0, 0)))
    out = xp[:, 0:s] * w[0]
    for k in range(1, k_width):
        out = out + xp[:, k:k + s] * w[k]
    return out + b


def _lin_rec_combine(left, right):
    a1, b1 = left
    a2, b2 = right
    return a1 * a2, a2 * b1 + b2


def rg_lru_group(x_raw, gate_raw, conv_w, conv_b, w_rgate, b_rgate, w_igate, b_igate, lru_a):
    bsz, s, _ = x_raw.shape
    xc = causal_depthwise_conv(x_raw, conv_w, conv_b)
    xh = xc.reshape(bsz, s, LRU_HEADS, LRU_HEAD_DIM)
    r = jax.nn.sigmoid(jnp.einsum('bshi,hij->bshj', xh, w_rgate) + b_rgate).reshape(bsz, s, LRU_WIDTH)
    i = jax.nn.sigmoid(jnp.einsum('bshi,hij->bshj', xh, w_igate) + b_igate).reshape(bsz, s, LRU_WIDTH)
    log_a = -LRU_C * r.astype(jnp.float32) * jax.nn.softplus(-lru_a.astype(jnp.float32))
    a = jnp.exp(log_a)
    mult = jnp.sqrt(-jnp.expm1(2.0 * log_a))
    bx = mult * (i * xc).astype(jnp.float32)
    _, h = lax.associative_scan(_lin_rec_combine, (a, bx), axis=1)
    return h.astype(x_raw.dtype) * jax.nn.gelu(gate_raw)


def gmlp_group(u_raw, v_raw, v_norm_g, v_norm_b, w_spatial, b_spatial):
    bsz, s, _ = u_raw.shape
    u = jax.nn.gelu(u_raw)
    v = layernorm(jax.nn.gelu(v_raw), v_norm_g, v_norm_b)
    vb = v.reshape(bsz, s // GMLP_BLOCK, GMLP_BLOCK, GMLP_GROUPS, GMLP_GROUP_DIM)
    pos = jnp.arange(GMLP_BLOCK)
    mask = (pos[None, :] // CHUNK) <= (pos[:, None] // CHUNK)
    ws = jnp.where(mask[None], w_spatial, jnp.zeros_like(w_spatial))
    sp = jnp.einsum('gij,bnjgc->bnigc', ws, vb) + b_spatial.T[None, None, :, :, None]
    return u * sp.reshape(bsz, s, GMLP_WIDTH)


def setup_inputs(seed: int = 0) -> dict:
    key = jax.random.key(seed)
    ks = jax.random.split(key, 32)
    L = DEPTH

    def nrm(k, shape, scale):
        return jax.random.normal(k, shape, jnp.float32) * scale

    x = nrm(ks[0], (BATCH, SEQ, D_MODEL), 1.0)
    c = nrm(ks[1], (BATCH, D_MODEL), 1.0)
    w_ada = nrm(ks[2], (L, D_MODEL, N_MOD * D_MODEL), 0.5 * D_MODEL ** -0.5)
    b_ada = nrm(ks[3], (L, N_MOD * D_MODEL), 0.02)
    g_mix_pre = 1.0 + nrm(ks[4], (L, D_MODEL), 0.02)
    g_mix_post = 1.0 + nrm(ks[5], (L, D_MODEL), 0.02)
    w_in = nrm(ks[6], (L, D_MODEL, IN_COLS), D_MODEL ** -0.5)
    conv_w = nrm(ks[7], (L, LRU_CONV_WIDTH, LRU_WIDTH), LRU_CONV_WIDTH ** -0.5)
    conv_b = nrm(ks[8], (L, LRU_WIDTH), 0.02)
    w_rgate = nrm(ks[9], (L, LRU_HEADS, LRU_HEAD_DIM, LRU_HEAD_DIM), LRU_HEAD_DIM ** -0.5)
    b_rgate = nrm(ks[10], (L, LRU_HEADS, LRU_HEAD_DIM), 0.02)
    w_igate = nrm(ks[11], (L, LRU_HEADS, LRU_HEAD_DIM, LRU_HEAD_DIM), LRU_HEAD_DIM ** -0.5)
    b_igate = nrm(ks[12], (L, LRU_HEADS, LRU_HEAD_DIM), 0.02)
    a_c = jax.random.uniform(ks[13], (L, LRU_WIDTH), jnp.float32, 0.9, 0.999)
    p = a_c ** (1.0 / LRU_C)
    lru_a = jnp.log(p) - jnp.log1p(-p)
    v_norm_g = 1.0 + nrm(ks[14], (L, GMLP_WIDTH), 0.02)
    v_norm_b = nrm(ks[15], (L, GMLP_WIDTH), 0.02)
    w_spatial = nrm(ks[16], (L, GMLP_GROUPS, GMLP_BLOCK, GMLP_BLOCK), GMLP_BLOCK ** -0.5)
    b_spatial = 1.0 + nrm(ks[17], (L, GMLP_GROUPS, GMLP_BLOCK), 0.02)
    g_lru_out = 1.0 + nrm(ks[18], (L, LRU_WIDTH), 0.02)
    g_gmlp_out = 1.0 + nrm(ks[19], (L, GMLP_WIDTH), 0.02)
    w_out = nrm(ks[20], (L, MIX_WIDTH, D_MODEL), MIX_WIDTH ** -0.5)
    g_ffn_pre = 1.0 + nrm(ks[21], (L, D_MODEL), 0.02)
    g_ffn_post = 1.0 + nrm(ks[22], (L, D_MODEL), 0.02)
    w_up = nrm(ks[23], (L, D_MODEL, 2 * D_FF), D_MODEL ** -0.5)
    ffn_conv_w = nrm(ks[24], (L, FFN_CONV_WIDTH, 2 * D_FF), FFN_CONV_WIDTH ** -0.5)
    ffn_conv_b = nrm(ks[25], (L, 2 * D_FF), 0.02)
    w_down = nrm(ks[26], (L, D_FF, D_MODEL), D_FF ** -0.5)
    return {"x": x, "c": c, "w_ada": w_ada, "b_ada": b_ada,
            "g_mix_pre": g_mix_pre, "g_mix_post": g_mix_post, "w_in": w_in,
            "conv_w": conv_w, "conv_b": conv_b, "w_rgate": w_rgate, "b_rgate": b_rgate,
            "w_igate": w_igate, "b_igate": b_igate, "lru_a": lru_a,
            "v_norm_g": v_norm_g, "v_norm_b": v_norm_b, "w_spatial": w_spatial, "b_spatial": b_spatial,
            "g_lru_out": g_lru_out, "g_gmlp_out": g_gmlp_out, "w_out": w_out,
            "g_ffn_pre": g_ffn_pre, "g_ffn_post": g_ffn_post, "w_up": w_up,
            "ffn_conv_w": ffn_conv_w, "ffn_conv_b": ffn_conv_b, "w_down": w_down}


def reference(x, c, w_ada, b_ada, g_mix_pre, g_mix_post, w_in, conv_w, conv_b,
              w_rgate, b_rgate, w_igate, b_igate, lru_a, v_norm_g, v_norm_b,
              w_spatial, b_spatial, g_lru_out, g_gmlp_out, w_out,
              g_ffn_pre, g_ffn_post, w_up, ffn_conv_w, ffn_conv_b, w_down):
    c_act = jax.nn.silu(c)
    for l in range(DEPTH):
        mod = c_act @ w_ada[l] + b_ada[l]
        sh_m, sc_m, gt_m, sh_f, sc_f, gt_f = [m[:, None, :] for m in jnp.split(mod, N_MOD, axis=-1)]

        h = rmsnorm(x, g_mix_pre[l]) * (1.0 + sc_m) + sh_m
        z = h @ w_in[l]
        lru_x, lru_gate, g_u, g_v = jnp.split(
            z, [LRU_WIDTH, 2 * LRU_WIDTH, 2 * LRU_WIDTH + GMLP_WIDTH], axis=-1)
        y_lru = rg_lru_group(lru_x, lru_gate, conv_w[l], conv_b[l], w_rgate[l], b_rgate[l],
                             w_igate[l], b_igate[l], lru_a[l])
        y_gmlp = gmlp_group(g_u, g_v, v_norm_g[l], v_norm_b[l], w_spatial[l], b_spatial[l])
        y = jnp.concatenate([rmsnorm(y_lru, g_lru_out[l]), rmsnorm(y_gmlp, g_gmlp_out[l])], axis=-1)
        y = y @ w_out[l]
        x = x + gt_m * rmsnorm(y, g_mix_post[l])

        h = rmsnorm(x, g_ffn_pre[l]) * (1.0 + sc_f) + sh_f
        up = causal_depthwise_conv(h @ w_up[l], ffn_conv_w[l], ffn_conv_b[l])
        g_ff, v_ff = jnp.split(up, 2, axis=-1)
        y = (jax.nn.gelu(g_ff) * v_ff) @ w_down[l]
        x = x + gt_f * rmsnorm(y, g_ffn_post[l])
    return x
```

```python
import functools

import jax
import jax.numpy as jnp
from jax import lax
from jax.experimental import pallas as pl
from jax.experimental.pallas import tpu as pltpu

CHUNK = 64
LRU_WIDTH = 512
LRU_HEADS = 8
LRU_HEAD_DIM = LRU_WIDTH // LRU_HEADS
LRU_CONV_WIDTH = 4
LRU_C = 8.0
GMLP_WIDTH = 512
GMLP_GROUPS = 4
GMLP_GROUP_DIM = GMLP_WIDTH // GMLP_GROUPS
GMLP_BLOCK = 128
FFN_CONV_WIDTH = 3
N_MOD = 6
EPS = 1e-6

SUBLANES = 8
GATE_TILE = 256

MIX_T = 256
FFN_T = 512
FFN_FC = 512
VMEM_LIMIT = 48 * 1024 * 1024

F32 = jnp.float32
BF16 = jnp.bfloat16


def _dot(a, b):
    return jnp.dot(a, b, preferred_element_type=F32)


def _rms(x):
    return x * lax.rsqrt(jnp.mean(x * x, axis=-1, keepdims=True) + EPS)


def _shift_rows(v, d, fill):
    t, c = v.shape
    if d % SUBLANES == 0:
        return jnp.concatenate([jnp.full((d, c), fill, v.dtype), v[: t - d]], axis=0)
    r = pltpu.roll(v, d, axis=0)
    row = lax.broadcasted_iota(jnp.int32, (SUBLANES, c), 0)
    head = jnp.where(row >= d, r[:SUBLANES], fill)
    return jnp.concatenate([head, r[SUBLANES:]], axis=0)


def _causal_conv(tail, x, w, b):
    k_width = w.shape[0]
    t = x.shape[0]
    xp = jnp.concatenate([tail, x], axis=0)
    out = x * w[k_width - 1 : k_width] + b
    for k in range(k_width - 1):
        shifted = pltpu.roll(xp, k_width - 1 - k, axis=0)[SUBLANES : SUBLANES + t]
        out = out + shifted * w[k : k + 1]
    return out


def _mod_kernel(c_ref, w_ref, b_ref, o_ref):
    c = c_ref[...]
    c_act = c * jax.nn.sigmoid(c)
    o_ref[...] = _dot(c_act.astype(BF16), w_ref[...].astype(BF16)) + b_ref[...]


def _modulation(c_pad, w_ada, b_ada):
    rows, d = c_pad.shape
    n = w_ada.shape[1]
    bn = d
    return pl.pallas_call(
        _mod_kernel,
        grid=(n // bn,),
        in_specs=[
            pl.BlockSpec((rows, d), lambda j: (0, 0)),
            pl.BlockSpec((d, bn), lambda j: (0, j)),
            pl.BlockSpec((1, bn), lambda j: (0, j)),
        ],
        out_specs=pl.BlockSpec((rows, bn), lambda j: (0, j)),
        out_shape=jax.ShapeDtypeStruct((rows, n), F32),
        compiler_params=pltpu.CompilerParams(
            dimension_semantics=("arbitrary",), vmem_limit_bytes=VMEM_LIMIT),
        name="adaln_mod",
    )(c_pad, w_ada, b_ada)


def _mixer_kernel(x_ref, mod_ref, gpre_ref, gpost_ref, win_ref, cw_ref, cb_ref,
                  wgate_ref, bgate_ref, lrua_ref, vng_ref, vnb_ref, wsp_ref, bsp_ref,
                  glru_ref, ggmlp_ref, wout_ref, o_ref, tail_ref, state_ref):
    t_idx = pl.program_id(1)
    t = x_ref.shape[0]
    d = x_ref.shape[1]

    @pl.when(t_idx == 0)
    def _():
        tail_ref[...] = jnp.zeros_like(tail_ref)
        state_ref[...] = jnp.zeros_like(state_ref)

    x = x_ref[...]
    mod = mod_ref[...]
    sh_m = mod[:, 0:d]
    sc_m = mod[:, d:2 * d]
    gt_m = mod[:, 2 * d:3 * d]

    h = _rms(x) * gpre_ref[...] * (1.0 + sc_m) + sh_m
    z = _dot(h.astype(BF16), win_ref[...])
    lru_x = z[:, 0:LRU_WIDTH]
    lru_gate = z[:, LRU_WIDTH:2 * LRU_WIDTH]
    g_u = z[:, 2 * LRU_WIDTH:2 * LRU_WIDTH + GMLP_WIDTH]
    g_v = z[:, 2 * LRU_WIDTH + GMLP_WIDTH:]

    xc = _causal_conv(tail_ref[...], lru_x, cw_ref[...], cb_ref[...])
    tail_ref[...] = lru_x[t - SUBLANES:]
    xcb = xc.astype(BF16)
    gates = []
    for j in range(LRU_WIDTH // GATE_TILE):
        gates.append(_dot(xcb[:, j * GATE_TILE:(j + 1) * GATE_TILE], wgate_ref[j]))
    r_pre = jnp.concatenate([g[:, :GATE_TILE] for g in gates], axis=1)
    i_pre = jnp.concatenate([g[:, GATE_TILE:] for g in gates], axis=1)
    bgate = bgate_ref[...]
    r = jax.nn.sigmoid(r_pre + bgate[:, :LRU_WIDTH])
    i = jax.nn.sigmoid(i_pre + bgate[:, LRU_WIDTH:])
    neg_a = -lrua_ref[...]
    softplus = jnp.maximum(neg_a, 0.0) + jnp.log1p(jnp.exp(-jnp.abs(neg_a)))
    log_a = (-LRU_C) * r * softplus
    a = jnp.exp(log_a)
    th = jnp.tanh(log_a)
    mult = jnp.sqrt((-2.0 * th) / (1.0 - th))
    bx = mult * (i * xc)

    row0 = lax.broadcasted_iota(jnp.int32, (SUBLANES, LRU_WIDTH), 0) == 0
    first = jnp.where(row0, a[:SUBLANES] * state_ref[0:1, :], 0.0)
    hs = jnp.concatenate([bx[:SUBLANES] + first, bx[SUBLANES:]], axis=0)
    step = 1
    while step < t:
        hs = hs + a * _shift_rows(hs, step, 0.0)
        if step * 2 < t:
            a = a * _shift_rows(a, step, 1.0)
        step *= 2
    state_ref[...] = hs[t - SUBLANES:][SUBLANES - 1:SUBLANES] + jnp.zeros_like(state_ref)
    y_lru = hs * jax.nn.gelu(lru_gate)
    yl = _rms(y_lru) * glru_ref[...]

    u = jax.nn.gelu(g_u)
    gv = jax.nn.gelu(g_v)
    mu = jnp.mean(gv, axis=-1, keepdims=True)
    var = jnp.mean(jnp.square(gv - mu), axis=-1, keepdims=True)
    v = (gv - mu) * lax.rsqrt(var + EPS) * vng_ref[...] + vnb_ref[...]
    vb = v.astype(BF16)
    nb = t // GMLP_BLOCK
    pi = lax.broadcasted_iota(jnp.int32, (GMLP_BLOCK, GMLP_BLOCK), 0) // CHUNK
    pj = lax.broadcasted_iota(jnp.int32, (GMLP_BLOCK, GMLP_BLOCK), 1) // CHUNK
    mask = pj <= pi
    sp_groups = []
    for g in range(GMLP_GROUPS):
        ws = jnp.where(mask, wsp_ref[g], 0.0).astype(BF16)
        cols = slice(g * GMLP_GROUP_DIM, (g + 1) * GMLP_GROUP_DIM)
        rhs = jnp.concatenate(
            [vb[n * GMLP_BLOCK:(n + 1) * GMLP_BLOCK, cols] for n in range(nb)], axis=1)
        sp_groups.append(_dot(ws, rhs))
    bsp = bsp_ref[...]
    sp_rows = []
    for n in range(nb):
        blk = jnp.concatenate(
            [sg[:, n * GMLP_GROUP_DIM:(n + 1) * GMLP_GROUP_DIM] for sg in sp_groups], axis=1)
        sp_rows.append(blk + bsp)
    sp = jnp.concatenate(sp_rows, axis=0)
    y_gmlp = u * sp
    yg = _rms(y_gmlp) * ggmlp_ref[...]

    y = jnp.concatenate([yl, yg], axis=1).astype(BF16)
    y = _dot(y, wout_ref[...])
    o_ref[...] = x + gt_m * (_rms(y) * gpost_ref[...])


def _mixer(x, mod3, g_pre, g_post, w_in, conv_w, conv_b, w_gate, b_gate, lru_a,
           v_norm_g, v_norm_b, w_spatial, b_sp_full, g_lru_out, g_gmlp_out, w_out):
    bsz, s, d = x.shape
    t = MIX_T
    full = lambda a: pl.BlockSpec(a.shape, lambda b, i: (0,) * a.ndim)
    in_arrays = [g_pre, g_post, w_in, conv_w, conv_b, w_gate, b_gate, lru_a,
                 v_norm_g, v_norm_b, w_spatial, b_sp_full, g_lru_out, g_gmlp_out, w_out]
    return pl.pallas_call(
        _mixer_kernel,
        grid=(bsz, s // t),
        in_specs=[
            pl.BlockSpec((None, t, d), lambda b, i: (b, i, 0)),
            pl.BlockSpec((None, 1, mod3.shape[2]), lambda b, i: (b, 0, 0)),
        ] + [full(a) for a in in_arrays],
        out_specs=pl.BlockSpec((None, t, d), lambda b, i: (b, i, 0)),
        out_shape=jax.ShapeDtypeStruct(x.shape, x.dtype),
        scratch_shapes=[
            pltpu.VMEM((SUBLANES, LRU_WIDTH), F32),
            pltpu.VMEM((SUBLANES, LRU_WIDTH), F32),
        ],
        compiler_params=pltpu.CompilerParams(
            dimension_semantics=("arbitrary", "arbitrary"), vmem_limit_bytes=VMEM_LIMIT),
        name="token_mixer",
    )(x, mod3, *in_arrays)


def _ffn_kernel(x_ref, mod_ref, gpre_ref, gpost_ref, wg_ref, wv_ref, cwg_ref, cwv_ref,
                cbg_ref, cbv_ref, wd_ref, o_ref, h_ref, acc_ref, tailg_ref, tailv_ref):
    t_idx = pl.program_id(1)
    j = pl.program_id(2)
    nj = pl.num_programs(2)
    t = x_ref.shape[0]
    d = x_ref.shape[1]
    mod = mod_ref[...]

    @pl.when(j == 0)
    def _():
        sh_f = mod[:, 3 * d:4 * d]
        sc_f = mod[:, 4 * d:5 * d]
        h = _rms(x_ref[...]) * gpre_ref[...] * (1.0 + sc_f) + sh_f
        h_ref[...] = h.astype(BF16)

    @pl.when(t_idx == 0)
    def _():
        tailg_ref[j] = jnp.zeros(tailg_ref.shape[1:], F32)
        tailv_ref[j] = jnp.zeros(tailv_ref.shape[1:], F32)

    hb = h_ref[...]
    up_g = _dot(hb, wg_ref[...])
    up_v = _dot(hb, wv_ref[...])
    cg = _causal_conv(tailg_ref[j], up_g, cwg_ref[...], cbg_ref[...])
    cv = _causal_conv(tailv_ref[j], up_v, cwv_ref[...], cbv_ref[...])
    tailg_ref[j] = up_g[t - SUBLANES:]
    tailv_ref[j] = up_v[t - SUBLANES:]
    act = (jax.nn.gelu(cg) * cv).astype(BF16)
    part = _dot(act, wd_ref[...])

    @pl.when(j == 0)
    def _():
        acc_ref[...] = part

    @pl.when(j > 0)
    def _():
        acc_ref[...] += part

    @pl.when(j == nj - 1)
    def _():
        gt_f = mod[:, 5 * d:6 * d]
        o_ref[...] = x_ref[...] + gt_f * (_rms(acc_ref[...]) * gpost_ref[...])


def _ffn(x, mod3, g_pre, g_post, w_up, conv_w, conv_b, w_down):
    bsz, s, d = x.shape
    d_ff = w_down.shape[0]
    t, fc = FFN_T, FFN_FC
    nj = d_ff // fc
    vec = lambda: pl.BlockSpec((1, d), lambda b, i, j: (0, 0))
    return pl.pallas_call(
        _ffn_kernel,
        grid=(bsz, s // t, nj),
        in_specs=[
            pl.BlockSpec((None, t, d), lambda b, i, j: (b, i, 0)),
            pl.BlockSpec((None, 1, mod3.shape[2]), lambda b, i, j: (b, 0, 0)),
            vec(), vec(),
            pl.BlockSpec((d, fc), lambda b, i, j: (0, j)),
            pl.BlockSpec((d, fc), lambda b, i, j: (0, nj + j)),
            pl.BlockSpec((FFN_CONV_WIDTH, fc), lambda b, i, j: (0, j)),
            pl.BlockSpec((FFN_CONV_WIDTH, fc), lambda b, i, j: (0, nj + j)),
            pl.BlockSpec((1, fc), lambda b, i, j: (0, j)),
            pl.BlockSpec((1, fc), lambda b, i, j: (0, nj + j)),
            pl.BlockSpec((fc, d), lambda b, i, j: (j, 0)),
        ],
        out_specs=pl.BlockSpec((None, t, d), lambda b, i, j: (b, i, 0)),
        out_shape=jax.ShapeDtypeStruct(x.shape, x.dtype),
        scratch_shapes=[
            pltpu.VMEM((t, d), BF16),
            pltpu.VMEM((t, d), F32),
            pltpu.VMEM((nj, SUBLANES, fc), F32),
            pltpu.VMEM((nj, SUBLANES, fc), F32),
        ],
        compiler_params=pltpu.CompilerParams(
            dimension_semantics=("arbitrary", "arbitrary", "arbitrary"),
            vmem_limit_bytes=VMEM_LIMIT),
        name="conv_ffn",
    )(x, mod3, g_pre, g_post, w_up, w_up, conv_w, conv_w, conv_b, conv_b, w_down)


def _gate_weights(w_rgate, w_igate):
    heads_per_tile = GATE_TILE // LRU_HEAD_DIM
    tiles = []
    for j in range(LRU_WIDTH // GATE_TILE):
        hs = slice(j * heads_per_tile, (j + 1) * heads_per_tile)
        r_bd = jax.scipy.linalg.block_diag(*w_rgate[hs])
        i_bd = jax.scipy.linalg.block_diag(*w_igate[hs])
        tiles.append(jnp.concatenate([r_bd, i_bd], axis=1))
    return jnp.stack(tiles).astype(BF16)


def kernel(x, c, w_ada, b_ada, g_mix_pre, g_mix_post, w_in, conv_w, conv_b, w_rgate, b_rgate, w_igate, b_igate, lru_a, v_norm_g, v_norm_b, w_spatial, b_spatial, g_lru_out, g_gmlp_out, w_out, g_ffn_pre, g_ffn_post, w_up, ffn_conv_w, ffn_conv_b, w_down):
    depth = w_ada.shape[0]
    bsz, s, d = x.shape
    c_pad = jnp.pad(c, ((0, SUBLANES - bsz % SUBLANES if bsz % SUBLANES else 0), (0, 0)))
    for l in range(depth):
        mod = _modulation(c_pad, w_ada[l], b_ada[l][None, :])
        mod3 = mod.reshape(mod.shape[0], 1, mod.shape[1])
        row = lambda a: a[l][None, :]
        w_gate = _gate_weights(w_rgate[l], w_igate[l])
        b_gate = jnp.concatenate([b_rgate[l].reshape(1, -1), b_igate[l].reshape(1, -1)], axis=1)
        b_sp_full = jnp.repeat(b_spatial[l].T, GMLP_GROUP_DIM, axis=1)
        x = _mixer(x, mod3, row(g_mix_pre), row(g_mix_post), w_in[l].astype(BF16),
                   conv_w[l], row(conv_b), w_gate, b_gate, row(lru_a),
                   row(v_norm_g), row(v_norm_b), w_spatial[l], b_sp_full,
                   row(g_lru_out), row(g_gmlp_out), w_out[l].astype(BF16))
        x = _ffn(x, mod3, row(g_ffn_pre), row(g_ffn_post), w_up[l].astype(BF16),
                 ffn_conv_w[l], row(ffn_conv_b), w_down[l].astype(BF16))
    return x
```

```python
import jax
import jax.numpy as jnp
from jax import lax
from jax.experimental import pallas as pl
from jax.experimental.pallas import tpu as pltpu

CHUNK = 64
LRU_WIDTH = 512
LRU_HEADS = 8
LRU_HEAD_DIM = LRU_WIDTH // LRU_HEADS
LRU_CONV_WIDTH = 4
LRU_C = 8.0
GMLP_WIDTH = 512
GMLP_GROUPS = 4
GMLP_GROUP_DIM = GMLP_WIDTH // GMLP_GROUPS
GMLP_BLOCK = 128
FFN_CONV_WIDTH = 3
N_MOD = 6
EPS = 1e-6

SUBLANES = 8
LANES = 128
GATE_TILE = 256

MIX_T = 256
FFN_T = 512
FFN_FC = 512
FFN_SLOTS = 4
VMEM_LIMIT = 56 * 1024 * 1024

F32 = jnp.float32
BF16 = jnp.bfloat16


def _dot(a, b):
    return jnp.dot(a, b, preferred_element_type=F32)


def _rms(x):
    return x * lax.rsqrt(jnp.mean(x * x, axis=-1, keepdims=True) + EPS)


def _gelu2_times(x, v):
    c0 = 0.7978845608028654
    z = x * (c0 + (c0 * 0.044715) * (x * x))
    return (x * v) * (1.0 + jnp.tanh(z))


def _shift_rows(v, d, fill):
    t, c = v.shape
    if d % SUBLANES == 0:
        return jnp.concatenate([jnp.full((d, c), fill, v.dtype), v[: t - d]], axis=0)
    r = pltpu.roll(v, d, axis=0)
    row = lax.broadcasted_iota(jnp.int32, (SUBLANES, c), 0)
    head = jnp.where(row >= d, r[:SUBLANES], fill)
    return jnp.concatenate([head, r[SUBLANES:]], axis=0)


def _causal_conv(tail, x, w, b):
    k_width = w.shape[0]
    t = x.shape[0]
    xp = jnp.concatenate([tail, x], axis=0)
    out = x * w[k_width - 1 : k_width] + b
    for k in range(k_width - 1):
        shifted = pltpu.roll(xp, k_width - 1 - k, axis=0)[SUBLANES : SUBLANES + t]
        out = out + shifted * w[k : k + 1]
    return out


def _rows(ref, lead, start, size):
    return ref[pl.ds(lead, 1, stride=2), pl.ds(start, size), :][0]


def _conv_via_scratch(scr_ref, lead0, tail, x, w, b):
    k_width = w.shape[0]
    t, c = x.shape
    outs = []
    for g in range(c // LANES):
        cols = slice(g * LANES, (g + 1) * LANES)
        scr_ref[lead0 + g, 0:SUBLANES, :] = tail[:, cols]
        scr_ref[lead0 + g, SUBLANES:SUBLANES + t, :] = x[:, cols]
    for g in range(c // LANES):
        cols = slice(g * LANES, (g + 1) * LANES)
        out = x[:, cols] * w[k_width - 1:k_width, cols] + b[:, cols]
        for k in range(k_width - 1):
            shifted = _rows(scr_ref, lead0 + g, SUBLANES - (k_width - 1 - k), t)
            out = out + shifted * w[k:k + 1, cols]
        outs.append(out)
    return outs


def _mod_kernel(c_ref, w_ref, b_ref, o_ref):
    c = c_ref[...]
    c_act = c * jax.nn.sigmoid(c)
    o_ref[...] = _dot(c_act.astype(BF16), w_ref[...].astype(BF16)) + b_ref[...]


def _modulation(c_pad, w_ada, b_ada):
    rows, d = c_pad.shape
    n = w_ada.shape[1]
    bn = d
    return pl.pallas_call(
        _mod_kernel,
        grid=(n // bn,),
        in_specs=[
            pl.BlockSpec((rows, d), lambda j: (0, 0)),
            pl.BlockSpec((d, bn), lambda j: (0, j)),
            pl.BlockSpec((1, bn), lambda j: (0, j)),
        ],
        out_specs=pl.BlockSpec((rows, bn), lambda j: (0, j)),
        out_shape=jax.ShapeDtypeStruct((rows, n), F32),
        compiler_params=pltpu.CompilerParams(
            dimension_semantics=("arbitrary",), vmem_limit_bytes=VMEM_LIMIT),
        name="adaln_mod",
    )(c_pad, w_ada, b_ada)


def _mixer_kernel(x_ref, mod_ref, gpre_ref, gpost_ref, win_ref, cw_ref, cb_ref,
                  wgate_ref, bgate_ref, lrua_ref, vng_ref, vnb_ref, wsp_ref, bsp_ref,
                  glru_ref, ggmlp_ref, wout_ref, o_ref, tail_ref, state_ref):
    t_idx = pl.program_id(1)
    t = x_ref.shape[0]
    d = x_ref.shape[1]

    @pl.when(t_idx == 0)
    def _():
        tail_ref[...] = jnp.zeros_like(tail_ref)
        state_ref[...] = jnp.zeros_like(state_ref)

    x = x_ref[...]
    mod = mod_ref[...]
    sh_m = mod[:, 0:d]
    sc_m = mod[:, d:2 * d]
    gt_m = mod[:, 2 * d:3 * d]

    h = _rms(x) * gpre_ref[...] * (1.0 + sc_m) + sh_m
    z = _dot(h.astype(BF16), win_ref[...])
    lru_x = z[:, 0:LRU_WIDTH]
    lru_gate = z[:, LRU_WIDTH:2 * LRU_WIDTH]
    g_u = z[:, 2 * LRU_WIDTH:2 * LRU_WIDTH + GMLP_WIDTH]
    g_v = z[:, 2 * LRU_WIDTH + GMLP_WIDTH:]

    xc = _causal_conv(tail_ref[...], lru_x, cw_ref[...], cb_ref[...])
    tail_ref[...] = lru_x[t - SUBLANES:]
    xcb = xc.astype(BF16)
    gates = []
    for j in range(LRU_WIDTH // GATE_TILE):
        gates.append(_dot(xcb[:, j * GATE_TILE:(j + 1) * GATE_TILE], wgate_ref[j]))
    r_pre = jnp.concatenate([g[:, :GATE_TILE] for g in gates], axis=1)
    i_pre = jnp.concatenate([g[:, GATE_TILE:] for g in gates], axis=1)
    bgate = bgate_ref[...]
    r = jax.nn.sigmoid(r_pre + bgate[:, :LRU_WIDTH])
    i = jax.nn.sigmoid(i_pre + bgate[:, LRU_WIDTH:])
    neg_a = -lrua_ref[...]
    softplus = jnp.maximum(neg_a, 0.0) + jnp.log1p(jnp.exp(-jnp.abs(neg_a)))
    log_a = (-LRU_C) * r * softplus
    a = jnp.exp(log_a)
    th = jnp.tanh(log_a)
    mult = jnp.sqrt((-2.0 * th) / (1.0 - th))
    bx = mult * (i * xc)

    row0 = lax.broadcasted_iota(jnp.int32, (SUBLANES, LRU_WIDTH), 0) == 0
    first = jnp.where(row0, a[:SUBLANES] * state_ref[0:1, :], 0.0)
    hs = jnp.concatenate([bx[:SUBLANES] + first, bx[SUBLANES:]], axis=0)
    step = 1
    while step < t:
        hs = hs + a * _shift_rows(hs, step, 0.0)
        if step * 2 < t:
            a = a * _shift_rows(a, step, 1.0)
        step *= 2
    state_ref[...] = hs[t - SUBLANES:][SUBLANES - 1:SUBLANES] + jnp.zeros_like(state_ref)
    y_lru = hs * jax.nn.gelu(lru_gate)
    yl = _rms(y_lru) * glru_ref[...]

    u = jax.nn.gelu(g_u)
    gv = jax.nn.gelu(g_v)
    mu = jnp.mean(gv, axis=-1, keepdims=True)
    var = jnp.mean(jnp.square(gv - mu), axis=-1, keepdims=True)
    v = (gv - mu) * lax.rsqrt(var + EPS) * vng_ref[...] + vnb_ref[...]
    vb = v.astype(BF16)
    nb = t // GMLP_BLOCK
    pi = lax.broadcasted_iota(jnp.int32, (GMLP_BLOCK, GMLP_BLOCK), 0) // CHUNK
    pj = lax.broadcasted_iota(jnp.int32, (GMLP_BLOCK, GMLP_BLOCK), 1) // CHUNK
    mask = pj <= pi
    sp_groups = []
    for g in range(GMLP_GROUPS):
        ws = jnp.where(mask, wsp_ref[g], 0.0).astype(BF16)
        cols = slice(g * GMLP_GROUP_DIM, (g + 1) * GMLP_GROUP_DIM)
        rhs = jnp.concatenate(
            [vb[n * GMLP_BLOCK:(n + 1) * GMLP_BLOCK, cols] for n in range(nb)], axis=1)
        sp_groups.append(_dot(ws, rhs))
    bsp = bsp_ref[...]
    sp_rows = []
    for n in range(nb):
        blk = jnp.concatenate(
            [sg[:, n * GMLP_GROUP_DIM:(n + 1) * GMLP_GROUP_DIM] for sg in sp_groups], axis=1)
        sp_rows.append(blk + bsp)
    sp = jnp.concatenate(sp_rows, axis=0)
    y_gmlp = u * sp
    yg = _rms(y_gmlp) * ggmlp_ref[...]

    y = jnp.concatenate([yl, yg], axis=1).astype(BF16)
    y = _dot(y, wout_ref[...])
    o_ref[...] = x + gt_m * (_rms(y) * gpost_ref[...])


def _mixer(x, mod3, g_pre, g_post, w_in, conv_w, conv_b, w_gate, b_gate, lru_a,
           v_norm_g, v_norm_b, w_spatial, b_sp_full, g_lru_out, g_gmlp_out, w_out):
    bsz, s, d = x.shape
    t = MIX_T
    full = lambda a: pl.BlockSpec(a.shape, lambda b, i: (0,) * a.ndim)
    in_arrays = [g_pre, g_post, w_in, conv_w, conv_b, w_gate, b_gate, lru_a,
                 v_norm_g, v_norm_b, w_spatial, b_sp_full, g_lru_out, g_gmlp_out, w_out]
    return pl.pallas_call(
        _mixer_kernel,
        grid=(bsz, s // t),
        in_specs=[
            pl.BlockSpec((None, t, d), lambda b, i: (b, i, 0)),
            pl.BlockSpec((None, 1, mod3.shape[2]), lambda b, i: (b, 0, 0)),
        ] + [full(a) for a in in_arrays],
        out_specs=pl.BlockSpec((None, t, d), lambda b, i: (b, i, 0)),
        out_shape=jax.ShapeDtypeStruct(x.shape, x.dtype),
        scratch_shapes=[
            pltpu.VMEM((SUBLANES, LRU_WIDTH), F32),
            pltpu.VMEM((SUBLANES, LRU_WIDTH), F32),
        ],
        compiler_params=pltpu.CompilerParams(
            dimension_semantics=("arbitrary", "arbitrary"), vmem_limit_bytes=VMEM_LIMIT),
        name="token_mixer",
    )(x, mod3, *in_arrays)


def _ffn_kernel(x_ref, mod_ref, gpre_ref, gpost_ref, wup_ref, cw_ref, cb_ref, wd_ref,
                o_ref, scr_ref, tail_ref):
    t_idx = pl.program_id(1)
    t, d = x_ref.shape
    d_ff = wd_ref.shape[0]
    fc = FFN_FC
    lg = fc // LANES

    @pl.when(t_idx == 0)
    def _():
        tail_ref[...] = jnp.zeros_like(tail_ref)

    mod = mod_ref[...]
    sh_f = mod[:, 3 * d:4 * d]
    sc_f = mod[:, 4 * d:5 * d]
    gt_f = mod[:, 5 * d:6 * d]
    x = x_ref[...]
    hb = (_rms(x) * (gpre_ref[...] * (1.0 + sc_f)) + sh_f).astype(BF16)

    acc = None
    slot = 0
    for j in range(d_ff // fc):
        conv = []
        for base, scale in ((0, 1.0), (d_ff, 0.5)):
            cols = slice(base + j * fc, base + (j + 1) * fc)
            up = _dot(hb, wup_ref[:, cols])
            tail = tail_ref[:, cols]
            tail_ref[:, cols] = up[t - SUBLANES:]
            conv.append(_conv_via_scratch(scr_ref, slot * lg, tail, up,
                                          cw_ref[:, cols] * scale, cb_ref[:, cols] * scale))
            slot = (slot + 1) % FFN_SLOTS
        act = jnp.concatenate(
            [_gelu2_times(cg, cv) for cg, cv in zip(*conv)], axis=1).astype(BF16)
        part = _dot(act, wd_ref[j * fc:(j + 1) * fc, :])
        acc = part if acc is None else acc + part
    o_ref[...] = x + (gt_f * gpost_ref[...]) * _rms(acc)


def _ffn(x, mod3, g_pre, g_post, w_up, conv_w, conv_b, w_down):
    bsz, s, d = x.shape
    t = FFN_T
    resident = lambda a: pl.BlockSpec(a.shape, lambda b, i: (0,) * a.ndim,
                                      pipeline_mode=pl.Buffered(1))
    return pl.pallas_call(
        _ffn_kernel,
        grid=(bsz, s // t),
        in_specs=[
            pl.BlockSpec((None, t, d), lambda b, i: (b, i, 0)),
            pl.BlockSpec((None, 1, mod3.shape[2]), lambda b, i: (b, 0, 0)),
            resident(g_pre), resident(g_post), resident(w_up), resident(conv_w),
            resident(conv_b), resident(w_down),
        ],
        out_specs=pl.BlockSpec((None, t, d), lambda b, i: (b, i, 0)),
        out_shape=jax.ShapeDtypeStruct(x.shape, x.dtype),
        scratch_shapes=[
            pltpu.VMEM((FFN_SLOTS * (FFN_FC // LANES), SUBLANES + t, LANES), F32),
            pltpu.VMEM((SUBLANES, w_up.shape[1]), F32),
        ],
        compiler_params=pltpu.CompilerParams(
            dimension_semantics=("arbitrary", "arbitrary"), vmem_limit_bytes=VMEM_LIMIT),
        name="conv_ffn",
    )(x, mod3, g_pre, g_post, w_up, conv_w, conv_b, w_down)


def _gate_weights(w_rgate, w_igate):
    heads_per_tile = GATE_TILE // LRU_HEAD_DIM
    tiles = []
    for j in range(LRU_WIDTH // GATE_TILE):
        hs = slice(j * heads_per_tile, (j + 1) * heads_per_tile)
        r_bd = jax.scipy.linalg.block_diag(*w_rgate[hs])
        i_bd = jax.scipy.linalg.block_diag(*w_igate[hs])
        tiles.append(jnp.concatenate([r_bd, i_bd], axis=1))
    return jnp.stack(tiles).astype(BF16)


def kernel(x, c, w_ada, b_ada, g_mix_pre, g_mix_post, w_in, conv_w, conv_b, w_rgate, b_rgate, w_igate, b_igate, lru_a, v_norm_g, v_norm_b, w_spatial, b_spatial, g_lru_out, g_gmlp_out, w_out, g_ffn_pre, g_ffn_post, w_up, ffn_conv_w, ffn_conv_b, w_down):
    depth = w_ada.shape[0]
    bsz, s, d = x.shape
    c_pad = jnp.pad(c, ((0, SUBLANES - bsz % SUBLANES if bsz % SUBLANES else 0), (0, 0)))
    for l in range(depth):
        mod = _modulation(c_pad, w_ada[l], b_ada[l][None, :])
        mod3 = mod.reshape(mod.shape[0], 1, mod.shape[1])
        row = lambda a: a[l][None, :]
        w_gate = _gate_weights(w_rgate[l], w_igate[l])
        b_gate = jnp.concatenate([b_rgate[l].reshape(1, -1), b_igate[l].reshape(1, -1)], axis=1)
        b_sp_full = jnp.repeat(b_spatial[l].T, GMLP_GROUP_DIM, axis=1)
        x = _mixer(x, mod3, row(g_mix_pre), row(g_mix_post), w_in[l].astype(BF16),
                   conv_w[l], row(conv_b), w_gate, b_gate, row(lru_a),
                   row(v_norm_g), row(v_norm_b), w_spatial[l], b_sp_full,
                   row(g_lru_out), row(g_gmlp_out), w_out[l].astype(BF16))
        x = _ffn(x, mod3, row(g_ffn_pre), row(g_ffn_post), w_up[l].astype(BF16),
                 ffn_conv_w[l], row(ffn_conv_b), w_down[l].astype(BF16))
    return x
```

```python
import jax
import jax.numpy as jnp
from jax import lax
from jax.experimental import pallas as pl
from jax.experimental.pallas import tpu as pltpu

CHUNK = 64
LRU_WIDTH = 512
LRU_HEADS = 8
LRU_HEAD_DIM = LRU_WIDTH // LRU_HEADS
LRU_CONV_WIDTH = 4
LRU_C = 8.0
GMLP_WIDTH = 512
GMLP_GROUPS = 4
GMLP_GROUP_DIM = GMLP_WIDTH // GMLP_GROUPS
GMLP_BLOCK = 128
FFN_CONV_WIDTH = 3
N_MOD = 6
EPS = 1e-6

SUBLANES = 8
LANES = 128
GATE_TILE = 256

MIX_T = 256
FFN_T = 512
FFN_FC = 512
FFN_SLOTS = 4
VMEM_LIMIT = 56 * 1024 * 1024

F32 = jnp.float32
BF16 = jnp.bfloat16


def _dot(a, b):
    return jnp.dot(a, b, preferred_element_type=F32)


def _rms(x):
    return x * lax.rsqrt(jnp.mean(x * x, axis=-1, keepdims=True) + EPS)


def _gelu(x):
    c0 = 0.7978845608028654
    hx = 0.5 * x
    return hx + hx * jnp.tanh(x * (c0 + (c0 * 0.044715) * (x * x)))


def _gelu2_times(x, v):
    c0 = 0.7978845608028654
    z = x * (c0 + (c0 * 0.044715) * (x * x))
    return (x * v) * (1.0 + jnp.tanh(z))


def _rows(ref, lead, start, size):
    return ref[pl.ds(lead, 1, stride=2), pl.ds(start, size), :][0]


def _put_rows(ref, lead, start, val):
    ref[pl.ds(lead, 1, stride=2), pl.ds(start, val.shape[0]), :] = val[None]


def _linear_scan(a, b, state_ref, a_scr, b_scr, h_scr):
    t = a[0].shape[0]
    seg = t // SUBLANES
    pitch = seg + 1 - seg % 2
    n = len(a)
    for c in range(n):
        for s in range(SUBLANES):
            _put_rows(a_scr, c, s * pitch, a[c][s * seg:(s + 1) * seg])
            _put_rows(b_scr, c, s * pitch, b[c][s * seg:(s + 1) * seg])
    step = lambda ref, c, j: ref[c, pl.ds(j, SUBLANES, stride=pitch), :]
    row = lax.broadcasted_iota(jnp.int32, (SUBLANES, LANES), 0)
    shifted = lambda v, sh, fill: jnp.where(row >= sh, pltpu.roll(v, sh, axis=0), fill)

    prod = [None] * n
    end = [None] * n
    for j in range(seg):
        for c in range(n):
            aj, bj = step(a_scr, c, j), step(b_scr, c, j)
            prod[c] = aj if j == 0 else aj * prod[c]
            end[c] = bj if j == 0 else aj * end[c] + bj

    h = []
    for c in range(n):
        cols = slice(c * LANES, (c + 1) * LANES)
        h0 = state_ref[SUBLANES - 1:SUBLANES, cols]
        p = prod[c]
        e = end[c] + jnp.where(row == 0, p * h0, 0.0)
        for sh in (1, 2, 4):
            e = e + p * shifted(e, sh, 0.0)
            if sh < 4:
                p = p * shifted(p, sh, 1.0)
        state_ref[:, cols] = e
        h.append(shifted(e, 1, h0))
    for j in range(seg):
        for c in range(n):
            h[c] = step(a_scr, c, j) * h[c] + step(b_scr, c, j)
            h_scr[c, pl.ds(j, SUBLANES, stride=pitch), :] = h[c]
    return [jnp.concatenate([_rows(h_scr, c, s * pitch, seg) for s in range(SUBLANES)], axis=0)
            for c in range(n)]


def _conv_via_scratch(scr_ref, lead0, tail, x, w, b):
    k_width = w.shape[0]
    t, c = x.shape
    outs = []
    for g in range(c // LANES):
        cols = slice(g * LANES, (g + 1) * LANES)
        scr_ref[lead0 + g, 0:SUBLANES, :] = tail[:, cols]
        scr_ref[lead0 + g, SUBLANES:SUBLANES + t, :] = x[:, cols]
    for g in range(c // LANES):
        cols = slice(g * LANES, (g + 1) * LANES)
        out = x[:, cols] * w[k_width - 1:k_width, cols] + b[:, cols]
        for k in range(k_width - 1):
            shifted = _rows(scr_ref, lead0 + g, SUBLANES - (k_width - 1 - k), t)
            out = out + shifted * w[k:k + 1, cols]
        outs.append(out)
    return outs


def _mod_kernel(c_ref, w_ref, b_ref, o_ref):
    c = c_ref[...]
    c_act = c * jax.nn.sigmoid(c)
    o_ref[...] = _dot(c_act.astype(BF16), w_ref[...].astype(BF16)) + b_ref[...]


def _modulation(c_pad, w_ada, b_ada):
    rows, d = c_pad.shape
    n = w_ada.shape[1]
    bn = d
    return pl.pallas_call(
        _mod_kernel,
        grid=(n // bn,),
        in_specs=[
            pl.BlockSpec((rows, d), lambda j: (0, 0)),
            pl.BlockSpec((d, bn), lambda j: (0, j)),
            pl.BlockSpec((1, bn), lambda j: (0, j)),
        ],
        out_specs=pl.BlockSpec((rows, bn), lambda j: (0, j)),
        out_shape=jax.ShapeDtypeStruct((rows, n), F32),
        compiler_params=pltpu.CompilerParams(
            dimension_semantics=("arbitrary",), vmem_limit_bytes=VMEM_LIMIT),
        name="adaln_mod",
    )(c_pad, w_ada, b_ada)


def _mixer_kernel(x_ref, mod_ref, gpre_ref, gpost_ref, win_ref, cw_ref, cb_ref,
                  wgate_ref, bgate_ref, lrua_ref, vng_ref, vnb_ref, wsp_ref, bsp_ref,
                  glru_ref, ggmlp_ref, wout_ref, o_ref,
                  conv_scr, a_scr, b_scr, h_scr, tail_ref, state_ref):
    t_idx = pl.program_id(1)
    t = x_ref.shape[0]
    d = x_ref.shape[1]
    slabs = lambda v: [v[:, c * LANES:(c + 1) * LANES] for c in range(v.shape[1] // LANES)]

    @pl.when(t_idx == 0)
    def _():
        tail_ref[...] = jnp.zeros_like(tail_ref)
        state_ref[...] = jnp.zeros_like(state_ref)

    x = x_ref[...]
    mod = mod_ref[...]
    sh_m = mod[:, 0:d]
    sc_m = mod[:, d:2 * d]
    gt_m = mod[:, 2 * d:3 * d]

    h = _rms(x) * (gpre_ref[...] * (1.0 + sc_m)) + sh_m
    z = _dot(h.astype(BF16), win_ref[...])
    lru_x = z[:, 0:LRU_WIDTH]
    lru_gate = z[:, LRU_WIDTH:2 * LRU_WIDTH]
    g_u = z[:, 2 * LRU_WIDTH:2 * LRU_WIDTH + GMLP_WIDTH]
    g_v = z[:, 2 * LRU_WIDTH + GMLP_WIDTH:]

    tail = tail_ref[...]
    tail_ref[...] = lru_x[t - SUBLANES:]
    xc = jnp.concatenate(
        _conv_via_scratch(conv_scr, 0, tail, lru_x, cw_ref[...], cb_ref[...]), axis=1)
    xcb = xc.astype(BF16)
    gates = []
    for j in range(LRU_WIDTH // GATE_TILE):
        gates.append(_dot(xcb[:, j * GATE_TILE:(j + 1) * GATE_TILE], wgate_ref[j]))
    r_pre = jnp.concatenate([g[:, :GATE_TILE] for g in gates], axis=1)
    i_pre = jnp.concatenate([g[:, GATE_TILE:] for g in gates], axis=1)
    bgate = bgate_ref[...]
    r = jax.nn.sigmoid(r_pre + bgate[:, :LRU_WIDTH])
    i = jax.nn.sigmoid(i_pre + bgate[:, LRU_WIDTH:])
    neg_a = -lrua_ref[...]
    softplus = jnp.maximum(neg_a, 0.0) + jnp.log1p(jnp.exp(-jnp.abs(neg_a)))
    log_a = r * ((-LRU_C) * softplus)
    a = jnp.exp(log_a)
    th = jnp.tanh(log_a)
    q = (-2.0 * th) / (1.0 - th)
    mult = jnp.where(q > 0.0, q * lax.rsqrt(q), 0.0)
    bx = (mult * xc) * i
    hs = _linear_scan(slabs(a), slabs(bx), state_ref, a_scr, b_scr, h_scr)
    y_lru = 0.5 * _gelu2_times(lru_gate, jnp.concatenate(hs, axis=1))
    yl = _rms(y_lru) * glru_ref[...]

    gv = _gelu(g_v)
    mu = jnp.mean(gv, axis=-1, keepdims=True)
    cen = gv - mu
    var = jnp.mean(cen * cen, axis=-1, keepdims=True)
    v = cen * lax.rsqrt(var + EPS) * vng_ref[...] + vnb_ref[...]
    vb = v.astype(BF16)
    nb = t // GMLP_BLOCK
    pi = lax.broadcasted_iota(jnp.int32, (GMLP_BLOCK, GMLP_BLOCK), 0) // CHUNK
    pj = lax.broadcasted_iota(jnp.int32, (GMLP_BLOCK, GMLP_BLOCK), 1) // CHUNK
    mask = pj <= pi
    sp_groups = []
    for g in range(GMLP_GROUPS):
        ws = jnp.where(mask, 0.5 * wsp_ref[g], 0.0).astype(BF16)
        cols = slice(g * GMLP_GROUP_DIM, (g + 1) * GMLP_GROUP_DIM)
        rhs = jnp.concatenate(
            [vb[n * GMLP_BLOCK:(n + 1) * GMLP_BLOCK, cols] for n in range(nb)], axis=1)
        sp_groups.append(_dot(ws, rhs))
    bsp = 0.5 * bsp_ref[...]
    sp_rows = []
    for n in range(nb):
        blk = jnp.concatenate(
            [sg[:, n * GMLP_GROUP_DIM:(n + 1) * GMLP_GROUP_DIM] for sg in sp_groups], axis=1)
        sp_rows.append(blk + bsp)
    sp_half = jnp.concatenate(sp_rows, axis=0)
    y_gmlp = _gelu2_times(g_u, sp_half)
    yg = _rms(y_gmlp) * ggmlp_ref[...]

    y = jnp.concatenate([yl, yg], axis=1).astype(BF16)
    y = _dot(y, wout_ref[...])
    o_ref[...] = x + (gt_m * gpost_ref[...]) * _rms(y)


def _mixer(x, mod3, g_pre, g_post, w_in, conv_w, conv_b, w_gate, b_gate, lru_a,
           v_norm_g, v_norm_b, w_spatial, b_sp_full, g_lru_out, g_gmlp_out, w_out):
    bsz, s, d = x.shape
    t = MIX_T
    n_lg = LRU_WIDTH // LANES
    scan_rows = SUBLANES * (t // SUBLANES + 1)
    full = lambda a: pl.BlockSpec(a.shape, lambda b, i: (0,) * a.ndim)
    in_arrays = [g_pre, g_post, w_in, conv_w, conv_b, w_gate, b_gate, lru_a,
                 v_norm_g, v_norm_b, w_spatial, b_sp_full, g_lru_out, g_gmlp_out, w_out]
    return pl.pallas_call(
        _mixer_kernel,
        grid=(bsz, s // t),
        in_specs=[
            pl.BlockSpec((None, t, d), lambda b, i: (b, i, 0)),
            pl.BlockSpec((None, 1, mod3.shape[2]), lambda b, i: (b, 0, 0)),
        ] + [full(a) for a in in_arrays],
        out_specs=pl.BlockSpec((None, t, d), lambda b, i: (b, i, 0)),
        out_shape=jax.ShapeDtypeStruct(x.shape, x.dtype),
        scratch_shapes=[
            pltpu.VMEM((n_lg, SUBLANES + t, LANES), F32),
            pltpu.VMEM((n_lg, scan_rows, LANES), F32),
            pltpu.VMEM((n_lg, scan_rows, LANES), F32),
            pltpu.VMEM((n_lg, scan_rows, LANES), F32),
            pltpu.VMEM((SUBLANES, LRU_WIDTH), F32),
            pltpu.VMEM((SUBLANES, LRU_WIDTH), F32),
        ],
        compiler_params=pltpu.CompilerParams(
            dimension_semantics=("arbitrary", "arbitrary"), vmem_limit_bytes=VMEM_LIMIT),
        name="token_mixer",
    )(x, mod3, *in_arrays)


def _ffn_kernel(x_ref, mod_ref, gpre_ref, gpost_ref, wup_ref, cw_ref, cb_ref, wd_ref,
                o_ref, scr_ref, tail_ref):
    t_idx = pl.program_id(1)
    t, d = x_ref.shape
    d_ff = wd_ref.shape[0]
    fc = FFN_FC
    lg = fc // LANES

    @pl.when(t_idx == 0)
    def _():
        tail_ref[...] = jnp.zeros_like(tail_ref)

    mod = mod_ref[...]
    sh_f = mod[:, 3 * d:4 * d]
    sc_f = mod[:, 4 * d:5 * d]
    gt_f = mod[:, 5 * d:6 * d]
    x = x_ref[...]
    hb = (_rms(x) * (gpre_ref[...] * (1.0 + sc_f)) + sh_f).astype(BF16)

    acc = None
    slot = 0
    for j in range(d_ff // fc):
        conv = []
        for base, scale in ((0, 1.0), (d_ff, 0.5)):
            cols = slice(base + j * fc, base + (j + 1) * fc)
            up = _dot(hb, wup_ref[:, cols])
            tail = tail_ref[:, cols]
            tail_ref[:, cols] = up[t - SUBLANES:]
            conv.append(_conv_via_scratch(scr_ref, slot * lg, tail, up,
                                          cw_ref[:, cols] * scale, cb_ref[:, cols] * scale))
            slot = (slot + 1) % FFN_SLOTS
        act = jnp.concatenate(
            [_gelu2_times(cg, cv) for cg, cv in zip(*conv)], axis=1).astype(BF16)
        part = _dot(act, wd_ref[j * fc:(j + 1) * fc, :])
        acc = part if acc is None else acc + part
    o_ref[...] = x + (gt_f * gpost_ref[...]) * _rms(acc)


def _ffn(x, mod3, g_pre, g_post, w_up, conv_w, conv_b, w_down):
    bsz, s, d = x.shape
    t = FFN_T
    resident = lambda a: pl.BlockSpec(a.shape, lambda b, i: (0,) * a.ndim,
                                      pipeline_mode=pl.Buffered(1))
    return pl.pallas_call(
        _ffn_kernel,
        grid=(bsz, s // t),
        in_specs=[
            pl.BlockSpec((None, t, d), lambda b, i: (b, i, 0)),
            pl.BlockSpec((None, 1, mod3.shape[2]), lambda b, i: (b, 0, 0)),
            resident(g_pre), resident(g_post), resident(w_up), resident(conv_w),
            resident(conv_b), resident(w_down),
        ],
        out_specs=pl.BlockSpec((None, t, d), lambda b, i: (b, i, 0)),
        out_shape=jax.ShapeDtypeStruct(x.shape, x.dtype),
        scratch_shapes=[
            pltpu.VMEM((FFN_SLOTS * (FFN_FC // LANES), SUBLANES + t, LANES), F32),
            pltpu.VMEM((SUBLANES, w_up.shape[1]), F32),
        ],
        compiler_params=pltpu.CompilerParams(
            dimension_semantics=("arbitrary", "arbitrary"), vmem_limit_bytes=VMEM_LIMIT),
        name="conv_ffn",
    )(x, mod3, g_pre, g_post, w_up, conv_w, conv_b, w_down)


def _gate_weights(w_rgate, w_igate):
    heads_per_tile = GATE_TILE // LRU_HEAD_DIM
    tiles = []
    for j in range(LRU_WIDTH // GATE_TILE):
        hs = slice(j * heads_per_tile, (j + 1) * heads_per_tile)
        r_bd = jax.scipy.linalg.block_diag(*w_rgate[hs])
        i_bd = jax.scipy.linalg.block_diag(*w_igate[hs])
        tiles.append(jnp.concatenate([r_bd, i_bd], axis=1))
    return jnp.stack(tiles).astype(BF16)


def kernel(x, c, w_ada, b_ada, g_mix_pre, g_mix_post, w_in, conv_w, conv_b, w_rgate, b_rgate, w_igate, b_igate, lru_a, v_norm_g, v_norm_b, w_spatial, b_spatial, g_lru_out, g_gmlp_out, w_out, g_ffn_pre, g_ffn_post, w_up, ffn_conv_w, ffn_conv_b, w_down):
    depth = w_ada.shape[0]
    bsz, s, d = x.shape
    c_pad = jnp.pad(c, ((0, SUBLANES - bsz % SUBLANES if bsz % SUBLANES else 0), (0, 0)))
    for l in range(depth):
        mod = _modulation(c_pad, w_ada[l], b_ada[l][None, :])
        mod3 = mod.reshape(mod.shape[0], 1, mod.shape[1])
        row = lambda a: a[l][None, :]
        w_gate = _gate_weights(w_rgate[l], w_igate[l])
        b_gate = jnp.concatenate([b_rgate[l].reshape(1, -1), b_igate[l].reshape(1, -1)], axis=1)
        b_sp_full = jnp.repeat(b_spatial[l].T, GMLP_GROUP_DIM, axis=1)
        x = _mixer(x, mod3, row(g_mix_pre), row(g_mix_post), w_in[l].astype(BF16),
                   conv_w[l], row(conv_b), w_gate, b_gate, row(lru_a),
                   row(v_norm_g), row(v_norm_b), w_spatial[l], b_sp_full,
                   row(g_lru_out), row(g_gmlp_out), w_out[l].astype(BF16))
        x = _ffn(x, mod3, row(g_ffn_pre), row(g_ffn_post), w_up[l].astype(BF16),
                 ffn_conv_w[l], row(ffn_conv_b), w_down[l].astype(BF16))
    return x
```

```python
import jax
import jax.numpy as jnp
from jax import lax
from jax.experimental import pallas as pl
from jax.experimental.pallas import tpu as pltpu

CHUNK = 64
LRU_WIDTH = 512
LRU_HEADS = 8
LRU_HEAD_DIM = LRU_WIDTH // LRU_HEADS
LRU_CONV_WIDTH = 4
LRU_C = 8.0
GMLP_WIDTH = 512
GMLP_GROUPS = 4
GMLP_GROUP_DIM = GMLP_WIDTH // GMLP_GROUPS
GMLP_BLOCK = 128
FFN_CONV_WIDTH = 3
N_MOD = 6
EPS = 1e-6

SUBLANES = 8
LANES = 128
GATE_TILE = 256

MIX_T = 256
MIX_NB = 2
FFN_T = 512
FFN_FC = 512
FFN_SLOTS = 4
VMEM_LIMIT = 56 * 1024 * 1024

F32 = jnp.float32
BF16 = jnp.bfloat16


def _dot(a, b):
    return jnp.dot(a, b, preferred_element_type=F32)


def _rms(x):
    return x * lax.rsqrt(jnp.mean(x * x, axis=-1, keepdims=True) + EPS)


def _gelu(x):
    c0 = 0.7978845608028654
    hx = 0.5 * x
    return hx + hx * jnp.tanh(x * (c0 + (c0 * 0.044715) * (x * x)))


def _gelu2_times(x, v):
    c0 = 0.7978845608028654
    z = x * (c0 + (c0 * 0.044715) * (x * x))
    return (x * v) * (1.0 + jnp.tanh(z))


def _rows(ref, lead, start, size):
    return ref[pl.ds(lead, 1, stride=2), pl.ds(start, size), :][0]


def _put_rows(ref, lead, start, val):
    ref[pl.ds(lead, 1, stride=2), pl.ds(start, val.shape[0]), :] = val[None]


def _linear_scan(a, b, state_ref, a_scr, b_scr, h_scr):
    t = a[0].shape[0]
    seg = t // SUBLANES
    pitch = seg + 1 - seg % 2
    n = len(a)
    for c in range(n):
        for s in range(SUBLANES):
            _put_rows(a_scr, c, s * pitch, a[c][s * seg:(s + 1) * seg])
            _put_rows(b_scr, c, s * pitch, b[c][s * seg:(s + 1) * seg])
    step = lambda ref, c, j: ref[c, pl.ds(j, SUBLANES, stride=pitch), :]
    row = lax.broadcasted_iota(jnp.int32, (SUBLANES, LANES), 0)
    shifted = lambda v, sh, fill: jnp.where(row >= sh, pltpu.roll(v, sh, axis=0), fill)

    prod = [None] * n
    end = [None] * n
    for j in range(seg):
        for c in range(n):
            aj, bj = step(a_scr, c, j), step(b_scr, c, j)
            prod[c] = aj if j == 0 else aj * prod[c]
            end[c] = bj if j == 0 else aj * end[c] + bj

    h = []
    for c in range(n):
        cols = slice(c * LANES, (c + 1) * LANES)
        h0 = state_ref[SUBLANES - 1:SUBLANES, cols]
        p = prod[c]
        e = end[c] + jnp.where(row == 0, p * h0, 0.0)
        for sh in (1, 2, 4):
            e = e + p * shifted(e, sh, 0.0)
            if sh < 4:
                p = p * shifted(p, sh, 1.0)
        state_ref[:, cols] = e
        h.append(shifted(e, 1, h0))
    for j in range(seg):
        for c in range(n):
            h[c] = step(a_scr, c, j) * h[c] + step(b_scr, c, j)
            h_scr[c, pl.ds(j, SUBLANES, stride=pitch), :] = h[c]
    return [jnp.concatenate([_rows(h_scr, c, s * pitch, seg) for s in range(SUBLANES)], axis=0)
            for c in range(n)]


def _conv_via_scratch(scr_ref, lead0, tail, x, w, b):
    k_width = w.shape[0]
    t, c = x.shape
    outs = []
    for g in range(c // LANES):
        cols = slice(g * LANES, (g + 1) * LANES)
        scr_ref[lead0 + g, 0:SUBLANES, :] = tail[:, cols]
        scr_ref[lead0 + g, SUBLANES:SUBLANES + t, :] = x[:, cols]
    for g in range(c // LANES):
        cols = slice(g * LANES, (g + 1) * LANES)
        out = x[:, cols] * w[k_width - 1:k_width, cols] + b[:, cols]
        for k in range(k_width - 1):
            shifted = _rows(scr_ref, lead0 + g, SUBLANES - (k_width - 1 - k), t)
            out = out + shifted * w[k:k + 1, cols]
        outs.append(out)
    return outs


def _mod_kernel(c_ref, w_ref, b_ref, o_ref):
    c = c_ref[...]
    c_act = c * jax.nn.sigmoid(c)
    o_ref[...] = _dot(c_act.astype(BF16), w_ref[...].astype(BF16)) + b_ref[...]


def _modulation(c_pad, w_ada, b_ada):
    rows, d = c_pad.shape
    n = w_ada.shape[1]
    bn = d
    return pl.pallas_call(
        _mod_kernel,
        grid=(n // bn,),
        in_specs=[
            pl.BlockSpec((rows, d), lambda j: (0, 0)),
            pl.BlockSpec((d, bn), lambda j: (0, j)),
            pl.BlockSpec((1, bn), lambda j: (0, j)),
        ],
        out_specs=pl.BlockSpec((rows, bn), lambda j: (0, j)),
        out_shape=jax.ShapeDtypeStruct((rows, n), F32),
        compiler_params=pltpu.CompilerParams(
            dimension_semantics=("arbitrary",), vmem_limit_bytes=VMEM_LIMIT),
        name="adaln_mod",
    )(c_pad, w_ada, b_ada)


def _mixer_kernel(x_ref, mod_ref, gpre_ref, gpost_ref, win_ref, cw_ref, cb_ref,
                  wgate_ref, bgate_ref, lrua_ref, vng_ref, vnb_ref, wsp_ref, bsp_ref,
                  glru_ref, ggmlp_ref, wout_ref, o_ref, *scratch):
    nb_tiles = x_ref.shape[0]
    tail_ref, state_ref = scratch[4 * nb_tiles:]

    @pl.when(pl.program_id(1) == 0)
    def _():
        tail_ref[...] = jnp.zeros_like(tail_ref)
        state_ref[...] = jnp.zeros_like(state_ref)

    project = lambda k: _mix_in(x_ref.at[k], mod_ref.at[k], gpre_ref, win_ref)
    z = project(0)
    for k in range(nb_tiles):
        z_next = project(k + 1) if k + 1 < nb_tiles else None
        y = _mix_body(z, cw_ref, cb_ref, wgate_ref, bgate_ref, lrua_ref, vng_ref, vnb_ref,
                      wsp_ref, bsp_ref, glru_ref, ggmlp_ref,
                      *scratch[k:4 * nb_tiles:nb_tiles], tail_ref.at[k], state_ref.at[k])
        _mix_out(y, x_ref.at[k], mod_ref.at[k], gpost_ref, wout_ref, o_ref.at[k])
        z = z_next


def _mix_in(x_ref, mod_ref, gpre_ref, win_ref):
    d = x_ref.shape[1]
    mod = mod_ref[...]
    sh_m = mod[:, 0:d]
    sc_m = mod[:, d:2 * d]
    h = _rms(x_ref[...]) * (gpre_ref[...] * (1.0 + sc_m)) + sh_m
    return _dot(h.astype(BF16), win_ref[...])


def _mix_out(y, x_ref, mod_ref, gpost_ref, wout_ref, o_ref):
    d = x_ref.shape[1]
    gt_m = mod_ref[...][:, 2 * d:3 * d]
    y = _dot(y, wout_ref[...])
    o_ref[...] = x_ref[...] + (gt_m * gpost_ref[...]) * _rms(y)


def _mix_body(z, cw_ref, cb_ref, wgate_ref, bgate_ref, lrua_ref, vng_ref, vnb_ref,
              wsp_ref, bsp_ref, glru_ref, ggmlp_ref,
              conv_scr, a_scr, b_scr, h_scr, tail_ref, state_ref):
    t = z.shape[0]
    slabs = lambda v: [v[:, c * LANES:(c + 1) * LANES] for c in range(v.shape[1] // LANES)]
    lru_x = z[:, 0:LRU_WIDTH]
    lru_gate = z[:, LRU_WIDTH:2 * LRU_WIDTH]
    g_u = z[:, 2 * LRU_WIDTH:2 * LRU_WIDTH + GMLP_WIDTH]
    g_v = z[:, 2 * LRU_WIDTH + GMLP_WIDTH:]

    tail = tail_ref[...]
    tail_ref[...] = lru_x[t - SUBLANES:]
    xc = jnp.concatenate(
        _conv_via_scratch(conv_scr, 0, tail, lru_x, cw_ref[...], cb_ref[...]), axis=1)
    xcb = xc.astype(BF16)
    gates = []
    for j in range(LRU_WIDTH // GATE_TILE):
        gates.append(_dot(xcb[:, j * GATE_TILE:(j + 1) * GATE_TILE], wgate_ref[j]))
    r_pre = jnp.concatenate([g[:, :GATE_TILE] for g in gates], axis=1)
    i_pre = jnp.concatenate([g[:, GATE_TILE:] for g in gates], axis=1)
    bgate = bgate_ref[...]
    r = jax.nn.sigmoid(r_pre + bgate[:, :LRU_WIDTH])
    i = jax.nn.sigmoid(i_pre + bgate[:, LRU_WIDTH:])
    neg_a = -lrua_ref[...]
    softplus = jnp.maximum(neg_a, 0.0) + jnp.log1p(jnp.exp(-jnp.abs(neg_a)))
    log_a = r * ((-LRU_C) * softplus)
    a = jnp.exp(log_a)
    th = jnp.tanh(log_a)
    q = (-2.0 * th) / (1.0 - th)
    mult = jnp.where(q > 0.0, q * lax.rsqrt(q), 0.0)
    bx = (mult * xc) * i
    hs = _linear_scan(slabs(a), slabs(bx), state_ref, a_scr, b_scr, h_scr)
    y_lru = 0.5 * _gelu2_times(lru_gate, jnp.concatenate(hs, axis=1))
    yl = _rms(y_lru) * glru_ref[...]

    gv = _gelu(g_v)
    mu = jnp.mean(gv, axis=-1, keepdims=True)
    cen = gv - mu
    var = jnp.mean(cen * cen, axis=-1, keepdims=True)
    v = cen * lax.rsqrt(var + EPS) * vng_ref[...] + vnb_ref[...]
    vb = v.astype(BF16)
    nb = t // GMLP_BLOCK
    pi = lax.broadcasted_iota(jnp.int32, (GMLP_BLOCK, GMLP_BLOCK), 0) // CHUNK
    pj = lax.broadcasted_iota(jnp.int32, (GMLP_BLOCK, GMLP_BLOCK), 1) // CHUNK
    mask = pj <= pi
    sp_groups = []
    for g in range(GMLP_GROUPS):
        ws = jnp.where(mask, 0.5 * wsp_ref[g], 0.0).astype(BF16)
        cols = slice(g * GMLP_GROUP_DIM, (g + 1) * GMLP_GROUP_DIM)
        rhs = jnp.concatenate(
            [vb[n * GMLP_BLOCK:(n + 1) * GMLP_BLOCK, cols] for n in range(nb)], axis=1)
        sp_groups.append(_dot(ws, rhs))
    bsp = 0.5 * bsp_ref[...]
    sp_rows = []
    for n in range(nb):
        blk = jnp.concatenate(
            [sg[:, n * GMLP_GROUP_DIM:(n + 1) * GMLP_GROUP_DIM] for sg in sp_groups], axis=1)
        sp_rows.append(blk + bsp)
    sp_half = jnp.concatenate(sp_rows, axis=0)
    y_gmlp = _gelu2_times(g_u, sp_half)
    yg = _rms(y_gmlp) * ggmlp_ref[...]

    return jnp.concatenate([yl, yg], axis=1).astype(BF16)


def _mixer(x, mod3, g_pre, g_post, w_in, conv_w, conv_b, w_gate, b_gate, lru_a,
           v_norm_g, v_norm_b, w_spatial, b_sp_full, g_lru_out, g_gmlp_out, w_out):
    bsz, s, d = x.shape
    t = MIX_T
    n_lg = LRU_WIDTH // LANES
    scan_rows = SUBLANES * (t // SUBLANES + 1)
    full = lambda a: pl.BlockSpec(a.shape, lambda b, i: (0,) * a.ndim)
    in_arrays = [g_pre, g_post, w_in, conv_w, conv_b, w_gate, b_gate, lru_a,
                 v_norm_g, v_norm_b, w_spatial, b_sp_full, g_lru_out, g_gmlp_out, w_out]
    nb = MIX_NB
    assert bsz % nb == 0 and s % t == 0
    return pl.pallas_call(
        _mixer_kernel,
        grid=(bsz // nb, s // t),
        in_specs=[
            pl.BlockSpec((nb, t, d), lambda b, i: (b, i, 0)),
            pl.BlockSpec((nb, 1, mod3.shape[2]), lambda b, i: (b, 0, 0)),
        ] + [full(a) for a in in_arrays],
        out_specs=pl.BlockSpec((nb, t, d), lambda b, i: (b, i, 0)),
        out_shape=jax.ShapeDtypeStruct(x.shape, x.dtype),
        scratch_shapes=(
            [pltpu.VMEM((n_lg, SUBLANES + t, LANES), F32)] * nb
            + [pltpu.VMEM((n_lg, scan_rows, LANES), F32)] * (3 * nb)
            + [pltpu.VMEM((nb, SUBLANES, LRU_WIDTH), F32)] * 2),
        compiler_params=pltpu.CompilerParams(
            dimension_semantics=("arbitrary", "arbitrary"), vmem_limit_bytes=VMEM_LIMIT),
        name="token_mixer",
    )(x, mod3, *in_arrays)


def _ffn_kernel(x_ref, mod_ref, gpre_ref, gpost_ref, wup_ref, cw_ref, cb_ref, wd_ref,
                o_ref, scr_ref, tail_ref):
    t_idx = pl.program_id(1)
    t, d = x_ref.shape
    d_ff = wd_ref.shape[0]
    fc = FFN_FC
    lg = fc // LANES

    @pl.when(t_idx == 0)
    def _():
        tail_ref[...] = jnp.zeros_like(tail_ref)

    mod = mod_ref[...]
    sh_f = mod[:, 3 * d:4 * d]
    sc_f = mod[:, 4 * d:5 * d]
    gt_f = mod[:, 5 * d:6 * d]
    x = x_ref[...]
    hb = (_rms(x) * (gpre_ref[...] * (1.0 + sc_f)) + sh_f).astype(BF16)

    n_chunks = d_ff // fc
    halves = ((0, 1.0), (d_ff, 0.5))
    chunk_cols = lambda base, j: slice(base + j * fc, base + (j + 1) * fc)
    up_proj = lambda j: [_dot(hb, wup_ref[:, chunk_cols(base, j)]) for base, _ in halves]

    def gated(j, ups):
        conv = []
        for k, (base, scale) in enumerate(halves):
            cols = chunk_cols(base, j)
            tail = tail_ref[:, cols]
            tail_ref[:, cols] = ups[k][t - SUBLANES:]
            slot = (2 * j + k) % FFN_SLOTS
            conv.append(_conv_via_scratch(scr_ref, slot * lg, tail, ups[k],
                                          cw_ref[:, cols] * scale, cb_ref[:, cols] * scale))
        return jnp.concatenate(
            [_gelu2_times(cg, cv) for cg, cv in zip(*conv)], axis=1).astype(BF16)

    acc = None
    ups = up_proj(0)
    for j in range(n_chunks):
        ups_next = up_proj(j + 1) if j + 1 < n_chunks else None
        part = _dot(gated(j, ups), wd_ref[j * fc:(j + 1) * fc, :])
        acc = part if acc is None else acc + part
        ups = ups_next
    o_ref[...] = x + (gt_f * gpost_ref[...]) * _rms(acc)


def _ffn(x, mod3, g_pre, g_post, w_up, conv_w, conv_b, w_down):
    bsz, s, d = x.shape
    t = FFN_T
    resident = lambda a: pl.BlockSpec(a.shape, lambda b, i: (0,) * a.ndim,
                                      pipeline_mode=pl.Buffered(1))
    return pl.pallas_call(
        _ffn_kernel,
        grid=(bsz, s // t),
        in_specs=[
            pl.BlockSpec((None, t, d), lambda b, i: (b, i, 0)),
            pl.BlockSpec((None, 1, mod3.shape[2]), lambda b, i: (b, 0, 0)),
            resident(g_pre), resident(g_post), resident(w_up), resident(conv_w),
            resident(conv_b), resident(w_down),
        ],
        out_specs=pl.BlockSpec((None, t, d), lambda b, i: (b, i, 0)),
        out_shape=jax.ShapeDtypeStruct(x.shape, x.dtype),
        scratch_shapes=[
            pltpu.VMEM((FFN_SLOTS * (FFN_FC // LANES), SUBLANES + t, LANES), F32),
            pltpu.VMEM((SUBLANES, w_up.shape[1]), F32),
        ],
        compiler_params=pltpu.CompilerParams(
            dimension_semantics=("arbitrary", "arbitrary"), vmem_limit_bytes=VMEM_LIMIT),
        name="conv_ffn",
    )(x, mod3, g_pre, g_post, w_up, conv_w, conv_b, w_down)


def _gate_weights(w_rgate, w_igate):
    heads_per_tile = GATE_TILE // LRU_HEAD_DIM
    tiles = []
    for j in range(LRU_WIDTH // GATE_TILE):
        hs = slice(j * heads_per_tile, (j + 1) * heads_per_tile)
        r_bd = jax.scipy.linalg.block_diag(*w_rgate[hs])
        i_bd = jax.scipy.linalg.block_diag(*w_igate[hs])
        tiles.append(jnp.concatenate([r_bd, i_bd], axis=1))
    return jnp.stack(tiles).astype(BF16)


def kernel(x, c, w_ada, b_ada, g_mix_pre, g_mix_post, w_in, conv_w, conv_b, w_rgate, b_rgate, w_igate, b_igate, lru_a, v_norm_g, v_norm_b, w_spatial, b_spatial, g_lru_out, g_gmlp_out, w_out, g_ffn_pre, g_ffn_post, w_up, ffn_conv_w, ffn_conv_b, w_down):
    depth = w_ada.shape[0]
    bsz, s, d = x.shape
    c_pad = jnp.pad(c, ((0, SUBLANES - bsz % SUBLANES if bsz % SUBLANES else 0), (0, 0)))
    for l in range(depth):
        mod = _modulation(c_pad, w_ada[l], b_ada[l][None, :])
        mod3 = mod.reshape(mod.shape[0], 1, mod.shape[1])
        row = lambda a: a[l][None, :]
        w_gate = _gate_weights(w_rgate[l], w_igate[l])
        b_gate = jnp.concatenate([b_rgate[l].reshape(1, -1), b_igate[l].reshape(1, -1)], axis=1)
        b_sp_full = jnp.repeat(b_spatial[l].T, GMLP_GROUP_DIM, axis=1)
        x = _mixer(x, mod3, row(g_mix_pre), row(g_mix_post), w_in[l].astype(BF16),
                   conv_w[l], row(conv_b), w_gate, b_gate, row(lru_a),
                   row(v_norm_g), row(v_norm_b), w_spatial[l], b_sp_full,
                   row(g_lru_out), row(g_gmlp_out), w_out[l].astype(BF16))
        x = _ffn(x, mod3, row(g_ffn_pre), row(g_ffn_post), w_up[l].astype(BF16),
                 ffn_conv_w[l], row(ffn_conv_b), w_down[l].astype(BF16))
    return x
```

```python
import jax
import jax.numpy as jnp
from jax import lax
from jax.experimental import pallas as pl
from jax.experimental.pallas import tpu as pltpu

CHUNK = 64
LRU_WIDTH = 512
LRU_HEADS = 8
LRU_HEAD_DIM = LRU_WIDTH // LRU_HEADS
LRU_CONV_WIDTH = 4
LRU_C = 8.0
GMLP_WIDTH = 512
GMLP_GROUPS = 4
GMLP_GROUP_DIM = GMLP_WIDTH // GMLP_GROUPS
GMLP_BLOCK = 128
FFN_CONV_WIDTH = 3
N_MOD = 6
EPS = 1e-6

SUBLANES = 8
LANES = 128
GATE_TILE = 256

MIX_T = 256
MIX_NB = 2
FFN_T = 512
FFN_FC = 512
FFN_SLOTS = 4
VMEM_LIMIT = 56 * 1024 * 1024

F32 = jnp.float32
BF16 = jnp.bfloat16
U32 = jnp.uint32


def _dot(a, b):
    return jnp.dot(a, b, preferred_element_type=F32)


def _pack_rows(w):
    k, n = w.shape[-2:]
    wb = w.astype(BF16).reshape(*w.shape[:-2], k // 2, 2, n)
    return lax.bitcast_convert_type(jnp.swapaxes(wb, -1, -2), U32)


def _as_bf16(packed):
    return pltpu.bitcast(packed, BF16)


def _rms(x):
    return x * lax.rsqrt(jnp.mean(x * x, axis=-1, keepdims=True) + EPS)


def _gelu(x):
    c0 = 0.7978845608028654
    hx = 0.5 * x
    return hx + hx * jnp.tanh(x * (c0 + (c0 * 0.044715) * (x * x)))


def _gelu2_times(x, v):
    c0 = 0.7978845608028654
    z = x * (c0 + (c0 * 0.044715) * (x * x))
    return (x * v) * (1.0 + jnp.tanh(z))


def _rows(ref, lead, start, size):
    return ref[pl.ds(lead, 1, stride=2), pl.ds(start, size), :][0]


def _put_rows(ref, lead, start, val):
    ref[pl.ds(lead, 1, stride=2), pl.ds(start, val.shape[0]), :] = val[None]


def _linear_scan(a, b, state_ref, a_scr, b_scr, h_scr):
    t = a[0].shape[0]
    seg = t // SUBLANES
    pitch = seg + 1 - seg % 2
    n = len(a)
    for c in range(n):
        for s in range(SUBLANES):
            _put_rows(a_scr, c, s * pitch, a[c][s * seg:(s + 1) * seg])
            _put_rows(b_scr, c, s * pitch, b[c][s * seg:(s + 1) * seg])
    step = lambda ref, c, j: ref[c, pl.ds(j, SUBLANES, stride=pitch), :]
    row = lax.broadcasted_iota(jnp.int32, (SUBLANES, LANES), 0)
    shifted = lambda v, sh, fill: jnp.where(row >= sh, pltpu.roll(v, sh, axis=0), fill)

    prod = [None] * n
    end = [None] * n
    for j in range(seg):
        for c in range(n):
            aj, bj = step(a_scr, c, j), step(b_scr, c, j)
            prod[c] = aj if j == 0 else aj * prod[c]
            end[c] = bj if j == 0 else aj * end[c] + bj

    h = []
    for c in range(n):
        cols = slice(c * LANES, (c + 1) * LANES)
        h0 = state_ref[SUBLANES - 1:SUBLANES, cols]
        p = prod[c]
        e = end[c] + jnp.where(row == 0, p * h0, 0.0)
        for sh in (1, 2, 4):
            e = e + p * shifted(e, sh, 0.0)
            if sh < 4:
                p = p * shifted(p, sh, 1.0)
        state_ref[:, cols] = e
        h.append(shifted(e, 1, h0))
    for j in range(seg):
        for c in range(n):
            h[c] = step(a_scr, c, j) * h[c] + step(b_scr, c, j)
            h_scr[c, pl.ds(j, SUBLANES, stride=pitch), :] = h[c]
    return [jnp.concatenate([_rows(h_scr, c, s * pitch, seg) for s in range(SUBLANES)], axis=0)
            for c in range(n)]


def _conv_via_scratch(scr_ref, lead0, tail, x, w, b):
    k_width = w.shape[0]
    t, c = x.shape
    outs = []
    for g in range(c // LANES):
        cols = slice(g * LANES, (g + 1) * LANES)
        scr_ref[lead0 + g, 0:SUBLANES, :] = tail[:, cols]
        scr_ref[lead0 + g, SUBLANES:SUBLANES + t, :] = x[:, cols]
    for g in range(c // LANES):
        cols = slice(g * LANES, (g + 1) * LANES)
        out = x[:, cols] * w[k_width - 1:k_width, cols] + b[:, cols]
        for k in range(k_width - 1):
            shifted = _rows(scr_ref, lead0 + g, SUBLANES - (k_width - 1 - k), t)
            out = out + shifted * w[k:k + 1, cols]
        outs.append(out)
    return outs


def _mod_kernel(c_ref, w_ref, b_ref, o_ref):
    c = c_ref[...]
    c_act = c * jax.nn.sigmoid(c)
    o_ref[...] = _dot(c_act.astype(BF16), w_ref[...].astype(BF16)) + b_ref[...]


def _modulation(c_pad, w_ada, b_ada):
    rows, d = c_pad.shape
    n = w_ada.shape[1]
    bn = d
    return pl.pallas_call(
        _mod_kernel,
        grid=(n // bn,),
        in_specs=[
            pl.BlockSpec((rows, d), lambda j: (0, 0)),
            pl.BlockSpec((d, bn), lambda j: (0, j)),
            pl.BlockSpec((1, bn), lambda j: (0, j)),
        ],
        out_specs=pl.BlockSpec((rows, bn), lambda j: (0, j)),
        out_shape=jax.ShapeDtypeStruct((rows, n), F32),
        compiler_params=pltpu.CompilerParams(
            dimension_semantics=("arbitrary",), vmem_limit_bytes=VMEM_LIMIT),
        name="adaln_mod",
    )(c_pad, w_ada, b_ada)


def _mixer_kernel(x_ref, mod_ref, gpre_ref, gpost_ref, win_ref, cw_ref, cb_ref,
                  wgate_ref, bgate_ref, lrua_ref, vng_ref, vnb_ref, wsp_ref, bsp_ref,
                  glru_ref, ggmlp_ref, wout_ref, o_ref, *scratch):
    nb_tiles = x_ref.shape[0]
    tail_ref, state_ref = scratch[4 * nb_tiles:]

    @pl.when(pl.program_id(1) == 0)
    def _():
        tail_ref[...] = jnp.zeros_like(tail_ref)
        state_ref[...] = jnp.zeros_like(state_ref)

    project = lambda k: _mix_in(x_ref.at[k], mod_ref.at[k], gpre_ref, win_ref)
    z = project(0)
    for k in range(nb_tiles):
        conv_scr, a_scr, b_scr, h_scr = scratch[k:4 * nb_tiles:nb_tiles]
        xc, gates = _mix_gates(z, cw_ref, cb_ref, wgate_ref, conv_scr, tail_ref.at[k])
        z_next = project(k + 1) if k + 1 < nb_tiles else None
        y = _mix_body(z, xc, gates, bgate_ref, lrua_ref, vng_ref, vnb_ref, wsp_ref, bsp_ref,
                      glru_ref, ggmlp_ref, a_scr, b_scr, h_scr, state_ref.at[k])
        _mix_out(y, x_ref.at[k], mod_ref.at[k], gpost_ref, wout_ref, o_ref.at[k])
        z = z_next


def _mix_in(x_ref, mod_ref, gpre_ref, win_ref):
    d = x_ref.shape[1]
    mod = mod_ref[...]
    sh_m = mod[:, 0:d]
    sc_m = mod[:, d:2 * d]
    h = _rms(x_ref[...]) * (gpre_ref[...] * (1.0 + sc_m)) + sh_m
    return _dot(h.astype(BF16), _as_bf16(win_ref[...]))


def _mix_gates(z, cw_ref, cb_ref, wgate_ref, conv_scr, tail_ref):
    t = z.shape[0]
    lru_x = z[:, 0:LRU_WIDTH]
    tail = tail_ref[...]
    tail_ref[...] = lru_x[t - SUBLANES:]
    xc = jnp.concatenate(
        _conv_via_scratch(conv_scr, 0, tail, lru_x, cw_ref[...], cb_ref[...]), axis=1)
    xcb = xc.astype(BF16)
    gates = [_dot(xcb[:, j * GATE_TILE:(j + 1) * GATE_TILE], _as_bf16(wgate_ref[j]))
             for j in range(LRU_WIDTH // GATE_TILE)]
    return xc, gates


def _mix_out(y, x_ref, mod_ref, gpost_ref, wout_ref, o_ref):
    d = x_ref.shape[1]
    gt_m = mod_ref[...][:, 2 * d:3 * d]
    y = _dot(y, _as_bf16(wout_ref[...]))
    o_ref[...] = x_ref[...] + (gt_m * gpost_ref[...]) * _rms(y)


def _mix_body(z, xc, gates, bgate_ref, lrua_ref, vng_ref, vnb_ref, wsp_ref, bsp_ref,
              glru_ref, ggmlp_ref, a_scr, b_scr, h_scr, state_ref):
    t = z.shape[0]
    slabs = lambda v: [v[:, c * LANES:(c + 1) * LANES] for c in range(v.shape[1] // LANES)]
    lru_gate = z[:, LRU_WIDTH:2 * LRU_WIDTH]
    g_u = z[:, 2 * LRU_WIDTH:2 * LRU_WIDTH + GMLP_WIDTH]
    g_v = z[:, 2 * LRU_WIDTH + GMLP_WIDTH:]

    r_pre = jnp.concatenate([g[:, :GATE_TILE] for g in gates], axis=1)
    i_pre = jnp.concatenate([g[:, GATE_TILE:] for g in gates], axis=1)
    bgate = bgate_ref[...]
    r_gate = jax.nn.sigmoid(r_pre + bgate[:, :LRU_WIDTH])
    i_gate = jax.nn.sigmoid(i_pre + bgate[:, LRU_WIDTH:])
    neg_a = -lrua_ref[...]
    softplus = jnp.maximum(neg_a, 0.0) + jnp.log1p(jnp.exp(-jnp.abs(neg_a)))
    log_a = r_gate * ((-LRU_C) * softplus)
    a = jnp.exp(log_a)
    th = jnp.tanh(log_a)
    q = (-2.0 * th) / (1.0 - th)
    mult = jnp.where(q > 0.0, q * lax.rsqrt(q), 0.0)
    bx = (mult * xc) * i_gate
    hs = _linear_scan(slabs(a), slabs(bx), state_ref, a_scr, b_scr, h_scr)
    y_lru = 0.5 * _gelu2_times(lru_gate, jnp.concatenate(hs, axis=1))
    yl = _rms(y_lru) * glru_ref[...]

    gv = _gelu(g_v)
    mu = jnp.mean(gv, axis=-1, keepdims=True)
    cen = gv - mu
    var = jnp.mean(cen * cen, axis=-1, keepdims=True)
    v = cen * lax.rsqrt(var + EPS) * vng_ref[...] + vnb_ref[...]
    vb = v.astype(BF16)
    nb = t // GMLP_BLOCK
    pi = lax.broadcasted_iota(jnp.int32, (GMLP_BLOCK, GMLP_BLOCK), 0) // CHUNK
    pj = lax.broadcasted_iota(jnp.int32, (GMLP_BLOCK, GMLP_BLOCK), 1) // CHUNK
    mask = pj <= pi
    sp_groups = []
    for g in range(GMLP_GROUPS):
        ws = jnp.where(mask, 0.5 * wsp_ref[g], 0.0).astype(BF16)
        cols = slice(g * GMLP_GROUP_DIM, (g + 1) * GMLP_GROUP_DIM)
        rhs = jnp.concatenate(
            [vb[n * GMLP_BLOCK:(n + 1) * GMLP_BLOCK, cols] for n in range(nb)], axis=1)
        sp_groups.append(_dot(ws, rhs))
    bsp = 0.5 * bsp_ref[...]
    sp_rows = []
    for n in range(nb):
        blk = jnp.concatenate(
            [sg[:, n * GMLP_GROUP_DIM:(n + 1) * GMLP_GROUP_DIM] for sg in sp_groups], axis=1)
        sp_rows.append(blk + bsp)
    sp_half = jnp.concatenate(sp_rows, axis=0)
    y_gmlp = _gelu2_times(g_u, sp_half)
    yg = _rms(y_gmlp) * ggmlp_ref[...]

    return jnp.concatenate([yl, yg], axis=1).astype(BF16)


def _mixer(x, mod3, g_pre, g_post, w_in, conv_w, conv_b, w_gate, b_gate, lru_a,
           v_norm_g, v_norm_b, w_spatial, b_sp_full, g_lru_out, g_gmlp_out, w_out):
    bsz, s, d = x.shape
    t = MIX_T
    nb = MIX_NB
    assert bsz % nb == 0 and s % t == 0
    n_lg = LRU_WIDTH // LANES
    scan_rows = SUBLANES * (t // SUBLANES + 1)
    full = lambda a: pl.BlockSpec(a.shape, lambda b, i: (0,) * a.ndim)
    in_arrays = [g_pre, g_post, w_in, conv_w, conv_b, w_gate, b_gate, lru_a,
                 v_norm_g, v_norm_b, w_spatial, b_sp_full, g_lru_out, g_gmlp_out, w_out]
    return pl.pallas_call(
        _mixer_kernel,
        grid=(bsz // nb, s // t),
        in_specs=[
            pl.BlockSpec((nb, t, d), lambda b, i: (b, i, 0)),
            pl.BlockSpec((nb, 1, mod3.shape[2]), lambda b, i: (b, 0, 0)),
        ] + [full(a) for a in in_arrays],
        out_specs=pl.BlockSpec((nb, t, d), lambda b, i: (b, i, 0)),
        out_shape=jax.ShapeDtypeStruct(x.shape, x.dtype),
        scratch_shapes=(
            [pltpu.VMEM((n_lg, SUBLANES + t, LANES), F32)] * nb
            + [pltpu.VMEM((n_lg, scan_rows, LANES), F32)] * (3 * nb)
            + [pltpu.VMEM((nb, SUBLANES, LRU_WIDTH), F32)] * 2),
        compiler_params=pltpu.CompilerParams(
            dimension_semantics=("arbitrary", "arbitrary"), vmem_limit_bytes=VMEM_LIMIT),
        name="token_mixer",
    )(x, mod3, *in_arrays)


def _ffn_kernel(x_ref, mod_ref, gpre_ref, gpost_ref, wup_ref, cw_ref, cb_ref, wd_ref,
                o_ref, scr_ref, tail_ref):
    t_idx = pl.program_id(1)
    t, d = x_ref.shape
    d_ff = 2 * wd_ref.shape[0]
    fc = FFN_FC
    lg = fc // LANES

    @pl.when(t_idx == 0)
    def _():
        tail_ref[...] = jnp.zeros_like(tail_ref)

    mod = mod_ref[...]
    sh_f = mod[:, 3 * d:4 * d]
    sc_f = mod[:, 4 * d:5 * d]
    gt_f = mod[:, 5 * d:6 * d]
    x = x_ref[...]
    hb = (_rms(x) * (gpre_ref[...] * (1.0 + sc_f)) + sh_f).astype(BF16)

    n_chunks = d_ff // fc
    halves = ((0, 1.0), (d_ff, 0.5))
    chunk_cols = lambda base, j: slice(base + j * fc, base + (j + 1) * fc)
    up_proj = lambda j: [_dot(hb, _as_bf16(wup_ref[:, chunk_cols(base, j)])) for base, _ in halves]

    def gated(j, ups):
        conv = []
        for k, (base, scale) in enumerate(halves):
            cols = chunk_cols(base, j)
            tail = tail_ref[:, cols]
            tail_ref[:, cols] = ups[k][t - SUBLANES:]
            slot = (2 * j + k) % FFN_SLOTS
            conv.append(_conv_via_scratch(scr_ref, slot * lg, tail, ups[k],
                                          cw_ref[:, cols] * scale, cb_ref[:, cols] * scale))
        return jnp.concatenate(
            [_gelu2_times(cg, cv) for cg, cv in zip(*conv)], axis=1).astype(BF16)

    acc = None
    ups = up_proj(0)
    for j in range(n_chunks):
        ups_next = up_proj(j + 1) if j + 1 < n_chunks else None
        part = _dot(gated(j, ups), _as_bf16(wd_ref[j * fc // 2:(j + 1) * fc // 2, :]))
        acc = part if acc is None else acc + part
        ups = ups_next
    o_ref[...] = x + (gt_f * gpost_ref[...]) * _rms(acc)


def _ffn(x, mod3, g_pre, g_post, w_up, conv_w, conv_b, w_down):
    bsz, s, d = x.shape
    t = FFN_T
    resident = lambda a: pl.BlockSpec(a.shape, lambda b, i: (0,) * a.ndim,
                                      pipeline_mode=pl.Buffered(1))
    return pl.pallas_call(
        _ffn_kernel,
        grid=(bsz, s // t),
        in_specs=[
            pl.BlockSpec((None, t, d), lambda b, i: (b, i, 0)),
            pl.BlockSpec((None, 1, mod3.shape[2]), lambda b, i: (b, 0, 0)),
            resident(g_pre), resident(g_post), resident(w_up), resident(conv_w),
            resident(conv_b), resident(w_down),
        ],
        out_specs=pl.BlockSpec((None, t, d), lambda b, i: (b, i, 0)),
        out_shape=jax.ShapeDtypeStruct(x.shape, x.dtype),
        scratch_shapes=[
            pltpu.VMEM((FFN_SLOTS * (FFN_FC // LANES), SUBLANES + t, LANES), F32),
            pltpu.VMEM((SUBLANES, w_up.shape[1]), F32),
        ],
        compiler_params=pltpu.CompilerParams(
            dimension_semantics=("arbitrary", "arbitrary"), vmem_limit_bytes=VMEM_LIMIT),
        name="conv_ffn",
    )(x, mod3, g_pre, g_post, w_up, conv_w, conv_b, w_down)


def _gate_weights(w_rgate, w_igate):
    heads_per_tile = GATE_TILE // LRU_HEAD_DIM
    tiles = []
    for j in range(LRU_WIDTH // GATE_TILE):
        hs = slice(j * heads_per_tile, (j + 1) * heads_per_tile)
        r_bd = jax.scipy.linalg.block_diag(*w_rgate[hs])
        i_bd = jax.scipy.linalg.block_diag(*w_igate[hs])
        tiles.append(jnp.concatenate([r_bd, i_bd], axis=1))
    return _pack_rows(jnp.stack(tiles))


def kernel(x, c, w_ada, b_ada, g_mix_pre, g_mix_post, w_in, conv_w, conv_b, w_rgate, b_rgate, w_igate, b_igate, lru_a, v_norm_g, v_norm_b, w_spatial, b_spatial, g_lru_out, g_gmlp_out, w_out, g_ffn_pre, g_ffn_post, w_up, ffn_conv_w, ffn_conv_b, w_down):
    depth = w_ada.shape[0]
    bsz, s, d = x.shape
    c_pad = jnp.pad(c, ((0, SUBLANES - bsz % SUBLANES if bsz % SUBLANES else 0), (0, 0)))
    for l in range(depth):
        mod = _modulation(c_pad, w_ada[l], b_ada[l][None, :])
        mod3 = mod.reshape(mod.shape[0], 1, mod.shape[1])
        row = lambda a: a[l][None, :]
        w_gate = _gate_weights(w_rgate[l], w_igate[l])
        b_gate = jnp.concatenate([b_rgate[l].reshape(1, -1), b_igate[l].reshape(1, -1)], axis=1)
        b_sp_full = jnp.repeat(b_spatial[l].T, GMLP_GROUP_DIM, axis=1)
        x = _mixer(x, mod3, row(g_mix_pre), row(g_mix_post), _pack_rows(w_in[l]),
                   conv_w[l], row(conv_b), w_gate, b_gate, row(lru_a),
                   row(v_norm_g), row(v_norm_b), w_spatial[l], b_sp_full,
                   row(g_lru_out), row(g_gmlp_out), _pack_rows(w_out[l]))
        x = _ffn(x, mod3, row(g_ffn_pre), row(g_ffn_post), _pack_rows(w_up[l]),
                 ffn_conv_w[l], row(ffn_conv_b), _pack_rows(w_down[l]))
    return x
```

```python
import jax
import jax.numpy as jnp
from jax import lax
from jax.experimental import pallas as pl
from jax.experimental.pallas import tpu as pltpu

CHUNK = 64
LRU_WIDTH = 512
LRU_HEADS = 8
LRU_HEAD_DIM = LRU_WIDTH // LRU_HEADS
LRU_CONV_WIDTH = 4
LRU_C = 8.0
GMLP_WIDTH = 512
GMLP_GROUPS = 4
GMLP_GROUP_DIM = GMLP_WIDTH // GMLP_GROUPS
GMLP_BLOCK = 128
FFN_CONV_WIDTH = 3
N_MOD = 6
EPS = 1e-6

SUBLANES = 8
LANES = 128
GATE_TILE = 256

MIX_T = 256
MIX_NB = 2
FFN_T = 512
FFN_FC = 512
FFN_SLOTS = 4
PACK_ROWS = 256
VMEM_LIMIT = 56 * 1024 * 1024

F32 = jnp.float32
BF16 = jnp.bfloat16
U32 = jnp.uint32


def _dot(a, b):
    return jnp.dot(a, b, preferred_element_type=F32)


def _pack_rows_xla(w):
    k, n = w.shape[-2:]
    wb = w.astype(BF16).reshape(*w.shape[:-2], k // 2, 2, n)
    return lax.bitcast_convert_type(jnp.swapaxes(wb, -1, -2), U32)


def _pack_kernel(w_ref, o_ref):
    o_ref[...] = pltpu.bitcast(w_ref[...].astype(BF16), U32)


def _pack_rows(w):
    k, n = w.shape
    bk = min(k, PACK_ROWS)
    assert k % bk == 0
    return pl.pallas_call(
        _pack_kernel,
        grid=(k // bk,),
        in_specs=[pl.BlockSpec((bk, n), lambda i: (i, 0))],
        out_specs=pl.BlockSpec((bk // 2, n), lambda i: (i, 0)),
        out_shape=jax.ShapeDtypeStruct((k // 2, n), U32),
        compiler_params=pltpu.CompilerParams(
            dimension_semantics=("arbitrary",), vmem_limit_bytes=VMEM_LIMIT),
        name="pack_weight",
    )(w)


def _as_bf16(packed):
    return pltpu.bitcast(packed, BF16)


def _rms(x):
    return x * lax.rsqrt(jnp.mean(x * x, axis=-1, keepdims=True) + EPS)


def _gelu(x):
    c0 = 0.7978845608028654
    hx = 0.5 * x
    return hx + hx * jnp.tanh(x * (c0 + (c0 * 0.044715) * (x * x)))


def _gelu2_times(x, v):
    c0 = 0.7978845608028654
    z = x * (c0 + (c0 * 0.044715) * (x * x))
    return (x * v) * (1.0 + jnp.tanh(z))


def _rows(ref, lead, start, size):
    return ref[pl.ds(lead, 1, stride=2), pl.ds(start, size), :][0]


def _put_rows(ref, lead, start, val):
    ref[pl.ds(lead, 1, stride=2), pl.ds(start, val.shape[0]), :] = val[None]


def _linear_scan(a, b, state_ref, a_scr, b_scr, h_scr):
    t = a[0].shape[0]
    seg = t // SUBLANES
    pitch = seg + 1 - seg % 2
    n = len(a)
    for c in range(n):
        for s in range(SUBLANES):
            _put_rows(a_scr, c, s * pitch, a[c][s * seg:(s + 1) * seg])
            _put_rows(b_scr, c, s * pitch, b[c][s * seg:(s + 1) * seg])
    step = lambda ref, c, j: ref[c, pl.ds(j, SUBLANES, stride=pitch), :]
    row = lax.broadcasted_iota(jnp.int32, (SUBLANES, LANES), 0)
    shifted = lambda v, sh, fill: jnp.where(row >= sh, pltpu.roll(v, sh, axis=0), fill)

    prod = [None] * n
    end = [None] * n
    for j in range(seg):
        for c in range(n):
            aj, bj = step(a_scr, c, j), step(b_scr, c, j)
            prod[c] = aj if j == 0 else aj * prod[c]
            end[c] = bj if j == 0 else aj * end[c] + bj

    h = []
    for c in range(n):
        cols = slice(c * LANES, (c + 1) * LANES)
        h0 = state_ref[SUBLANES - 1:SUBLANES, cols]
        p = prod[c]
        e = end[c] + jnp.where(row == 0, p * h0, 0.0)
        for sh in (1, 2, 4):
            e = e + p * shifted(e, sh, 0.0)
            if sh < 4:
                p = p * shifted(p, sh, 1.0)
        state_ref[:, cols] = e
        h.append(shifted(e, 1, h0))
    for j in range(seg):
        for c in range(n):
            h[c] = step(a_scr, c, j) * h[c] + step(b_scr, c, j)
            h_scr[c, pl.ds(j, SUBLANES, stride=pitch), :] = h[c]
    return [jnp.concatenate([_rows(h_scr, c, s * pitch, seg) for s in range(SUBLANES)], axis=0)
            for c in range(n)]


def _conv_via_scratch(scr_ref, lead0, tail, x, w, b):
    k_width = w.shape[0]
    t, c = x.shape
    outs = []
    for g in range(c // LANES):
        cols = slice(g * LANES, (g + 1) * LANES)
        scr_ref[lead0 + g, 0:SUBLANES, :] = tail[:, cols]
        scr_ref[lead0 + g, SUBLANES:SUBLANES + t, :] = x[:, cols]
    for g in range(c // LANES):
        cols = slice(g * LANES, (g + 1) * LANES)
        out = x[:, cols] * w[k_width - 1:k_width, cols] + b[:, cols]
        for k in range(k_width - 1):
            shifted = _rows(scr_ref, lead0 + g, SUBLANES - (k_width - 1 - k), t)
            out = out + shifted * w[k:k + 1, cols]
        outs.append(out)
    return outs


def _mod_kernel(c_ref, w_ref, b_ref, o_ref):
    c = c_ref[...]
    c_act = c * jax.nn.sigmoid(c)
    o_ref[...] = _dot(c_act.astype(BF16), w_ref[...].astype(BF16)) + b_ref[...]


def _modulation(c_pad, w_ada, b_ada):
    rows, d = c_pad.shape
    n = w_ada.shape[1]
    bn = d
    return pl.pallas_call(
        _mod_kernel,
        grid=(n // bn,),
        in_specs=[
            pl.BlockSpec((rows, d), lambda j: (0, 0)),
            pl.BlockSpec((d, bn), lambda j: (0, j)),
            pl.BlockSpec((1, bn), lambda j: (0, j)),
        ],
        out_specs=pl.BlockSpec((rows, bn), lambda j: (0, j)),
        out_shape=jax.ShapeDtypeStruct((rows, n), F32),
        compiler_params=pltpu.CompilerParams(
            dimension_semantics=("arbitrary",), vmem_limit_bytes=VMEM_LIMIT),
        name="adaln_mod",
    )(c_pad, w_ada, b_ada)


def _mixer_kernel(x_ref, mod_ref, gpre_ref, gpost_ref, win_ref, cw_ref, cb_ref,
                  wgate_ref, bgate_ref, lrua_ref, vng_ref, vnb_ref, wsp_ref, bsp_ref,
                  glru_ref, ggmlp_ref, wout_ref, o_ref, *scratch):
    nb_tiles = x_ref.shape[0]
    tail_ref, state_ref = scratch[4 * nb_tiles:]

    @pl.when(pl.program_id(1) == 0)
    def _():
        tail_ref[...] = jnp.zeros_like(tail_ref)
        state_ref[...] = jnp.zeros_like(state_ref)

    project = lambda k: _mix_in(x_ref.at[k], mod_ref.at[k], gpre_ref, win_ref)
    z = project(0)
    for k in range(nb_tiles):
        conv_scr, a_scr, b_scr, h_scr = scratch[k:4 * nb_tiles:nb_tiles]
        xc, gates = _mix_gates(z, cw_ref, cb_ref, wgate_ref, conv_scr, tail_ref.at[k])
        z_next = project(k + 1) if k + 1 < nb_tiles else None
        y = _mix_body(z, xc, gates, bgate_ref, lrua_ref, vng_ref, vnb_ref, wsp_ref, bsp_ref,
                      glru_ref, ggmlp_ref, a_scr, b_scr, h_scr, state_ref.at[k])
        _mix_out(y, x_ref.at[k], mod_ref.at[k], gpost_ref, wout_ref, o_ref.at[k])
        z = z_next


def _mix_in(x_ref, mod_ref, gpre_ref, win_ref):
    d = x_ref.shape[1]
    mod = mod_ref[...]
    sh_m = mod[:, 0:d]
    sc_m = mod[:, d:2 * d]
    h = _rms(x_ref[...]) * (gpre_ref[...] * (1.0 + sc_m)) + sh_m
    return _dot(h.astype(BF16), _as_bf16(win_ref[...]))


def _mix_gates(z, cw_ref, cb_ref, wgate_ref, conv_scr, tail_ref):
    t = z.shape[0]
    lru_x = z[:, 0:LRU_WIDTH]
    tail = tail_ref[...]
    tail_ref[...] = lru_x[t - SUBLANES:]
    xc = jnp.concatenate(
        _conv_via_scratch(conv_scr, 0, tail, lru_x, cw_ref[...], cb_ref[...]), axis=1)
    xcb = xc.astype(BF16)
    gates = [_dot(xcb[:, j * GATE_TILE:(j + 1) * GATE_TILE], _as_bf16(wgate_ref[j]))
             for j in range(LRU_WIDTH // GATE_TILE)]
    return xc, gates


def _mix_out(y, x_ref, mod_ref, gpost_ref, wout_ref, o_ref):
    d = x_ref.shape[1]
    gt_m = mod_ref[...][:, 2 * d:3 * d]
    y = _dot(y, _as_bf16(wout_ref[...]))
    o_ref[...] = x_ref[...] + (gt_m * gpost_ref[...]) * _rms(y)


def _mix_body(z, xc, gates, bgate_ref, lrua_ref, vng_ref, vnb_ref, wsp_ref, bsp_ref,
              glru_ref, ggmlp_ref, a_scr, b_scr, h_scr, state_ref):
    t = z.shape[0]
    slabs = lambda v: [v[:, c * LANES:(c + 1) * LANES] for c in range(v.shape[1] // LANES)]
    lru_gate = z[:, LRU_WIDTH:2 * LRU_WIDTH]
    g_u = z[:, 2 * LRU_WIDTH:2 * LRU_WIDTH + GMLP_WIDTH]
    g_v = z[:, 2 * LRU_WIDTH + GMLP_WIDTH:]

    r_pre = jnp.concatenate([g[:, :GATE_TILE] for g in gates], axis=1)
    i_pre = jnp.concatenate([g[:, GATE_TILE:] for g in gates], axis=1)
    bgate = bgate_ref[...]
    r_gate = jax.nn.sigmoid(r_pre + bgate[:, :LRU_WIDTH])
    i_gate = jax.nn.sigmoid(i_pre + bgate[:, LRU_WIDTH:])
    neg_a = -lrua_ref[...]
    softplus = jnp.maximum(neg_a, 0.0) + jnp.log1p(jnp.exp(-jnp.abs(neg_a)))
    log_a = r_gate * ((-LRU_C) * softplus)
    a = jnp.exp(log_a)
    th = jnp.tanh(log_a)
    q = (-2.0 * th) / (1.0 - th)
    mult = jnp.where(q > 0.0, q * lax.rsqrt(q), 0.0)
    bx = (mult * xc) * i_gate
    hs = _linear_scan(slabs(a), slabs(bx), state_ref, a_scr, b_scr, h_scr)
    y_lru = 0.5 * _gelu2_times(lru_gate, jnp.concatenate(hs, axis=1))
    yl = _rms(y_lru) * glru_ref[...]

    gv = _gelu(g_v)
    mu = jnp.mean(gv, axis=-1, keepdims=True)
    cen = gv - mu
    var = jnp.mean(cen * cen, axis=-1, keepdims=True)
    v = cen * lax.rsqrt(var + EPS) * vng_ref[...] + vnb_ref[...]
    vb = v.astype(BF16)
    nb = t // GMLP_BLOCK
    pi = lax.broadcasted_iota(jnp.int32, (GMLP_BLOCK, GMLP_BLOCK), 0) // CHUNK
    pj = lax.broadcasted_iota(jnp.int32, (GMLP_BLOCK, GMLP_BLOCK), 1) // CHUNK
    mask = pj <= pi
    sp_groups = []
    for g in range(GMLP_GROUPS):
        ws = jnp.where(mask, 0.5 * wsp_ref[g], 0.0).astype(BF16)
        cols = slice(g * GMLP_GROUP_DIM, (g + 1) * GMLP_GROUP_DIM)
        rhs = jnp.concatenate(
            [vb[n * GMLP_BLOCK:(n + 1) * GMLP_BLOCK, cols] for n in range(nb)], axis=1)
        sp_groups.append(_dot(ws, rhs))
    bsp = 0.5 * bsp_ref[...]
    sp_rows = []
    for n in range(nb):
        blk = jnp.concatenate(
            [sg[:, n * GMLP_GROUP_DIM:(n + 1) * GMLP_GROUP_DIM] for sg in sp_groups], axis=1)
        sp_rows.append(blk + bsp)
    sp_half = jnp.concatenate(sp_rows, axis=0)
    y_gmlp = _gelu2_times(g_u, sp_half)
    yg = _rms(y_gmlp) * ggmlp_ref[...]

    return jnp.concatenate([yl, yg], axis=1).astype(BF16)


def _mixer(x, mod3, g_pre, g_post, w_in, conv_w, conv_b, w_gate, b_gate, lru_a,
           v_norm_g, v_norm_b, w_spatial, b_sp_full, g_lru_out, g_gmlp_out, w_out):
    bsz, s, d = x.shape
    t = MIX_T
    nb = MIX_NB
    assert bsz % nb == 0 and s % t == 0
    n_lg = LRU_WIDTH // LANES
    scan_rows = SUBLANES * (t // SUBLANES + 1)
    full = lambda a: pl.BlockSpec(a.shape, lambda b, i: (0,) * a.ndim)
    in_arrays = [g_pre, g_post, w_in, conv_w, conv_b, w_gate, b_gate, lru_a,
                 v_norm_g, v_norm_b, w_spatial, b_sp_full, g_lru_out, g_gmlp_out, w_out]
    return pl.pallas_call(
        _mixer_kernel,
        grid=(bsz // nb, s // t),
        in_specs=[
            pl.BlockSpec((nb, t, d), lambda b, i: (b, i, 0)),
            pl.BlockSpec((nb, 1, mod3.shape[2]), lambda b, i: (b, 0, 0)),
        ] + [full(a) for a in in_arrays],
        out_specs=pl.BlockSpec((nb, t, d), lambda b, i: (b, i, 0)),
        out_shape=jax.ShapeDtypeStruct(x.shape, x.dtype),
        scratch_shapes=(
            [pltpu.VMEM((n_lg, SUBLANES + t, LANES), F32)] * nb
            + [pltpu.VMEM((n_lg, scan_rows, LANES), F32)] * (3 * nb)
            + [pltpu.VMEM((nb, SUBLANES, LRU_WIDTH), F32)] * 2),
        compiler_params=pltpu.CompilerParams(
            dimension_semantics=("arbitrary", "arbitrary"), vmem_limit_bytes=VMEM_LIMIT),
        name="token_mixer",
    )(x, mod3, *in_arrays)


def _ffn_kernel(x_ref, mod_ref, gpre_ref, gpost_ref, wup_ref, cw_ref, cb_ref, wd_ref,
                o_ref, scr_ref, tail_ref):
    t_idx = pl.program_id(1)
    t, d = x_ref.shape
    d_ff = 2 * wd_ref.shape[0]
    fc = FFN_FC
    lg = fc // LANES

    @pl.when(t_idx == 0)
    def _():
        tail_ref[...] = jnp.zeros_like(tail_ref)

    mod = mod_ref[...]
    sh_f = mod[:, 3 * d:4 * d]
    sc_f = mod[:, 4 * d:5 * d]
    gt_f = mod[:, 5 * d:6 * d]
    x = x_ref[...]
    hb = (_rms(x) * (gpre_ref[...] * (1.0 + sc_f)) + sh_f).astype(BF16)

    n_chunks = d_ff // fc
    halves = ((0, 1.0), (d_ff, 0.5))
    chunk_cols = lambda base, j: slice(base + j * fc, base + (j + 1) * fc)
    up_proj = lambda j: [_dot(hb, _as_bf16(wup_ref[:, chunk_cols(base, j)])) for base, _ in halves]

    def gated(j, ups):
        conv = []
        for k, (base, scale) in enumerate(halves):
            cols = chunk_cols(base, j)
            tail = tail_ref[:, cols]
            tail_ref[:, cols] = ups[k][t - SUBLANES:]
            slot = (2 * j + k) % FFN_SLOTS
            conv.append(_conv_via_scratch(scr_ref, slot * lg, tail, ups[k],
                                          cw_ref[:, cols] * scale, cb_ref[:, cols] * scale))
        return jnp.concatenate(
            [_gelu2_times(cg, cv) for cg, cv in zip(*conv)], axis=1).astype(BF16)

    acc = None
    ups = up_proj(0)
    for j in range(n_chunks):
        ups_next = up_proj(j + 1) if j + 1 < n_chunks else None
        part = _dot(gated(j, ups), _as_bf16(wd_ref[j * fc // 2:(j + 1) * fc // 2, :]))
        acc = part if acc is None else acc + part
        ups = ups_next
    o_ref[...] = x + (gt_f * gpost_ref[...]) * _rms(acc)


def _ffn(x, mod3, g_pre, g_post, w_up, conv_w, conv_b, w_down):
    bsz, s, d = x.shape
    t = FFN_T
    resident = lambda a: pl.BlockSpec(a.shape, lambda b, i: (0,) * a.ndim,
                                      pipeline_mode=pl.Buffered(1))
    return pl.pallas_call(
        _ffn_kernel,
        grid=(bsz, s // t),
        in_specs=[
            pl.BlockSpec((None, t, d), lambda b, i: (b, i, 0)),
            pl.BlockSpec((None, 1, mod3.shape[2]), lambda b, i: (b, 0, 0)),
            resident(g_pre), resident(g_post), resident(w_up), resident(conv_w),
            resident(conv_b), resident(w_down),
        ],
        out_specs=pl.BlockSpec((None, t, d), lambda b, i: (b, i, 0)),
        out_shape=jax.ShapeDtypeStruct(x.shape, x.dtype),
        scratch_shapes=[
            pltpu.VMEM((FFN_SLOTS * (FFN_FC // LANES), SUBLANES + t, LANES), F32),
            pltpu.VMEM((SUBLANES, w_up.shape[1]), F32),
        ],
        compiler_params=pltpu.CompilerParams(
            dimension_semantics=("arbitrary", "arbitrary"), vmem_limit_bytes=VMEM_LIMIT),
        name="conv_ffn",
    )(x, mod3, g_pre, g_post, w_up, conv_w, conv_b, w_down)


def _gate_weights(w_rgate, w_igate):
    heads_per_tile = GATE_TILE // LRU_HEAD_DIM
    tiles = []
    for j in range(LRU_WIDTH // GATE_TILE):
        hs = slice(j * heads_per_tile, (j + 1) * heads_per_tile)
        r_bd = jax.scipy.linalg.block_diag(*w_rgate[hs])
        i_bd = jax.scipy.linalg.block_diag(*w_igate[hs])
        tiles.append(jnp.concatenate([r_bd, i_bd], axis=1))
    return _pack_rows_xla(jnp.stack(tiles))


def kernel(x, c, w_ada, b_ada, g_mix_pre, g_mix_post, w_in, conv_w, conv_b, w_rgate, b_rgate, w_igate, b_igate, lru_a, v_norm_g, v_norm_b, w_spatial, b_spatial, g_lru_out, g_gmlp_out, w_out, g_ffn_pre, g_ffn_post, w_up, ffn_conv_w, ffn_conv_b, w_down):
    depth = w_ada.shape[0]
    bsz, s, d = x.shape
    c_pad = jnp.pad(c, ((0, SUBLANES - bsz % SUBLANES if bsz % SUBLANES else 0), (0, 0)))
    for l in range(depth):
        mod = _modulation(c_pad, w_ada[l], b_ada[l][None, :])
        mod3 = mod.reshape(mod.shape[0], 1, mod.shape[1])
        row = lambda a: a[l][None, :]
        w_gate = _gate_weights(w_rgate[l], w_igate[l])
        b_gate = jnp.concatenate([b_rgate[l].reshape(1, -1), b_igate[l].reshape(1, -1)], axis=1)
        b_sp_full = jnp.repeat(b_spatial[l].T, GMLP_GROUP_DIM, axis=1)
        x = _mixer(x, mod3, row(g_mix_pre), row(g_mix_post), _pack_rows(w_in[l]),
                   conv_w[l], row(conv_b), w_gate, b_gate, row(lru_a),
                   row(v_norm_g), row(v_norm_b), w_spatial[l], b_sp_full,
                   row(g_lru_out), row(g_gmlp_out), _pack_rows(w_out[l]))
        x = _ffn(x, mod3, row(g_ffn_pre), row(g_ffn_post), _pack_rows(w_up[l]),
                 ffn_conv_w[l], row(ffn_conv_b), _pack_rows(w_down[l]))
    return x
```

```python
import jax
import jax.numpy as jnp
from jax import lax
from jax.experimental import pallas as pl
from jax.experimental.pallas import tpu as pltpu

CHUNK = 64
LRU_WIDTH = 512
LRU_HEADS = 8
LRU_HEAD_DIM = LRU_WIDTH // LRU_HEADS
LRU_CONV_WIDTH = 4
LRU_C = 8.0
GMLP_WIDTH = 512
GMLP_GROUPS = 4
GMLP_GROUP_DIM = GMLP_WIDTH // GMLP_GROUPS
GMLP_BLOCK = 128
FFN_CONV_WIDTH = 3
N_MOD = 6
EPS = 1e-6

SUBLANES = 8
LANES = 128
GATE_TILE = 256

MIX_T = 256
MIX_NB = 4
FFN_T = 512
FFN_FC = 512
FFN_SLOTS = 4
PACK_BLOCK_BYTES = 2 * 1024 * 1024
VMEM_LIMIT = 56 * 1024 * 1024

F32 = jnp.float32
BF16 = jnp.bfloat16
U32 = jnp.uint32


def _dot(a, b):
    return jnp.dot(a, b, preferred_element_type=F32)


def _pack_rows_xla(w):
    k, n = w.shape[-2:]
    wb = w.astype(BF16).reshape(*w.shape[:-2], k // 2, 2, n)
    return lax.bitcast_convert_type(jnp.swapaxes(wb, -1, -2), U32)


def _pack_kernel(w_ref, o_ref):
    o_ref[...] = pltpu.bitcast(w_ref[...].astype(BF16), U32)


def _pack_rows(w):
    k, n = w.shape
    bk = k
    while bk * n * 4 > PACK_BLOCK_BYTES and bk % (4 * SUBLANES) == 0:
        bk //= 2
    assert k % bk == 0
    return pl.pallas_call(
        _pack_kernel,
        grid=(k // bk,),
        in_specs=[pl.BlockSpec((bk, n), lambda i: (i, 0))],
        out_specs=pl.BlockSpec((bk // 2, n), lambda i: (i, 0)),
        out_shape=jax.ShapeDtypeStruct((k // 2, n), U32),
        compiler_params=pltpu.CompilerParams(
            dimension_semantics=("arbitrary",), vmem_limit_bytes=VMEM_LIMIT),
        name="pack_weight",
    )(w)


def _as_bf16(packed):
    return pltpu.bitcast(packed, BF16)


def _zero_row_from(x):
    last = lax.bitcast_convert_type(x[x.shape[0] - SUBLANES:], U32)
    zero = lax.shift_right_logical(lax.shift_right_logical(last, U32(16)), U32(16))
    return zero[0:1].astype(F32)


def _rms(x):
    return x * lax.rsqrt(jnp.mean(x * x, axis=-1, keepdims=True) + EPS)


def _gelu(x):
    c0 = 0.7978845608028654
    hx = 0.5 * x
    return hx + hx * jnp.tanh(x * (c0 + (c0 * 0.044715) * (x * x)))


def _gelu2_times(x, v):
    c0 = 0.7978845608028654
    z = x * (c0 + (c0 * 0.044715) * (x * x))
    return (x * v) * (1.0 + jnp.tanh(z))


def _rows(ref, lead, start, size):
    return ref[pl.ds(lead, 1, stride=2), pl.ds(start, size), :][0]


def _put_rows(ref, lead, start, val):
    ref[pl.ds(lead, 1, stride=2), pl.ds(start, val.shape[0]), :] = val[None]


def _linear_scan(a, b, state_ref, a_scr, b_scr, h_scr, h0_add=None):
    t = a[0].shape[0]
    seg = t // SUBLANES
    pitch = seg + 1 - seg % 2
    n = len(a)
    for c in range(n):
        for s in range(SUBLANES):
            _put_rows(a_scr, c, s * pitch, a[c][s * seg:(s + 1) * seg])
            _put_rows(b_scr, c, s * pitch, b[c][s * seg:(s + 1) * seg])
    step = lambda ref, c, j: ref[c, pl.ds(j, SUBLANES, stride=pitch), :]
    row = lax.broadcasted_iota(jnp.int32, (SUBLANES, LANES), 0)
    shifted = lambda v, sh, fill: jnp.where(row >= sh, pltpu.roll(v, sh, axis=0), fill)

    prod = [None] * n
    end = [None] * n
    for j in range(seg):
        for c in range(n):
            aj, bj = step(a_scr, c, j), step(b_scr, c, j)
            prod[c] = aj if j == 0 else aj * prod[c]
            end[c] = bj if j == 0 else aj * end[c] + bj

    h = []
    for c in range(n):
        cols = slice(c * LANES, (c + 1) * LANES)
        h0 = state_ref[SUBLANES - 1:SUBLANES, cols]
        if h0_add is not None:
            h0 = h0 + h0_add[:, cols]
        p = prod[c]
        e = end[c] + jnp.where(row == 0, p * h0, 0.0)
        for sh in (1, 2, 4):
            e = e + p * shifted(e, sh, 0.0)
            if sh < 4:
                p = p * shifted(p, sh, 1.0)
        state_ref[:, cols] = e
        h.append(shifted(e, 1, h0))
    for j in range(seg):
        for c in range(n):
            h[c] = step(a_scr, c, j) * h[c] + step(b_scr, c, j)
            h_scr[c, pl.ds(j, SUBLANES, stride=pitch), :] = h[c]
    return [jnp.concatenate([_rows(h_scr, c, s * pitch, seg) for s in range(SUBLANES)], axis=0)
            for c in range(n)]


def _conv_via_scratch(scr_ref, lead0, tail, x, w, b):
    k_width = w.shape[0]
    t, c = x.shape
    outs = []
    for g in range(c // LANES):
        cols = slice(g * LANES, (g + 1) * LANES)
        scr_ref[lead0 + g, 0:SUBLANES, :] = tail[:, cols]
        scr_ref[lead0 + g, SUBLANES:SUBLANES + t, :] = x[:, cols]
    for g in range(c // LANES):
        cols = slice(g * LANES, (g + 1) * LANES)
        out = x[:, cols] * w[k_width - 1:k_width, cols] + b[:, cols]
        for k in range(k_width - 1):
            shifted = _rows(scr_ref, lead0 + g, SUBLANES - (k_width - 1 - k), t)
            out = out + shifted * w[k:k + 1, cols]
        outs.append(out)
    return outs


def _mod_kernel(c_ref, w_ref, b_ref, o_ref):
    c = c_ref[...]
    c_act = c * jax.nn.sigmoid(c)
    o_ref[...] = _dot(c_act.astype(BF16), w_ref[...].astype(BF16)) + b_ref[...]


def _modulation(c_pad, w_ada, b_ada):
    rows, d = c_pad.shape
    n = w_ada.shape[1]
    bn = d
    return pl.pallas_call(
        _mod_kernel,
        grid=(n // bn,),
        in_specs=[
            pl.BlockSpec((rows, d), lambda j: (0, 0)),
            pl.BlockSpec((d, bn), lambda j: (0, j)),
            pl.BlockSpec((1, bn), lambda j: (0, j)),
        ],
        out_specs=pl.BlockSpec((rows, bn), lambda j: (0, j)),
        out_shape=jax.ShapeDtypeStruct((rows, n), F32),
        compiler_params=pltpu.CompilerParams(
            dimension_semantics=("arbitrary",), vmem_limit_bytes=VMEM_LIMIT),
        name="adaln_mod",
    )(c_pad, w_ada, b_ada)


def _mixer_kernel(x_ref, mod_ref, gpre_ref, gpost_ref, win_ref, cw_ref, cb_ref,
                  wgate_ref, bgate_ref, lrua_ref, vng_ref, vnb_ref, wsp_ref, bsp_ref,
                  glru_ref, ggmlp_ref, wout_ref, o_ref, *scratch):
    nb_tiles = x_ref.shape[0]
    tail_ref, state_ref = scratch[4 * nb_tiles:]

    @pl.when(pl.program_id(1) == 0)
    def _():
        tail_ref[...] = jnp.zeros_like(tail_ref)
        state_ref[...] = jnp.zeros_like(state_ref)

    project = lambda k: _mix_in(x_ref.at[k], mod_ref.at[k], gpre_ref, win_ref)
    z = project(0)
    for k in range(nb_tiles):
        conv_scr, a_scr, b_scr, h_scr = scratch[k:4 * nb_tiles:nb_tiles]
        xc, gates = _mix_gates(z[0], cw_ref, cb_ref, wgate_ref, conv_scr, tail_ref.at[k])
        sp_groups = _mix_positions(z[3], vng_ref, vnb_ref, wsp_ref)
        z_next = project(k + 1) if k + 1 < nb_tiles else None
        hooks = [_zero_row_from(part) for part in z_next] if z_next else [None] * len(z)
        y = _mix_body(z, xc, gates, sp_groups, hooks, bgate_ref, lrua_ref, bsp_ref,
                      glru_ref, ggmlp_ref, a_scr, b_scr, h_scr, state_ref.at[k])
        _mix_out(y, x_ref.at[k], mod_ref.at[k], gpost_ref, wout_ref, o_ref.at[k])
        z = z_next


def _mix_in(x_ref, mod_ref, gpre_ref, win_ref):
    d = x_ref.shape[1]
    mod = mod_ref[...]
    sh_m = mod[:, 0:d]
    sc_m = mod[:, d:2 * d]
    hb = (_rms(x_ref[...]) * (gpre_ref[...] * (1.0 + sc_m)) + sh_m).astype(BF16)
    widths = (LRU_WIDTH, LRU_WIDTH, GMLP_WIDTH, GMLP_WIDTH)
    starts = [sum(widths[:q]) for q in range(len(widths))]
    return [_dot(hb, _as_bf16(win_ref[:, c0:c0 + w])) for c0, w in zip(starts, widths)]


def _mix_gates(lru_x, cw_ref, cb_ref, wgate_ref, conv_scr, tail_ref):
    t = lru_x.shape[0]
    tail = tail_ref[...]
    tail_ref[...] = lru_x[t - SUBLANES:]
    xc = jnp.concatenate(
        _conv_via_scratch(conv_scr, 0, tail, lru_x, cw_ref[...], cb_ref[...]), axis=1)
    xcb = xc.astype(BF16)
    gates = [_dot(xcb[:, j * GATE_TILE:(j + 1) * GATE_TILE], _as_bf16(wgate_ref[j]))
             for j in range(LRU_WIDTH // GATE_TILE)]
    return xc, gates


def _mix_out(y, x_ref, mod_ref, gpost_ref, wout_ref, o_ref):
    d = x_ref.shape[1]
    gt_m = mod_ref[...][:, 2 * d:3 * d]
    y = _dot(y, _as_bf16(wout_ref[...]))
    o_ref[...] = x_ref[...] + (gt_m * gpost_ref[...]) * _rms(y)


def _mix_positions(g_v, vng_ref, vnb_ref, wsp_ref):
    t = g_v.shape[0]
    gv = _gelu(g_v)
    mu = jnp.mean(gv, axis=-1, keepdims=True)
    cen = gv - mu
    var = jnp.mean(cen * cen, axis=-1, keepdims=True)
    v = cen * lax.rsqrt(var + EPS) * vng_ref[...] + vnb_ref[...]
    vb = v.astype(BF16)
    nb = t // GMLP_BLOCK
    pi = lax.broadcasted_iota(jnp.int32, (GMLP_BLOCK, GMLP_BLOCK), 0) // CHUNK
    pj = lax.broadcasted_iota(jnp.int32, (GMLP_BLOCK, GMLP_BLOCK), 1) // CHUNK
    mask = pj <= pi
    sp_groups = []
    for g in range(GMLP_GROUPS):
        ws = jnp.where(mask, 0.5 * wsp_ref[g], 0.0).astype(BF16)
        cols = slice(g * GMLP_GROUP_DIM, (g + 1) * GMLP_GROUP_DIM)
        rhs = jnp.concatenate(
            [vb[n * GMLP_BLOCK:(n + 1) * GMLP_BLOCK, cols] for n in range(nb)], axis=1)
        sp_groups.append(_dot(ws, rhs))
    return sp_groups


def _mix_body(z, xc, gates, sp_groups, hooks, bgate_ref, lrua_ref, bsp_ref,
              glru_ref, ggmlp_ref, a_scr, b_scr, h_scr, state_ref):
    _, lru_gate, g_u, _ = z
    t = lru_gate.shape[0]
    slabs = lambda v: [v[:, c * LANES:(c + 1) * LANES] for c in range(v.shape[1] // LANES)]
    hooked = lambda v, hook: v if hook is None else v + hook

    r_pre = jnp.concatenate([g[:, :GATE_TILE] for g in gates], axis=1)
    i_pre = jnp.concatenate([g[:, GATE_TILE:] for g in gates], axis=1)
    bgate = bgate_ref[...]
    r_gate = jax.nn.sigmoid(r_pre + bgate[:, :LRU_WIDTH])
    i_gate = jax.nn.sigmoid(i_pre + bgate[:, LRU_WIDTH:])
    neg_a = -lrua_ref[...]
    softplus = jnp.maximum(neg_a, 0.0) + jnp.log1p(jnp.exp(-jnp.abs(neg_a)))
    log_a = r_gate * ((-LRU_C) * softplus)
    a = jnp.exp(log_a)
    th = jnp.tanh(log_a)
    q = (-2.0 * th) / (1.0 - th)
    mult = jnp.where(q > 0.0, q * lax.rsqrt(q), 0.0)
    bx = (mult * xc) * i_gate
    hs = _linear_scan(slabs(a), slabs(bx), state_ref, a_scr, b_scr, h_scr, h0_add=hooks[0])
    y_lru = 0.5 * _gelu2_times(lru_gate, jnp.concatenate(hs, axis=1))
    yl = _rms(y_lru) * hooked(glru_ref[...], hooks[1])

    nb = t // GMLP_BLOCK
    bsp = hooked(0.5 * bsp_ref[...], hooks[2])
    sp_rows = []
    for n in range(nb):
        blk = jnp.concatenate(
            [sg[:, n * GMLP_GROUP_DIM:(n + 1) * GMLP_GROUP_DIM] for sg in sp_groups], axis=1)
        sp_rows.append(blk + bsp)
    sp_half = jnp.concatenate(sp_rows, axis=0)
    y_gmlp = _gelu2_times(g_u, sp_half)
    yg = _rms(y_gmlp) * hooked(ggmlp_ref[...], hooks[3])

    return jnp.concatenate([yl, yg], axis=1).astype(BF16)


def _mixer(x, mod3, g_pre, g_post, w_in, conv_w, conv_b, w_gate, b_gate, lru_a,
           v_norm_g, v_norm_b, w_spatial, b_sp_full, g_lru_out, g_gmlp_out, w_out):
    bsz, s, d = x.shape
    t = MIX_T
    nb = MIX_NB
    assert bsz % nb == 0 and s % t == 0
    n_lg = LRU_WIDTH // LANES
    scan_rows = SUBLANES * (t // SUBLANES + 1)
    full = lambda a: pl.BlockSpec(a.shape, lambda b, i: (0,) * a.ndim)
    in_arrays = [g_pre, g_post, w_in, conv_w, conv_b, w_gate, b_gate, lru_a,
                 v_norm_g, v_norm_b, w_spatial, b_sp_full, g_lru_out, g_gmlp_out, w_out]
    return pl.pallas_call(
        _mixer_kernel,
        grid=(bsz // nb, s // t),
        in_specs=[
            pl.BlockSpec((nb, t, d), lambda b, i: (b, i, 0)),
            pl.BlockSpec((nb, 1, mod3.shape[2]), lambda b, i: (b, 0, 0)),
        ] + [full(a) for a in in_arrays],
        out_specs=pl.BlockSpec((nb, t, d), lambda b, i: (b, i, 0)),
        out_shape=jax.ShapeDtypeStruct(x.shape, x.dtype),
        scratch_shapes=(
            [pltpu.VMEM((n_lg, SUBLANES + t, LANES), F32)] * nb
            + [pltpu.VMEM((n_lg, scan_rows, LANES), F32)] * (3 * nb)
            + [pltpu.VMEM((nb, SUBLANES, LRU_WIDTH), F32)] * 2),
        compiler_params=pltpu.CompilerParams(
            dimension_semantics=("arbitrary", "arbitrary"), vmem_limit_bytes=VMEM_LIMIT),
        name="token_mixer",
    )(x, mod3, *in_arrays)


def _ffn_kernel(x_ref, mod_ref, gpre_ref, gpost_ref, wup_ref, cw_ref, cb_ref, wd_ref,
                o_ref, scr_ref, tail_ref):
    t_idx = pl.program_id(1)
    t, d = x_ref.shape
    d_ff = 2 * wd_ref.shape[0]
    fc = FFN_FC
    lg = fc // LANES

    @pl.when(t_idx == 0)
    def _():
        tail_ref[...] = jnp.zeros_like(tail_ref)

    mod = mod_ref[...]
    sh_f = mod[:, 3 * d:4 * d]
    sc_f = mod[:, 4 * d:5 * d]
    gt_f = mod[:, 5 * d:6 * d]
    x = x_ref[...]
    hb = (_rms(x) * (gpre_ref[...] * (1.0 + sc_f)) + sh_f).astype(BF16)

    n_chunks = d_ff // fc
    halves = ((0, 1.0), (d_ff, 0.5))
    chunk_cols = lambda base, j: slice(base + j * fc, base + (j + 1) * fc)
    up_proj = lambda j: [_dot(hb, _as_bf16(wup_ref[:, chunk_cols(base, j)])) for base, _ in halves]

    def gated(j, ups):
        conv = []
        for k, (base, scale) in enumerate(halves):
            cols = chunk_cols(base, j)
            tail = tail_ref[:, cols]
            tail_ref[:, cols] = ups[k][t - SUBLANES:]
            slot = (2 * j + k) % FFN_SLOTS
            conv.append(_conv_via_scratch(scr_ref, slot * lg, tail, ups[k],
                                          cw_ref[:, cols] * scale, cb_ref[:, cols] * scale))
        return jnp.concatenate(
            [_gelu2_times(cg, cv) for cg, cv in zip(*conv)], axis=1).astype(BF16)

    acc = None
    ups = up_proj(0)
    for j in range(n_chunks):
        ups_next = up_proj(j + 1) if j + 1 < n_chunks else None
        part = _dot(gated(j, ups), _as_bf16(wd_ref[j * fc // 2:(j + 1) * fc // 2, :]))
        acc = part if acc is None else acc + part
        ups = ups_next
    o_ref[...] = x + (gt_f * gpost_ref[...]) * _rms(acc)


def _ffn(x, mod3, g_pre, g_post, w_up, conv_w, conv_b, w_down):
    bsz, s, d = x.shape
    t = FFN_T
    resident = lambda a: pl.BlockSpec(a.shape, lambda b, i: (0,) * a.ndim,
                                      pipeline_mode=pl.Buffered(1))
    return pl.pallas_call(
        _ffn_kernel,
        grid=(bsz, s // t),
        in_specs=[
            pl.BlockSpec((None, t, d), lambda b, i: (b, i, 0)),
            pl.BlockSpec((None, 1, mod3.shape[2]), lambda b, i: (b, 0, 0)),
            resident(g_pre), resident(g_post), resident(w_up), resident(conv_w),
            resident(conv_b), resident(w_down),
        ],
        out_specs=pl.BlockSpec((None, t, d), lambda b, i: (b, i, 0)),
        out_shape=jax.ShapeDtypeStruct(x.shape, x.dtype),
        scratch_shapes=[
            pltpu.VMEM((FFN_SLOTS * (FFN_FC // LANES), SUBLANES + t, LANES), F32),
            pltpu.VMEM((SUBLANES, w_up.shape[1]), F32),
        ],
        compiler_params=pltpu.CompilerParams(
            dimension_semantics=("arbitrary", "arbitrary"), vmem_limit_bytes=VMEM_LIMIT),
        name="conv_ffn",
    )(x, mod3, g_pre, g_post, w_up, conv_w, conv_b, w_down)


def _gate_weights(w_rgate, w_igate):
    heads_per_tile = GATE_TILE // LRU_HEAD_DIM
    tiles = []
    for j in range(LRU_WIDTH // GATE_TILE):
        hs = slice(j * heads_per_tile, (j + 1) * heads_per_tile)
        r_bd = jax.scipy.linalg.block_diag(*w_rgate[hs])
        i_bd = jax.scipy.linalg.block_diag(*w_igate[hs])
        tiles.append(jnp.concatenate([r_bd, i_bd], axis=1))
    return _pack_rows_xla(jnp.stack(tiles))


def kernel(x, c, w_ada, b_ada, g_mix_pre, g_mix_post, w_in, conv_w, conv_b, w_rgate, b_rgate, w_igate, b_igate, lru_a, v_norm_g, v_norm_b, w_spatial, b_spatial, g_lru_out, g_gmlp_out, w_out, g_ffn_pre, g_ffn_post, w_up, ffn_conv_w, ffn_conv_b, w_down):
    depth = w_ada.shape[0]
    bsz, s, d = x.shape
    c_pad = jnp.pad(c, ((0, SUBLANES - bsz % SUBLANES if bsz % SUBLANES else 0), (0, 0)))
    for l in range(depth):
        mod = _modulation(c_pad, w_ada[l], b_ada[l][None, :])
        mod3 = mod.reshape(mod.shape[0], 1, mod.shape[1])
        row = lambda a: a[l][None, :]
        w_gate = _gate_weights(w_rgate[l], w_igate[l])
        b_gate = jnp.concatenate([b_rgate[l].reshape(1, -1), b_igate[l].reshape(1, -1)], axis=1)
        b_sp_full = jnp.repeat(b_spatial[l].T, GMLP_GROUP_DIM, axis=1)
        x = _mixer(x, mod3, row(g_mix_pre), row(g_mix_post), _pack_rows(w_in[l]),
                   conv_w[l], row(conv_b), w_gate, b_gate, row(lru_a),
                   row(v_norm_g), row(v_norm_b), w_spatial[l], b_sp_full,
                   row(g_lru_out), row(g_gmlp_out), _pack_rows(w_out[l]))
        x = _ffn(x, mod3, row(g_ffn_pre), row(g_ffn_post), _pack_rows(w_up[l]),
                 ffn_conv_w[l], row(ffn_conv_b), _pack_rows(w_down[l]))
    return x
```

```python
import jax
import jax.numpy as jnp
from jax import lax
from jax.experimental import pallas as pl
from jax.experimental.pallas import tpu as pltpu

CHUNK = 64
LRU_WIDTH = 512
LRU_HEADS = 8
LRU_HEAD_DIM = LRU_WIDTH // LRU_HEADS
LRU_CONV_WIDTH = 4
LRU_C = 8.0
GMLP_WIDTH = 512
GMLP_GROUPS = 4
GMLP_GROUP_DIM = GMLP_WIDTH // GMLP_GROUPS
GMLP_BLOCK = 128
FFN_CONV_WIDTH = 3
N_MOD = 6
EPS = 1e-6

SUBLANES = 8
LANES = 128
GATE_TILE = 256

MIX_T = 256
MIX_NB = 4
FFN_T = 512
FFN_FC = 512
FFN_SLOTS = 4
PACK_BLOCK_BYTES = 4 * 1024 * 1024
VMEM_LIMIT = 56 * 1024 * 1024

F32 = jnp.float32
BF16 = jnp.bfloat16
U32 = jnp.uint32


def _dot(a, b):
    return jnp.dot(a, b, preferred_element_type=F32)


def _pack_rows_xla(w):
    k, n = w.shape[-2:]
    wb = w.astype(BF16).reshape(*w.shape[:-2], k // 2, 2, n)
    return lax.bitcast_convert_type(jnp.swapaxes(wb, -1, -2), U32)


def _pack_kernel(w_ref, o_ref):
    o_ref[...] = pltpu.bitcast(w_ref[...].astype(BF16), U32)


def _pack_rows(w):
    k, n = w.shape
    bk = k
    while bk * n * 4 > PACK_BLOCK_BYTES and bk % (4 * SUBLANES) == 0:
        bk //= 2
    assert k % bk == 0
    return pl.pallas_call(
        _pack_kernel,
        grid=(k // bk,),
        in_specs=[pl.BlockSpec((bk, n), lambda i: (i, 0))],
        out_specs=pl.BlockSpec((bk // 2, n), lambda i: (i, 0)),
        out_shape=jax.ShapeDtypeStruct((k // 2, n), U32),
        compiler_params=pltpu.CompilerParams(
            dimension_semantics=("arbitrary",), vmem_limit_bytes=VMEM_LIMIT),
        name="pack_weight",
    )(w)


def _as_bf16(packed):
    return pltpu.bitcast(packed, BF16)


def _zero_row_from(x):
    last = lax.bitcast_convert_type(x[x.shape[0] - SUBLANES:], U32)
    zero = lax.shift_right_logical(lax.shift_right_logical(last, U32(16)), U32(16))
    return zero[0:1].astype(F32)


def _rms(x):
    return x * lax.rsqrt(jnp.mean(x * x, axis=-1, keepdims=True) + EPS)


def _gelu(x):
    c0 = 0.7978845608028654
    hx = 0.5 * x
    return hx + hx * jnp.tanh(x * (c0 + (c0 * 0.044715) * (x * x)))


def _gelu2_times(x, v):
    c0 = 0.7978845608028654
    z = x * (c0 + (c0 * 0.044715) * (x * x))
    return (x * v) * (1.0 + jnp.tanh(z))


def _rows(ref, lead, start, size):
    return ref[pl.ds(lead, 1, stride=2), pl.ds(start, size), :][0]


def _put_rows(ref, lead, start, val):
    ref[pl.ds(lead, 1, stride=2), pl.ds(start, val.shape[0]), :] = val[None]


def _linear_scan(a, b, state_ref, a_scr, b_scr, h_scr, h0_add=None):
    t = a[0].shape[0]
    seg = t // SUBLANES
    pitch = seg + 1 - seg % 2
    n = len(a)
    for c in range(n):
        for s in range(SUBLANES):
            _put_rows(a_scr, c, s * pitch, a[c][s * seg:(s + 1) * seg])
            _put_rows(b_scr, c, s * pitch, b[c][s * seg:(s + 1) * seg])
    step = lambda ref, c, j: ref[c, pl.ds(j, SUBLANES, stride=pitch), :]
    row = lax.broadcasted_iota(jnp.int32, (SUBLANES, LANES), 0)
    shifted = lambda v, sh, fill: jnp.where(row >= sh, pltpu.roll(v, sh, axis=0), fill)

    prod = [None] * n
    end = [None] * n
    for j in range(seg):
        for c in range(n):
            aj, bj = step(a_scr, c, j), step(b_scr, c, j)
            prod[c] = aj if j == 0 else aj * prod[c]
            end[c] = bj if j == 0 else aj * end[c] + bj

    h = []
    for c in range(n):
        cols = slice(c * LANES, (c + 1) * LANES)
        h0 = state_ref[SUBLANES - 1:SUBLANES, cols]
        if h0_add is not None:
            h0 = h0 + h0_add[:, cols]
        p = prod[c]
        e = end[c] + jnp.where(row == 0, p * h0, 0.0)
        for sh in (1, 2, 4):
            e = e + p * shifted(e, sh, 0.0)
            if sh < 4:
                p = p * shifted(p, sh, 1.0)
        state_ref[:, cols] = e
        h.append(shifted(e, 1, h0))
    for j in range(seg):
        for c in range(n):
            h[c] = step(a_scr, c, j) * h[c] + step(b_scr, c, j)
            h_scr[c, pl.ds(j, SUBLANES, stride=pitch), :] = h[c]
    return [jnp.concatenate([_rows(h_scr, c, s * pitch, seg) for s in range(SUBLANES)], axis=0)
            for c in range(n)]


def _conv_via_scratch(scr_ref, lead0, tail, x, w, b):
    k_width = w.shape[0]
    t, c = x.shape
    outs = []
    for g in range(c // LANES):
        cols = slice(g * LANES, (g + 1) * LANES)
        scr_ref[lead0 + g, 0:SUBLANES, :] = tail[:, cols]
        scr_ref[lead0 + g, SUBLANES:SUBLANES + t, :] = x[:, cols]
    for g in range(c // LANES):
        cols = slice(g * LANES, (g + 1) * LANES)
        out = x[:, cols] * w[k_width - 1:k_width, cols] + b[:, cols]
        for k in range(k_width - 1):
            shifted = _rows(scr_ref, lead0 + g, SUBLANES - (k_width - 1 - k), t)
            out = out + shifted * w[k:k + 1, cols]
        outs.append(out)
    return outs


def _mod_kernel(c_ref, w_ref, b_ref, o_ref):
    c = c_ref[...]
    c_act = c * jax.nn.sigmoid(c)
    o_ref[...] = _dot(c_act.astype(BF16), w_ref[...].astype(BF16)) + b_ref[...]


def _modulation(c_pad, w_ada, b_ada):
    rows, d = c_pad.shape
    n = w_ada.shape[1]
    bn = d
    return pl.pallas_call(
        _mod_kernel,
        grid=(n // bn,),
        in_specs=[
            pl.BlockSpec((rows, d), lambda j: (0, 0)),
            pl.BlockSpec((d, bn), lambda j: (0, j)),
            pl.BlockSpec((1, bn), lambda j: (0, j)),
        ],
        out_specs=pl.BlockSpec((rows, bn), lambda j: (0, j)),
        out_shape=jax.ShapeDtypeStruct((rows, n), F32),
        compiler_params=pltpu.CompilerParams(
            dimension_semantics=("arbitrary",), vmem_limit_bytes=VMEM_LIMIT),
        name="adaln_mod",
    )(c_pad, w_ada, b_ada)


def _mixer_kernel(x_ref, mod_ref, gpre_ref, gpost_ref, win_ref, cw_ref, cb_ref,
                  wgate_ref, bgate_ref, lrua_ref, vng_ref, vnb_ref, wsp_ref, bsp_ref,
                  glru_ref, ggmlp_ref, wout_ref, o_ref, *scratch):
    nb_tiles = x_ref.shape[0]
    tail_ref, state_ref = scratch[4 * nb_tiles:]

    @pl.when(pl.program_id(1) == 0)
    def _():
        tail_ref[...] = jnp.zeros_like(tail_ref)
        state_ref[...] = jnp.zeros_like(state_ref)

    project = lambda k: _mix_in(x_ref.at[k], mod_ref.at[k], gpre_ref, win_ref)
    z = project(0)
    for k in range(nb_tiles):
        conv_scr, a_scr, b_scr, h_scr = scratch[k:4 * nb_tiles:nb_tiles]
        xc, gates = _mix_gates(z[0], cw_ref, cb_ref, wgate_ref, conv_scr, tail_ref.at[k])
        sp_groups = _mix_positions(z[3], vng_ref, vnb_ref, wsp_ref)
        z_next = project(k + 1) if k + 1 < nb_tiles else None
        hooks = [_zero_row_from(part) for part in z_next] if z_next else [None] * len(z)
        y = _mix_body(z, xc, gates, sp_groups, hooks, bgate_ref, lrua_ref, bsp_ref,
                      glru_ref, ggmlp_ref, a_scr, b_scr, h_scr, state_ref.at[k])
        _mix_out(y, x_ref.at[k], mod_ref.at[k], gpost_ref, wout_ref, o_ref.at[k])
        z = z_next


def _mix_in(x_ref, mod_ref, gpre_ref, win_ref):
    d = x_ref.shape[1]
    mod = mod_ref[...]
    sh_m = mod[:, 0:d]
    sc_m = mod[:, d:2 * d]
    hb = (_rms(x_ref[...]) * (gpre_ref[...] * (1.0 + sc_m)) + sh_m).astype(BF16)
    widths = (LRU_WIDTH, LRU_WIDTH, GMLP_WIDTH, GMLP_WIDTH)
    starts = [sum(widths[:q]) for q in range(len(widths))]
    return [_dot(hb, _as_bf16(win_ref[:, c0:c0 + w])) for c0, w in zip(starts, widths)]


def _mix_gates(lru_x, cw_ref, cb_ref, wgate_ref, conv_scr, tail_ref):
    t = lru_x.shape[0]
    tail = tail_ref[...]
    tail_ref[...] = lru_x[t - SUBLANES:]
    xc = jnp.concatenate(
        _conv_via_scratch(conv_scr, 0, tail, lru_x, cw_ref[...], cb_ref[...]), axis=1)
    xcb = xc.astype(BF16)
    gates = [_dot(xcb[:, j * GATE_TILE:(j + 1) * GATE_TILE], _as_bf16(wgate_ref[j]))
             for j in range(LRU_WIDTH // GATE_TILE)]
    return xc, gates


def _mix_out(y, x_ref, mod_ref, gpost_ref, wout_ref, o_ref):
    d = x_ref.shape[1]
    gt_m = mod_ref[...][:, 2 * d:3 * d]
    y = _dot(y, _as_bf16(wout_ref[...]))
    o_ref[...] = x_ref[...] + (gt_m * gpost_ref[...]) * _rms(y)


def _mix_positions(g_v, vng_ref, vnb_ref, wsp_ref):
    t = g_v.shape[0]
    gv = _gelu(g_v)
    mu = jnp.mean(gv, axis=-1, keepdims=True)
    cen = gv - mu
    var = jnp.mean(cen * cen, axis=-1, keepdims=True)
    v = cen * lax.rsqrt(var + EPS) * vng_ref[...] + vnb_ref[...]
    vb = v.astype(BF16)
    nb = t // GMLP_BLOCK
    pi = lax.broadcasted_iota(jnp.int32, (GMLP_BLOCK, GMLP_BLOCK), 0) // CHUNK
    pj = lax.broadcasted_iota(jnp.int32, (GMLP_BLOCK, GMLP_BLOCK), 1) // CHUNK
    mask = pj <= pi
    sp_groups = []
    for g in range(GMLP_GROUPS):
        ws = jnp.where(mask, 0.5 * wsp_ref[g], 0.0).astype(BF16)
        cols = slice(g * GMLP_GROUP_DIM, (g + 1) * GMLP_GROUP_DIM)
        rhs = jnp.concatenate(
            [vb[n * GMLP_BLOCK:(n + 1) * GMLP_BLOCK, cols] for n in range(nb)], axis=1)
        sp_groups.append(_dot(ws, rhs))
    return sp_groups


def _mix_body(z, xc, gates, sp_groups, hooks, bgate_ref, lrua_ref, bsp_ref,
              glru_ref, ggmlp_ref, a_scr, b_scr, h_scr, state_ref):
    _, lru_gate, g_u, _ = z
    t = lru_gate.shape[0]
    slabs = lambda v: [v[:, c * LANES:(c + 1) * LANES] for c in range(v.shape[1] // LANES)]
    hooked = lambda v, hook: v if hook is None else v + hook

    r_pre = jnp.concatenate([g[:, :GATE_TILE] for g in gates], axis=1)
    i_pre = jnp.concatenate([g[:, GATE_TILE:] for g in gates], axis=1)
    bgate = bgate_ref[...]
    r_gate = jax.nn.sigmoid(r_pre + bgate[:, :LRU_WIDTH])
    i_gate = jax.nn.sigmoid(i_pre + bgate[:, LRU_WIDTH:])
    neg_a = -lrua_ref[...]
    softplus = jnp.maximum(neg_a, 0.0) + jnp.log1p(jnp.exp(-jnp.abs(neg_a)))
    log_a = r_gate * ((-LRU_C) * softplus)
    a = jnp.exp(log_a)
    th = jnp.tanh(log_a)
    q = (-2.0 * th) / (1.0 - th)
    mult = jnp.where(q > 0.0, q * lax.rsqrt(q), 0.0)
    bx = (mult * xc) * i_gate
    hs = _linear_scan(slabs(a), slabs(bx), state_ref, a_scr, b_scr, h_scr, h0_add=hooks[0])
    y_lru = 0.5 * _gelu2_times(lru_gate, jnp.concatenate(hs, axis=1))
    yl = _rms(y_lru) * hooked(glru_ref[...], hooks[1])

    nb = t // GMLP_BLOCK
    bsp = hooked(0.5 * bsp_ref[...], hooks[2])
    sp_rows = []
    for n in range(nb):
        blk = jnp.concatenate(
            [sg[:, n * GMLP_GROUP_DIM:(n + 1) * GMLP_GROUP_DIM] for sg in sp_groups], axis=1)
        sp_rows.append(blk + bsp)
    sp_half = jnp.concatenate(sp_rows, axis=0)
    y_gmlp = _gelu2_times(g_u, sp_half)
    yg = _rms(y_gmlp) * hooked(ggmlp_ref[...], hooks[3])

    return jnp.concatenate([yl, yg], axis=1).astype(BF16)


def _mixer(x, mod3, g_pre, g_post, w_in, conv_w, conv_b, w_gate, b_gate, lru_a,
           v_norm_g, v_norm_b, w_spatial, b_sp_full, g_lru_out, g_gmlp_out, w_out):
    bsz, s, d = x.shape
    t = MIX_T
    nb = MIX_NB
    assert bsz % nb == 0 and s % t == 0
    n_lg = LRU_WIDTH // LANES
    scan_rows = SUBLANES * (t // SUBLANES + 1)
    full = lambda a: pl.BlockSpec(a.shape, lambda b, i: (0,) * a.ndim)
    in_arrays = [g_pre, g_post, w_in, conv_w, conv_b, w_gate, b_gate, lru_a,
                 v_norm_g, v_norm_b, w_spatial, b_sp_full, g_lru_out, g_gmlp_out, w_out]
    return pl.pallas_call(
        _mixer_kernel,
        grid=(bsz // nb, s // t),
        in_specs=[
            pl.BlockSpec((nb, t, d), lambda b, i: (b, i, 0)),
            pl.BlockSpec((nb, 1, mod3.shape[2]), lambda b, i: (b, 0, 0)),
        ] + [full(a) for a in in_arrays],
        out_specs=pl.BlockSpec((nb, t, d), lambda b, i: (b, i, 0)),
        out_shape=jax.ShapeDtypeStruct(x.shape, x.dtype),
        scratch_shapes=(
            [pltpu.VMEM((n_lg, SUBLANES + t, LANES), F32)] * nb
            + [pltpu.VMEM((n_lg, scan_rows, LANES), F32)] * (3 * nb)
            + [pltpu.VMEM((nb, SUBLANES, LRU_WIDTH), F32)] * 2),
        compiler_params=pltpu.CompilerParams(
            dimension_semantics=("arbitrary", "arbitrary"), vmem_limit_bytes=VMEM_LIMIT),
        name="token_mixer",
    )(x, mod3, *in_arrays)


def _ffn_kernel(x_ref, mod_ref, gpre_ref, gpost_ref, wup_ref, cw_ref, cb_ref, wd_ref,
                o_ref, scr_ref, tail_ref):
    t_idx = pl.program_id(1)
    t, d = x_ref.shape
    d_ff = 2 * wd_ref.shape[0]
    fc = FFN_FC
    lg = fc // LANES

    @pl.when(t_idx == 0)
    def _():
        tail_ref[...] = jnp.zeros_like(tail_ref)

    mod = mod_ref[...]
    sh_f = mod[:, 3 * d:4 * d]
    sc_f = mod[:, 4 * d:5 * d]
    gt_f = mod[:, 5 * d:6 * d]
    x = x_ref[...]
    hb = (_rms(x) * (gpre_ref[...] * (1.0 + sc_f)) + sh_f).astype(BF16)

    n_chunks = d_ff // fc
    halves = ((0, 1.0), (d_ff, 0.5))
    chunk_cols = lambda base, j: slice(base + j * fc, base + (j + 1) * fc)
    up_proj = lambda j: [_dot(hb, _as_bf16(wup_ref[:, chunk_cols(base, j)])) for base, _ in halves]

    def gated(j, ups):
        conv = []
        for k, (base, scale) in enumerate(halves):
            cols = chunk_cols(base, j)
            tail = tail_ref[:, cols]
            tail_ref[:, cols] = ups[k][t - SUBLANES:]
            slot = (2 * j + k) % FFN_SLOTS
            conv.append(_conv_via_scratch(scr_ref, slot * lg, tail, ups[k],
                                          cw_ref[:, cols] * scale, cb_ref[:, cols] * scale))
        return jnp.concatenate(
            [_gelu2_times(cg, cv) for cg, cv in zip(*conv)], axis=1).astype(BF16)

    acc = None
    ups = up_proj(0)
    for j in range(n_chunks):
        ups_next = up_proj(j + 1) if j + 1 < n_chunks else None
        part = _dot(gated(j, ups), _as_bf16(wd_ref[j * fc // 2:(j + 1) * fc // 2, :]))
        acc = part if acc is None else acc + part
        ups = ups_next
    o_ref[...] = x + (gt_f * gpost_ref[...]) * _rms(acc)


def _ffn(x, mod3, g_pre, g_post, w_up, conv_w, conv_b, w_down):
    bsz, s, d = x.shape
    t = FFN_T
    resident = lambda a: pl.BlockSpec(a.shape, lambda b, i: (0,) * a.ndim,
                                      pipeline_mode=pl.Buffered(1))
    return pl.pallas_call(
        _ffn_kernel,
        grid=(bsz, s // t),
        in_specs=[
            pl.BlockSpec((None, t, d), lambda b, i: (b, i, 0)),
            pl.BlockSpec((None, 1, mod3.shape[2]), lambda b, i: (b, 0, 0)),
            resident(g_pre), resident(g_post), resident(w_up), resident(conv_w),
            resident(conv_b), resident(w_down),
        ],
        out_specs=pl.BlockSpec((None, t, d), lambda b, i: (b, i, 0)),
        out_shape=jax.ShapeDtypeStruct(x.shape, x.dtype),
        scratch_shapes=[
            pltpu.VMEM((FFN_SLOTS * (FFN_FC // LANES), SUBLANES + t, LANES), F32),
            pltpu.VMEM((SUBLANES, w_up.shape[1]), F32),
        ],
        compiler_params=pltpu.CompilerParams(
            dimension_semantics=("arbitrary", "arbitrary"), vmem_limit_bytes=VMEM_LIMIT),
        name="conv_ffn",
    )(x, mod3, g_pre, g_post, w_up, conv_w, conv_b, w_down)


def _gate_weights(w_rgate, w_igate):
    heads_per_tile = GATE_TILE // LRU_HEAD_DIM
    tiles = []
    for j in range(LRU_WIDTH // GATE_TILE):
        hs = slice(j * heads_per_tile, (j + 1) * heads_per_tile)
        r_bd = jax.scipy.linalg.block_diag(*w_rgate[hs])
        i_bd = jax.scipy.linalg.block_diag(*w_igate[hs])
        tiles.append(jnp.concatenate([r_bd, i_bd], axis=1))
    return _pack_rows_xla(jnp.stack(tiles))


def kernel(x, c, w_ada, b_ada, g_mix_pre, g_mix_post, w_in, conv_w, conv_b, w_rgate, b_rgate, w_igate, b_igate, lru_a, v_norm_g, v_norm_b, w_spatial, b_spatial, g_lru_out, g_gmlp_out, w_out, g_ffn_pre, g_ffn_post, w_up, ffn_conv_w, ffn_conv_b, w_down):
    depth = w_ada.shape[0]
    bsz, s, d = x.shape
    c_pad = jnp.pad(c, ((0, SUBLANES - bsz % SUBLANES if bsz % SUBLANES else 0), (0, 0)))
    for l in range(depth):
        mod = _modulation(c_pad, w_ada[l], b_ada[l][None, :])
        mod3 = mod.reshape(mod.shape[0], 1, mod.shape[1])
        row = lambda a: a[l][None, :]
        w_gate = _gate_weights(w_rgate[l], w_igate[l])
        b_gate = jnp.concatenate([b_rgate[l].reshape(1, -1), b_igate[l].reshape(1, -1)], axis=1)
        b_sp_full = jnp.repeat(b_spatial[l].T, GMLP_GROUP_DIM, axis=1)
        x = _mixer(x, mod3, row(g_mix_pre), row(g_mix_post), _pack_rows(w_in[l]),
                   conv_w[l], row(conv_b), w_gate, b_gate, row(lru_a),
                   row(v_norm_g), row(v_norm_b), w_spatial[l], b_sp_full,
                   row(g_lru_out), row(g_gmlp_out), _pack_rows(w_out[l]))
        x = _ffn(x, mod3, row(g_ffn_pre), row(g_ffn_post), _pack_rows(w_up[l]),
                 ffn_conv_w[l], row(ffn_conv_b), _pack_rows(w_down[l]))
    return x
```

```python
import jax
import jax.numpy as jnp
from jax import lax
from jax.experimental import pallas as pl
from jax.experimental.pallas import tpu as pltpu

CHUNK = 64
LRU_WIDTH = 512
LRU_HEADS = 8
LRU_HEAD_DIM = LRU_WIDTH // LRU_HEADS
LRU_CONV_WIDTH = 4
LRU_C = 8.0
GMLP_WIDTH = 512
GMLP_GROUPS = 4
GMLP_GROUP_DIM = GMLP_WIDTH // GMLP_GROUPS
GMLP_BLOCK = 128
FFN_CONV_WIDTH = 3
N_MOD = 6
EPS = 1e-6

SUBLANES = 8
LANES = 128
GATE_TILE = 256

MIX_T = 256
MIX_NB = 4
FFN_T = 512
FFN_FC = 512
FFN_SLOTS = 4
PACK_BLOCK_BYTES = 4 * 1024 * 1024
VMEM_LIMIT = 56 * 1024 * 1024

F32 = jnp.float32
BF16 = jnp.bfloat16
U32 = jnp.uint32


def _dot(a, b):
    return jnp.dot(a, b, preferred_element_type=F32)


def _pack_rows_xla(w):
    k, n = w.shape[-2:]
    wb = w.astype(BF16).reshape(*w.shape[:-2], k // 2, 2, n)
    return lax.bitcast_convert_type(jnp.swapaxes(wb, -1, -2), U32)


def _pack_kernel(w_ref, o_ref):
    o_ref[...] = pltpu.bitcast(w_ref[...].astype(BF16), U32)


def _pack_rows(w):
    k, n = w.shape
    bk = k
    while bk * n * 4 > PACK_BLOCK_BYTES and bk % (4 * SUBLANES) == 0:
        bk //= 2
    assert k % bk == 0
    return pl.pallas_call(
        _pack_kernel,
        grid=(k // bk,),
        in_specs=[pl.BlockSpec((bk, n), lambda i: (i, 0))],
        out_specs=pl.BlockSpec((bk // 2, n), lambda i: (i, 0)),
        out_shape=jax.ShapeDtypeStruct((k // 2, n), U32),
        compiler_params=pltpu.CompilerParams(
            dimension_semantics=("arbitrary",), vmem_limit_bytes=VMEM_LIMIT),
        name="pack_weight",
    )(w)


def _as_bf16(packed):
    return pltpu.bitcast(packed, BF16)


def _zero_row_from(x):
    last = lax.bitcast_convert_type(x[x.shape[0] - SUBLANES:], U32)
    zero = lax.shift_right_logical(lax.shift_right_logical(last, U32(16)), U32(16))
    return zero[0:1].astype(F32)


def _rms(x):
    return x * lax.rsqrt(jnp.mean(x * x, axis=-1, keepdims=True) + EPS)


def _gelu(x):
    c0 = 0.7978845608028654
    hx = 0.5 * x
    return hx + hx * jnp.tanh(x * (c0 + (c0 * 0.044715) * (x * x)))


def _gelu2_times(x, v):
    c0 = 0.7978845608028654
    z = x * (c0 + (c0 * 0.044715) * (x * x))
    return (x * v) * (1.0 + jnp.tanh(z))


def _rows(ref, lead, start, size):
    return ref[pl.ds(lead, 1, stride=2), pl.ds(start, size), :][0]


def _put_rows(ref, lead, start, val):
    ref[pl.ds(lead, 1, stride=2), pl.ds(start, val.shape[0]), :] = val[None]


def _linear_scan(a, b, state_ref, a_scr, b_scr, h_scr, h0_add=None):
    t = a[0].shape[0]
    seg = t // SUBLANES
    pitch = seg + 1 - seg % 2
    n = len(a)
    for c in range(n):
        for s in range(SUBLANES):
            _put_rows(a_scr, c, s * pitch, a[c][s * seg:(s + 1) * seg])
            _put_rows(b_scr, c, s * pitch, b[c][s * seg:(s + 1) * seg])
    step = lambda ref, c, j: ref[c, pl.ds(j, SUBLANES, stride=pitch), :]
    row = lax.broadcasted_iota(jnp.int32, (SUBLANES, LANES), 0)
    shifted = lambda v, sh, fill: jnp.where(row >= sh, pltpu.roll(v, sh, axis=0), fill)

    prod = [None] * n
    end = [None] * n
    for j in range(seg):
        for c in range(n):
            aj, bj = step(a_scr, c, j), step(b_scr, c, j)
            prod[c] = aj if j == 0 else aj * prod[c]
            end[c] = bj if j == 0 else aj * end[c] + bj

    h = []
    for c in range(n):
        cols = slice(c * LANES, (c + 1) * LANES)
        h0 = state_ref[SUBLANES - 1:SUBLANES, cols]
        if h0_add is not None:
            h0 = h0 + h0_add[:, cols]
        p = prod[c]
        e = end[c] + jnp.where(row == 0, p * h0, 0.0)
        for sh in (1, 2, 4):
            e = e + p * shifted(e, sh, 0.0)
            if sh < 4:
                p = p * shifted(p, sh, 1.0)
        state_ref[:, cols] = e
        h.append(shifted(e, 1, h0))
    for j in range(seg):
        for c in range(n):
            h[c] = step(a_scr, c, j) * h[c] + step(b_scr, c, j)
            h_scr[c, pl.ds(j, SUBLANES, stride=pitch), :] = h[c]
    return [jnp.concatenate([_rows(h_scr, c, s * pitch, seg) for s in range(SUBLANES)], axis=0)
            for c in range(n)]


def _conv_via_scratch(scr_ref, lead0, tail, x, w, b):
    k_width = w.shape[0]
    t, c = x.shape
    outs = []
    for g in range(c // LANES):
        cols = slice(g * LANES, (g + 1) * LANES)
        scr_ref[lead0 + g, 0:SUBLANES, :] = tail[:, cols]
        scr_ref[lead0 + g, SUBLANES:SUBLANES + t, :] = x[:, cols]
    for g in range(c // LANES):
        cols = slice(g * LANES, (g + 1) * LANES)
        out = x[:, cols] * w[k_width - 1:k_width, cols] + b[:, cols]
        for k in range(k_width - 1):
            shifted = _rows(scr_ref, lead0 + g, SUBLANES - (k_width - 1 - k), t)
            out = out + shifted * w[k:k + 1, cols]
        outs.append(out)
    return outs


def _mod_kernel(c_ref, w_ref, b_ref, o_ref):
    c = c_ref[...]
    c_act = c * jax.nn.sigmoid(c)
    o_ref[...] = _dot(c_act.astype(BF16), w_ref[...].astype(BF16)) + b_ref[...]


def _modulation(c_pad, w_ada, b_ada):
    rows, d = c_pad.shape
    n = w_ada.shape[1]
    bn = d
    return pl.pallas_call(
        _mod_kernel,
        grid=(n // bn,),
        in_specs=[
            pl.BlockSpec((rows, d), lambda j: (0, 0)),
            pl.BlockSpec((d, bn), lambda j: (0, j)),
            pl.BlockSpec((1, bn), lambda j: (0, j)),
        ],
        out_specs=pl.BlockSpec((rows, bn), lambda j: (0, j)),
        out_shape=jax.ShapeDtypeStruct((rows, n), F32),
        compiler_params=pltpu.CompilerParams(
            dimension_semantics=("arbitrary",), vmem_limit_bytes=VMEM_LIMIT),
        name="adaln_mod",
    )(c_pad, w_ada, b_ada)


def _mixer_kernel(x_ref, mod_ref, gpre_ref, gpost_ref, win_ref, cw_ref, cb_ref,
                  wgate_ref, bgate_ref, lrua_ref, vng_ref, vnb_ref, wsp_ref, bsp_ref,
                  glru_ref, ggmlp_ref, wout_ref, o_ref, *scratch):
    nb_tiles = x_ref.shape[0]
    tail_ref, state_ref = scratch[4 * nb_tiles:]

    @pl.when(pl.program_id(1) == 0)
    def _():
        tail_ref[...] = jnp.zeros_like(tail_ref)
        state_ref[...] = jnp.zeros_like(state_ref)

    def project(k):
        part = _mix_in(x_ref.at[k], mod_ref.at[k], gpre_ref, win_ref)
        lru_x = part(0)
        rest = [part(q) for q in (1, 2, 3)] if k == 0 else None
        xc, gates = _mix_gates(lru_x, cw_ref, cb_ref, wgate_ref, scratch[k], tail_ref.at[k])
        rest = rest or [part(q) for q in (1, 2, 3)]
        return [lru_x] + rest, xc, gates

    z, xc, gates = project(0)
    for k in range(nb_tiles):
        a_scr, b_scr, h_scr = scratch[nb_tiles + k:4 * nb_tiles:nb_tiles]
        sp_groups = _mix_positions(z[3], vng_ref, vnb_ref, wsp_ref)
        nxt = project(k + 1) if k + 1 < nb_tiles else None
        hooks = [_zero_row_from(part) for part in nxt[0]] if nxt else [None] * len(z)
        y = _mix_body(z, xc, gates, sp_groups, hooks, bgate_ref, lrua_ref, bsp_ref,
                      glru_ref, ggmlp_ref, a_scr, b_scr, h_scr, state_ref.at[k])
        _mix_out(y, x_ref.at[k], mod_ref.at[k], gpost_ref, wout_ref, o_ref.at[k])
        if nxt:
            z, xc, gates = nxt


def _mix_in(x_ref, mod_ref, gpre_ref, win_ref):
    d = x_ref.shape[1]
    mod = mod_ref[...]
    sh_m = mod[:, 0:d]
    sc_m = mod[:, d:2 * d]
    hb = (_rms(x_ref[...]) * (gpre_ref[...] * (1.0 + sc_m)) + sh_m).astype(BF16)
    widths = (LRU_WIDTH, LRU_WIDTH, GMLP_WIDTH, GMLP_WIDTH)
    starts = [sum(widths[:q]) for q in range(len(widths))]
    return lambda q: _dot(hb, _as_bf16(win_ref[:, starts[q]:starts[q] + widths[q]]))


def _mix_gates(lru_x, cw_ref, cb_ref, wgate_ref, conv_scr, tail_ref):
    t = lru_x.shape[0]
    tail = tail_ref[...]
    tail_ref[...] = lru_x[t - SUBLANES:]
    xc = jnp.concatenate(
        _conv_via_scratch(conv_scr, 0, tail, lru_x, cw_ref[...], cb_ref[...]), axis=1)
    xcb = xc.astype(BF16)
    gates = [_dot(xcb[:, j * GATE_TILE:(j + 1) * GATE_TILE], _as_bf16(wgate_ref[j]))
             for j in range(LRU_WIDTH // GATE_TILE)]
    return xc, gates


def _mix_out(y, x_ref, mod_ref, gpost_ref, wout_ref, o_ref):
    d = x_ref.shape[1]
    gt_m = mod_ref[...][:, 2 * d:3 * d]
    y = _dot(y, _as_bf16(wout_ref[...]))
    o_ref[...] = x_ref[...] + (gt_m * gpost_ref[...]) * _rms(y)


def _mix_positions(g_v, vng_ref, vnb_ref, wsp_ref):
    t = g_v.shape[0]
    gv = _gelu(g_v)
    mu = jnp.mean(gv, axis=-1, keepdims=True)
    cen = gv - mu
    var = jnp.mean(cen * cen, axis=-1, keepdims=True)
    v = cen * lax.rsqrt(var + EPS) * vng_ref[...] + vnb_ref[...]
    vb = v.astype(BF16)
    nb = t // GMLP_BLOCK
    pi = lax.broadcasted_iota(jnp.int32, (GMLP_BLOCK, GMLP_BLOCK), 0) // CHUNK
    pj = lax.broadcasted_iota(jnp.int32, (GMLP_BLOCK, GMLP_BLOCK), 1) // CHUNK
    mask = pj <= pi
    sp_groups = []
    for g in range(GMLP_GROUPS):
        ws = jnp.where(mask, 0.5 * wsp_ref[g], 0.0).astype(BF16)
        cols = slice(g * GMLP_GROUP_DIM, (g + 1) * GMLP_GROUP_DIM)
        rhs = jnp.concatenate(
            [vb[n * GMLP_BLOCK:(n + 1) * GMLP_BLOCK, cols] for n in range(nb)], axis=1)
        sp_groups.append(_dot(ws, rhs))
    return sp_groups


def _mix_body(z, xc, gates, sp_groups, hooks, bgate_ref, lrua_ref, bsp_ref,
              glru_ref, ggmlp_ref, a_scr, b_scr, h_scr, state_ref):
    _, lru_gate, g_u, _ = z
    t = lru_gate.shape[0]
    slabs = lambda v: [v[:, c * LANES:(c + 1) * LANES] for c in range(v.shape[1] // LANES)]
    hooked = lambda v, hook: v if hook is None else v + hook

    r_pre = jnp.concatenate([g[:, :GATE_TILE] for g in gates], axis=1)
    i_pre = jnp.concatenate([g[:, GATE_TILE:] for g in gates], axis=1)
    bgate = bgate_ref[...]
    r_gate = jax.nn.sigmoid(r_pre + bgate[:, :LRU_WIDTH])
    i_gate = jax.nn.sigmoid(i_pre + bgate[:, LRU_WIDTH:])
    neg_a = -lrua_ref[...]
    softplus = jnp.maximum(neg_a, 0.0) + jnp.log1p(jnp.exp(-jnp.abs(neg_a)))
    log_a = r_gate * ((-LRU_C) * softplus)
    a = jnp.exp(log_a)
    th = jnp.tanh(log_a)
    q = (-2.0 * th) / (1.0 - th)
    mult = jnp.where(q > 0.0, q * lax.rsqrt(q), 0.0)
    bx = (mult * xc) * i_gate
    hs = _linear_scan(slabs(a), slabs(bx), state_ref, a_scr, b_scr, h_scr, h0_add=hooks[0])
    y_lru = 0.5 * _gelu2_times(lru_gate, jnp.concatenate(hs, axis=1))
    yl = _rms(y_lru) * hooked(glru_ref[...], hooks[1])

    nb = t // GMLP_BLOCK
    bsp = hooked(0.5 * bsp_ref[...], hooks[2])
    sp_rows = []
    for n in range(nb):
        blk = jnp.concatenate(
            [sg[:, n * GMLP_GROUP_DIM:(n + 1) * GMLP_GROUP_DIM] for sg in sp_groups], axis=1)
        sp_rows.append(blk + bsp)
    sp_half = jnp.concatenate(sp_rows, axis=0)
    y_gmlp = _gelu2_times(g_u, sp_half)
    yg = _rms(y_gmlp) * hooked(ggmlp_ref[...], hooks[3])

    return jnp.concatenate([yl, yg], axis=1).astype(BF16)


def _mixer(x, mod3, g_pre, g_post, w_in, conv_w, conv_b, w_gate, b_gate, lru_a,
           v_norm_g, v_norm_b, w_spatial, b_sp_full, g_lru_out, g_gmlp_out, w_out):
    bsz, s, d = x.shape
    t = MIX_T
    nb = MIX_NB
    assert bsz % nb == 0 and s % t == 0
    n_lg = LRU_WIDTH // LANES
    scan_rows = SUBLANES * (t // SUBLANES + 1)
    full = lambda a: pl.BlockSpec(a.shape, lambda b, i: (0,) * a.ndim)
    in_arrays = [g_pre, g_post, w_in, conv_w, conv_b, w_gate, b_gate, lru_a,
                 v_norm_g, v_norm_b, w_spatial, b_sp_full, g_lru_out, g_gmlp_out, w_out]
    return pl.pallas_call(
        _mixer_kernel,
        grid=(bsz // nb, s // t),
        in_specs=[
            pl.BlockSpec((nb, t, d), lambda b, i: (b, i, 0)),
            pl.BlockSpec((nb, 1, mod3.shape[2]), lambda b, i: (b, 0, 0)),
        ] + [full(a) for a in in_arrays],
        out_specs=pl.BlockSpec((nb, t, d), lambda b, i: (b, i, 0)),
        out_shape=jax.ShapeDtypeStruct(x.shape, x.dtype),
        scratch_shapes=(
            [pltpu.VMEM((n_lg, SUBLANES + t, LANES), F32)] * nb
            + [pltpu.VMEM((n_lg, scan_rows, LANES), F32)] * (3 * nb)
            + [pltpu.VMEM((nb, SUBLANES, LRU_WIDTH), F32)] * 2),
        compiler_params=pltpu.CompilerParams(
            dimension_semantics=("arbitrary", "arbitrary"), vmem_limit_bytes=VMEM_LIMIT),
        name="token_mixer",
    )(x, mod3, *in_arrays)


def _ffn_kernel(x_ref, mod_ref, gpre_ref, gpost_ref, wup_ref, cw_ref, cb_ref, wd_ref,
                o_ref, scr_ref, tail_ref):
    t_idx = pl.program_id(1)
    t, d = x_ref.shape
    d_ff = 2 * wd_ref.shape[0]
    fc = FFN_FC
    lg = fc // LANES

    @pl.when(t_idx == 0)
    def _():
        tail_ref[...] = jnp.zeros_like(tail_ref)

    mod = mod_ref[...]
    sh_f = mod[:, 3 * d:4 * d]
    sc_f = mod[:, 4 * d:5 * d]
    gt_f = mod[:, 5 * d:6 * d]
    x = x_ref[...]
    hb = (_rms(x) * (gpre_ref[...] * (1.0 + sc_f)) + sh_f).astype(BF16)

    n_chunks = d_ff // fc
    halves = ((0, 1.0), (d_ff, 0.5))
    chunk_cols = lambda base, j: slice(base + j * fc, base + (j + 1) * fc)
    up_proj = lambda j: [_dot(hb, _as_bf16(wup_ref[:, chunk_cols(base, j)])) for base, _ in halves]

    def gated(j, ups):
        conv = []
        for k, (base, scale) in enumerate(halves):
            cols = chunk_cols(base, j)
            tail = tail_ref[:, cols]
            tail_ref[:, cols] = ups[k][t - SUBLANES:]
            slot = (2 * j + k) % FFN_SLOTS
            conv.append(_conv_via_scratch(scr_ref, slot * lg, tail, ups[k],
                                          cw_ref[:, cols] * scale, cb_ref[:, cols] * scale))
        return jnp.concatenate(
            [_gelu2_times(cg, cv) for cg, cv in zip(*conv)], axis=1).astype(BF16)

    acc = None
    ups = up_proj(0)
    for j in range(n_chunks):
        ups_next = up_proj(j + 1) if j + 1 < n_chunks else None
        part = _dot(gated(j, ups), _as_bf16(wd_ref[j * fc // 2:(j + 1) * fc // 2, :]))
        acc = part if acc is None else acc + part
        ups = ups_next
    o_ref[...] = x + (gt_f * gpost_ref[...]) * _rms(acc)


def _ffn(x, mod3, g_pre, g_post, w_up, conv_w, conv_b, w_down):
    bsz, s, d = x.shape
    t = FFN_T
    resident = lambda a: pl.BlockSpec(a.shape, lambda b, i: (0,) * a.ndim,
                                      pipeline_mode=pl.Buffered(1))
    return pl.pallas_call(
        _ffn_kernel,
        grid=(bsz, s // t),
        in_specs=[
            pl.BlockSpec((None, t, d), lambda b, i: (b, i, 0)),
            pl.BlockSpec((None, 1, mod3.shape[2]), lambda b, i: (b, 0, 0)),
            resident(g_pre), resident(g_post), resident(w_up), resident(conv_w),
            resident(conv_b), resident(w_down),
        ],
        out_specs=pl.BlockSpec((None, t, d), lambda b, i: (b, i, 0)),
        out_shape=jax.ShapeDtypeStruct(x.shape, x.dtype),
        scratch_shapes=[
            pltpu.VMEM((FFN_SLOTS * (FFN_FC // LANES), SUBLANES + t, LANES), F32),
            pltpu.VMEM((SUBLANES, w_up.shape[1]), F32),
        ],
        compiler_params=pltpu.CompilerParams(
            dimension_semantics=("arbitrary", "arbitrary"), vmem_limit_bytes=VMEM_LIMIT),
        name="conv_ffn",
    )(x, mod3, g_pre, g_post, w_up, conv_w, conv_b, w_down)


def _gate_weights(w_rgate, w_igate):
    heads_per_tile = GATE_TILE // LRU_HEAD_DIM
    tiles = []
    for j in range(LRU_WIDTH // GATE_TILE):
        hs = slice(j * heads_per_tile, (j + 1) * heads_per_tile)
        r_bd = jax.scipy.linalg.block_diag(*w_rgate[hs])
        i_bd = jax.scipy.linalg.block_diag(*w_igate[hs])
        tiles.append(jnp.concatenate([r_bd, i_bd], axis=1))
    return _pack_rows_xla(jnp.stack(tiles))


def kernel(x, c, w_ada, b_ada, g_mix_pre, g_mix_post, w_in, conv_w, conv_b, w_rgate, b_rgate, w_igate, b_igate, lru_a, v_norm_g, v_norm_b, w_spatial, b_spatial, g_lru_out, g_gmlp_out, w_out, g_ffn_pre, g_ffn_post, w_up, ffn_conv_w, ffn_conv_b, w_down):
    depth = w_ada.shape[0]
    bsz, s, d = x.shape
    c_pad = jnp.pad(c, ((0, SUBLANES - bsz % SUBLANES if bsz % SUBLANES else 0), (0, 0)))
    for l in range(depth):
        mod = _modulation(c_pad, w_ada[l], b_ada[l][None, :])
        mod3 = mod.reshape(mod.shape[0], 1, mod.shape[1])
        row = lambda a: a[l][None, :]
        w_gate = _gate_weights(w_rgate[l], w_igate[l])
        b_gate = jnp.concatenate([b_rgate[l].reshape(1, -1), b_igate[l].reshape(1, -1)], axis=1)
        b_sp_full = jnp.repeat(b_spatial[l].T, GMLP_GROUP_DIM, axis=1)
        x = _mixer(x, mod3, row(g_mix_pre), row(g_mix_post), _pack_rows(w_in[l]),
                   conv_w[l], row(conv_b), w_gate, b_gate, row(lru_a),
                   row(v_norm_g), row(v_norm_b), w_spatial[l], b_sp_full,
                   row(g_lru_out), row(g_gmlp_out), _pack_rows(w_out[l]))
        x = _ffn(x, mod3, row(g_ffn_pre), row(g_ffn_post), _pack_rows(w_up[l]),
                 ffn_conv_w[l], row(ffn_conv_b), _pack_rows(w_down[l]))
    return x
```

```python
import jax
import jax.numpy as jnp
from jax import lax
from jax.experimental import pallas as pl
from jax.experimental.pallas import tpu as pltpu

CHUNK = 64
LRU_WIDTH = 512
LRU_HEADS = 8
LRU_HEAD_DIM = LRU_WIDTH // LRU_HEADS
LRU_CONV_WIDTH = 4
LRU_C = 8.0
GMLP_WIDTH = 512
GMLP_GROUPS = 4
GMLP_GROUP_DIM = GMLP_WIDTH // GMLP_GROUPS
GMLP_BLOCK = 128
FFN_CONV_WIDTH = 3
N_MOD = 6
EPS = 1e-6

SUBLANES = 8
LANES = 128
GATE_TILE = 256

MIX_T = 256
MIX_NB = 4
FFN_T = 512
FFN_FC = 512
FFN_SLOTS = 4
PACK_BLOCK_BYTES = 4 * 1024 * 1024
VMEM_LIMIT = 56 * 1024 * 1024

F32 = jnp.float32
BF16 = jnp.bfloat16
U32 = jnp.uint32


def _dot(a, b):
    return jnp.dot(a, b, preferred_element_type=F32)


def _pack_rows_xla(w):
    k, n = w.shape[-2:]
    wb = w.astype(BF16).reshape(*w.shape[:-2], k // 2, 2, n)
    return lax.bitcast_convert_type(jnp.swapaxes(wb, -1, -2), U32)


def _pack_kernel(w_ref, o_ref):
    o_ref[...] = pltpu.bitcast(w_ref[...].astype(BF16), U32)


def _pack_rows(w):
    k, n = w.shape
    bk = k
    while bk * n * 4 > PACK_BLOCK_BYTES and bk % (4 * SUBLANES) == 0:
        bk //= 2
    assert k % bk == 0
    return pl.pallas_call(
        _pack_kernel,
        grid=(k // bk,),
        in_specs=[pl.BlockSpec((bk, n), lambda i: (i, 0))],
        out_specs=pl.BlockSpec((bk // 2, n), lambda i: (i, 0)),
        out_shape=jax.ShapeDtypeStruct((k // 2, n), U32),
        compiler_params=pltpu.CompilerParams(
            dimension_semantics=("arbitrary",), vmem_limit_bytes=VMEM_LIMIT),
        name="pack_weight",
    )(w)


def _as_bf16(packed):
    return pltpu.bitcast(packed, BF16)


def _zero_row_from(x):
    last = lax.bitcast_convert_type(x[x.shape[0] - SUBLANES:], U32)
    zero = lax.shift_right_logical(lax.shift_right_logical(last, U32(16)), U32(16))
    return zero[0:1].astype(F32)


def _rms(x):
    return x * lax.rsqrt(jnp.mean(x * x, axis=-1, keepdims=True) + EPS)


def _gelu(x):
    c0 = 0.7978845608028654
    hx = 0.5 * x
    return hx + hx * jnp.tanh(x * (c0 + (c0 * 0.044715) * (x * x)))


def _gelu2_times(x, v):
    c0 = 0.7978845608028654
    z = x * (c0 + (c0 * 0.044715) * (x * x))
    return (x * v) * (1.0 + jnp.tanh(z))


def _rows(ref, lead, start, size):
    return ref[pl.ds(lead, 1, stride=2), pl.ds(start, size), :][0]


def _put_rows(ref, lead, start, val):
    ref[pl.ds(lead, 1, stride=2), pl.ds(start, val.shape[0]), :] = val[None]


def _linear_scan(a, b, state_ref, a_scr, b_scr, h_scr, h0_add=None):
    t = a[0].shape[0]
    seg = t // SUBLANES
    pitch = seg + 1 - seg % 2
    n = len(a)
    for c in range(n):
        for s in range(SUBLANES):
            _put_rows(a_scr, c, s * pitch, a[c][s * seg:(s + 1) * seg])
            _put_rows(b_scr, c, s * pitch, b[c][s * seg:(s + 1) * seg])
    step = lambda ref, c, j: ref[c, pl.ds(j, SUBLANES, stride=pitch), :]
    row = lax.broadcasted_iota(jnp.int32, (SUBLANES, LANES), 0)
    shifted = lambda v, sh, fill: jnp.where(row >= sh, pltpu.roll(v, sh, axis=0), fill)

    prod = [None] * n
    end = [None] * n
    for j in range(seg):
        for c in range(n):
            aj, bj = step(a_scr, c, j), step(b_scr, c, j)
            prod[c] = aj if j == 0 else aj * prod[c]
            end[c] = bj if j == 0 else aj * end[c] + bj

    h = []
    for c in range(n):
        cols = slice(c * LANES, (c + 1) * LANES)
        h0 = state_ref[SUBLANES - 1:SUBLANES, cols]
        if h0_add is not None:
            h0 = h0 + h0_add[:, cols]
        p = prod[c]
        e = end[c] + jnp.where(row == 0, p * h0, 0.0)
        for sh in (1, 2, 4):
            e = e + p * shifted(e, sh, 0.0)
            if sh < 4:
                p = p * shifted(p, sh, 1.0)
        state_ref[:, cols] = e
        h.append(shifted(e, 1, h0))
    for j in range(seg):
        for c in range(n):
            h[c] = step(a_scr, c, j) * h[c] + step(b_scr, c, j)
            h_scr[c, pl.ds(j, SUBLANES, stride=pitch), :] = h[c]
    return [jnp.concatenate([_rows(h_scr, c, s * pitch, seg) for s in range(SUBLANES)], axis=0)
            for c in range(n)]


def _conv_via_scratch(scr_ref, lead0, tail, x, w, b):
    k_width = w.shape[0]
    t, c = x.shape
    outs = []
    for g in range(c // LANES):
        cols = slice(g * LANES, (g + 1) * LANES)
        scr_ref[lead0 + g, 0:SUBLANES, :] = tail[:, cols]
        scr_ref[lead0 + g, SUBLANES:SUBLANES + t, :] = x[:, cols]
    for g in range(c // LANES):
        cols = slice(g * LANES, (g + 1) * LANES)
        out = x[:, cols] * w[k_width - 1:k_width, cols] + b[:, cols]
        for k in range(k_width - 1):
            shifted = _rows(scr_ref, lead0 + g, SUBLANES - (k_width - 1 - k), t)
            out = out + shifted * w[k:k + 1, cols]
        outs.append(out)
    return outs


def _mod_kernel(c_ref, w_ref, b_ref, o_ref):
    c = c_ref[...]
    c_act = c * jax.nn.sigmoid(c)
    o_ref[...] = _dot(c_act.astype(BF16), w_ref[...].astype(BF16)) + b_ref[...]


def _modulation(c_pad, w_ada, b_ada):
    rows, d = c_pad.shape
    n = w_ada.shape[1]
    bn = d
    return pl.pallas_call(
        _mod_kernel,
        grid=(n // bn,),
        in_specs=[
            pl.BlockSpec((rows, d), lambda j: (0, 0)),
            pl.BlockSpec((d, bn), lambda j: (0, j)),
            pl.BlockSpec((1, bn), lambda j: (0, j)),
        ],
        out_specs=pl.BlockSpec((rows, bn), lambda j: (0, j)),
        out_shape=jax.ShapeDtypeStruct((rows, n), F32),
        compiler_params=pltpu.CompilerParams(
            dimension_semantics=("arbitrary",), vmem_limit_bytes=VMEM_LIMIT),
        name="adaln_mod",
    )(c_pad, w_ada, b_ada)


def _mixer_kernel(x_ref, mod_ref, gpre_ref, gpost_ref, win_ref, cw_ref, cb_ref,
                  wgate_ref, bgate_ref, lrua_ref, vng_ref, vnb_ref, wsp_ref, bsp_ref,
                  glru_ref, ggmlp_ref, wout_ref, o_ref, *scratch):
    nb_tiles = x_ref.shape[0]
    tail_ref, state_ref = scratch[4 * nb_tiles:]

    @pl.when(pl.program_id(1) == 0)
    def _():
        tail_ref[...] = jnp.zeros_like(tail_ref)
        state_ref[...] = jnp.zeros_like(state_ref)

    project = lambda k: _mix_in(x_ref.at[k], mod_ref.at[k], gpre_ref, win_ref)
    z = project(0)
    for k in range(nb_tiles):
        conv_scr, a_scr, b_scr, h_scr = scratch[k:4 * nb_tiles:nb_tiles]
        xc, gates = _mix_gates(z[0], cw_ref, cb_ref, wgate_ref, conv_scr, tail_ref.at[k])
        sp_groups = _mix_positions(z[3], vng_ref, vnb_ref, wsp_ref)
        z_next = project(k + 1) if k + 1 < nb_tiles else None
        hooks = [_zero_row_from(part) for part in z_next] if z_next else [None] * len(z)
        y = _mix_body(z, xc, gates, sp_groups, hooks, bgate_ref, lrua_ref, bsp_ref,
                      glru_ref, ggmlp_ref, a_scr, b_scr, h_scr, state_ref.at[k])
        _mix_out(y, x_ref.at[k], mod_ref.at[k], gpost_ref, wout_ref, o_ref.at[k])
        z = z_next


def _mix_in(x_ref, mod_ref, gpre_ref, win_ref):
    d = x_ref.shape[1]
    mod = mod_ref[...]
    sh_m = mod[:, 0:d]
    sc_m = mod[:, d:2 * d]
    hb = (_rms(x_ref[...]) * (gpre_ref[...] * (1.0 + sc_m)) + sh_m).astype(BF16)
    widths = (LRU_WIDTH, LRU_WIDTH, GMLP_WIDTH, GMLP_WIDTH)
    starts = [sum(widths[:q]) for q in range(len(widths))]
    return [_dot(hb, _as_bf16(win_ref[:, c0:c0 + w])) for c0, w in zip(starts, widths)]


def _mix_gates(lru_x, cw_ref, cb_ref, wgate_ref, conv_scr, tail_ref):
    t = lru_x.shape[0]
    tail = tail_ref[...]
    tail_ref[...] = lru_x[t - SUBLANES:]
    xc = jnp.concatenate(
        _conv_via_scratch(conv_scr, 0, tail, lru_x, cw_ref[...], cb_ref[...]), axis=1)
    xcb = xc.astype(BF16)
    gates = [_dot(xcb[:, j * GATE_TILE:(j + 1) * GATE_TILE], _as_bf16(wgate_ref[j]))
             for j in range(LRU_WIDTH // GATE_TILE)]
    return xc, gates


def _mix_out(y, x_ref, mod_ref, gpost_ref, wout_ref, o_ref):
    d = x_ref.shape[1]
    gt_m = mod_ref[...][:, 2 * d:3 * d]
    y = _dot(y, _as_bf16(wout_ref[...]))
    o_ref[...] = x_ref[...] + (gt_m * gpost_ref[...]) * _rms(y)


def _mix_positions(g_v, vng_ref, vnb_ref, wsp_ref):
    t = g_v.shape[0]
    gv = _gelu(g_v)
    mu = jnp.mean(gv, axis=-1, keepdims=True)
    cen = gv - mu
    var = jnp.mean(cen * cen, axis=-1, keepdims=True)
    v = cen * lax.rsqrt(var + EPS) * vng_ref[...] + vnb_ref[...]
    vb = v.astype(BF16)
    nb = t // GMLP_BLOCK
    pi = lax.broadcasted_iota(jnp.int32, (GMLP_BLOCK, GMLP_BLOCK), 0) // CHUNK
    pj = lax.broadcasted_iota(jnp.int32, (GMLP_BLOCK, GMLP_BLOCK), 1) // CHUNK
    mask = pj <= pi
    sp_groups = []
    for g in range(GMLP_GROUPS):
        ws = jnp.where(mask, 0.5 * wsp_ref[g], 0.0).astype(BF16)
        cols = slice(g * GMLP_GROUP_DIM, (g + 1) * GMLP_GROUP_DIM)
        rhs = jnp.concatenate(
            [vb[n * GMLP_BLOCK:(n + 1) * GMLP_BLOCK, cols] for n in range(nb)], axis=1)
        sp_groups.append(_dot(ws, rhs))
    return sp_groups


def _mix_body(z, xc, gates, sp_groups, hooks, bgate_ref, lrua_ref, bsp_ref,
              glru_ref, ggmlp_ref, a_scr, b_scr, h_scr, state_ref):
    _, lru_gate, g_u, _ = z
    t = lru_gate.shape[0]
    slabs = lambda v: [v[:, c * LANES:(c + 1) * LANES] for c in range(v.shape[1] // LANES)]
    hooked = lambda v, hook: v if hook is None else v + hook

    r_pre = jnp.concatenate([g[:, :GATE_TILE] for g in gates], axis=1)
    i_pre = jnp.concatenate([g[:, GATE_TILE:] for g in gates], axis=1)
    bgate = bgate_ref[...]
    r_gate = jax.nn.sigmoid(r_pre + bgate[:, :LRU_WIDTH])
    i_gate = jax.nn.sigmoid(i_pre + bgate[:, LRU_WIDTH:])
    neg_a = -lrua_ref[...]
    softplus = jnp.maximum(neg_a, 0.0) + jnp.log1p(jnp.exp(-jnp.abs(neg_a)))
    log_a = r_gate * ((-LRU_C) * softplus)
    a = jnp.exp(log_a)
    th = jnp.tanh(log_a)
    q = (-2.0 * th) / (1.0 - th)
    mult = jnp.where(q > 0.0, q * lax.rsqrt(q), 0.0)
    bx = (mult * xc) * i_gate
    hs = _linear_scan(slabs(a), slabs(bx), state_ref, a_scr, b_scr, h_scr, h0_add=hooks[0])
    y_lru = 0.5 * _gelu2_times(lru_gate, jnp.concatenate(hs, axis=1))
    yl = _rms(y_lru) * hooked(glru_ref[...], hooks[1])

    nb = t // GMLP_BLOCK
    bsp = hooked(0.5 * bsp_ref[...], hooks[2])
    sp_rows = []
    for n in range(nb):
        blk = jnp.concatenate(
            [sg[:, n * GMLP_GROUP_DIM:(n + 1) * GMLP_GROUP_DIM] for sg in sp_groups], axis=1)
        sp_rows.append(blk + bsp)
    sp_half = jnp.concatenate(sp_rows, axis=0)
    y_gmlp = _gelu2_times(g_u, sp_half)
    yg = _rms(y_gmlp) * hooked(ggmlp_ref[...], hooks[3])

    return jnp.concatenate([yl, yg], axis=1).astype(BF16)


def _mixer(x, mod3, g_pre, g_post, w_in, conv_w, conv_b, w_gate, b_gate, lru_a,
           v_norm_g, v_norm_b, w_spatial, b_sp_full, g_lru_out, g_gmlp_out, w_out):
    bsz, s, d = x.shape
    t = MIX_T
    nb = MIX_NB
    assert bsz % nb == 0 and s % t == 0
    n_lg = LRU_WIDTH // LANES
    scan_rows = SUBLANES * (t // SUBLANES + 1)
    full = lambda a: pl.BlockSpec(a.shape, lambda b, i: (0,) * a.ndim)
    in_arrays = [g_pre, g_post, w_in, conv_w, conv_b, w_gate, b_gate, lru_a,
                 v_norm_g, v_norm_b, w_spatial, b_sp_full, g_lru_out, g_gmlp_out, w_out]
    return pl.pallas_call(
        _mixer_kernel,
        grid=(bsz // nb, s // t),
        in_specs=[
            pl.BlockSpec((nb, t, d), lambda b, i: (b, i, 0)),
            pl.BlockSpec((nb, 1, mod3.shape[2]), lambda b, i: (b, 0, 0)),
        ] + [full(a) for a in in_arrays],
        out_specs=pl.BlockSpec((nb, t, d), lambda b, i: (b, i, 0)),
        out_shape=jax.ShapeDtypeStruct(x.shape, x.dtype),
        scratch_shapes=(
            [pltpu.VMEM((n_lg, SUBLANES + t, LANES), F32)] * nb
            + [pltpu.VMEM((n_lg, scan_rows, LANES), F32)] * (3 * nb)
            + [pltpu.VMEM((nb, SUBLANES, LRU_WIDTH), F32)] * 2),
        compiler_params=pltpu.CompilerParams(
            dimension_semantics=("arbitrary", "arbitrary"), vmem_limit_bytes=VMEM_LIMIT),
        name="token_mixer",
    )(x, mod3, *in_arrays)


def _ffn_kernel(x_ref, mod_ref, gpre_ref, gpost_ref, wup_ref, cw_ref, cb_ref, wd_ref,
                o_ref, scr_ref, tail_ref):
    t_idx = pl.program_id(1)
    t, d = x_ref.shape
    d_ff = wd_ref.shape[0]
    fc = FFN_FC
    lg = fc // LANES

    @pl.when(t_idx == 0)
    def _():
        tail_ref[...] = jnp.zeros_like(tail_ref)

    mod = mod_ref[...]
    sh_f = mod[:, 3 * d:4 * d]
    sc_f = mod[:, 4 * d:5 * d]
    gt_f = mod[:, 5 * d:6 * d]
    x = x_ref[...]
    hb = (_rms(x) * (gpre_ref[...] * (1.0 + sc_f)) + sh_f).astype(BF16)

    n_chunks = d_ff // fc
    halves = ((0, 1.0), (d_ff, 0.5))
    chunk_cols = lambda base, j: slice(base + j * fc, base + (j + 1) * fc)
    up_proj = lambda j: [_dot(hb, wup_ref[:, chunk_cols(base, j)].astype(BF16)) for base, _ in halves]

    def gated(j, ups):
        conv = []
        for k, (base, scale) in enumerate(halves):
            cols = chunk_cols(base, j)
            tail = tail_ref[:, cols]
            tail_ref[:, cols] = ups[k][t - SUBLANES:]
            slot = (2 * j + k) % FFN_SLOTS
            conv.append(_conv_via_scratch(scr_ref, slot * lg, tail, ups[k],
                                          cw_ref[:, cols] * scale, cb_ref[:, cols] * scale))
        return jnp.concatenate(
            [_gelu2_times(cg, cv) for cg, cv in zip(*conv)], axis=1).astype(BF16)

    acc = None
    ups = up_proj(0)
    for j in range(n_chunks):
        ups_next = up_proj(j + 1) if j + 1 < n_chunks else None
        part = _dot(gated(j, ups), wd_ref[j * fc:(j + 1) * fc, :].astype(BF16))
        acc = part if acc is None else acc + part
        ups = ups_next
    o_ref[...] = x + (gt_f * gpost_ref[...]) * _rms(acc)


def _ffn(x, mod3, g_pre, g_post, w_up, conv_w, conv_b, w_down):
    bsz, s, d = x.shape
    t = FFN_T
    resident = lambda a: pl.BlockSpec(a.shape, lambda b, i: (0,) * a.ndim,
                                      pipeline_mode=pl.Buffered(1))
    return pl.pallas_call(
        _ffn_kernel,
        grid=(bsz, s // t),
        in_specs=[
            pl.BlockSpec((None, t, d), lambda b, i: (b, i, 0)),
            pl.BlockSpec((None, 1, mod3.shape[2]), lambda b, i: (b, 0, 0)),
            resident(g_pre), resident(g_post), resident(w_up), resident(conv_w),
            resident(conv_b), resident(w_down),
        ],
        out_specs=pl.BlockSpec((None, t, d), lambda b, i: (b, i, 0)),
        out_shape=jax.ShapeDtypeStruct(x.shape, x.dtype),
        scratch_shapes=[
            pltpu.VMEM((FFN_SLOTS * (FFN_FC // LANES), SUBLANES + t, LANES), F32),
            pltpu.VMEM((SUBLANES, w_up.shape[1]), F32),
        ],
        compiler_params=pltpu.CompilerParams(
            dimension_semantics=("arbitrary", "arbitrary"), vmem_limit_bytes=VMEM_LIMIT),
        name="conv_ffn",
    )(x, mod3, g_pre, g_post, w_up, conv_w, conv_b, w_down)


def _gate_weights(w_rgate, w_igate):
    heads_per_tile = GATE_TILE // LRU_HEAD_DIM
    tiles = []
    for j in range(LRU_WIDTH // GATE_TILE):
        hs = slice(j * heads_per_tile, (j + 1) * heads_per_tile)
        r_bd = jax.scipy.linalg.block_diag(*w_rgate[hs])
        i_bd = jax.scipy.linalg.block_diag(*w_igate[hs])
        tiles.append(jnp.concatenate([r_bd, i_bd], axis=1))
    return _pack_rows_xla(jnp.stack(tiles))


def kernel(x, c, w_ada, b_ada, g_mix_pre, g_mix_post, w_in, conv_w, conv_b, w_rgate, b_rgate, w_igate, b_igate, lru_a, v_norm_g, v_norm_b, w_spatial, b_spatial, g_lru_out, g_gmlp_out, w_out, g_ffn_pre, g_ffn_post, w_up, ffn_conv_w, ffn_conv_b, w_down):
    depth = w_ada.shape[0]
    bsz, s, d = x.shape
    c_pad = jnp.pad(c, ((0, SUBLANES - bsz % SUBLANES if bsz % SUBLANES else 0), (0, 0)))
    for l in range(depth):
        mod = _modulation(c_pad, w_ada[l], b_ada[l][None, :])
        mod3 = mod.reshape(mod.shape[0], 1, mod.shape[1])
        row = lambda a: a[l][None, :]
        w_gate = _gate_weights(w_rgate[l], w_igate[l])
        b_gate = jnp.concatenate([b_rgate[l].reshape(1, -1), b_igate[l].reshape(1, -1)], axis=1)
        b_sp_full = jnp.repeat(b_spatial[l].T, GMLP_GROUP_DIM, axis=1)
        x = _mixer(x, mod3, row(g_mix_pre), row(g_mix_post), _pack_rows(w_in[l]),
                   conv_w[l], row(conv_b), w_gate, b_gate, row(lru_a),
                   row(v_norm_g), row(v_norm_b), w_spatial[l], b_sp_full,
                   row(g_lru_out), row(g_gmlp_out), _pack_rows(w_out[l]))
        x = _ffn(x, mod3, row(g_ffn_pre), row(g_ffn_post), w_up[l],
                 ffn_conv_w[l], row(ffn_conv_b), w_down[l])
    return x
```

```python
import jax
import jax.numpy as jnp
from jax import lax
from jax.experimental import pallas as pl
from jax.experimental.pallas import tpu as pltpu

CHUNK = 64
LRU_WIDTH = 512
LRU_HEADS = 8
LRU_HEAD_DIM = LRU_WIDTH // LRU_HEADS
LRU_CONV_WIDTH = 4
LRU_C = 8.0
GMLP_WIDTH = 512
GMLP_GROUPS = 4
GMLP_GROUP_DIM = GMLP_WIDTH // GMLP_GROUPS
GMLP_BLOCK = 128
FFN_CONV_WIDTH = 3
N_MOD = 6
EPS = 1e-6

SUBLANES = 8
LANES = 128
GATE_TILE = 256

MIX_T = 256
MIX_NB = 4
FFN_T = 512
FFN_FC = 512
FFN_SLOTS = 4
PACK_BLOCK_BYTES = 4 * 1024 * 1024
VMEM_LIMIT = 56 * 1024 * 1024

F32 = jnp.float32
BF16 = jnp.bfloat16
U32 = jnp.uint32


def _dot(a, b):
    return jnp.dot(a, b, preferred_element_type=F32)


def _pack_rows_xla(w):
    k, n = w.shape[-2:]
    wb = w.astype(BF16).reshape(*w.shape[:-2], k // 2, 2, n)
    return lax.bitcast_convert_type(jnp.swapaxes(wb, -1, -2), U32)


def _pack_kernel(w_ref, o_ref):
    o_ref[...] = pltpu.bitcast(w_ref[...].astype(BF16), U32)


def _pack_rows(w):
    k, n = w.shape
    bk = k
    while bk * n * 4 > PACK_BLOCK_BYTES and bk % (4 * SUBLANES) == 0:
        bk //= 2
    assert k % bk == 0
    return pl.pallas_call(
        _pack_kernel,
        grid=(k // bk,),
        in_specs=[pl.BlockSpec((bk, n), lambda i: (i, 0))],
        out_specs=pl.BlockSpec((bk // 2, n), lambda i: (i, 0)),
        out_shape=jax.ShapeDtypeStruct((k // 2, n), U32),
        compiler_params=pltpu.CompilerParams(
            dimension_semantics=("arbitrary",), vmem_limit_bytes=VMEM_LIMIT),
        name="pack_weight",
    )(w)


def _as_bf16(packed):
    return pltpu.bitcast(packed, BF16)


def _zero_row_from(x):
    last = lax.bitcast_convert_type(x[x.shape[0] - SUBLANES:], U32)
    zero = lax.shift_right_logical(lax.shift_right_logical(last, U32(16)), U32(16))
    return zero[0:1].astype(F32)


def _rms(x):
    return x * lax.rsqrt(jnp.mean(x * x, axis=-1, keepdims=True) + EPS)


def _gelu(x):
    c0 = 0.7978845608028654
    hx = 0.5 * x
    return hx + hx * jnp.tanh(x * (c0 + (c0 * 0.044715) * (x * x)))


def _gelu2_times(x, v):
    c0 = 0.7978845608028654
    z = x * (c0 + (c0 * 0.044715) * (x * x))
    return (x * v) * (1.0 + jnp.tanh(z))


def _rows(ref, lead, start, size):
    return ref[pl.ds(lead, 1, stride=2), pl.ds(start, size), :][0]


def _put_rows(ref, lead, start, val):
    ref[pl.ds(lead, 1, stride=2), pl.ds(start, val.shape[0]), :] = val[None]


def _linear_scan(a, b, state_ref, a_scr, b_scr, h_scr, h0_add=None):
    t = a[0].shape[0]
    seg = t // SUBLANES
    pitch = seg + 1 - seg % 2
    n = len(a)
    for c in range(n):
        for s in range(SUBLANES):
            _put_rows(a_scr, c, s * pitch, a[c][s * seg:(s + 1) * seg])
            _put_rows(b_scr, c, s * pitch, b[c][s * seg:(s + 1) * seg])
    step = lambda ref, c, j: ref[c, pl.ds(j, SUBLANES, stride=pitch), :]
    row = lax.broadcasted_iota(jnp.int32, (SUBLANES, LANES), 0)
    shifted = lambda v, sh, fill: jnp.where(row >= sh, pltpu.roll(v, sh, axis=0), fill)

    prod = [None] * n
    end = [None] * n
    for j in range(seg):
        for c in range(n):
            aj, bj = step(a_scr, c, j), step(b_scr, c, j)
            prod[c] = aj if j == 0 else aj * prod[c]
            end[c] = bj if j == 0 else aj * end[c] + bj

    h = []
    for c in range(n):
        cols = slice(c * LANES, (c + 1) * LANES)
        h0 = state_ref[SUBLANES - 1:SUBLANES, cols]
        if h0_add is not None:
            h0 = h0 + h0_add[:, cols]
        p = prod[c]
        e = end[c] + jnp.where(row == 0, p * h0, 0.0)
        for sh in (1, 2, 4):
            e = e + p * shifted(e, sh, 0.0)
            if sh < 4:
                p = p * shifted(p, sh, 1.0)
        state_ref[:, cols] = e
        h.append(shifted(e, 1, h0))
    for j in range(seg):
        for c in range(n):
            h[c] = step(a_scr, c, j) * h[c] + step(b_scr, c, j)
            h_scr[c, pl.ds(j, SUBLANES, stride=pitch), :] = h[c]
    return [jnp.concatenate([_rows(h_scr, c, s * pitch, seg) for s in range(SUBLANES)], axis=0)
            for c in range(n)]


def _conv_via_scratch(scr_ref, lead0, tail, x, w, b):
    k_width = w.shape[0]
    t, c = x.shape
    outs = []
    for g in range(c // LANES):
        cols = slice(g * LANES, (g + 1) * LANES)
        scr_ref[lead0 + g, 0:SUBLANES, :] = tail[:, cols]
        scr_ref[lead0 + g, SUBLANES:SUBLANES + t, :] = x[:, cols]
    for g in range(c // LANES):
        cols = slice(g * LANES, (g + 1) * LANES)
        out = x[:, cols] * w[k_width - 1:k_width, cols] + b[:, cols]
        for k in range(k_width - 1):
            shifted = _rows(scr_ref, lead0 + g, SUBLANES - (k_width - 1 - k), t)
            out = out + shifted * w[k:k + 1, cols]
        outs.append(out)
    return outs


def _mod_kernel(c_ref, w_ref, b_ref, o_ref):
    c = c_ref[...]
    rows = c.shape[0]
    c_act = c * jax.nn.sigmoid(c)
    pad = (-rows) % SUBLANES
    if pad:
        c_act = jnp.concatenate([c_act, jnp.zeros((pad, c.shape[1]), F32)], axis=0)
    mod = _dot(c_act.astype(BF16), w_ref[...].astype(BF16))
    o_ref[...] = mod[:rows] + b_ref[...]


def _modulation(c, w_ada, b_ada):
    rows, d = c.shape
    n = w_ada.shape[1]
    bn = d
    return pl.pallas_call(
        _mod_kernel,
        grid=(n // bn,),
        in_specs=[
            pl.BlockSpec((rows, d), lambda j: (0, 0)),
            pl.BlockSpec((d, bn), lambda j: (0, j)),
            pl.BlockSpec((1, bn), lambda j: (0, j)),
        ],
        out_specs=pl.BlockSpec((rows, bn), lambda j: (0, j)),
        out_shape=jax.ShapeDtypeStruct((rows, n), F32),
        compiler_params=pltpu.CompilerParams(
            dimension_semantics=("arbitrary",), vmem_limit_bytes=VMEM_LIMIT),
        name="adaln_mod",
    )(c, w_ada, b_ada)


def _mixer_kernel(x_ref, mod_ref, gpre_ref, gpost_ref, win_ref, cw_ref, cb_ref,
                  wgate_ref, brg_ref, big_ref, lrua_ref, vng_ref, vnb_ref, wsp_ref, bsp_ref,
                  glru_ref, ggmlp_ref, wout_ref, o_ref, *scratch):
    nb_tiles = x_ref.shape[0]
    tail_ref, state_ref = scratch[4 * nb_tiles:]

    @pl.when(pl.program_id(1) == 0)
    def _():
        tail_ref[...] = jnp.zeros_like(tail_ref)
        state_ref[...] = jnp.zeros_like(state_ref)

    project = lambda k: _mix_in(x_ref.at[k], mod_ref.at[k], gpre_ref, win_ref)
    z = project(0)
    for k in range(nb_tiles):
        conv_scr, a_scr, b_scr, h_scr = scratch[k:4 * nb_tiles:nb_tiles]
        xc, gates = _mix_gates(z[0], cw_ref, cb_ref, wgate_ref, conv_scr, tail_ref.at[k])
        sp_groups = _mix_positions(z[3], vng_ref, vnb_ref, wsp_ref)
        z_next = project(k + 1) if k + 1 < nb_tiles else None
        hooks = [_zero_row_from(part) for part in z_next] if z_next else [None] * len(z)
        y = _mix_body(z, xc, gates, sp_groups, hooks, brg_ref, big_ref, lrua_ref, bsp_ref,
                      glru_ref, ggmlp_ref, a_scr, b_scr, h_scr, state_ref.at[k])
        _mix_out(y, x_ref.at[k], mod_ref.at[k], gpost_ref, wout_ref, o_ref.at[k])
        z = z_next


def _mix_in(x_ref, mod_ref, gpre_ref, win_ref):
    d = x_ref.shape[1]
    mod = mod_ref[...]
    sh_m = mod[:, 0:d]
    sc_m = mod[:, d:2 * d]
    hb = (_rms(x_ref[...]) * (gpre_ref[...] * (1.0 + sc_m)) + sh_m).astype(BF16)
    widths = (LRU_WIDTH, LRU_WIDTH, GMLP_WIDTH, GMLP_WIDTH)
    starts = [sum(widths[:q]) for q in range(len(widths))]
    return [_dot(hb, _as_bf16(win_ref[:, c0:c0 + w])) for c0, w in zip(starts, widths)]


def _mix_gates(lru_x, cw_ref, cb_ref, wgate_ref, conv_scr, tail_ref):
    t = lru_x.shape[0]
    tail = tail_ref[...]
    tail_ref[...] = lru_x[t - SUBLANES:]
    xc = jnp.concatenate(
        _conv_via_scratch(conv_scr, 0, tail, lru_x, cw_ref[...], cb_ref[...]), axis=1)
    xcb = xc.astype(BF16)
    gates = [_dot(xcb[:, j * GATE_TILE:(j + 1) * GATE_TILE], _as_bf16(wgate_ref[j]))
             for j in range(LRU_WIDTH // GATE_TILE)]
    return xc, gates


def _mix_out(y, x_ref, mod_ref, gpost_ref, wout_ref, o_ref):
    d = x_ref.shape[1]
    gt_m = mod_ref[...][:, 2 * d:3 * d]
    y = _dot(y, _as_bf16(wout_ref[...]))
    o_ref[...] = x_ref[...] + (gt_m * gpost_ref[...]) * _rms(y)


def _mix_positions(g_v, vng_ref, vnb_ref, wsp_ref):
    t = g_v.shape[0]
    gv = _gelu(g_v)
    mu = jnp.mean(gv, axis=-1, keepdims=True)
    cen = gv - mu
    var = jnp.mean(cen * cen, axis=-1, keepdims=True)
    v = cen * lax.rsqrt(var + EPS) * vng_ref[...] + vnb_ref[...]
    vb = v.astype(BF16)
    nb = t // GMLP_BLOCK
    pi = lax.broadcasted_iota(jnp.int32, (GMLP_BLOCK, GMLP_BLOCK), 0) // CHUNK
    pj = lax.broadcasted_iota(jnp.int32, (GMLP_BLOCK, GMLP_BLOCK), 1) // CHUNK
    mask = pj <= pi
    sp_groups = []
    for g in range(GMLP_GROUPS):
        ws = jnp.where(mask, 0.5 * wsp_ref[g], 0.0).astype(BF16)
        cols = slice(g * GMLP_GROUP_DIM, (g + 1) * GMLP_GROUP_DIM)
        rhs = jnp.concatenate(
            [vb[n * GMLP_BLOCK:(n + 1) * GMLP_BLOCK, cols] for n in range(nb)], axis=1)
        sp_groups.append(_dot(ws, rhs))
    return sp_groups


def _mix_body(z, xc, gates, sp_groups, hooks, brg_ref, big_ref, lrua_ref, bsp_ref,
              glru_ref, ggmlp_ref, a_scr, b_scr, h_scr, state_ref):
    _, lru_gate, g_u, _ = z
    t = lru_gate.shape[0]
    slabs = lambda v: [v[:, c * LANES:(c + 1) * LANES] for c in range(v.shape[1] // LANES)]
    hooked = lambda v, hook: v if hook is None else v + hook

    r_pre = jnp.concatenate([g[:, :GATE_TILE] for g in gates], axis=1)
    i_pre = jnp.concatenate([g[:, GATE_TILE:] for g in gates], axis=1)
    r_gate = jax.nn.sigmoid(r_pre + brg_ref[...])
    i_gate = jax.nn.sigmoid(i_pre + big_ref[...])
    neg_a = -lrua_ref[...]
    softplus = jnp.maximum(neg_a, 0.0) + jnp.log1p(jnp.exp(-jnp.abs(neg_a)))
    log_a = r_gate * ((-LRU_C) * softplus)
    a = jnp.exp(log_a)
    th = jnp.tanh(log_a)
    q = (-2.0 * th) / (1.0 - th)
    mult = jnp.where(q > 0.0, q * lax.rsqrt(q), 0.0)
    bx = (mult * xc) * i_gate
    hs = _linear_scan(slabs(a), slabs(bx), state_ref, a_scr, b_scr, h_scr, h0_add=hooks[0])
    y_lru = 0.5 * _gelu2_times(lru_gate, jnp.concatenate(hs, axis=1))
    yl = _rms(y_lru) * hooked(glru_ref[...], hooks[1])

    nb = t // GMLP_BLOCK
    bsp = hooked(0.5 * bsp_ref[...], hooks[2])
    sp_rows = []
    for n in range(nb):
        blk = jnp.concatenate(
            [sg[:, n * GMLP_GROUP_DIM:(n + 1) * GMLP_GROUP_DIM] for sg in sp_groups], axis=1)
        sp_rows.append(blk + bsp)
    sp_half = jnp.concatenate(sp_rows, axis=0)
    y_gmlp = _gelu2_times(g_u, sp_half)
    yg = _rms(y_gmlp) * hooked(ggmlp_ref[...], hooks[3])

    return jnp.concatenate([yl, yg], axis=1).astype(BF16)


def _mixer(x, mod3, g_pre, g_post, w_in, conv_w, conv_b, w_gate, b_rgate, b_igate, lru_a,
           v_norm_g, v_norm_b, w_spatial, b_sp_full, g_lru_out, g_gmlp_out, w_out):
    bsz, s, d = x.shape
    t = MIX_T
    nb = MIX_NB
    assert bsz % nb == 0 and s % t == 0
    n_lg = LRU_WIDTH // LANES
    scan_rows = SUBLANES * (t // SUBLANES + 1)
    full = lambda a: pl.BlockSpec(a.shape, lambda b, i: (0,) * a.ndim)
    in_arrays = [g_pre, g_post, w_in, conv_w, conv_b, w_gate, b_rgate, b_igate, lru_a,
                 v_norm_g, v_norm_b, w_spatial, b_sp_full, g_lru_out, g_gmlp_out, w_out]
    return pl.pallas_call(
        _mixer_kernel,
        grid=(bsz // nb, s // t),
        in_specs=[
            pl.BlockSpec((nb, t, d), lambda b, i: (b, i, 0)),
            pl.BlockSpec((nb, 1, mod3.shape[2]), lambda b, i: (b, 0, 0)),
        ] + [full(a) for a in in_arrays],
        out_specs=pl.BlockSpec((nb, t, d), lambda b, i: (b, i, 0)),
        out_shape=jax.ShapeDtypeStruct(x.shape, x.dtype),
        scratch_shapes=(
            [pltpu.VMEM((n_lg, SUBLANES + t, LANES), F32)] * nb
            + [pltpu.VMEM((n_lg, scan_rows, LANES), F32)] * (3 * nb)
            + [pltpu.VMEM((nb, SUBLANES, LRU_WIDTH), F32)] * 2),
        compiler_params=pltpu.CompilerParams(
            dimension_semantics=("arbitrary", "arbitrary"), vmem_limit_bytes=VMEM_LIMIT),
        name="token_mixer",
    )(x, mod3, *in_arrays)


def _ffn_kernel(x_ref, mod_ref, gpre_ref, gpost_ref, wup_ref, cw_ref, cb_ref, wd_ref,
                o_ref, scr_ref, tail_ref):
    t_idx = pl.program_id(1)
    t, d = x_ref.shape
    d_ff = 2 * wd_ref.shape[0]
    fc = FFN_FC
    lg = fc // LANES

    @pl.when(t_idx == 0)
    def _():
        tail_ref[...] = jnp.zeros_like(tail_ref)

    mod = mod_ref[...]
    sh_f = mod[:, 3 * d:4 * d]
    sc_f = mod[:, 4 * d:5 * d]
    gt_f = mod[:, 5 * d:6 * d]
    x = x_ref[...]
    hb = (_rms(x) * (gpre_ref[...] * (1.0 + sc_f)) + sh_f).astype(BF16)

    n_chunks = d_ff // fc
    halves = ((0, 1.0), (d_ff, 0.5))
    chunk_cols = lambda base, j: slice(base + j * fc, base + (j + 1) * fc)
    up_proj = lambda j: [_dot(hb, _as_bf16(wup_ref[:, chunk_cols(base, j)])) for base, _ in halves]

    def gated(j, ups):
        conv = []
        for k, (base, scale) in enumerate(halves):
            cols = chunk_cols(base, j)
            tail = tail_ref[:, cols]
            tail_ref[:, cols] = ups[k][t - SUBLANES:]
            slot = (2 * j + k) % FFN_SLOTS
            conv.append(_conv_via_scratch(scr_ref, slot * lg, tail, ups[k],
                                          cw_ref[:, cols] * scale, cb_ref[:, cols] * scale))
        return jnp.concatenate(
            [_gelu2_times(cg, cv) for cg, cv in zip(*conv)], axis=1).astype(BF16)

    acc = None
    ups = up_proj(0)
    for j in range(n_chunks):
        ups_next = up_proj(j + 1) if j + 1 < n_chunks else None
        part = _dot(gated(j, ups), _as_bf16(wd_ref[j * fc // 2:(j + 1) * fc // 2, :]))
        acc = part if acc is None else acc + part
        ups = ups_next
    o_ref[...] = x + (gt_f * gpost_ref[...]) * _rms(acc)


def _ffn(x, mod3, g_pre, g_post, w_up, conv_w, conv_b, w_down):
    bsz, s, d = x.shape
    t = FFN_T
    resident = lambda a: pl.BlockSpec(a.shape, lambda b, i: (0,) * a.ndim,
                                      pipeline_mode=pl.Buffered(1))
    return pl.pallas_call(
        _ffn_kernel,
        grid=(bsz, s // t),
        in_specs=[
            pl.BlockSpec((None, t, d), lambda b, i: (b, i, 0)),
            pl.BlockSpec((None, 1, mod3.shape[2]), lambda b, i: (b, 0, 0)),
            resident(g_pre), resident(g_post), resident(w_up), resident(conv_w),
            resident(conv_b), resident(w_down),
        ],
        out_specs=pl.BlockSpec((None, t, d), lambda b, i: (b, i, 0)),
        out_shape=jax.ShapeDtypeStruct(x.shape, x.dtype),
        scratch_shapes=[
            pltpu.VMEM((FFN_SLOTS * (FFN_FC // LANES), SUBLANES + t, LANES), F32),
            pltpu.VMEM((SUBLANES, w_up.shape[1]), F32),
        ],
        compiler_params=pltpu.CompilerParams(
            dimension_semantics=("arbitrary", "arbitrary"), vmem_limit_bytes=VMEM_LIMIT),
        name="conv_ffn",
    )(x, mod3, g_pre, g_post, w_up, conv_w, conv_b, w_down)


def _gate_weights(w_rgate, w_igate):
    per_tile = GATE_TILE // LRU_HEAD_DIM
    n_tiles = LRU_WIDTH // GATE_TILE
    eye = jnp.eye(per_tile, dtype=F32)

    def block_diag(w):
        w4 = w.reshape(n_tiles, per_tile, LRU_HEAD_DIM, LRU_HEAD_DIM)
        return (w4[:, :, :, None, :] * eye[None, :, None, :, None]).reshape(
            n_tiles, GATE_TILE, GATE_TILE)

    return _pack_rows_xla(jnp.concatenate([block_diag(w_rgate), block_diag(w_igate)], axis=2))


def kernel(x, c, w_ada, b_ada, g_mix_pre, g_mix_post, w_in, conv_w, conv_b, w_rgate, b_rgate, w_igate, b_igate, lru_a, v_norm_g, v_norm_b, w_spatial, b_spatial, g_lru_out, g_gmlp_out, w_out, g_ffn_pre, g_ffn_post, w_up, ffn_conv_w, ffn_conv_b, w_down):
    depth = w_ada.shape[0]
    bsz, s, d = x.shape
    for l in range(depth):
        mod = _modulation(c, w_ada[l], b_ada[l][None, :])
        mod3 = mod.reshape(mod.shape[0], 1, mod.shape[1])
        row = lambda a: a[l][None, :]
        w_gate = _gate_weights(w_rgate[l], w_igate[l])
        b_gate = (b_rgate[l].reshape(1, -1), b_igate[l].reshape(1, -1))
        b_sp_full = jnp.repeat(b_spatial[l].T, GMLP_GROUP_DIM, axis=1)
        x = _mixer(x, mod3, row(g_mix_pre), row(g_mix_post), _pack_rows(w_in[l]),
                   conv_w[l], row(conv_b), w_gate, *b_gate, row(lru_a),
                   row(v_norm_g), row(v_norm_b), w_spatial[l], b_sp_full,
                   row(g_lru_out), row(g_gmlp_out), _pack_rows(w_out[l]))
        x = _ffn(x, mod3, row(g_ffn_pre), row(g_ffn_post), _pack_rows(w_up[l]),
                 ffn_conv_w[l], row(ffn_conv_b), _pack_rows(w_down[l]))
    return x
```

```python
import jax
import jax.numpy as jnp
from jax import lax
from jax.experimental import pallas as pl
from jax.experimental.pallas import tpu as pltpu

CHUNK = 64
LRU_WIDTH = 512
LRU_HEADS = 8
LRU_HEAD_DIM = LRU_WIDTH // LRU_HEADS
LRU_CONV_WIDTH = 4
LRU_C = 8.0
GMLP_WIDTH = 512
GMLP_GROUPS = 4
GMLP_GROUP_DIM = GMLP_WIDTH // GMLP_GROUPS
GMLP_BLOCK = 128
FFN_CONV_WIDTH = 3
N_MOD = 6
EPS = 1e-6

SUBLANES = 8
LANES = 128
GATE_TILE = 256

MIX_T = 256
MIX_NB = 4
FFN_T = 512
FFN_FC = 512
FFN_SLOTS = 4
PACK_BLOCK_BYTES = 4 * 1024 * 1024
VMEM_LIMIT = 56 * 1024 * 1024

F32 = jnp.float32
BF16 = jnp.bfloat16
U32 = jnp.uint32


def _dot(a, b):
    return jnp.dot(a, b, preferred_element_type=F32)


def _pack_rows_xla(w):
    k, n = w.shape[-2:]
    wb = w.astype(BF16).reshape(*w.shape[:-2], k // 2, 2, n)
    return lax.bitcast_convert_type(jnp.swapaxes(wb, -1, -2), U32)


def _pack_kernel(w_ref, o_ref):
    o_ref[...] = pltpu.bitcast(w_ref[...].astype(BF16), U32)


def _pack_rows(w):
    k, n = w.shape
    bk = k
    while bk * n * 4 > PACK_BLOCK_BYTES and bk % (4 * SUBLANES) == 0:
        bk //= 2
    assert k % bk == 0
    return pl.pallas_call(
        _pack_kernel,
        grid=(k // bk,),
        in_specs=[pl.BlockSpec((bk, n), lambda i: (i, 0))],
        out_specs=pl.BlockSpec((bk // 2, n), lambda i: (i, 0)),
        out_shape=jax.ShapeDtypeStruct((k // 2, n), U32),
        compiler_params=pltpu.CompilerParams(
            dimension_semantics=("arbitrary",), vmem_limit_bytes=VMEM_LIMIT),
        name="pack_weight",
    )(w)


def _as_bf16(packed):
    return pltpu.bitcast(packed, BF16)


def _zero_row_from(x):
    last = lax.bitcast_convert_type(x[x.shape[0] - SUBLANES:], U32)
    zero = lax.shift_right_logical(lax.shift_right_logical(last, U32(16)), U32(16))
    return zero[0:1].astype(F32)


def _rms(x):
    return x * lax.rsqrt(jnp.mean(x * x, axis=-1, keepdims=True) + EPS)


def _gelu(x):
    c0 = 0.7978845608028654
    hx = 0.5 * x
    return hx + hx * jnp.tanh(x * (c0 + (c0 * 0.044715) * (x * x)))


def _gelu2_times(x, v):
    c0 = 0.7978845608028654
    z = x * (c0 + (c0 * 0.044715) * (x * x))
    return (x * v) * (1.0 + jnp.tanh(z))


def _rows(ref, lead, start, size):
    return ref[pl.ds(lead, 1, stride=2), pl.ds(start, size), :][0]


def _put_rows(ref, lead, start, val):
    ref[pl.ds(lead, 1, stride=2), pl.ds(start, val.shape[0]), :] = val[None]


def _linear_scan(a, b, state_ref, a_scr, b_scr, h_scr, h0_add=None):
    t = a[0].shape[0]
    seg = t // SUBLANES
    pitch = seg + 1 - seg % 2
    n = len(a)
    for c in range(n):
        for s in range(SUBLANES):
            _put_rows(a_scr, c, s * pitch, a[c][s * seg:(s + 1) * seg])
            _put_rows(b_scr, c, s * pitch, b[c][s * seg:(s + 1) * seg])
    step = lambda ref, c, j: ref[c, pl.ds(j, SUBLANES, stride=pitch), :]
    row = lax.broadcasted_iota(jnp.int32, (SUBLANES, LANES), 0)
    shifted = lambda v, sh, fill: jnp.where(row >= sh, pltpu.roll(v, sh, axis=0), fill)

    prod = [None] * n
    end = [None] * n
    for j in range(seg):
        for c in range(n):
            aj, bj = step(a_scr, c, j), step(b_scr, c, j)
            prod[c] = aj if j == 0 else aj * prod[c]
            end[c] = bj if j == 0 else aj * end[c] + bj

    h = []
    for c in range(n):
        cols = slice(c * LANES, (c + 1) * LANES)
        h0 = state_ref[SUBLANES - 1:SUBLANES, cols]
        if h0_add is not None:
            h0 = h0 + h0_add[:, cols]
        p = prod[c]
        e = end[c] + jnp.where(row == 0, p * h0, 0.0)
        for sh in (1, 2, 4):
            e = e + p * shifted(e, sh, 0.0)
            if sh < 4:
                p = p * shifted(p, sh, 1.0)
        state_ref[:, cols] = e
        h.append(shifted(e, 1, h0))
    for j in range(seg):
        for c in range(n):
            h[c] = step(a_scr, c, j) * h[c] + step(b_scr, c, j)
            h_scr[c, pl.ds(j, SUBLANES, stride=pitch), :] = h[c]
    return [jnp.concatenate([_rows(h_scr, c, s * pitch, seg) for s in range(SUBLANES)], axis=0)
            for c in range(n)]


def _conv_via_scratch(scr_ref, lead0, tail, x, w, b):
    k_width = w.shape[0]
    t, c = x.shape
    outs = []
    for g in range(c // LANES):
        cols = slice(g * LANES, (g + 1) * LANES)
        scr_ref[lead0 + g, 0:SUBLANES, :] = tail[:, cols]
        scr_ref[lead0 + g, SUBLANES:SUBLANES + t, :] = x[:, cols]
    for g in range(c // LANES):
        cols = slice(g * LANES, (g + 1) * LANES)
        out = x[:, cols] * w[k_width - 1:k_width, cols] + b[:, cols]
        for k in range(k_width - 1):
            shifted = _rows(scr_ref, lead0 + g, SUBLANES - (k_width - 1 - k), t)
            out = out + shifted * w[k:k + 1, cols]
        outs.append(out)
    return outs


def _mod_kernel(c_ref, w_ref, b_ref, o_ref):
    c = c_ref[...]
    rows = c.shape[0]
    c_act = c * jax.nn.sigmoid(c)
    pad = (-rows) % SUBLANES
    if pad:
        c_act = jnp.concatenate([c_act, jnp.zeros((pad, c.shape[1]), F32)], axis=0)
    mod = _dot(c_act.astype(BF16), w_ref[...].astype(BF16))
    o_ref[...] = mod[:rows] + b_ref[...]


def _modulation(c, w_ada, b_ada):
    rows, d = c.shape
    n = w_ada.shape[1]
    bn = d
    return pl.pallas_call(
        _mod_kernel,
        grid=(n // bn,),
        in_specs=[
            pl.BlockSpec((rows, d), lambda j: (0, 0)),
            pl.BlockSpec((d, bn), lambda j: (0, j)),
            pl.BlockSpec((1, bn), lambda j: (0, j)),
        ],
        out_specs=pl.BlockSpec((rows, bn), lambda j: (0, j)),
        out_shape=jax.ShapeDtypeStruct((rows, n), F32),
        compiler_params=pltpu.CompilerParams(
            dimension_semantics=("arbitrary",), vmem_limit_bytes=VMEM_LIMIT),
        name="adaln_mod",
    )(c, w_ada, b_ada)


def _mixer_kernel(x_ref, mod_ref, gpre_ref, gpost_ref, win_ref, cw_ref, cb_ref,
                  wgate_ref, brg_ref, big_ref, lrua_ref, vng_ref, vnb_ref, wsp_ref, bsp_ref,
                  glru_ref, ggmlp_ref, wout_ref, o_ref, *scratch):
    nb_tiles = x_ref.shape[0]
    tail_ref, state_ref = scratch[4 * nb_tiles:]

    @pl.when(pl.program_id(1) == 0)
    def _():
        tail_ref[...] = jnp.zeros_like(tail_ref)
        state_ref[...] = jnp.zeros_like(state_ref)

    project = lambda k: _mix_in(x_ref.at[k], mod_ref[k:k + 1, :], gpre_ref, win_ref)
    z = project(0)
    for k in range(nb_tiles):
        conv_scr, a_scr, b_scr, h_scr = scratch[k:4 * nb_tiles:nb_tiles]
        xc, gates = _mix_gates(z[0], cw_ref, cb_ref, wgate_ref, conv_scr, tail_ref.at[k])
        sp_groups = _mix_positions(z[3], vng_ref, vnb_ref, wsp_ref)
        z_next = project(k + 1) if k + 1 < nb_tiles else None
        hooks = [_zero_row_from(part) for part in z_next] if z_next else [None] * len(z)
        y = _mix_body(z, xc, gates, sp_groups, hooks, brg_ref, big_ref, lrua_ref, bsp_ref,
                      glru_ref, ggmlp_ref, a_scr, b_scr, h_scr, state_ref.at[k])
        _mix_out(y, x_ref.at[k], mod_ref[k:k + 1, :], gpost_ref, wout_ref, o_ref.at[k])
        z = z_next


def _mix_in(x_ref, mod, gpre_ref, win_ref):
    d = x_ref.shape[1]
    sh_m = mod[:, 0:d]
    sc_m = mod[:, d:2 * d]
    hb = (_rms(x_ref[...]) * (gpre_ref[...] * (1.0 + sc_m)) + sh_m).astype(BF16)
    widths = (LRU_WIDTH, LRU_WIDTH, GMLP_WIDTH, GMLP_WIDTH)
    starts = [sum(widths[:q]) for q in range(len(widths))]
    return [_dot(hb, _as_bf16(win_ref[:, c0:c0 + w])) for c0, w in zip(starts, widths)]


def _mix_gates(lru_x, cw_ref, cb_ref, wgate_ref, conv_scr, tail_ref):
    t = lru_x.shape[0]
    tail = tail_ref[...]
    tail_ref[...] = lru_x[t - SUBLANES:]
    xc = jnp.concatenate(
        _conv_via_scratch(conv_scr, 0, tail, lru_x, cw_ref[...], cb_ref[...]), axis=1)
    xcb = xc.astype(BF16)
    gates = [_dot(xcb[:, j * GATE_TILE:(j + 1) * GATE_TILE], _as_bf16(wgate_ref[j]))
             for j in range(LRU_WIDTH // GATE_TILE)]
    return xc, gates


def _mix_out(y, x_ref, mod, gpost_ref, wout_ref, o_ref):
    d = x_ref.shape[1]
    gt_m = mod[:, 2 * d:3 * d]
    y = _dot(y, _as_bf16(wout_ref[...]))
    o_ref[...] = x_ref[...] + (gt_m * gpost_ref[...]) * _rms(y)


def _mix_positions(g_v, vng_ref, vnb_ref, wsp_ref):
    t = g_v.shape[0]
    gv = _gelu(g_v)
    mu = jnp.mean(gv, axis=-1, keepdims=True)
    cen = gv - mu
    var = jnp.mean(cen * cen, axis=-1, keepdims=True)
    v = cen * lax.rsqrt(var + EPS) * vng_ref[...] + vnb_ref[...]
    vb = v.astype(BF16)
    nb = t // GMLP_BLOCK
    pi = lax.broadcasted_iota(jnp.int32, (GMLP_BLOCK, GMLP_BLOCK), 0) // CHUNK
    pj = lax.broadcasted_iota(jnp.int32, (GMLP_BLOCK, GMLP_BLOCK), 1) // CHUNK
    mask = pj <= pi
    sp_groups = []
    for g in range(GMLP_GROUPS):
        ws = jnp.where(mask, 0.5 * wsp_ref[g], 0.0).astype(BF16)
        cols = slice(g * GMLP_GROUP_DIM, (g + 1) * GMLP_GROUP_DIM)
        rhs = jnp.concatenate(
            [vb[n * GMLP_BLOCK:(n + 1) * GMLP_BLOCK, cols] for n in range(nb)], axis=1)
        sp_groups.append(_dot(ws, rhs))
    return sp_groups


def _mix_body(z, xc, gates, sp_groups, hooks, brg_ref, big_ref, lrua_ref, bsp_ref,
              glru_ref, ggmlp_ref, a_scr, b_scr, h_scr, state_ref):
    _, lru_gate, g_u, _ = z
    t = lru_gate.shape[0]
    slabs = lambda v: [v[:, c * LANES:(c + 1) * LANES] for c in range(v.shape[1] // LANES)]
    hooked = lambda v, hook: v if hook is None else v + hook

    r_pre = jnp.concatenate([g[:, :GATE_TILE] for g in gates], axis=1)
    i_pre = jnp.concatenate([g[:, GATE_TILE:] for g in gates], axis=1)
    r_gate = jax.nn.sigmoid(r_pre + brg_ref[...])
    i_gate = jax.nn.sigmoid(i_pre + big_ref[...])
    neg_a = -lrua_ref[...]
    softplus = jnp.maximum(neg_a, 0.0) + jnp.log1p(jnp.exp(-jnp.abs(neg_a)))
    log_a = r_gate * ((-LRU_C) * softplus)
    a = jnp.exp(log_a)
    th = jnp.tanh(log_a)
    q = (-2.0 * th) / (1.0 - th)
    mult = jnp.where(q > 0.0, q * lax.rsqrt(q), 0.0)
    bx = (mult * xc) * i_gate
    hs = _linear_scan(slabs(a), slabs(bx), state_ref, a_scr, b_scr, h_scr, h0_add=hooks[0])
    y_lru = 0.5 * _gelu2_times(lru_gate, jnp.concatenate(hs, axis=1))
    yl = _rms(y_lru) * hooked(glru_ref[...], hooks[1])

    nb = t // GMLP_BLOCK
    bsp = hooked(0.5 * bsp_ref[...], hooks[2])
    sp_rows = []
    for n in range(nb):
        blk = jnp.concatenate(
            [sg[:, n * GMLP_GROUP_DIM:(n + 1) * GMLP_GROUP_DIM] for sg in sp_groups], axis=1)
        sp_rows.append(blk + bsp)
    sp_half = jnp.concatenate(sp_rows, axis=0)
    y_gmlp = _gelu2_times(g_u, sp_half)
    yg = _rms(y_gmlp) * hooked(ggmlp_ref[...], hooks[3])

    return jnp.concatenate([yl, yg], axis=1).astype(BF16)


def _mixer(x, mod, g_pre, g_post, w_in, conv_w, conv_b, w_gate, b_rgate, b_igate, lru_a,
           v_norm_g, v_norm_b, w_spatial, b_sp_full, g_lru_out, g_gmlp_out, w_out):
    bsz, s, d = x.shape
    t = MIX_T
    nb = MIX_NB
    assert bsz % nb == 0 and s % t == 0
    n_lg = LRU_WIDTH // LANES
    scan_rows = SUBLANES * (t // SUBLANES + 1)
    full = lambda a: pl.BlockSpec(a.shape, lambda b, i: (0,) * a.ndim)
    in_arrays = [g_pre, g_post, w_in, conv_w, conv_b, w_gate, b_rgate, b_igate, lru_a,
                 v_norm_g, v_norm_b, w_spatial, b_sp_full, g_lru_out, g_gmlp_out, w_out]
    return pl.pallas_call(
        _mixer_kernel,
        grid=(bsz // nb, s // t),
        in_specs=[
            pl.BlockSpec((nb, t, d), lambda b, i: (b, i, 0)),
            pl.BlockSpec((nb, mod.shape[1]), lambda b, i: (b, 0)),
        ] + [full(a) for a in in_arrays],
        out_specs=pl.BlockSpec((nb, t, d), lambda b, i: (b, i, 0)),
        out_shape=jax.ShapeDtypeStruct(x.shape, x.dtype),
        scratch_shapes=(
            [pltpu.VMEM((n_lg, SUBLANES + t, LANES), F32)] * nb
            + [pltpu.VMEM((n_lg, scan_rows, LANES), F32)] * (3 * nb)
            + [pltpu.VMEM((nb, SUBLANES, LRU_WIDTH), F32)] * 2),
        compiler_params=pltpu.CompilerParams(
            dimension_semantics=("arbitrary", "arbitrary"), vmem_limit_bytes=VMEM_LIMIT),
        name="token_mixer",
    )(x, mod, *in_arrays)


def _ffn_kernel(x_ref, mod_ref, gpre_ref, gpost_ref, wup_ref, cw_ref, cb_ref, wd_ref,
                o_ref, scr_ref, tail_ref):
    t_idx = pl.program_id(1)
    t, d = x_ref.shape
    d_ff = 2 * wd_ref.shape[0]
    fc = FFN_FC
    lg = fc // LANES

    @pl.when(t_idx == 0)
    def _():
        tail_ref[...] = jnp.zeros_like(tail_ref)

    mod = mod_ref[pl.ds(pl.program_id(0), 1), :]
    sh_f = mod[:, 3 * d:4 * d]
    sc_f = mod[:, 4 * d:5 * d]
    gt_f = mod[:, 5 * d:6 * d]
    x = x_ref[...]
    hb = (_rms(x) * (gpre_ref[...] * (1.0 + sc_f)) + sh_f).astype(BF16)

    n_chunks = d_ff // fc
    halves = ((0, 1.0), (d_ff, 0.5))
    chunk_cols = lambda base, j: slice(base + j * fc, base + (j + 1) * fc)
    up_proj = lambda j: [_dot(hb, _as_bf16(wup_ref[:, chunk_cols(base, j)])) for base, _ in halves]

    def gated(j, ups):
        conv = []
        for k, (base, scale) in enumerate(halves):
            cols = chunk_cols(base, j)
            tail = tail_ref[:, cols]
            tail_ref[:, cols] = ups[k][t - SUBLANES:]
            slot = (2 * j + k) % FFN_SLOTS
            conv.append(_conv_via_scratch(scr_ref, slot * lg, tail, ups[k],
                                          cw_ref[:, cols] * scale, cb_ref[:, cols] * scale))
        return jnp.concatenate(
            [_gelu2_times(cg, cv) for cg, cv in zip(*conv)], axis=1).astype(BF16)

    acc = None
    ups = up_proj(0)
    for j in range(n_chunks):
        ups_next = up_proj(j + 1) if j + 1 < n_chunks else None
        part = _dot(gated(j, ups), _as_bf16(wd_ref[j * fc // 2:(j + 1) * fc // 2, :]))
        acc = part if acc is None else acc + part
        ups = ups_next
    o_ref[...] = x + (gt_f * gpost_ref[...]) * _rms(acc)


def _ffn(x, mod, g_pre, g_post, w_up, conv_w, conv_b, w_down):
    bsz, s, d = x.shape
    t = FFN_T
    resident = lambda a: pl.BlockSpec(a.shape, lambda b, i: (0,) * a.ndim,
                                      pipeline_mode=pl.Buffered(1))
    return pl.pallas_call(
        _ffn_kernel,
        grid=(bsz, s // t),
        in_specs=[
            pl.BlockSpec((None, t, d), lambda b, i: (b, i, 0)),
            pl.BlockSpec(mod.shape, lambda b, i: (0, 0)),
            resident(g_pre), resident(g_post), resident(w_up), resident(conv_w),
            resident(conv_b), resident(w_down),
        ],
        out_specs=pl.BlockSpec((None, t, d), lambda b, i: (b, i, 0)),
        out_shape=jax.ShapeDtypeStruct(x.shape, x.dtype),
        scratch_shapes=[
            pltpu.VMEM((FFN_SLOTS * (FFN_FC // LANES), SUBLANES + t, LANES), F32),
            pltpu.VMEM((SUBLANES, w_up.shape[1]), F32),
        ],
        compiler_params=pltpu.CompilerParams(
            dimension_semantics=("arbitrary", "arbitrary"), vmem_limit_bytes=VMEM_LIMIT),
        name="conv_ffn",
    )(x, mod, g_pre, g_post, w_up, conv_w, conv_b, w_down)


def _gate_weights(w_rgate, w_igate):
    per_tile = GATE_TILE // LRU_HEAD_DIM
    n_tiles = LRU_WIDTH // GATE_TILE
    eye = jnp.eye(per_tile, dtype=F32)

    def block_diag(w):
        w4 = w.reshape(n_tiles, per_tile, LRU_HEAD_DIM, LRU_HEAD_DIM)
        return (w4[:, :, :, None, :] * eye[None, :, None, :, None]).reshape(
            n_tiles, GATE_TILE, GATE_TILE)

    return _pack_rows_xla(jnp.concatenate([block_diag(w_rgate), block_diag(w_igate)], axis=2))


def kernel(x, c, w_ada, b_ada, g_mix_pre, g_mix_post, w_in, conv_w, conv_b, w_rgate, b_rgate, w_igate, b_igate, lru_a, v_norm_g, v_norm_b, w_spatial, b_spatial, g_lru_out, g_gmlp_out, w_out, g_ffn_pre, g_ffn_post, w_up, ffn_conv_w, ffn_conv_b, w_down):
    depth = w_ada.shape[0]
    bsz, s, d = x.shape
    for l in range(depth):
        mod = _modulation(c, w_ada[l], b_ada[l:l + 1])
        row = lambda a: a[l:l + 1]
        w_gate = _gate_weights(w_rgate[l], w_igate[l])
        b_gate = (b_rgate[l].reshape(1, -1), b_igate[l].reshape(1, -1))
        b_sp_full = jnp.repeat(b_spatial[l].T, GMLP_GROUP_DIM, axis=1)
        x = _mixer(x, mod, row(g_mix_pre), row(g_mix_post), _pack_rows(w_in[l]),
                   conv_w[l], row(conv_b), w_gate, *b_gate, row(lru_a),
                   row(v_norm_g), row(v_norm_b), w_spatial[l], b_sp_full,
                   row(g_lru_out), row(g_gmlp_out), _pack_rows(w_out[l]))
        x = _ffn(x, mod, row(g_ffn_pre), row(g_ffn_post), _pack_rows(w_up[l]),
                 ffn_conv_w[l], row(ffn_conv_b), _pack_rows(w_down[l]))
    return x
```

```python
import functools

import jax
import jax.numpy as jnp
from jax import lax
from jax.experimental import pallas as pl
from jax.experimental.pallas import tpu as pltpu

CHUNK = 64
LRU_WIDTH = 512
LRU_HEADS = 8
LRU_HEAD_DIM = LRU_WIDTH // LRU_HEADS
LRU_CONV_WIDTH = 4
LRU_C = 8.0
GMLP_WIDTH = 512
GMLP_GROUPS = 4
GMLP_GROUP_DIM = GMLP_WIDTH // GMLP_GROUPS
GMLP_BLOCK = 128
FFN_CONV_WIDTH = 3
N_MOD = 6
EPS = 1e-6

SUBLANES = 8
LANES = 128
GATE_TILE = 256

MIX_T = 256
MIX_NB = 4
FFN_T = 512
FFN_FC = 512
FFN_SLOTS = 4
PACK_BLOCK_BYTES = 4 * 1024 * 1024
VMEM_LIMIT = 56 * 1024 * 1024

F32 = jnp.float32
BF16 = jnp.bfloat16
U32 = jnp.uint32


def _dot(a, b):
    return jnp.dot(a, b, preferred_element_type=F32)


def _pack_rows_xla(w):
    k, n = w.shape[-2:]
    wb = w.astype(BF16).reshape(*w.shape[:-2], k // 2, 2, n)
    return lax.bitcast_convert_type(jnp.swapaxes(wb, -1, -2), U32)


def _pack_kernel(w_ref, o_ref):
    o_ref[...] = pltpu.bitcast(w_ref[...].astype(BF16), U32)


def _pack_rows(w):
    k, n = w.shape
    bk = k
    while bk * n * 4 > PACK_BLOCK_BYTES and bk % (4 * SUBLANES) == 0:
        bk //= 2
    assert k % bk == 0
    return pl.pallas_call(
        _pack_kernel,
        grid=(k // bk,),
        in_specs=[pl.BlockSpec((bk, n), lambda i: (i, 0))],
        out_specs=pl.BlockSpec((bk // 2, n), lambda i: (i, 0)),
        out_shape=jax.ShapeDtypeStruct((k // 2, n), U32),
        compiler_params=pltpu.CompilerParams(
            dimension_semantics=("arbitrary",), vmem_limit_bytes=VMEM_LIMIT),
        name="pack_weight",
    )(w)


def _as_bf16(packed):
    return pltpu.bitcast(packed, BF16)


def _zero_row_from(x):
    last = lax.bitcast_convert_type(x[x.shape[0] - SUBLANES:], U32)
    zero = lax.shift_right_logical(lax.shift_right_logical(last, U32(16)), U32(16))
    return zero[0:1].astype(F32)


def _rms(x):
    return x * lax.rsqrt(jnp.mean(x * x, axis=-1, keepdims=True) + EPS)


def _gelu(x):
    c0 = 0.7978845608028654
    hx = 0.5 * x
    return hx + hx * jnp.tanh(x * (c0 + (c0 * 0.044715) * (x * x)))


def _gelu2_times(x, v):
    c0 = 0.7978845608028654
    z = x * (c0 + (c0 * 0.044715) * (x * x))
    return (x * v) * (1.0 + jnp.tanh(z))


def _rows(ref, lead, start, size):
    return ref[pl.ds(lead, 1, stride=2), pl.ds(start, size), :][0]


def _put_rows(ref, lead, start, val):
    ref[pl.ds(lead, 1, stride=2), pl.ds(start, val.shape[0]), :] = val[None]


def _linear_scan(a, b, state_ref, a_scr, b_scr, h_scr, h0_add=None):
    t = a[0].shape[0]
    seg = t // SUBLANES
    pitch = seg + 1 - seg % 2
    n = len(a)
    for c in range(n):
        for s in range(SUBLANES):
            _put_rows(a_scr, c, s * pitch, a[c][s * seg:(s + 1) * seg])
            _put_rows(b_scr, c, s * pitch, b[c][s * seg:(s + 1) * seg])
    step = lambda ref, c, j: ref[c, pl.ds(j, SUBLANES, stride=pitch), :]
    row = lax.broadcasted_iota(jnp.int32, (SUBLANES, LANES), 0)
    shifted = lambda v, sh, fill: jnp.where(row >= sh, pltpu.roll(v, sh, axis=0), fill)

    prod = [None] * n
    end = [None] * n
    for j in range(seg):
        for c in range(n):
            aj, bj = step(a_scr, c, j), step(b_scr, c, j)
            prod[c] = aj if j == 0 else aj * prod[c]
            end[c] = bj if j == 0 else aj * end[c] + bj

    h = []
    for c in range(n):
        cols = slice(c * LANES, (c + 1) * LANES)
        h0 = state_ref[SUBLANES - 1:SUBLANES, cols]
        if h0_add is not None:
            h0 = h0 + h0_add[:, cols]
        p = prod[c]
        e = end[c] + jnp.where(row == 0, p * h0, 0.0)
        for sh in (1, 2, 4):
            e = e + p * shifted(e, sh, 0.0)
            if sh < 4:
                p = p * shifted(p, sh, 1.0)
        state_ref[:, cols] = e
        h.append(shifted(e, 1, h0))
    for j in range(seg):
        for c in range(n):
            h[c] = step(a_scr, c, j) * h[c] + step(b_scr, c, j)
            h_scr[c, pl.ds(j, SUBLANES, stride=pitch), :] = h[c]
    return [jnp.concatenate([_rows(h_scr, c, s * pitch, seg) for s in range(SUBLANES)], axis=0)
            for c in range(n)]


def _conv_via_scratch(scr_ref, lead0, tail, x, w, b):
    k_width = w.shape[0]
    t, c = x.shape
    outs = []
    for g in range(c // LANES):
        cols = slice(g * LANES, (g + 1) * LANES)
        scr_ref[lead0 + g, 0:SUBLANES, :] = tail[:, cols]
        scr_ref[lead0 + g, SUBLANES:SUBLANES + t, :] = x[:, cols]
    for g in range(c // LANES):
        cols = slice(g * LANES, (g + 1) * LANES)
        out = x[:, cols] * w[k_width - 1:k_width, cols] + b[:, cols]
        for k in range(k_width - 1):
            shifted = _rows(scr_ref, lead0 + g, SUBLANES - (k_width - 1 - k), t)
            out = out + shifted * w[k:k + 1, cols]
        outs.append(out)
    return outs


def _mod_kernel(c_ref, w_ref, b_ref, o_ref):
    c = c_ref[...]
    rows = c.shape[0]
    c_act = c * jax.nn.sigmoid(c)
    pad = (-rows) % SUBLANES
    if pad:
        c_act = jnp.concatenate([c_act, jnp.zeros((pad, c.shape[1]), F32)], axis=0)
    mod = _dot(c_act.astype(BF16), w_ref[...].astype(BF16))
    o_ref[...] = mod[:rows] + b_ref[...]


def _modulation(c, w_ada, b_ada):
    rows, d = c.shape
    n = w_ada.shape[1]
    bn = d
    return pl.pallas_call(
        _mod_kernel,
        grid=(n // bn,),
        in_specs=[
            pl.BlockSpec((rows, d), lambda j: (0, 0)),
            pl.BlockSpec((d, bn), lambda j: (0, j)),
            pl.BlockSpec((1, bn), lambda j: (0, j)),
        ],
        out_specs=pl.BlockSpec((rows, bn), lambda j: (0, j)),
        out_shape=jax.ShapeDtypeStruct((rows, n), F32),
        compiler_params=pltpu.CompilerParams(
            dimension_semantics=("arbitrary",), vmem_limit_bytes=VMEM_LIMIT),
        name="adaln_mod",
    )(c, w_ada, b_ada)


def _mixer_kernel(x_ref, mod_ref, gpre_ref, gpost_ref, win_ref, cw_ref, cb_ref,
                  wgate_ref, brg_ref, big_ref, lrua_ref, vng_ref, vnb_ref, wsp_ref, bsp_ref,
                  glru_ref, ggmlp_ref, wout_ref, *rest, n_later):
    later_in, o_ref, later_out = rest[:n_later], rest[n_later], rest[n_later + 1:2 * n_later + 1]
    scratch = rest[2 * n_later + 1:]
    for w_ref, p_ref in zip(later_in, later_out):
        p_ref[...] = pltpu.bitcast(w_ref[...].astype(BF16), U32)
    nb_tiles = x_ref.shape[0]
    tail_ref, state_ref = scratch[4 * nb_tiles:]

    @pl.when(pl.program_id(1) == 0)
    def _():
        tail_ref[...] = jnp.zeros_like(tail_ref)
        state_ref[...] = jnp.zeros_like(state_ref)

    project = lambda k: _mix_in(x_ref.at[k], mod_ref[k:k + 1, :], gpre_ref, win_ref)
    z = project(0)
    for k in range(nb_tiles):
        conv_scr, a_scr, b_scr, h_scr = scratch[k:4 * nb_tiles:nb_tiles]
        xc, gates = _mix_gates(z[0], cw_ref, cb_ref, wgate_ref, conv_scr, tail_ref.at[k])
        sp_groups = _mix_positions(z[3], vng_ref, vnb_ref, wsp_ref)
        z_next = project(k + 1) if k + 1 < nb_tiles else None
        hooks = [_zero_row_from(part) for part in z_next] if z_next else [None] * len(z)
        y = _mix_body(z, xc, gates, sp_groups, hooks, brg_ref, big_ref, lrua_ref, bsp_ref,
                      glru_ref, ggmlp_ref, a_scr, b_scr, h_scr, state_ref.at[k])
        _mix_out(y, x_ref.at[k], mod_ref[k:k + 1, :], gpost_ref, wout_ref, o_ref.at[k])
        z = z_next


def _mix_in(x_ref, mod, gpre_ref, win_ref):
    d = x_ref.shape[1]
    sh_m = mod[:, 0:d]
    sc_m = mod[:, d:2 * d]
    hb = (_rms(x_ref[...]) * (gpre_ref[...] * (1.0 + sc_m)) + sh_m).astype(BF16)
    widths = (LRU_WIDTH, LRU_WIDTH, GMLP_WIDTH, GMLP_WIDTH)
    starts = [sum(widths[:q]) for q in range(len(widths))]
    return [_dot(hb, _as_bf16(win_ref[:, c0:c0 + w])) for c0, w in zip(starts, widths)]


def _mix_gates(lru_x, cw_ref, cb_ref, wgate_ref, conv_scr, tail_ref):
    t = lru_x.shape[0]
    tail = tail_ref[...]
    tail_ref[...] = lru_x[t - SUBLANES:]
    xc = jnp.concatenate(
        _conv_via_scratch(conv_scr, 0, tail, lru_x, cw_ref[...], cb_ref[...]), axis=1)
    xcb = xc.astype(BF16)
    gates = [_dot(xcb[:, j * GATE_TILE:(j + 1) * GATE_TILE], _as_bf16(wgate_ref[j]))
             for j in range(LRU_WIDTH // GATE_TILE)]
    return xc, gates


def _mix_out(y, x_ref, mod, gpost_ref, wout_ref, o_ref):
    d = x_ref.shape[1]
    gt_m = mod[:, 2 * d:3 * d]
    y = _dot(y, _as_bf16(wout_ref[...]))
    o_ref[...] = x_ref[...] + (gt_m * gpost_ref[...]) * _rms(y)


def _mix_positions(g_v, vng_ref, vnb_ref, wsp_ref):
    t = g_v.shape[0]
    gv = _gelu(g_v)
    mu = jnp.mean(gv, axis=-1, keepdims=True)
    cen = gv - mu
    var = jnp.mean(cen * cen, axis=-1, keepdims=True)
    v = cen * lax.rsqrt(var + EPS) * vng_ref[...] + vnb_ref[...]
    vb = v.astype(BF16)
    nb = t // GMLP_BLOCK
    pi = lax.broadcasted_iota(jnp.int32, (GMLP_BLOCK, GMLP_BLOCK), 0) // CHUNK
    pj = lax.broadcasted_iota(jnp.int32, (GMLP_BLOCK, GMLP_BLOCK), 1) // CHUNK
    mask = pj <= pi
    sp_groups = []
    for g in range(GMLP_GROUPS):
        ws = jnp.where(mask, 0.5 * wsp_ref[g], 0.0).astype(BF16)
        cols = slice(g * GMLP_GROUP_DIM, (g + 1) * GMLP_GROUP_DIM)
        rhs = jnp.concatenate(
            [vb[n * GMLP_BLOCK:(n + 1) * GMLP_BLOCK, cols] for n in range(nb)], axis=1)
        sp_groups.append(_dot(ws, rhs))
    return sp_groups


def _mix_body(z, xc, gates, sp_groups, hooks, brg_ref, big_ref, lrua_ref, bsp_ref,
              glru_ref, ggmlp_ref, a_scr, b_scr, h_scr, state_ref):
    _, lru_gate, g_u, _ = z
    t = lru_gate.shape[0]
    slabs = lambda v: [v[:, c * LANES:(c + 1) * LANES] for c in range(v.shape[1] // LANES)]
    hooked = lambda v, hook: v if hook is None else v + hook

    r_pre = jnp.concatenate([g[:, :GATE_TILE] for g in gates], axis=1)
    i_pre = jnp.concatenate([g[:, GATE_TILE:] for g in gates], axis=1)
    r_gate = jax.nn.sigmoid(r_pre + brg_ref[...])
    i_gate = jax.nn.sigmoid(i_pre + big_ref[...])
    neg_a = -lrua_ref[...]
    softplus = jnp.maximum(neg_a, 0.0) + jnp.log1p(jnp.exp(-jnp.abs(neg_a)))
    log_a = r_gate * ((-LRU_C) * softplus)
    a = jnp.exp(log_a)
    th = jnp.tanh(log_a)
    q = (-2.0 * th) / (1.0 - th)
    mult = jnp.where(q > 0.0, q * lax.rsqrt(q), 0.0)
    bx = (mult * xc) * i_gate
    hs = _linear_scan(slabs(a), slabs(bx), state_ref, a_scr, b_scr, h_scr, h0_add=hooks[0])
    y_lru = 0.5 * _gelu2_times(lru_gate, jnp.concatenate(hs, axis=1))
    yl = _rms(y_lru) * hooked(glru_ref[...], hooks[1])

    nb = t // GMLP_BLOCK
    bsp = hooked(0.5 * bsp_ref[...], hooks[2])
    sp_rows = []
    for n in range(nb):
        blk = jnp.concatenate(
            [sg[:, n * GMLP_GROUP_DIM:(n + 1) * GMLP_GROUP_DIM] for sg in sp_groups], axis=1)
        sp_rows.append(blk + bsp)
    sp_half = jnp.concatenate(sp_rows, axis=0)
    y_gmlp = _gelu2_times(g_u, sp_half)
    yg = _rms(y_gmlp) * hooked(ggmlp_ref[...], hooks[3])

    return jnp.concatenate([yl, yg], axis=1).astype(BF16)


def _mixer(x, mod, g_pre, g_post, w_in, conv_w, conv_b, w_gate, b_rgate, b_igate, lru_a,
           v_norm_g, v_norm_b, w_spatial, b_sp_full, g_lru_out, g_gmlp_out, w_out, later_weights):
    bsz, s, d = x.shape
    t = MIX_T
    nb = MIX_NB
    assert bsz % nb == 0 and s % t == 0
    n_lg = LRU_WIDTH // LANES
    scan_rows = SUBLANES * (t // SUBLANES + 1)
    full = lambda a: pl.BlockSpec(a.shape, lambda b, i: (0,) * a.ndim)
    in_arrays = [g_pre, g_post, w_in, conv_w, conv_b, w_gate, b_rgate, b_igate, lru_a,
                 v_norm_g, v_norm_b, w_spatial, b_sp_full, g_lru_out, g_gmlp_out, w_out]
    nt = s // t
    n_steps = (bsz // nb) * nt
    step = lambda b, i: (b * nt + i, 0)
    later_in, later_out, later_shapes = [], [], []
    for w in later_weights:
        k, n = w.shape
        assert k % (n_steps * 2 * SUBLANES) == 0
        later_in.append(pl.BlockSpec((k // n_steps, n), step))
        later_out.append(pl.BlockSpec((k // n_steps // 2, n), step))
        later_shapes.append(jax.ShapeDtypeStruct((k // 2, n), U32))
    out = pl.pallas_call(
        functools.partial(_mixer_kernel, n_later=len(later_weights)),
        grid=(bsz // nb, nt),
        in_specs=[
            pl.BlockSpec((nb, t, d), lambda b, i: (b, i, 0)),
            pl.BlockSpec((nb, mod.shape[1]), lambda b, i: (b, 0)),
        ] + [full(a) for a in in_arrays] + later_in,
        out_specs=[pl.BlockSpec((nb, t, d), lambda b, i: (b, i, 0))] + later_out,
        out_shape=[jax.ShapeDtypeStruct(x.shape, x.dtype)] + later_shapes,
        scratch_shapes=(
            [pltpu.VMEM((n_lg, SUBLANES + t, LANES), F32)] * nb
            + [pltpu.VMEM((n_lg, scan_rows, LANES), F32)] * (3 * nb)
            + [pltpu.VMEM((nb, SUBLANES, LRU_WIDTH), F32)] * 2),
        compiler_params=pltpu.CompilerParams(
            dimension_semantics=("arbitrary", "arbitrary"), vmem_limit_bytes=VMEM_LIMIT),
        name="token_mixer",
    )(x, mod, *in_arrays, *later_weights)
    return out[0], out[1:]


def _ffn_kernel(x_ref, mod_ref, gpre_ref, gpost_ref, wup_ref, cw_ref, cb_ref, wd_ref,
                o_ref, scr_ref, tail_ref):
    t_idx = pl.program_id(1)
    t, d = x_ref.shape
    d_ff = 2 * wd_ref.shape[0]
    fc = FFN_FC
    lg = fc // LANES

    @pl.when(t_idx == 0)
    def _():
        tail_ref[...] = jnp.zeros_like(tail_ref)

    mod = mod_ref[pl.ds(pl.program_id(0), 1), :]
    sh_f = mod[:, 3 * d:4 * d]
    sc_f = mod[:, 4 * d:5 * d]
    gt_f = mod[:, 5 * d:6 * d]
    x = x_ref[...]
    hb = (_rms(x) * (gpre_ref[...] * (1.0 + sc_f)) + sh_f).astype(BF16)

    n_chunks = d_ff // fc
    halves = ((0, 1.0), (d_ff, 0.5))
    chunk_cols = lambda base, j: slice(base + j * fc, base + (j + 1) * fc)
    up_proj = lambda j: [_dot(hb, _as_bf16(wup_ref[:, chunk_cols(base, j)])) for base, _ in halves]

    def gated(j, ups):
        conv = []
        for k, (base, scale) in enumerate(halves):
            cols = chunk_cols(base, j)
            tail = tail_ref[:, cols]
            tail_ref[:, cols] = ups[k][t - SUBLANES:]
            slot = (2 * j + k) % FFN_SLOTS
            conv.append(_conv_via_scratch(scr_ref, slot * lg, tail, ups[k],
                                          cw_ref[:, cols] * scale, cb_ref[:, cols] * scale))
        return jnp.concatenate(
            [_gelu2_times(cg, cv) for cg, cv in zip(*conv)], axis=1).astype(BF16)

    acc = None
    ups = up_proj(0)
    for j in range(n_chunks):
        ups_next = up_proj(j + 1) if j + 1 < n_chunks else None
        part = _dot(gated(j, ups), _as_bf16(wd_ref[j * fc // 2:(j + 1) * fc // 2, :]))
        acc = part if acc is None else acc + part
        ups = ups_next
    o_ref[...] = x + (gt_f * gpost_ref[...]) * _rms(acc)


def _ffn(x, mod, g_pre, g_post, w_up, conv_w, conv_b, w_down):
    bsz, s, d = x.shape
    t = FFN_T
    resident = lambda a: pl.BlockSpec(a.shape, lambda b, i: (0,) * a.ndim,
                                      pipeline_mode=pl.Buffered(1))
    return pl.pallas_call(
        _ffn_kernel,
        grid=(bsz, s // t),
        in_specs=[
            pl.BlockSpec((None, t, d), lambda b, i: (b, i, 0)),
            pl.BlockSpec(mod.shape, lambda b, i: (0, 0)),
            resident(g_pre), resident(g_post), resident(w_up), resident(conv_w),
            resident(conv_b), resident(w_down),
        ],
        out_specs=pl.BlockSpec((None, t, d), lambda b, i: (b, i, 0)),
        out_shape=jax.ShapeDtypeStruct(x.shape, x.dtype),
        scratch_shapes=[
            pltpu.VMEM((FFN_SLOTS * (FFN_FC // LANES), SUBLANES + t, LANES), F32),
            pltpu.VMEM((SUBLANES, w_up.shape[1]), F32),
        ],
        compiler_params=pltpu.CompilerParams(
            dimension_semantics=("arbitrary", "arbitrary"), vmem_limit_bytes=VMEM_LIMIT),
        name="conv_ffn",
    )(x, mod, g_pre, g_post, w_up, conv_w, conv_b, w_down)


def _gate_weights(w_rgate, w_igate):
    per_tile = GATE_TILE // LRU_HEAD_DIM
    n_tiles = LRU_WIDTH // GATE_TILE
    eye = jnp.eye(per_tile, dtype=F32)

    def block_diag(w):
        w4 = w.reshape(n_tiles, per_tile, LRU_HEAD_DIM, LRU_HEAD_DIM)
        return (w4[:, :, :, None, :] * eye[None, :, None, :, None]).reshape(
            n_tiles, GATE_TILE, GATE_TILE)

    return _pack_rows_xla(jnp.concatenate([block_diag(w_rgate), block_diag(w_igate)], axis=2))


def kernel(x, c, w_ada, b_ada, g_mix_pre, g_mix_post, w_in, conv_w, conv_b, w_rgate, b_rgate, w_igate, b_igate, lru_a, v_norm_g, v_norm_b, w_spatial, b_spatial, g_lru_out, g_gmlp_out, w_out, g_ffn_pre, g_ffn_post, w_up, ffn_conv_w, ffn_conv_b, w_down):
    depth = w_ada.shape[0]
    bsz, s, d = x.shape
    for l in range(depth):
        mod = _modulation(c, w_ada[l], b_ada[l:l + 1])
        row = lambda a: a[l:l + 1]
        w_gate = _gate_weights(w_rgate[l], w_igate[l])
        b_gate = (b_rgate[l].reshape(1, -1), b_igate[l].reshape(1, -1))
        b_sp_full = jnp.repeat(b_spatial[l].T, GMLP_GROUP_DIM, axis=1)
        x, (w_up_p, w_down_p) = _mixer(
                   x, mod, row(g_mix_pre), row(g_mix_post), _pack_rows(w_in[l]),
                   conv_w[l], row(conv_b), w_gate, *b_gate, row(lru_a),
                   row(v_norm_g), row(v_norm_b), w_spatial[l], b_sp_full,
                   row(g_lru_out), row(g_gmlp_out), _pack_rows(w_out[l]),
                   later_weights=[w_up[l], w_down[l]])
        x = _ffn(x, mod, row(g_ffn_pre), row(g_ffn_post), w_up_p,
                 ffn_conv_w[l], row(ffn_conv_b), w_down_p)
    return x
```

```python
import functools

import jax
import jax.numpy as jnp
from jax import lax
from jax.experimental import pallas as pl
from jax.experimental.pallas import tpu as pltpu

CHUNK = 64
LRU_WIDTH = 512
LRU_HEADS = 8
LRU_HEAD_DIM = LRU_WIDTH // LRU_HEADS
LRU_CONV_WIDTH = 4
LRU_C = 8.0
GMLP_WIDTH = 512
GMLP_GROUPS = 4
GMLP_GROUP_DIM = GMLP_WIDTH // GMLP_GROUPS
GMLP_BLOCK = 128
FFN_CONV_WIDTH = 3
N_MOD = 6
EPS = 1e-6

SUBLANES = 8
LANES = 128
GATE_TILE = 256

MOD_ROWS = 128
MIX_T = 256
MIX_NB = 4
FFN_T = 512
FFN_FC = 1024
FFN_SLOTS = 4
PACK_BLOCK_BYTES = 4 * 1024 * 1024
VMEM_LIMIT = 56 * 1024 * 1024

F32 = jnp.float32
BF16 = jnp.bfloat16
U32 = jnp.uint32


def _dot(a, b):
    return jnp.dot(a, b, preferred_element_type=F32)


def _pack_rows_xla(w):
    k, n = w.shape[-2:]
    wb = w.astype(BF16).reshape(*w.shape[:-2], k // 2, 2, n)
    return lax.bitcast_convert_type(jnp.swapaxes(wb, -1, -2), U32)


def _pack_kernel(w_ref, o_ref):
    o_ref[...] = pltpu.bitcast(w_ref[...].astype(BF16), U32)


def _pack_rows(w):
    k, n = w.shape
    bk = k
    while bk * n * 4 > PACK_BLOCK_BYTES and bk % (4 * SUBLANES) == 0:
        bk //= 2
    assert k % bk == 0
    return pl.pallas_call(
        _pack_kernel,
        grid=(k // bk,),
        in_specs=[pl.BlockSpec((bk, n), lambda i: (i, 0))],
        out_specs=pl.BlockSpec((bk // 2, n), lambda i: (i, 0)),
        out_shape=jax.ShapeDtypeStruct((k // 2, n), U32),
        compiler_params=pltpu.CompilerParams(
            dimension_semantics=("arbitrary",), vmem_limit_bytes=VMEM_LIMIT),
        name="pack_weight",
    )(w)


def _as_bf16(packed):
    return pltpu.bitcast(packed, BF16)


def _zero_row_from(x):
    last = lax.bitcast_convert_type(x[x.shape[0] - SUBLANES:], U32)
    zero = lax.shift_right_logical(lax.shift_right_logical(last, U32(16)), U32(16))
    return zero[0:1].astype(F32)


def _rms(x):
    return x * lax.rsqrt(jnp.mean(x * x, axis=-1, keepdims=True) + EPS)


def _gelu(x):
    c0 = 0.7978845608028654
    hx = 0.5 * x
    return hx + hx * jnp.tanh(x * (c0 + (c0 * 0.044715) * (x * x)))


def _gelu2_times(x, v):
    c0 = 0.7978845608028654
    z = x * (c0 + (c0 * 0.044715) * (x * x))
    return (x * v) * (1.0 + jnp.tanh(z))


def _rows(ref, lead, start, size):
    return ref[pl.ds(lead, 1, stride=2), pl.ds(start, size), :][0]


def _put_rows(ref, lead, start, val):
    ref[pl.ds(lead, 1, stride=2), pl.ds(start, val.shape[0]), :] = val[None]


def _linear_scan(a, b, state_ref, a_scr, b_scr, h_scr, h0_add=None):
    t = a[0].shape[0]
    seg = t // SUBLANES
    pitch = seg + 1 - seg % 2
    n = len(a)
    for c in range(n):
        for s in range(SUBLANES):
            _put_rows(a_scr, c, s * pitch, a[c][s * seg:(s + 1) * seg])
            _put_rows(b_scr, c, s * pitch, b[c][s * seg:(s + 1) * seg])
    step = lambda ref, c, j: ref[c, pl.ds(j, SUBLANES, stride=pitch), :]
    row = lax.broadcasted_iota(jnp.int32, (SUBLANES, LANES), 0)
    shifted = lambda v, sh, fill: jnp.where(row >= sh, pltpu.roll(v, sh, axis=0), fill)

    prod = [None] * n
    end = [None] * n
    for j in range(seg):
        for c in range(n):
            aj, bj = step(a_scr, c, j), step(b_scr, c, j)
            prod[c] = aj if j == 0 else aj * prod[c]
            end[c] = bj if j == 0 else aj * end[c] + bj

    h = []
    for c in range(n):
        cols = slice(c * LANES, (c + 1) * LANES)
        h0 = state_ref[SUBLANES - 1:SUBLANES, cols]
        if h0_add is not None:
            h0 = h0 + h0_add[:, cols]
        p = prod[c]
        e = end[c] + jnp.where(row == 0, p * h0, 0.0)
        for sh in (1, 2, 4):
            e = e + p * shifted(e, sh, 0.0)
            if sh < 4:
                p = p * shifted(p, sh, 1.0)
        state_ref[:, cols] = e
        h.append(shifted(e, 1, h0))
    for j in range(seg):
        for c in range(n):
            h[c] = step(a_scr, c, j) * h[c] + step(b_scr, c, j)
            h_scr[c, pl.ds(j, SUBLANES, stride=pitch), :] = h[c]
    return [jnp.concatenate([_rows(h_scr, c, s * pitch, seg) for s in range(SUBLANES)], axis=0)
            for c in range(n)]


def _conv_via_scratch(scr_ref, lead0, tail, x, w, b):
    k_width = w.shape[0]
    t, c = x.shape
    outs = []
    for g in range(c // LANES):
        cols = slice(g * LANES, (g + 1) * LANES)
        scr_ref[lead0 + g, 0:SUBLANES, :] = tail[:, cols]
        scr_ref[lead0 + g, SUBLANES:SUBLANES + t, :] = x[:, cols]
    for g in range(c // LANES):
        cols = slice(g * LANES, (g + 1) * LANES)
        out = x[:, cols] * w[k_width - 1:k_width, cols] + b[:, cols]
        for k in range(k_width - 1):
            shifted = _rows(scr_ref, lead0 + g, SUBLANES - (k_width - 1 - k), t)
            out = out + shifted * w[k:k + 1, cols]
        outs.append(out)
    return outs


def _mod_kernel(c_ref, w_ref, b_ref, o_ref):
    c = c_ref[...]
    rows = c.shape[0]
    c_act = c * jax.nn.sigmoid(c)
    pad = (-rows) % SUBLANES
    if pad:
        c_act = jnp.concatenate([c_act, jnp.zeros((pad, c.shape[1]), F32)], axis=0)
    part = _dot(c_act.astype(BF16), w_ref[...].astype(BF16))[:rows]

    @pl.when(pl.program_id(0) == 0)
    def _():
        o_ref[...] = part + b_ref[...]

    @pl.when(pl.program_id(0) > 0)
    def _():
        o_ref[...] += part


def _modulation(c, w_ada, b_ada):
    rows, d = c.shape
    n = w_ada.shape[1]
    bk = MOD_ROWS
    assert d % bk == 0
    return pl.pallas_call(
        _mod_kernel,
        grid=(d // bk,),
        in_specs=[
            pl.BlockSpec((rows, bk), lambda k: (0, k)),
            pl.BlockSpec((bk, n), lambda k: (k, 0)),
            pl.BlockSpec((1, n), lambda k: (0, 0)),
        ],
        out_specs=pl.BlockSpec((rows, n), lambda k: (0, 0)),
        out_shape=jax.ShapeDtypeStruct((rows, n), F32),
        compiler_params=pltpu.CompilerParams(
            dimension_semantics=("arbitrary",), vmem_limit_bytes=VMEM_LIMIT),
        name="adaln_mod",
    )(c, w_ada, b_ada)


def _mixer_kernel(x_ref, mod_ref, gpre_ref, gpost_ref, win_ref, cw_ref, cb_ref,
                  wgate_ref, brg_ref, big_ref, lrua_ref, vng_ref, vnb_ref, wsp_ref, bsp_ref,
                  glru_ref, ggmlp_ref, wout_ref, *rest, n_later):
    later_in, o_ref, later_out = rest[:n_later], rest[n_later], rest[n_later + 1:2 * n_later + 1]
    scratch = rest[2 * n_later + 1:]
    for w_ref, p_ref in zip(later_in, later_out):
        p_ref[...] = pltpu.bitcast(w_ref[...].astype(BF16), U32)
    nb_tiles = x_ref.shape[0]
    tail_ref, state_ref = scratch[4 * nb_tiles:]

    @pl.when(pl.program_id(1) == 0)
    def _():
        tail_ref[...] = jnp.zeros_like(tail_ref)
        state_ref[...] = jnp.zeros_like(state_ref)

    project = lambda k: _mix_in(x_ref.at[k], mod_ref[k:k + 1, :], gpre_ref, win_ref)
    z = project(0)
    for k in range(nb_tiles):
        conv_scr, a_scr, b_scr, h_scr = scratch[k:4 * nb_tiles:nb_tiles]
        xc, gates = _mix_gates(z[0], cw_ref, cb_ref, wgate_ref, conv_scr, tail_ref.at[k])
        sp_groups = _mix_positions(z[3], vng_ref, vnb_ref, wsp_ref)
        z_next = project(k + 1) if k + 1 < nb_tiles else None
        hooks = [_zero_row_from(part) for part in z_next] if z_next else [None] * len(z)
        y = _mix_body(z, xc, gates, sp_groups, hooks, brg_ref, big_ref, lrua_ref, bsp_ref,
                      glru_ref, ggmlp_ref, a_scr, b_scr, h_scr, state_ref.at[k])
        _mix_out(y, x_ref.at[k], mod_ref[k:k + 1, :], gpost_ref, wout_ref, o_ref.at[k])
        z = z_next


def _mix_in(x_ref, mod, gpre_ref, win_ref):
    d = x_ref.shape[1]
    sh_m = mod[:, 0:d]
    sc_m = mod[:, d:2 * d]
    hb = (_rms(x_ref[...]) * (gpre_ref[...] * (1.0 + sc_m)) + sh_m).astype(BF16)
    widths = (LRU_WIDTH, LRU_WIDTH, GMLP_WIDTH, GMLP_WIDTH)
    starts = [sum(widths[:q]) for q in range(len(widths))]
    return [_dot(hb, _as_bf16(win_ref[:, c0:c0 + w])) for c0, w in zip(starts, widths)]


def _mix_gates(lru_x, cw_ref, cb_ref, wgate_ref, conv_scr, tail_ref):
    t = lru_x.shape[0]
    tail = tail_ref[...]
    tail_ref[...] = lru_x[t - SUBLANES:]
    xc = jnp.concatenate(
        _conv_via_scratch(conv_scr, 0, tail, lru_x, cw_ref[...], cb_ref[...]), axis=1)
    xcb = xc.astype(BF16)
    gates = [_dot(xcb[:, j * GATE_TILE:(j + 1) * GATE_TILE], _as_bf16(wgate_ref[j]))
             for j in range(LRU_WIDTH // GATE_TILE)]
    return xc, gates


def _mix_out(y, x_ref, mod, gpost_ref, wout_ref, o_ref):
    d = x_ref.shape[1]
    gt_m = mod[:, 2 * d:3 * d]
    y = _dot(y, _as_bf16(wout_ref[...]))
    o_ref[...] = x_ref[...] + (gt_m * gpost_ref[...]) * _rms(y)


def _mix_positions(g_v, vng_ref, vnb_ref, wsp_ref):
    t = g_v.shape[0]
    gv = _gelu(g_v)
    mu = jnp.mean(gv, axis=-1, keepdims=True)
    cen = gv - mu
    var = jnp.mean(cen * cen, axis=-1, keepdims=True)
    v = cen * lax.rsqrt(var + EPS) * vng_ref[...] + vnb_ref[...]
    vb = v.astype(BF16)
    nb = t // GMLP_BLOCK
    pi = lax.broadcasted_iota(jnp.int32, (GMLP_BLOCK, GMLP_BLOCK), 0) // CHUNK
    pj = lax.broadcasted_iota(jnp.int32, (GMLP_BLOCK, GMLP_BLOCK), 1) // CHUNK
    mask = pj <= pi
    sp_groups = []
    for g in range(GMLP_GROUPS):
        ws = jnp.where(mask, 0.5 * wsp_ref[g], 0.0).astype(BF16)
        cols = slice(g * GMLP_GROUP_DIM, (g + 1) * GMLP_GROUP_DIM)
        rhs = jnp.concatenate(
            [vb[n * GMLP_BLOCK:(n + 1) * GMLP_BLOCK, cols] for n in range(nb)], axis=1)
        sp_groups.append(_dot(ws, rhs))
    return sp_groups


def _mix_body(z, xc, gates, sp_groups, hooks, brg_ref, big_ref, lrua_ref, bsp_ref,
              glru_ref, ggmlp_ref, a_scr, b_scr, h_scr, state_ref):
    _, lru_gate, g_u, _ = z
    t = lru_gate.shape[0]
    slabs = lambda v: [v[:, c * LANES:(c + 1) * LANES] for c in range(v.shape[1] // LANES)]
    hooked = lambda v, hook: v if hook is None else v + hook

    r_pre = jnp.concatenate([g[:, :GATE_TILE] for g in gates], axis=1)
    i_pre = jnp.concatenate([g[:, GATE_TILE:] for g in gates], axis=1)
    r_gate = jax.nn.sigmoid(r_pre + brg_ref[...])
    i_gate = jax.nn.sigmoid(i_pre + big_ref[...])
    neg_a = -lrua_ref[...]
    softplus = jnp.maximum(neg_a, 0.0) + jnp.log1p(jnp.exp(-jnp.abs(neg_a)))
    log_a = r_gate * ((-LRU_C) * softplus)
    a = jnp.exp(log_a)
    th = jnp.tanh(log_a)
    q = (-2.0 * th) / (1.0 - th)
    mult = jnp.where(q > 0.0, q * lax.rsqrt(q), 0.0)
    bx = (mult * xc) * i_gate
    hs = _linear_scan(slabs(a), slabs(bx), state_ref, a_scr, b_scr, h_scr, h0_add=hooks[0])
    y_lru = 0.5 * _gelu2_times(lru_gate, jnp.concatenate(hs, axis=1))
    yl = _rms(y_lru) * hooked(glru_ref[...], hooks[1])

    nb = t // GMLP_BLOCK
    bsp = hooked(0.5 * bsp_ref[...], hooks[2])
    sp_rows = []
    for n in range(nb):
        blk = jnp.concatenate(
            [sg[:, n * GMLP_GROUP_DIM:(n + 1) * GMLP_GROUP_DIM] for sg in sp_groups], axis=1)
        sp_rows.append(blk + bsp)
    sp_half = jnp.concatenate(sp_rows, axis=0)
    y_gmlp = _gelu2_times(g_u, sp_half)
    yg = _rms(y_gmlp) * hooked(ggmlp_ref[...], hooks[3])

    return jnp.concatenate([yl, yg], axis=1).astype(BF16)


def _mixer(x, mod, g_pre, g_post, w_in, conv_w, conv_b, w_gate, b_rgate, b_igate, lru_a,
           v_norm_g, v_norm_b, w_spatial, b_sp_full, g_lru_out, g_gmlp_out, w_out, later_weights,
           layer):
    bsz, s, d = x.shape
    t = MIX_T
    nb = MIX_NB
    assert bsz % nb == 0 and s % t == 0
    n_lg = LRU_WIDTH // LANES
    scan_rows = SUBLANES * (t // SUBLANES + 1)
    full = lambda a: pl.BlockSpec(a.shape, lambda b, i: (0,) * a.ndim)
    of_layer = lambda a: pl.BlockSpec((None,) + a.shape[1:],
                                      lambda b, i: (layer,) + (0,) * (a.ndim - 1))
    in_arrays = [g_pre, g_post, w_in, conv_w, conv_b, w_gate, b_rgate, b_igate, lru_a,
                 v_norm_g, v_norm_b, w_spatial, b_sp_full, g_lru_out, g_gmlp_out, w_out]
    in_array_specs = [of_layer(a) if a is conv_w or a is w_spatial else full(a)
                      for a in in_arrays]
    nt = s // t
    n_steps = (bsz // nb) * nt
    step = lambda b, i: (b * nt + i, 0)
    later_in, later_out, later_shapes = [], [], []
    for w in later_weights:
        k, n = w.shape
        assert k % (n_steps * 2 * SUBLANES) == 0
        later_in.append(pl.BlockSpec((k // n_steps, n), step))
        later_out.append(pl.BlockSpec((k // n_steps // 2, n), step))
        later_shapes.append(jax.ShapeDtypeStruct((k // 2, n), U32))
    out = pl.pallas_call(
        functools.partial(_mixer_kernel, n_later=len(later_weights)),
        grid=(bsz // nb, nt),
        in_specs=[
            pl.BlockSpec((nb, t, d), lambda b, i: (b, i, 0)),
            pl.BlockSpec((nb, mod.shape[1]), lambda b, i: (b, 0)),
        ] + in_array_specs + later_in,
        out_specs=[pl.BlockSpec((nb, t, d), lambda b, i: (b, i, 0))] + later_out,
        out_shape=[jax.ShapeDtypeStruct(x.shape, x.dtype)] + later_shapes,
        scratch_shapes=(
            [pltpu.VMEM((n_lg, SUBLANES + t, LANES), F32)] * nb
            + [pltpu.VMEM((n_lg, scan_rows, LANES), F32)] * (3 * nb)
            + [pltpu.VMEM((nb, SUBLANES, LRU_WIDTH), F32)] * 2),
        compiler_params=pltpu.CompilerParams(
            dimension_semantics=("arbitrary", "arbitrary"), vmem_limit_bytes=VMEM_LIMIT),
        name="token_mixer",
    )(x, mod, *in_arrays, *later_weights)
    return out[0], out[1:]


def _ffn_kernel(x_ref, mod_ref, gpre_ref, gpost_ref, wup_ref, cw_ref, cb_ref, wd_ref,
                o_ref, scr_ref, tail_ref):
    t_idx = pl.program_id(1)
    t, d = x_ref.shape
    d_ff = 2 * wd_ref.shape[0]
    fc = FFN_FC
    lg = fc // LANES

    @pl.when(t_idx == 0)
    def _():
        tail_ref[...] = jnp.zeros_like(tail_ref)

    mod = mod_ref[pl.ds(pl.program_id(0), 1), :]
    sh_f = mod[:, 3 * d:4 * d]
    sc_f = mod[:, 4 * d:5 * d]
    gt_f = mod[:, 5 * d:6 * d]
    x = x_ref[...]
    hb = (_rms(x) * (gpre_ref[...] * (1.0 + sc_f)) + sh_f).astype(BF16)

    n_chunks = d_ff // fc
    halves = ((0, 1.0), (d_ff, 0.5))
    chunk_cols = lambda base, j: slice(base + j * fc, base + (j + 1) * fc)
    up_proj = lambda j: [_dot(hb, _as_bf16(wup_ref[:, chunk_cols(base, j)])) for base, _ in halves]

    def gated(j, ups):
        conv = []
        for k, (base, scale) in enumerate(halves):
            cols = chunk_cols(base, j)
            tail = tail_ref[:, cols]
            tail_ref[:, cols] = ups[k][t - SUBLANES:]
            slot = (2 * j + k) % FFN_SLOTS
            conv.append(_conv_via_scratch(scr_ref, slot * lg, tail, ups[k],
                                          cw_ref[:, cols] * scale, cb_ref[:, cols] * scale))
        return jnp.concatenate(
            [_gelu2_times(cg, cv) for cg, cv in zip(*conv)], axis=1).astype(BF16)

    acc = None
    ups = up_proj(0)
    for j in range(n_chunks):
        ups_next = up_proj(j + 1) if j + 1 < n_chunks else None
        part = _dot(gated(j, ups), _as_bf16(wd_ref[j * fc // 2:(j + 1) * fc // 2, :]))
        acc = part if acc is None else acc + part
        ups = ups_next
    o_ref[...] = x + (gt_f * gpost_ref[...]) * _rms(acc)


def _ffn(x, mod, g_pre, g_post, w_up, conv_w, conv_b, w_down, layer):
    bsz, s, d = x.shape
    t = FFN_T
    resident = lambda a: pl.BlockSpec(a.shape, lambda b, i: (0,) * a.ndim,
                                      pipeline_mode=pl.Buffered(1))
    return pl.pallas_call(
        _ffn_kernel,
        grid=(bsz, s // t),
        in_specs=[
            pl.BlockSpec((None, t, d), lambda b, i: (b, i, 0)),
            pl.BlockSpec(mod.shape, lambda b, i: (0, 0)),
            resident(g_pre), resident(g_post), resident(w_up),
            pl.BlockSpec((None,) + conv_w.shape[1:], lambda b, i: (layer, 0, 0),
                         pipeline_mode=pl.Buffered(1)),
            resident(conv_b), resident(w_down),
        ],
        out_specs=pl.BlockSpec((None, t, d), lambda b, i: (b, i, 0)),
        out_shape=jax.ShapeDtypeStruct(x.shape, x.dtype),
        scratch_shapes=[
            pltpu.VMEM((FFN_SLOTS * (FFN_FC // LANES), SUBLANES + t, LANES), F32),
            pltpu.VMEM((SUBLANES, w_up.shape[1]), F32),
        ],
        compiler_params=pltpu.CompilerParams(
            dimension_semantics=("arbitrary", "arbitrary"), vmem_limit_bytes=VMEM_LIMIT),
        name="conv_ffn",
    )(x, mod, g_pre, g_post, w_up, conv_w, conv_b, w_down)


def _gate_weights(w_rgate, w_igate):
    per_tile = GATE_TILE // LRU_HEAD_DIM
    n_tiles = LRU_WIDTH // GATE_TILE
    eye = jnp.eye(per_tile, dtype=F32)

    def block_diag(w):
        w4 = w.reshape(n_tiles, per_tile, LRU_HEAD_DIM, LRU_HEAD_DIM)
        return (w4[:, :, :, None, :] * eye[None, :, None, :, None]).reshape(
            n_tiles, GATE_TILE, GATE_TILE)

    return _pack_rows_xla(jnp.concatenate([block_diag(w_rgate), block_diag(w_igate)], axis=2))


def kernel(x, c, w_ada, b_ada, g_mix_pre, g_mix_post, w_in, conv_w, conv_b, w_rgate, b_rgate, w_igate, b_igate, lru_a, v_norm_g, v_norm_b, w_spatial, b_spatial, g_lru_out, g_gmlp_out, w_out, g_ffn_pre, g_ffn_post, w_up, ffn_conv_w, ffn_conv_b, w_down):
    depth = w_ada.shape[0]
    bsz, s, d = x.shape
    for l in range(depth):
        mod = _modulation(c, w_ada[l], b_ada[l:l + 1])
        row = lambda a: a[l:l + 1]
        w_gate = _gate_weights(w_rgate[l:l + 1], w_igate[l:l + 1])
        b_gate = (b_rgate[l].reshape(1, -1), b_igate[l].reshape(1, -1))
        b_sp_full = jnp.repeat(b_spatial[l].T, GMLP_GROUP_DIM, axis=1)
        x, (w_up_p, w_down_p) = _mixer(
                   x, mod, row(g_mix_pre), row(g_mix_post), _pack_rows(w_in[l]),
                   conv_w, row(conv_b), w_gate, *b_gate, row(lru_a),
                   row(v_norm_g), row(v_norm_b), w_spatial, b_sp_full,
                   row(g_lru_out), row(g_gmlp_out), _pack_rows(w_out[l]),
                   later_weights=[w_up[l], w_down[l]], layer=l)
        x = _ffn(x, mod, row(g_ffn_pre), row(g_ffn_post), w_up_p,
                 ffn_conv_w, row(ffn_conv_b), w_down_p, layer=l)
    return x
```

```python
import functools

import jax
import jax.numpy as jnp
from jax import lax
from jax.experimental import pallas as pl
from jax.experimental.pallas import tpu as pltpu

CHUNK = 64
LRU_WIDTH = 512
LRU_HEADS = 8
LRU_HEAD_DIM = LRU_WIDTH // LRU_HEADS
LRU_CONV_WIDTH = 4
LRU_C = 8.0
GMLP_WIDTH = 512
GMLP_GROUPS = 4
GMLP_GROUP_DIM = GMLP_WIDTH // GMLP_GROUPS
GMLP_BLOCK = 128
FFN_CONV_WIDTH = 3
N_MOD = 6
EPS = 1e-6

SUBLANES = 8
LANES = 128
GATE_TILE = 256

MOD_ROWS = 128
MIX_T = 256
MIX_NB = 4
FFN_T = 512
FFN_FC = 1024
FFN_SLOTS = 4
PACK_BLOCK_BYTES = 4 * 1024 * 1024
VMEM_LIMIT = 56 * 1024 * 1024

F32 = jnp.float32
BF16 = jnp.bfloat16
U32 = jnp.uint32


def _dot(a, b):
    return jnp.dot(a, b, preferred_element_type=F32)


def _pack_rows_xla(w):
    k, n = w.shape[-2:]
    wb = w.astype(BF16).reshape(*w.shape[:-2], k // 2, 2, n)
    return lax.bitcast_convert_type(jnp.swapaxes(wb, -1, -2), U32)


def _pack_kernel(w_ref, o_ref):
    o_ref[...] = pltpu.bitcast(w_ref[...].astype(BF16), U32)


def _pack_rows(w, layer):
    _, k, n = w.shape
    bk = k
    while bk * n * 4 > PACK_BLOCK_BYTES and bk % (4 * SUBLANES) == 0:
        bk //= 2
    assert k % bk == 0
    return pl.pallas_call(
        _pack_kernel,
        grid=(k // bk,),
        in_specs=[pl.BlockSpec((None, bk, n), lambda i: (layer, i, 0))],
        out_specs=pl.BlockSpec((bk // 2, n), lambda i: (i, 0)),
        out_shape=jax.ShapeDtypeStruct((k // 2, n), U32),
        compiler_params=pltpu.CompilerParams(
            dimension_semantics=("arbitrary",), vmem_limit_bytes=VMEM_LIMIT),
        name="pack_weight",
    )(w)


def _as_bf16(packed):
    return pltpu.bitcast(packed, BF16)


def _zero_row_from(x):
    last = lax.bitcast_convert_type(x[x.shape[0] - SUBLANES:], U32)
    zero = lax.shift_right_logical(lax.shift_right_logical(last, U32(16)), U32(16))
    return zero[0:1].astype(F32)


def _rms(x):
    return x * lax.rsqrt(jnp.mean(x * x, axis=-1, keepdims=True) + EPS)


def _gelu(x):
    c0 = 0.7978845608028654
    hx = 0.5 * x
    return hx + hx * jnp.tanh(x * (c0 + (c0 * 0.044715) * (x * x)))


def _gelu2_times(x, v):
    c0 = 0.7978845608028654
    z = x * (c0 + (c0 * 0.044715) * (x * x))
    return (x * v) * (1.0 + jnp.tanh(z))


def _rows(ref, lead, start, size):
    return ref[pl.ds(lead, 1, stride=2), pl.ds(start, size), :][0]


def _put_rows(ref, lead, start, val):
    ref[pl.ds(lead, 1, stride=2), pl.ds(start, val.shape[0]), :] = val[None]


def _linear_scan(a, b, state_ref, a_scr, b_scr, h_scr, h0_add=None):
    t = a[0].shape[0]
    seg = t // SUBLANES
    pitch = seg + 1 - seg % 2
    n = len(a)
    for c in range(n):
        for s in range(SUBLANES):
            _put_rows(a_scr, c, s * pitch, a[c][s * seg:(s + 1) * seg])
            _put_rows(b_scr, c, s * pitch, b[c][s * seg:(s + 1) * seg])
    step = lambda ref, c, j: ref[c, pl.ds(j, SUBLANES, stride=pitch), :]
    row = lax.broadcasted_iota(jnp.int32, (SUBLANES, LANES), 0)
    shifted = lambda v, sh, fill: jnp.where(row >= sh, pltpu.roll(v, sh, axis=0), fill)

    prod = [None] * n
    end = [None] * n
    for j in range(seg):
        for c in range(n):
            aj, bj = step(a_scr, c, j), step(b_scr, c, j)
            prod[c] = aj if j == 0 else aj * prod[c]
            end[c] = bj if j == 0 else aj * end[c] + bj

    h = []
    for c in range(n):
        cols = slice(c * LANES, (c + 1) * LANES)
        h0 = state_ref[SUBLANES - 1:SUBLANES, cols]
        if h0_add is not None:
            h0 = h0 + h0_add[:, cols]
        p = prod[c]
        e = end[c] + jnp.where(row == 0, p * h0, 0.0)
        for sh in (1, 2, 4):
            e = e + p * shifted(e, sh, 0.0)
            if sh < 4:
                p = p * shifted(p, sh, 1.0)
        state_ref[:, cols] = e
        h.append(shifted(e, 1, h0))
    for j in range(seg):
        for c in range(n):
            h[c] = step(a_scr, c, j) * h[c] + step(b_scr, c, j)
            h_scr[c, pl.ds(j, SUBLANES, stride=pitch), :] = h[c]
    return [jnp.concatenate([_rows(h_scr, c, s * pitch, seg) for s in range(SUBLANES)], axis=0)
            for c in range(n)]


def _conv_via_scratch(scr_ref, lead0, tail, x, w, b):
    k_width = w.shape[0]
    t, c = x.shape
    outs = []
    for g in range(c // LANES):
        cols = slice(g * LANES, (g + 1) * LANES)
        scr_ref[lead0 + g, 0:SUBLANES, :] = tail[:, cols]
        scr_ref[lead0 + g, SUBLANES:SUBLANES + t, :] = x[:, cols]
    for g in range(c // LANES):
        cols = slice(g * LANES, (g + 1) * LANES)
        out = x[:, cols] * w[k_width - 1:k_width, cols] + b[:, cols]
        for k in range(k_width - 1):
            shifted = _rows(scr_ref, lead0 + g, SUBLANES - (k_width - 1 - k), t)
            out = out + shifted * w[k:k + 1, cols]
        outs.append(out)
    return outs


def _mod_kernel(c_ref, w_ref, b_ref, o_ref):
    c = c_ref[...]
    rows = c.shape[0]
    c_act = c * jax.nn.sigmoid(c)
    pad = (-rows) % SUBLANES
    if pad:
        c_act = jnp.concatenate([c_act, jnp.zeros((pad, c.shape[1]), F32)], axis=0)
    part = _dot(c_act.astype(BF16), w_ref[...].astype(BF16))[:rows]

    @pl.when(pl.program_id(0) == 0)
    def _():
        o_ref[...] = part + b_ref[...]

    @pl.when(pl.program_id(0) > 0)
    def _():
        o_ref[...] += part


def _modulation(c, w_ada, b_ada):
    rows, d = c.shape
    n = w_ada.shape[1]
    bk = MOD_ROWS
    assert d % bk == 0
    return pl.pallas_call(
        _mod_kernel,
        grid=(d // bk,),
        in_specs=[
            pl.BlockSpec((rows, bk), lambda k: (0, k)),
            pl.BlockSpec((bk, n), lambda k: (k, 0)),
            pl.BlockSpec((1, n), lambda k: (0, 0)),
        ],
        out_specs=pl.BlockSpec((rows, n), lambda k: (0, 0)),
        out_shape=jax.ShapeDtypeStruct((rows, n), F32),
        compiler_params=pltpu.CompilerParams(
            dimension_semantics=("arbitrary",), vmem_limit_bytes=VMEM_LIMIT),
        name="adaln_mod",
    )(c, w_ada, b_ada)


def _mixer_kernel(x_ref, mod_ref, gpre_ref, gpost_ref, win_ref, cw_ref, cb_ref,
                  wgate_ref, brg_ref, big_ref, lrua_ref, vng_ref, vnb_ref, wsp_ref, bsp_ref,
                  glru_ref, ggmlp_ref, wout_ref, *rest, n_later):
    later_in, o_ref, later_out = rest[:n_later], rest[n_later], rest[n_later + 1:2 * n_later + 1]
    scratch = rest[2 * n_later + 1:]
    for w_ref, p_ref in zip(later_in, later_out):
        p_ref[...] = pltpu.bitcast(w_ref[...].astype(BF16), U32)
    nb_tiles = x_ref.shape[0]
    tail_ref, state_ref = scratch[4 * nb_tiles:]

    @pl.when(pl.program_id(1) == 0)
    def _():
        tail_ref[...] = jnp.zeros_like(tail_ref)
        state_ref[...] = jnp.zeros_like(state_ref)

    mod_row = lambda k: mod_ref[pl.ds(pl.program_id(0) * nb_tiles + k, 1), :]
    project = lambda k: _mix_in(x_ref.at[k], mod_row(k), gpre_ref, win_ref)
    z = project(0)
    for k in range(nb_tiles):
        conv_scr, a_scr, b_scr, h_scr = scratch[k:4 * nb_tiles:nb_tiles]
        xc, gates = _mix_gates(z[0], cw_ref, cb_ref, wgate_ref, conv_scr, tail_ref.at[k])
        sp_groups = _mix_positions(z[3], vng_ref, vnb_ref, wsp_ref)
        z_next = project(k + 1) if k + 1 < nb_tiles else None
        hooks = [_zero_row_from(part) for part in z_next] if z_next else [None] * len(z)
        y = _mix_body(z, xc, gates, sp_groups, hooks, brg_ref, big_ref, lrua_ref, bsp_ref,
                      glru_ref, ggmlp_ref, a_scr, b_scr, h_scr, state_ref.at[k])
        _mix_out(y, x_ref.at[k], mod_row(k), gpost_ref, wout_ref, o_ref.at[k])
        z = z_next


def _mix_in(x_ref, mod, gpre_ref, win_ref):
    d = x_ref.shape[1]
    sh_m = mod[:, 0:d]
    sc_m = mod[:, d:2 * d]
    hb = (_rms(x_ref[...]) * (gpre_ref[...] * (1.0 + sc_m)) + sh_m).astype(BF16)
    widths = (LRU_WIDTH, LRU_WIDTH, GMLP_WIDTH, GMLP_WIDTH)
    starts = [sum(widths[:q]) for q in range(len(widths))]
    return [_dot(hb, _as_bf16(win_ref[:, c0:c0 + w])) for c0, w in zip(starts, widths)]


def _mix_gates(lru_x, cw_ref, cb_ref, wgate_ref, conv_scr, tail_ref):
    t = lru_x.shape[0]
    tail = tail_ref[...]
    tail_ref[...] = lru_x[t - SUBLANES:]
    xc = jnp.concatenate(
        _conv_via_scratch(conv_scr, 0, tail, lru_x, cw_ref[...], cb_ref[...]), axis=1)
    xcb = xc.astype(BF16)
    gates = [_dot(xcb[:, j * GATE_TILE:(j + 1) * GATE_TILE], _as_bf16(wgate_ref[j]))
             for j in range(LRU_WIDTH // GATE_TILE)]
    return xc, gates


def _mix_out(y, x_ref, mod, gpost_ref, wout_ref, o_ref):
    d = x_ref.shape[1]
    gt_m = mod[:, 2 * d:3 * d]
    y = _dot(y, _as_bf16(wout_ref[...]))
    o_ref[...] = x_ref[...] + (gt_m * gpost_ref[...]) * _rms(y)


def _mix_positions(g_v, vng_ref, vnb_ref, wsp_ref):
    t = g_v.shape[0]
    gv = _gelu(g_v)
    mu = jnp.mean(gv, axis=-1, keepdims=True)
    cen = gv - mu
    var = jnp.mean(cen * cen, axis=-1, keepdims=True)
    v = cen * lax.rsqrt(var + EPS) * vng_ref[...] + vnb_ref[...]
    vb = v.astype(BF16)
    nb = t // GMLP_BLOCK
    pi = lax.broadcasted_iota(jnp.int32, (GMLP_BLOCK, GMLP_BLOCK), 0) // CHUNK
    pj = lax.broadcasted_iota(jnp.int32, (GMLP_BLOCK, GMLP_BLOCK), 1) // CHUNK
    mask = pj <= pi
    sp_groups = []
    for g in range(GMLP_GROUPS):
        ws = jnp.where(mask, 0.5 * wsp_ref[g], 0.0).astype(BF16)
        cols = slice(g * GMLP_GROUP_DIM, (g + 1) * GMLP_GROUP_DIM)
        rhs = jnp.concatenate(
            [vb[n * GMLP_BLOCK:(n + 1) * GMLP_BLOCK, cols] for n in range(nb)], axis=1)
        sp_groups.append(_dot(ws, rhs))
    return sp_groups


def _mix_body(z, xc, gates, sp_groups, hooks, brg_ref, big_ref, lrua_ref, bsp_ref,
              glru_ref, ggmlp_ref, a_scr, b_scr, h_scr, state_ref):
    _, lru_gate, g_u, _ = z
    t = lru_gate.shape[0]
    slabs = lambda v: [v[:, c * LANES:(c + 1) * LANES] for c in range(v.shape[1] // LANES)]
    hooked = lambda v, hook: v if hook is None else v + hook

    r_pre = jnp.concatenate([g[:, :GATE_TILE] for g in gates], axis=1)
    i_pre = jnp.concatenate([g[:, GATE_TILE:] for g in gates], axis=1)
    r_gate = jax.nn.sigmoid(r_pre + brg_ref[...])
    i_gate = jax.nn.sigmoid(i_pre + big_ref[...])
    neg_a = -lrua_ref[...]
    softplus = jnp.maximum(neg_a, 0.0) + jnp.log1p(jnp.exp(-jnp.abs(neg_a)))
    log_a = r_gate * ((-LRU_C) * softplus)
    a = jnp.exp(log_a)
    th = jnp.tanh(log_a)
    q = (-2.0 * th) / (1.0 - th)
    mult = jnp.where(q > 0.0, q * lax.rsqrt(q), 0.0)
    bx = (mult * xc) * i_gate
    hs = _linear_scan(slabs(a), slabs(bx), state_ref, a_scr, b_scr, h_scr, h0_add=hooks[0])
    y_lru = 0.5 * _gelu2_times(lru_gate, jnp.concatenate(hs, axis=1))
    yl = _rms(y_lru) * hooked(glru_ref[...], hooks[1])

    nb = t // GMLP_BLOCK
    bsp = hooked(0.5 * bsp_ref[...], hooks[2])
    sp_rows = []
    for n in range(nb):
        blk = jnp.concatenate(
            [sg[:, n * GMLP_GROUP_DIM:(n + 1) * GMLP_GROUP_DIM] for sg in sp_groups], axis=1)
        sp_rows.append(blk + bsp)
    sp_half = jnp.concatenate(sp_rows, axis=0)
    y_gmlp = _gelu2_times(g_u, sp_half)
    yg = _rms(y_gmlp) * hooked(ggmlp_ref[...], hooks[3])

    return jnp.concatenate([yl, yg], axis=1).astype(BF16)


def _mixer(x, mod, g_pre, g_post, w_in, conv_w, conv_b, w_gate, b_rgate, b_igate, lru_a,
           v_norm_g, v_norm_b, w_spatial, b_sp_full, g_lru_out, g_gmlp_out, w_out, later_weights,
           layer):
    bsz, s, d = x.shape
    t = MIX_T
    nb = MIX_NB
    assert bsz % nb == 0 and s % t == 0
    n_lg = LRU_WIDTH // LANES
    scan_rows = SUBLANES * (t // SUBLANES + 1)
    full = lambda a: pl.BlockSpec(a.shape, lambda b, i: (0,) * a.ndim)
    of_layer = lambda a: pl.BlockSpec((None,) + a.shape[1:],
                                      lambda b, i: (layer,) + (0,) * (a.ndim - 1))
    in_arrays = [g_pre, g_post, w_in, conv_w, conv_b, w_gate, b_rgate, b_igate, lru_a,
                 v_norm_g, v_norm_b, w_spatial, b_sp_full, g_lru_out, g_gmlp_out, w_out]
    in_array_specs = [of_layer(a) if a is conv_w or a is w_spatial else full(a)
                      for a in in_arrays]
    nt = s // t
    n_steps = (bsz // nb) * nt
    step = lambda b, i: (b * nt + i, 0)
    later_in, later_out, later_shapes = [], [], []
    for w in later_weights:
        _, k, n = w.shape
        assert k % (n_steps * 2 * SUBLANES) == 0
        later_in.append(pl.BlockSpec((None, k // n_steps, n),
                                     lambda b, i: (layer, b * nt + i, 0)))
        later_out.append(pl.BlockSpec((k // n_steps // 2, n), step))
        later_shapes.append(jax.ShapeDtypeStruct((k // 2, n), U32))
    out = pl.pallas_call(
        functools.partial(_mixer_kernel, n_later=len(later_weights)),
        grid=(bsz // nb, nt),
        in_specs=[
            pl.BlockSpec((nb, t, d), lambda b, i: (b, i, 0)),
            pl.BlockSpec(mod.shape, lambda b, i: (0, 0)),
        ] + in_array_specs + later_in,
        out_specs=[pl.BlockSpec((nb, t, d), lambda b, i: (b, i, 0))] + later_out,
        out_shape=[jax.ShapeDtypeStruct(x.shape, x.dtype)] + later_shapes,
        scratch_shapes=(
            [pltpu.VMEM((n_lg, SUBLANES + t, LANES), F32)] * nb
            + [pltpu.VMEM((n_lg, scan_rows, LANES), F32)] * (3 * nb)
            + [pltpu.VMEM((nb, SUBLANES, LRU_WIDTH), F32)] * 2),
        compiler_params=pltpu.CompilerParams(
            dimension_semantics=("arbitrary", "arbitrary"), vmem_limit_bytes=VMEM_LIMIT),
        name="token_mixer",
    )(x, mod, *in_arrays, *later_weights)
    return out[0], out[1:]


def _ffn_kernel(x_ref, mod_ref, gpre_ref, gpost_ref, wup_ref, cw_ref, cb_ref, wd_ref,
                o_ref, scr_ref, tail_ref):
    t_idx = pl.program_id(1)
    t, d = x_ref.shape
    d_ff = 2 * wd_ref.shape[0]
    fc = FFN_FC
    lg = fc // LANES

    @pl.when(t_idx == 0)
    def _():
        tail_ref[...] = jnp.zeros_like(tail_ref)

    mod = mod_ref[pl.ds(pl.program_id(0), 1), :]
    sh_f = mod[:, 3 * d:4 * d]
    sc_f = mod[:, 4 * d:5 * d]
    gt_f = mod[:, 5 * d:6 * d]
    x = x_ref[...]
    hb = (_rms(x) * (gpre_ref[...] * (1.0 + sc_f)) + sh_f).astype(BF16)

    n_chunks = d_ff // fc
    halves = ((0, 1.0), (d_ff, 0.5))
    chunk_cols = lambda base, j: slice(base + j * fc, base + (j + 1) * fc)
    up_proj = lambda j: [_dot(hb, _as_bf16(wup_ref[:, chunk_cols(base, j)])) for base, _ in halves]

    def gated(j, ups):
        conv = []
        for k, (base, scale) in enumerate(halves):
            cols = chunk_cols(base, j)
            tail = tail_ref[:, cols]
            tail_ref[:, cols] = ups[k][t - SUBLANES:]
            slot = (2 * j + k) % FFN_SLOTS
            conv.append(_conv_via_scratch(scr_ref, slot * lg, tail, ups[k],
                                          cw_ref[:, cols] * scale, cb_ref[:, cols] * scale))
        return jnp.concatenate(
            [_gelu2_times(cg, cv) for cg, cv in zip(*conv)], axis=1).astype(BF16)

    acc = None
    ups = up_proj(0)
    for j in range(n_chunks):
        ups_next = up_proj(j + 1) if j + 1 < n_chunks else None
        part = _dot(gated(j, ups), _as_bf16(wd_ref[j * fc // 2:(j + 1) * fc // 2, :]))
        acc = part if acc is None else acc + part
        ups = ups_next
    o_ref[...] = x + (gt_f * gpost_ref[...]) * _rms(acc)


def _ffn(x, mod, g_pre, g_post, w_up, conv_w, conv_b, w_down, layer):
    bsz, s, d = x.shape
    t = FFN_T
    resident = lambda a: pl.BlockSpec(a.shape, lambda b, i: (0,) * a.ndim,
                                      pipeline_mode=pl.Buffered(1))
    return pl.pallas_call(
        _ffn_kernel,
        grid=(bsz, s // t),
        in_specs=[
            pl.BlockSpec((None, t, d), lambda b, i: (b, i, 0)),
            pl.BlockSpec(mod.shape, lambda b, i: (0, 0)),
            resident(g_pre), resident(g_post), resident(w_up),
            pl.BlockSpec((None,) + conv_w.shape[1:], lambda b, i: (layer, 0, 0),
                         pipeline_mode=pl.Buffered(1)),
            resident(conv_b), resident(w_down),
        ],
        out_specs=pl.BlockSpec((None, t, d), lambda b, i: (b, i, 0)),
        out_shape=jax.ShapeDtypeStruct(x.shape, x.dtype),
        scratch_shapes=[
            pltpu.VMEM((FFN_SLOTS * (FFN_FC // LANES), SUBLANES + t, LANES), F32),
            pltpu.VMEM((SUBLANES, w_up.shape[1]), F32),
        ],
        compiler_params=pltpu.CompilerParams(
            dimension_semantics=("arbitrary", "arbitrary"), vmem_limit_bytes=VMEM_LIMIT),
        name="conv_ffn",
    )(x, mod, g_pre, g_post, w_up, conv_w, conv_b, w_down)


def _gate_weights(w_rgate, w_igate):
    per_tile = GATE_TILE // LRU_HEAD_DIM
    n_tiles = LRU_WIDTH // GATE_TILE
    eye = jnp.eye(per_tile, dtype=F32)

    def block_diag(w):
        w4 = w.reshape(n_tiles, per_tile, LRU_HEAD_DIM, LRU_HEAD_DIM)
        return (w4[:, :, :, None, :] * eye[None, :, None, :, None]).reshape(
            n_tiles, GATE_TILE, GATE_TILE)

    return _pack_rows_xla(jnp.concatenate([block_diag(w_rgate), block_diag(w_igate)], axis=2))


def kernel(x, c, w_ada, b_ada, g_mix_pre, g_mix_post, w_in, conv_w, conv_b, w_rgate, b_rgate, w_igate, b_igate, lru_a, v_norm_g, v_norm_b, w_spatial, b_spatial, g_lru_out, g_gmlp_out, w_out, g_ffn_pre, g_ffn_post, w_up, ffn_conv_w, ffn_conv_b, w_down):
    depth = w_ada.shape[0]
    bsz, s, d = x.shape
    for l in range(depth):
        mod = _modulation(c, w_ada[l], b_ada[l:l + 1])
        row = lambda a: a[l:l + 1]
        w_gate = _gate_weights(w_rgate[l:l + 1], w_igate[l:l + 1])
        b_gate = (b_rgate[l].reshape(1, -1), b_igate[l].reshape(1, -1))
        b_sp_full = jnp.repeat(b_spatial[l].T, GMLP_GROUP_DIM, axis=1)
        x, (w_up_p, w_down_p) = _mixer(
                   x, mod, row(g_mix_pre), row(g_mix_post), _pack_rows(w_in, l),
                   conv_w, row(conv_b), w_gate, *b_gate, row(lru_a),
                   row(v_norm_g), row(v_norm_b), w_spatial, b_sp_full,
                   row(g_lru_out), row(g_gmlp_out), _pack_rows(w_out, l),
                   later_weights=[w_up, w_down], layer=l)
        x = _ffn(x, mod, row(g_ffn_pre), row(g_ffn_post), w_up_p,
                 ffn_conv_w, row(ffn_conv_b), w_down_p, layer=l)
    return x
```

```python
import functools

import jax
import jax.numpy as jnp
from jax import lax
from jax.experimental import pallas as pl
from jax.experimental.pallas import tpu as pltpu

CHUNK = 64
LRU_WIDTH = 512
LRU_HEADS = 8
LRU_HEAD_DIM = LRU_WIDTH // LRU_HEADS
LRU_CONV_WIDTH = 4
LRU_C = 8.0
GMLP_WIDTH = 512
GMLP_GROUPS = 4
GMLP_GROUP_DIM = GMLP_WIDTH // GMLP_GROUPS
GMLP_BLOCK = 128
FFN_CONV_WIDTH = 3
N_MOD = 6
EPS = 1e-6

SUBLANES = 8
LANES = 128
GATE_TILE = 256

MOD_ROWS = 128
MIX_T = 256
MIX_NB = 4
FFN_T = 512
FFN_FC = 1024
FFN_SLOTS = 4
PACK_BLOCK_BYTES = 4 * 1024 * 1024
VMEM_LIMIT = 56 * 1024 * 1024

F32 = jnp.float32
BF16 = jnp.bfloat16
U32 = jnp.uint32


def _dot(a, b):
    return jnp.dot(a, b, preferred_element_type=F32)


def _pack_rows_xla(w):
    k, n = w.shape[-2:]
    wb = w.astype(BF16).reshape(*w.shape[:-2], k // 2, 2, n)
    return lax.bitcast_convert_type(jnp.swapaxes(wb, -1, -2), U32)


def _pack_kernel(w_ref, o_ref):
    o_ref[...] = pltpu.bitcast(w_ref[...].astype(BF16), U32)


def _pack_rows(w, layer):
    _, k, n = w.shape
    bk = k
    while bk * n * 4 > PACK_BLOCK_BYTES and bk % (4 * SUBLANES) == 0:
        bk //= 2
    assert k % bk == 0
    return pl.pallas_call(
        _pack_kernel,
        grid=(k // bk,),
        in_specs=[pl.BlockSpec((None, bk, n), lambda i: (layer, i, 0))],
        out_specs=pl.BlockSpec((bk // 2, n), lambda i: (i, 0)),
        out_shape=jax.ShapeDtypeStruct((k // 2, n), U32),
        compiler_params=pltpu.CompilerParams(
            dimension_semantics=("arbitrary",), vmem_limit_bytes=VMEM_LIMIT),
        name="pack_weight",
    )(w)


def _as_bf16(packed):
    return pltpu.bitcast(packed, BF16)


def _zero_row_from(x):
    last = lax.bitcast_convert_type(x[x.shape[0] - SUBLANES:], U32)
    zero = lax.shift_right_logical(lax.shift_right_logical(last, U32(16)), U32(16))
    return zero[0:1].astype(F32)


def _rms(x):
    return x * lax.rsqrt(jnp.mean(x * x, axis=-1, keepdims=True) + EPS)


def _gelu(x):
    c0 = 0.7978845608028654
    hx = 0.5 * x
    return hx + hx * jnp.tanh(x * (c0 + (c0 * 0.044715) * (x * x)))


def _gelu2_times(x, v):
    c0 = 0.7978845608028654
    z = x * (c0 + (c0 * 0.044715) * (x * x))
    return (x * v) * (1.0 + jnp.tanh(z))


def _rows(ref, lead, start, size):
    return ref[pl.ds(lead, 1, stride=2), pl.ds(start, size), :][0]


def _put_rows(ref, lead, start, val):
    ref[pl.ds(lead, 1, stride=2), pl.ds(start, val.shape[0]), :] = val[None]


def _linear_scan(a, b, state_ref, a_scr, b_scr, h_scr, h0_add=None):
    t = a[0].shape[0]
    seg = t // SUBLANES
    pitch = seg + 1 - seg % 2
    n = len(a)
    for c in range(n):
        for s in range(SUBLANES):
            _put_rows(a_scr, c, s * pitch, a[c][s * seg:(s + 1) * seg])
            _put_rows(b_scr, c, s * pitch, b[c][s * seg:(s + 1) * seg])
    step = lambda ref, c, j: ref[c, pl.ds(j, SUBLANES, stride=pitch), :]
    row = lax.broadcasted_iota(jnp.int32, (SUBLANES, LANES), 0)
    shifted = lambda v, sh, fill: jnp.where(row >= sh, pltpu.roll(v, sh, axis=0), fill)

    prod = [None] * n
    end = [None] * n
    for j in range(seg):
        for c in range(n):
            aj, bj = step(a_scr, c, j), step(b_scr, c, j)
            prod[c] = aj if j == 0 else aj * prod[c]
            end[c] = bj if j == 0 else aj * end[c] + bj

    h = []
    for c in range(n):
        cols = slice(c * LANES, (c + 1) * LANES)
        h0 = state_ref[SUBLANES - 1:SUBLANES, cols]
        if h0_add is not None:
            h0 = h0 + h0_add[:, cols]
        p = prod[c]
        e = end[c] + jnp.where(row == 0, p * h0, 0.0)
        for sh in (1, 2, 4):
            e = e + p * shifted(e, sh, 0.0)
            if sh < 4:
                p = p * shifted(p, sh, 1.0)
        state_ref[:, cols] = e
        h.append(shifted(e, 1, h0))
    for j in range(seg):
        for c in range(n):
            h[c] = step(a_scr, c, j) * h[c] + step(b_scr, c, j)
            h_scr[c, pl.ds(j, SUBLANES, stride=pitch), :] = h[c]
    return [jnp.concatenate([_rows(h_scr, c, s * pitch, seg) for s in range(SUBLANES)], axis=0)
            for c in range(n)]


def _conv_via_scratch(scr_ref, lead0, tail, x, w, b):
    k_width = w.shape[0]
    t, c = x.shape
    outs = []
    for g in range(c // LANES):
        cols = slice(g * LANES, (g + 1) * LANES)
        scr_ref[lead0 + g, 0:SUBLANES, :] = tail[:, cols]
        scr_ref[lead0 + g, SUBLANES:SUBLANES + t, :] = x[:, cols]
    for g in range(c // LANES):
        cols = slice(g * LANES, (g + 1) * LANES)
        out = x[:, cols] * w[k_width - 1:k_width, cols] + b[:, cols]
        for k in range(k_width - 1):
            shifted = _rows(scr_ref, lead0 + g, SUBLANES - (k_width - 1 - k), t)
            out = out + shifted * w[k:k + 1, cols]
        outs.append(out)
    return outs


def _mod_kernel(c_ref, w_ref, b_ref, o_ref):
    c = c_ref[...]
    rows = c.shape[0]
    c_act = c * jax.nn.sigmoid(c)
    pad = (-rows) % SUBLANES
    if pad:
        c_act = jnp.concatenate([c_act, jnp.zeros((pad, c.shape[1]), F32)], axis=0)
    part = _dot(c_act.astype(BF16), w_ref[...].astype(BF16))[:rows]

    @pl.when(pl.program_id(0) == 0)
    def _():
        o_ref[...] = part + b_ref[...]

    @pl.when(pl.program_id(0) > 0)
    def _():
        o_ref[...] += part


def _modulation(c, w_ada, b_ada, layer):
    rows, d = c.shape
    n = w_ada.shape[2]
    bk = MOD_ROWS
    assert d % bk == 0
    return pl.pallas_call(
        _mod_kernel,
        grid=(d // bk,),
        in_specs=[
            pl.BlockSpec((rows, bk), lambda k: (0, k)),
            pl.BlockSpec((None, bk, n), lambda k: (layer, k, 0)),
            pl.BlockSpec((1, n), lambda k: (0, 0)),
        ],
        out_specs=pl.BlockSpec((rows, n), lambda k: (0, 0)),
        out_shape=jax.ShapeDtypeStruct((rows, n), F32),
        compiler_params=pltpu.CompilerParams(
            dimension_semantics=("arbitrary",), vmem_limit_bytes=VMEM_LIMIT),
        name="adaln_mod",
    )(c, w_ada, b_ada)


def _mixer_kernel(x_ref, mod_ref, gpre_ref, gpost_ref, win_ref, cw_ref, cb_ref,
                  wgate_ref, brg_ref, big_ref, lrua_ref, vng_ref, vnb_ref, wsp_ref, bsp_ref,
                  glru_ref, ggmlp_ref, wout_ref, *rest, n_later):
    later_in, o_ref, later_out = rest[:n_later], rest[n_later], rest[n_later + 1:2 * n_later + 1]
    scratch = rest[2 * n_later + 1:]
    for w_ref, p_ref in zip(later_in, later_out):
        p_ref[...] = pltpu.bitcast(w_ref[...].astype(BF16), U32)
    nb_tiles = x_ref.shape[0]
    tail_ref, state_ref = scratch[4 * nb_tiles:]

    @pl.when(pl.program_id(1) == 0)
    def _():
        tail_ref[...] = jnp.zeros_like(tail_ref)
        state_ref[...] = jnp.zeros_like(state_ref)

    mod_row = lambda k: mod_ref[pl.ds(pl.program_id(0) * nb_tiles + k, 1), :]
    project = lambda k: _mix_in(x_ref.at[k], mod_row(k), gpre_ref, win_ref)
    z = project(0)
    for k in range(nb_tiles):
        conv_scr, a_scr, b_scr, h_scr = scratch[k:4 * nb_tiles:nb_tiles]
        xc, gates = _mix_gates(z[0], cw_ref, cb_ref, wgate_ref, conv_scr, tail_ref.at[k])
        sp_groups = _mix_positions(z[3], vng_ref, vnb_ref, wsp_ref)
        z_next = project(k + 1) if k + 1 < nb_tiles else None
        hooks = [None] * len(z)
        y = _mix_body(z, xc, gates, sp_groups, hooks, brg_ref, big_ref, lrua_ref, bsp_ref,
                      glru_ref, ggmlp_ref, a_scr, b_scr, h_scr, state_ref.at[k])
        _mix_out(y, x_ref.at[k], mod_row(k), gpost_ref, wout_ref, o_ref.at[k])
        z = z_next


def _mix_in(x_ref, mod, gpre_ref, win_ref):
    d = x_ref.shape[1]
    sh_m = mod[:, 0:d]
    sc_m = mod[:, d:2 * d]
    hb = (_rms(x_ref[...]) * (gpre_ref[...] * (1.0 + sc_m)) + sh_m).astype(BF16)
    widths = (LRU_WIDTH, LRU_WIDTH, GMLP_WIDTH, GMLP_WIDTH)
    starts = [sum(widths[:q]) for q in range(len(widths))]
    return [_dot(hb, _as_bf16(win_ref[:, c0:c0 + w])) for c0, w in zip(starts, widths)]


def _mix_gates(lru_x, cw_ref, cb_ref, wgate_ref, conv_scr, tail_ref):
    t = lru_x.shape[0]
    tail = tail_ref[...]
    tail_ref[...] = lru_x[t - SUBLANES:]
    xc = jnp.concatenate(
        _conv_via_scratch(conv_scr, 0, tail, lru_x, cw_ref[...], cb_ref[...]), axis=1)
    xcb = xc.astype(BF16)
    gates = [_dot(xcb[:, j * GATE_TILE:(j + 1) * GATE_TILE], _as_bf16(wgate_ref[j]))
             for j in range(LRU_WIDTH // GATE_TILE)]
    return xc, gates


def _mix_out(y, x_ref, mod, gpost_ref, wout_ref, o_ref):
    d = x_ref.shape[1]
    gt_m = mod[:, 2 * d:3 * d]
    y = _dot(y, _as_bf16(wout_ref[...]))
    o_ref[...] = x_ref[...] + (gt_m * gpost_ref[...]) * _rms(y)


def _mix_positions(g_v, vng_ref, vnb_ref, wsp_ref):
    t = g_v.shape[0]
    gv = _gelu(g_v)
    mu = jnp.mean(gv, axis=-1, keepdims=True)
    cen = gv - mu
    var = jnp.mean(cen * cen, axis=-1, keepdims=True)
    v = cen * lax.rsqrt(var + EPS) * vng_ref[...] + vnb_ref[...]
    vb = v.astype(BF16)
    nb = t // GMLP_BLOCK
    pi = lax.broadcasted_iota(jnp.int32, (GMLP_BLOCK, GMLP_BLOCK), 0) // CHUNK
    pj = lax.broadcasted_iota(jnp.int32, (GMLP_BLOCK, GMLP_BLOCK), 1) // CHUNK
    mask = pj <= pi
    sp_groups = []
    for g in range(GMLP_GROUPS):
        ws = jnp.where(mask, 0.5 * wsp_ref[g], 0.0).astype(BF16)
        cols = slice(g * GMLP_GROUP_DIM, (g + 1) * GMLP_GROUP_DIM)
        rhs = jnp.concatenate(
            [vb[n * GMLP_BLOCK:(n + 1) * GMLP_BLOCK, cols] for n in range(nb)], axis=1)
        sp_groups.append(_dot(ws, rhs))
    return sp_groups


def _mix_body(z, xc, gates, sp_groups, hooks, brg_ref, big_ref, lrua_ref, bsp_ref,
              glru_ref, ggmlp_ref, a_scr, b_scr, h_scr, state_ref):
    _, lru_gate, g_u, _ = z
    t = lru_gate.shape[0]
    slabs = lambda v: [v[:, c * LANES:(c + 1) * LANES] for c in range(v.shape[1] // LANES)]
    hooked = lambda v, hook: v if hook is None else v + hook

    r_pre = jnp.concatenate([g[:, :GATE_TILE] for g in gates], axis=1)
    i_pre = jnp.concatenate([g[:, GATE_TILE:] for g in gates], axis=1)
    r_gate = jax.nn.sigmoid(r_pre + brg_ref[...])
    i_gate = jax.nn.sigmoid(i_pre + big_ref[...])
    neg_a = -lrua_ref[...]
    softplus = jnp.maximum(neg_a, 0.0) + jnp.log1p(jnp.exp(-jnp.abs(neg_a)))
    log_a = r_gate * ((-LRU_C) * softplus)
    a = jnp.exp(log_a)
    th = jnp.tanh(log_a)
    q = (-2.0 * th) / (1.0 - th)
    mult = jnp.where(q > 0.0, q * lax.rsqrt(q), 0.0)
    bx = (mult * xc) * i_gate
    hs = _linear_scan(slabs(a), slabs(bx), state_ref, a_scr, b_scr, h_scr, h0_add=hooks[0])
    y_lru = 0.5 * _gelu2_times(lru_gate, jnp.concatenate(hs, axis=1))
    yl = _rms(y_lru) * hooked(glru_ref[...], hooks[1])

    nb = t // GMLP_BLOCK
    bsp = hooked(0.5 * bsp_ref[...], hooks[2])
    sp_rows = []
    for n in range(nb):
        blk = jnp.concatenate(
            [sg[:, n * GMLP_GROUP_DIM:(n + 1) * GMLP_GROUP_DIM] for sg in sp_groups], axis=1)
        sp_rows.append(blk + bsp)
    sp_half = jnp.concatenate(sp_rows, axis=0)
    y_gmlp = _gelu2_times(g_u, sp_half)
    yg = _rms(y_gmlp) * hooked(ggmlp_ref[...], hooks[3])

    return jnp.concatenate([yl, yg], axis=1).astype(BF16)


def _mixer(x, mod, g_pre, g_post, w_in, conv_w, conv_b, w_gate, b_rgate, b_igate, lru_a,
           v_norm_g, v_norm_b, w_spatial, b_sp_full, g_lru_out, g_gmlp_out, w_out, later_weights,
           layer):
    bsz, s, d = x.shape
    t = MIX_T
    nb = MIX_NB
    assert bsz % nb == 0 and s % t == 0
    n_lg = LRU_WIDTH // LANES
    scan_rows = SUBLANES * (t // SUBLANES + 1)
    full = lambda a: pl.BlockSpec(a.shape, lambda b, i: (0,) * a.ndim)
    of_layer = lambda a: pl.BlockSpec((None,) + a.shape[1:],
                                      lambda b, i: (layer,) + (0,) * (a.ndim - 1))
    in_arrays = [g_pre, g_post, w_in, conv_w, conv_b, w_gate, b_rgate, b_igate, lru_a,
                 v_norm_g, v_norm_b, w_spatial, b_sp_full, g_lru_out, g_gmlp_out, w_out]
    in_array_specs = [of_layer(a) if a is conv_w or a is w_spatial else full(a)
                      for a in in_arrays]
    nt = s // t
    n_steps = (bsz // nb) * nt
    step = lambda b, i: (b * nt + i, 0)
    later_in, later_out, later_shapes = [], [], []
    for w in later_weights:
        _, k, n = w.shape
        assert k % (n_steps * 2 * SUBLANES) == 0
        later_in.append(pl.BlockSpec((None, k // n_steps, n),
                                     lambda b, i: (layer, b * nt + i, 0)))
        later_out.append(pl.BlockSpec((k // n_steps // 2, n), step))
        later_shapes.append(jax.ShapeDtypeStruct((k // 2, n), U32))
    out = pl.pallas_call(
        functools.partial(_mixer_kernel, n_later=len(later_weights)),
        grid=(bsz // nb, nt),
        in_specs=[
            pl.BlockSpec((nb, t, d), lambda b, i: (b, i, 0)),
            pl.BlockSpec(mod.shape, lambda b, i: (0, 0)),
        ] + in_array_specs + later_in,
        out_specs=[pl.BlockSpec((nb, t, d), lambda b, i: (b, i, 0))] + later_out,
        out_shape=[jax.ShapeDtypeStruct(x.shape, x.dtype)] + later_shapes,
        scratch_shapes=(
            [pltpu.VMEM((n_lg, SUBLANES + t, LANES), F32)] * nb
            + [pltpu.VMEM((n_lg, scan_rows, LANES), F32)] * (3 * nb)
            + [pltpu.VMEM((nb, SUBLANES, LRU_WIDTH), F32)] * 2),
        compiler_params=pltpu.CompilerParams(
            dimension_semantics=("arbitrary", "arbitrary"), vmem_limit_bytes=VMEM_LIMIT),
        name="token_mixer",
    )(x, mod, *in_arrays, *later_weights)
    return out[0], out[1:]


def _ffn_kernel(x_ref, mod_ref, gpre_ref, gpost_ref, wup_ref, cw_ref, cb_ref, wd_ref,
                o_ref, scr_ref, tail_ref):
    t_idx = pl.program_id(1)
    t, d = x_ref.shape
    d_ff = 2 * wd_ref.shape[0]
    fc = FFN_FC
    lg = fc // LANES

    @pl.when(t_idx == 0)
    def _():
        tail_ref[...] = jnp.zeros_like(tail_ref)

    mod = mod_ref[pl.ds(pl.program_id(0), 1), :]
    sh_f = mod[:, 3 * d:4 * d]
    sc_f = mod[:, 4 * d:5 * d]
    gt_f = mod[:, 5 * d:6 * d]
    x = x_ref[...]
    hb = (_rms(x) * (gpre_ref[...] * (1.0 + sc_f)) + sh_f).astype(BF16)

    n_chunks = d_ff // fc
    halves = ((0, 1.0), (d_ff, 0.5))
    chunk_cols = lambda base, j: slice(base + j * fc, base + (j + 1) * fc)
    up_proj = lambda j: [_dot(hb, _as_bf16(wup_ref[:, chunk_cols(base, j)])) for base, _ in halves]

    def gated(j, ups):
        conv = []
        for k, (base, scale) in enumerate(halves):
            cols = chunk_cols(base, j)
            tail = tail_ref[:, cols]
            tail_ref[:, cols] = ups[k][t - SUBLANES:]
            slot = (2 * j + k) % FFN_SLOTS
            conv.append(_conv_via_scratch(scr_ref, slot * lg, tail, ups[k],
                                          cw_ref[:, cols] * scale, cb_ref[:, cols] * scale))
        return jnp.concatenate(
            [_gelu2_times(cg, cv) for cg, cv in zip(*conv)], axis=1).astype(BF16)

    acc = None
    ups = up_proj(0)
    for j in range(n_chunks):
        ups_next = up_proj(j + 1) if j + 1 < n_chunks else None
        part = _dot(gated(j, ups), _as_bf16(wd_ref[j * fc // 2:(j + 1) * fc // 2, :]))
        acc = part if acc is None else acc + part
        ups = ups_next
    o_ref[...] = x + (gt_f * gpost_ref[...]) * _rms(acc)


def _ffn(x, mod, g_pre, g_post, w_up, conv_w, conv_b, w_down, layer):
    bsz, s, d = x.shape
    t = FFN_T
    resident = lambda a: pl.BlockSpec(a.shape, lambda b, i: (0,) * a.ndim,
                                      pipeline_mode=pl.Buffered(1))
    return pl.pallas_call(
        _ffn_kernel,
        grid=(bsz, s // t),
        in_specs=[
            pl.BlockSpec((None, t, d), lambda b, i: (b, i, 0)),
            pl.BlockSpec(mod.shape, lambda b, i: (0, 0)),
            resident(g_pre), resident(g_post), resident(w_up),
            pl.BlockSpec((None,) + conv_w.shape[1:], lambda b, i: (layer, 0, 0),
                         pipeline_mode=pl.Buffered(1)),
            resident(conv_b), resident(w_down),
        ],
        out_specs=pl.BlockSpec((None, t, d), lambda b, i: (b, i, 0)),
        out_shape=jax.ShapeDtypeStruct(x.shape, x.dtype),
        scratch_shapes=[
            pltpu.VMEM((FFN_SLOTS * (FFN_FC // LANES), SUBLANES + t, LANES), F32),
            pltpu.VMEM((SUBLANES, w_up.shape[1]), F32),
        ],
        compiler_params=pltpu.CompilerParams(
            dimension_semantics=("arbitrary", "arbitrary"), vmem_limit_bytes=VMEM_LIMIT),
        name="conv_ffn",
    )(x, mod, g_pre, g_post, w_up, conv_w, conv_b, w_down)


def _gate_weights(w_rgate, w_igate):
    per_tile = GATE_TILE // LRU_HEAD_DIM
    n_tiles = LRU_WIDTH // GATE_TILE
    eye = jnp.eye(per_tile, dtype=F32)

    def block_diag(w):
        w4 = w.reshape(n_tiles, per_tile, LRU_HEAD_DIM, LRU_HEAD_DIM)
        return (w4[:, :, :, None, :] * eye[None, :, None, :, None]).reshape(
            n_tiles, GATE_TILE, GATE_TILE)

    return _pack_rows_xla(jnp.concatenate([block_diag(w_rgate), block_diag(w_igate)], axis=2))


def kernel(x, c, w_ada, b_ada, g_mix_pre, g_mix_post, w_in, conv_w, conv_b, w_rgate, b_rgate, w_igate, b_igate, lru_a, v_norm_g, v_norm_b, w_spatial, b_spatial, g_lru_out, g_gmlp_out, w_out, g_ffn_pre, g_ffn_post, w_up, ffn_conv_w, ffn_conv_b, w_down):
    depth = w_ada.shape[0]
    bsz, s, d = x.shape
    for l in range(depth):
        mod = _modulation(c, w_ada, b_ada[l:l + 1], l)
        row = lambda a: a[l:l + 1]
        w_gate = _gate_weights(w_rgate[l:l + 1], w_igate[l:l + 1])
        b_gate = (b_rgate[l:l + 1].reshape(1, -1), b_igate[l:l + 1].reshape(1, -1))
        b_sp_full = jnp.broadcast_to(
            jnp.transpose(b_spatial[l:l + 1], (2, 0, 1))[..., None],
            (GMLP_BLOCK, 1, GMLP_GROUPS, GMLP_GROUP_DIM)).reshape(GMLP_BLOCK, GMLP_WIDTH)
        x, (w_up_p, w_down_p) = _mixer(
                   x, mod, row(g_mix_pre), row(g_mix_post), _pack_rows(w_in, l),
                   conv_w, row(conv_b), w_gate, *b_gate, row(lru_a),
                   row(v_norm_g), row(v_norm_b), w_spatial, b_sp_full,
                   row(g_lru_out), row(g_gmlp_out), _pack_rows(w_out, l),
                   later_weights=[w_up, w_down], layer=l)
        x = _ffn(x, mod, row(g_ffn_pre), row(g_ffn_post), w_up_p,
                 ffn_conv_w, row(ffn_conv_b), w_down_p, layer=l)
    return x
```

```python
import functools

import jax
import jax.numpy as jnp
from jax import lax
from jax.experimental import pallas as pl
from jax.experimental.pallas import tpu as pltpu

CHUNK = 64
LRU_WIDTH = 512
LRU_HEADS = 8
LRU_HEAD_DIM = LRU_WIDTH // LRU_HEADS
LRU_CONV_WIDTH = 4
LRU_C = 8.0
GMLP_WIDTH = 512
GMLP_GROUPS = 4
GMLP_GROUP_DIM = GMLP_WIDTH // GMLP_GROUPS
GMLP_BLOCK = 128
FFN_CONV_WIDTH = 3
N_MOD = 6
EPS = 1e-6

SUBLANES = 8
LANES = 128
GATE_TILE = 256

MOD_ROWS = 128
MIX_T = 256
MIX_NB = 4
FFN_T = 512
FFN_FC = 1024
FFN_SLOTS = 4
PACK_BLOCK_BYTES = 4 * 1024 * 1024
VMEM_LIMIT = 56 * 1024 * 1024

F32 = jnp.float32
BF16 = jnp.bfloat16
U32 = jnp.uint32


def _dot(a, b):
    return jnp.dot(a, b, preferred_element_type=F32)


def _pack_rows_xla(w):
    k, n = w.shape[-2:]
    wb = w.astype(BF16).reshape(*w.shape[:-2], k // 2, 2, n)
    return lax.bitcast_convert_type(jnp.swapaxes(wb, -1, -2), U32)


def _pack_kernel(w_ref, o_ref):
    o_ref[...] = pltpu.bitcast(w_ref[...].astype(BF16), U32)


def _pack_rows(w, layer):
    _, k, n = w.shape
    bk = k
    while bk * n * 4 > PACK_BLOCK_BYTES and bk % (4 * SUBLANES) == 0:
        bk //= 2
    assert k % bk == 0
    return pl.pallas_call(
        _pack_kernel,
        grid=(k // bk,),
        in_specs=[pl.BlockSpec((None, bk, n), lambda i: (layer, i, 0))],
        out_specs=pl.BlockSpec((bk // 2, n), lambda i: (i, 0)),
        out_shape=jax.ShapeDtypeStruct((k // 2, n), U32),
        compiler_params=pltpu.CompilerParams(
            dimension_semantics=("arbitrary",), vmem_limit_bytes=VMEM_LIMIT),
        name="pack_weight",
    )(w)


def _as_bf16(packed):
    return pltpu.bitcast(packed, BF16)


def _rms(x):
    return x * lax.rsqrt(jnp.mean(x * x, axis=-1, keepdims=True) + EPS)


def _gelu(x):
    c0 = 0.7978845608028654
    hx = 0.5 * x
    return hx + hx * jnp.tanh(x * (c0 + (c0 * 0.044715) * (x * x)))


def _gelu2_times(x, v):
    c0 = 0.7978845608028654
    z = x * (c0 + (c0 * 0.044715) * (x * x))
    return (x * v) * (1.0 + jnp.tanh(z))


def _rows(ref, lead, start, size):
    return ref[pl.ds(lead, 1, stride=2), pl.ds(start, size), :][0]


def _put_rows(ref, lead, start, val):
    ref[pl.ds(lead, 1, stride=2), pl.ds(start, val.shape[0]), :] = val[None]


def _linear_scan(a, b, state_ref, a_scr, b_scr, h_scr):
    t = a[0].shape[0]
    seg = t // SUBLANES
    pitch = seg + 1 - seg % 2
    n = len(a)
    for c in range(n):
        for s in range(SUBLANES):
            _put_rows(a_scr, c, s * pitch, a[c][s * seg:(s + 1) * seg])
            _put_rows(b_scr, c, s * pitch, b[c][s * seg:(s + 1) * seg])
    step = lambda ref, c, j: ref[c, pl.ds(j, SUBLANES, stride=pitch), :]
    row = lax.broadcasted_iota(jnp.int32, (SUBLANES, LANES), 0)
    shifted = lambda v, sh, fill: jnp.where(row >= sh, pltpu.roll(v, sh, axis=0), fill)

    prod = [None] * n
    end = [None] * n
    for j in range(seg):
        for c in range(n):
            aj, bj = step(a_scr, c, j), step(b_scr, c, j)
            prod[c] = aj if j == 0 else aj * prod[c]
            end[c] = bj if j == 0 else aj * end[c] + bj

    h = []
    for c in range(n):
        cols = slice(c * LANES, (c + 1) * LANES)
        h0 = state_ref[SUBLANES - 1:SUBLANES, cols]
        p = prod[c]
        e = end[c] + jnp.where(row == 0, p * h0, 0.0)
        for sh in (1, 2, 4):
            e = e + p * shifted(e, sh, 0.0)
            if sh < 4:
                p = p * shifted(p, sh, 1.0)
        state_ref[:, cols] = e
        h.append(shifted(e, 1, h0))
    for j in range(seg):
        for c in range(n):
            h[c] = step(a_scr, c, j) * h[c] + step(b_scr, c, j)
            h_scr[c, pl.ds(j, SUBLANES, stride=pitch), :] = h[c]
    return [jnp.concatenate([_rows(h_scr, c, s * pitch, seg) for s in range(SUBLANES)], axis=0)
            for c in range(n)]


def _conv_via_scratch(scr_ref, lead0, tail, x, w, b):
    k_width = w.shape[0]
    t, c = x.shape
    outs = []
    for g in range(c // LANES):
        cols = slice(g * LANES, (g + 1) * LANES)
        scr_ref[lead0 + g, 0:SUBLANES, :] = tail[:, cols]
        scr_ref[lead0 + g, SUBLANES:SUBLANES + t, :] = x[:, cols]
    for g in range(c // LANES):
        cols = slice(g * LANES, (g + 1) * LANES)
        out = x[:, cols] * w[k_width - 1:k_width, cols] + b[:, cols]
        for k in range(k_width - 1):
            shifted = _rows(scr_ref, lead0 + g, SUBLANES - (k_width - 1 - k), t)
            out = out + shifted * w[k:k + 1, cols]
        outs.append(out)
    return outs


def _mod_kernel(c_ref, w_ref, b_ref, o_ref):
    c = c_ref[...]
    rows = c.shape[0]
    c_act = c * jax.nn.sigmoid(c)
    pad = (-rows) % SUBLANES
    if pad:
        c_act = jnp.concatenate([c_act, jnp.zeros((pad, c.shape[1]), F32)], axis=0)
    part = _dot(c_act.astype(BF16), w_ref[...].astype(BF16))[:rows]

    @pl.when(pl.program_id(0) == 0)
    def _():
        o_ref[...] = part + b_ref[...]

    @pl.when(pl.program_id(0) > 0)
    def _():
        o_ref[...] += part


def _modulation(c, w_ada, b_ada, layer):
    rows, d = c.shape
    n = w_ada.shape[2]
    bk = MOD_ROWS
    assert d % bk == 0
    return pl.pallas_call(
        _mod_kernel,
        grid=(d // bk,),
        in_specs=[
            pl.BlockSpec((rows, bk), lambda k: (0, k)),
            pl.BlockSpec((None, bk, n), lambda k: (layer, k, 0)),
            pl.BlockSpec((1, n), lambda k: (0, 0)),
        ],
        out_specs=pl.BlockSpec((rows, n), lambda k: (0, 0)),
        out_shape=jax.ShapeDtypeStruct((rows, n), F32),
        compiler_params=pltpu.CompilerParams(
            dimension_semantics=("arbitrary",), vmem_limit_bytes=VMEM_LIMIT),
        name="adaln_mod",
    )(c, w_ada, b_ada)


def _mixer_kernel(x_ref, mod_ref, gpre_ref, gpost_ref, win_ref, cw_ref, cb_ref,
                  wgate_ref, brg_ref, big_ref, lrua_ref, vng_ref, vnb_ref, wsp_ref, bsp_ref,
                  glru_ref, ggmlp_ref, wout_ref, *rest, n_later):
    later_in, o_ref, later_out = rest[:n_later], rest[n_later], rest[n_later + 1:2 * n_later + 1]
    scratch = rest[2 * n_later + 1:]
    for w_ref, p_ref in zip(later_in, later_out):
        p_ref[...] = pltpu.bitcast(w_ref[...].astype(BF16), U32)
    nb_tiles = x_ref.shape[0]
    tail_ref, state_ref = scratch[4 * nb_tiles:]

    @pl.when(pl.program_id(1) == 0)
    def _():
        tail_ref[...] = jnp.zeros_like(tail_ref)
        state_ref[...] = jnp.zeros_like(state_ref)

    mod_row = lambda k: mod_ref[pl.ds(pl.program_id(0) * nb_tiles + k, 1), :]
    project = lambda k: _mix_in(x_ref.at[k], mod_row(k), gpre_ref, win_ref)
    z = project(0)
    for k in range(nb_tiles):
        conv_scr, a_scr, b_scr, h_scr = scratch[k:4 * nb_tiles:nb_tiles]
        xc, gates = _mix_gates(z[0], cw_ref, cb_ref, wgate_ref, conv_scr, tail_ref.at[k])
        z_next = project(k + 1) if k + 1 < nb_tiles else None
        sp_groups = _mix_positions(z[3], vng_ref, vnb_ref, wsp_ref)
        y = _mix_body(z, xc, gates, sp_groups, brg_ref, big_ref, lrua_ref, bsp_ref,
                      glru_ref, ggmlp_ref, a_scr, b_scr, h_scr, state_ref.at[k])
        _mix_out(y, x_ref.at[k], mod_row(k), gpost_ref, wout_ref, o_ref.at[k])
        z = z_next


def _mix_in(x_ref, mod, gpre_ref, win_ref):
    d = x_ref.shape[1]
    sh_m = mod[:, 0:d]
    sc_m = mod[:, d:2 * d]
    hb = (_rms(x_ref[...]) * (gpre_ref[...] * (1.0 + sc_m)) + sh_m).astype(BF16)
    widths = (LRU_WIDTH, LRU_WIDTH, GMLP_WIDTH, GMLP_WIDTH)
    starts = [sum(widths[:q]) for q in range(len(widths))]
    return [_dot(hb, _as_bf16(win_ref[:, c0:c0 + w])) for c0, w in zip(starts, widths)]


def _mix_gates(lru_x, cw_ref, cb_ref, wgate_ref, conv_scr, tail_ref):
    t = lru_x.shape[0]
    tail = tail_ref[...]
    tail_ref[...] = lru_x[t - SUBLANES:]
    xc = jnp.concatenate(
        _conv_via_scratch(conv_scr, 0, tail, lru_x, cw_ref[...], cb_ref[...]), axis=1)
    xcb = xc.astype(BF16)
    gates = [_dot(xcb[:, j * GATE_TILE:(j + 1) * GATE_TILE], _as_bf16(wgate_ref[j]))
             for j in range(LRU_WIDTH // GATE_TILE)]
    return xc, gates


def _mix_out(y, x_ref, mod, gpost_ref, wout_ref, o_ref):
    d = x_ref.shape[1]
    gt_m = mod[:, 2 * d:3 * d]
    y = _dot(y, _as_bf16(wout_ref[...]))
    o_ref[...] = x_ref[...] + (gt_m * gpost_ref[...]) * _rms(y)


def _mix_positions(g_v, vng_ref, vnb_ref, wsp_ref):
    t = g_v.shape[0]
    gv = _gelu(g_v)
    mu = jnp.mean(gv, axis=-1, keepdims=True)
    cen = gv - mu
    var = jnp.mean(cen * cen, axis=-1, keepdims=True)
    v = cen * lax.rsqrt(var + EPS) * vng_ref[...] + vnb_ref[...]
    vb = v.astype(BF16)
    nb = t // GMLP_BLOCK
    pi = lax.broadcasted_iota(jnp.int32, (GMLP_BLOCK, GMLP_BLOCK), 0) // CHUNK
    pj = lax.broadcasted_iota(jnp.int32, (GMLP_BLOCK, GMLP_BLOCK), 1) // CHUNK
    mask = pj <= pi
    sp_groups = []
    for g in range(GMLP_GROUPS):
        ws = jnp.where(mask, 0.5 * wsp_ref[g], 0.0).astype(BF16)
        cols = slice(g * GMLP_GROUP_DIM, (g + 1) * GMLP_GROUP_DIM)
        rhs = jnp.concatenate(
            [vb[n * GMLP_BLOCK:(n + 1) * GMLP_BLOCK, cols] for n in range(nb)], axis=1)
        sp_groups.append(_dot(ws, rhs))
    return sp_groups


def _mix_body(z, xc, gates, sp_groups, brg_ref, big_ref, lrua_ref, bsp_ref,
              glru_ref, ggmlp_ref, a_scr, b_scr, h_scr, state_ref):
    _, lru_gate, g_u, _ = z
    t = lru_gate.shape[0]
    slabs = lambda v: [v[:, c * LANES:(c + 1) * LANES] for c in range(v.shape[1] // LANES)]

    r_pre = jnp.concatenate([g[:, :GATE_TILE] for g in gates], axis=1)
    i_pre = jnp.concatenate([g[:, GATE_TILE:] for g in gates], axis=1)
    r_gate = jax.nn.sigmoid(r_pre + brg_ref[...])
    i_gate = jax.nn.sigmoid(i_pre + big_ref[...])
    neg_a = -lrua_ref[...]
    softplus = jnp.maximum(neg_a, 0.0) + jnp.log1p(jnp.exp(-jnp.abs(neg_a)))
    log_a = r_gate * ((-LRU_C) * softplus)
    a = jnp.exp(log_a)
    th = jnp.tanh(log_a)
    q = (-2.0 * th) / (1.0 - th)
    mult = jnp.where(q > 0.0, q * lax.rsqrt(q), 0.0)
    bx = (mult * xc) * i_gate
    hs = _linear_scan(slabs(a), slabs(bx), state_ref, a_scr, b_scr, h_scr)
    y_lru = 0.5 * _gelu2_times(lru_gate, jnp.concatenate(hs, axis=1))
    yl = _rms(y_lru) * glru_ref[...]

    nb = t // GMLP_BLOCK
    bsp = 0.5 * bsp_ref[...]
    sp_rows = []
    for n in range(nb):
        blk = jnp.concatenate(
            [sg[:, n * GMLP_GROUP_DIM:(n + 1) * GMLP_GROUP_DIM] for sg in sp_groups], axis=1)
        sp_rows.append(blk + bsp)
    sp_half = jnp.concatenate(sp_rows, axis=0)
    y_gmlp = _gelu2_times(g_u, sp_half)
    yg = _rms(y_gmlp) * ggmlp_ref[...]

    return jnp.concatenate([yl, yg], axis=1).astype(BF16)


def _mixer(x, mod, g_pre, g_post, w_in, conv_w, conv_b, w_gate, b_rgate, b_igate, lru_a,
           v_norm_g, v_norm_b, w_spatial, b_sp_full, g_lru_out, g_gmlp_out, w_out, later_weights,
           layer):
    bsz, s, d = x.shape
    t = MIX_T
    nb = MIX_NB
    assert bsz % nb == 0 and s % t == 0
    n_lg = LRU_WIDTH // LANES
    scan_rows = SUBLANES * (t // SUBLANES + 1)
    full = lambda a: pl.BlockSpec(a.shape, lambda b, i: (0,) * a.ndim)
    of_layer = lambda a: pl.BlockSpec((None,) + a.shape[1:],
                                      lambda b, i: (layer,) + (0,) * (a.ndim - 1))
    in_arrays = [g_pre, g_post, w_in, conv_w, conv_b, w_gate, b_rgate, b_igate, lru_a,
                 v_norm_g, v_norm_b, w_spatial, b_sp_full, g_lru_out, g_gmlp_out, w_out]
    in_array_specs = [of_layer(a) if a is conv_w or a is w_spatial else full(a)
                      for a in in_arrays]
    nt = s // t
    n_steps = (bsz // nb) * nt
    step = lambda b, i: (b * nt + i, 0)
    later_in, later_out, later_shapes = [], [], []
    for w in later_weights:
        _, k, n = w.shape
        assert k % (n_steps * 2 * SUBLANES) == 0
        later_in.append(pl.BlockSpec((None, k // n_steps, n),
                                     lambda b, i: (layer, b * nt + i, 0)))
        later_out.append(pl.BlockSpec((k // n_steps // 2, n), step))
        later_shapes.append(jax.ShapeDtypeStruct((k // 2, n), U32))
    out = pl.pallas_call(
        functools.partial(_mixer_kernel, n_later=len(later_weights)),
        grid=(bsz // nb, nt),
        in_specs=[
            pl.BlockSpec((nb, t, d), lambda b, i: (b, i, 0)),
            pl.BlockSpec(mod.shape, lambda b, i: (0, 0)),
        ] + in_array_specs + later_in,
        out_specs=[pl.BlockSpec((nb, t, d), lambda b, i: (b, i, 0))] + later_out,
        out_shape=[jax.ShapeDtypeStruct(x.shape, x.dtype)] + later_shapes,
        scratch_shapes=(
            [pltpu.VMEM((n_lg, SUBLANES + t, LANES), F32)] * nb
            + [pltpu.VMEM((n_lg, scan_rows, LANES), F32)] * (3 * nb)
            + [pltpu.VMEM((nb, SUBLANES, LRU_WIDTH), F32)] * 2),
        compiler_params=pltpu.CompilerParams(
            dimension_semantics=("arbitrary", "arbitrary"), vmem_limit_bytes=VMEM_LIMIT),
        name="token_mixer",
    )(x, mod, *in_arrays, *later_weights)
    return out[0], out[1:]


def _ffn_kernel(x_ref, mod_ref, gpre_ref, gpost_ref, wup_ref, cw_ref, cb_ref, wd_ref,
                o_ref, scr_ref, tail_ref):
    t_idx = pl.program_id(1)
    t, d = x_ref.shape
    d_ff = 2 * wd_ref.shape[0]
    fc = FFN_FC
    lg = fc // LANES

    @pl.when(t_idx == 0)
    def _():
        tail_ref[...] = jnp.zeros_like(tail_ref)

    mod = mod_ref[pl.ds(pl.program_id(0), 1), :]
    sh_f = mod[:, 3 * d:4 * d]
    sc_f = mod[:, 4 * d:5 * d]
    gt_f = mod[:, 5 * d:6 * d]
    x = x_ref[...]
    hb = (_rms(x) * (gpre_ref[...] * (1.0 + sc_f)) + sh_f).astype(BF16)

    n_chunks = d_ff // fc
    halves = ((0, 1.0), (d_ff, 0.5))
    chunk_cols = lambda base, j: slice(base + j * fc, base + (j + 1) * fc)
    up_proj = lambda j: [_dot(hb, _as_bf16(wup_ref[:, chunk_cols(base, j)])) for base, _ in halves]

    def gated(j, ups):
        conv = []
        for k, (base, scale) in enumerate(halves):
            cols = chunk_cols(base, j)
            tail = tail_ref[:, cols]
            tail_ref[:, cols] = ups[k][t - SUBLANES:]
            slot = (2 * j + k) % FFN_SLOTS
            conv.append(_conv_via_scratch(scr_ref, slot * lg, tail, ups[k],
                                          cw_ref[:, cols] * scale, cb_ref[:, cols] * scale))
        return jnp.concatenate(
            [_gelu2_times(cg, cv) for cg, cv in zip(*conv)], axis=1).astype(BF16)

    acc = None
    ups = up_proj(0)
    for j in range(n_chunks):
        ups_next = up_proj(j + 1) if j + 1 < n_chunks else None
        part = _dot(gated(j, ups), _as_bf16(wd_ref[j * fc // 2:(j + 1) * fc // 2, :]))
        acc = part if acc is None else acc + part
        ups = ups_next
    o_ref[...] = x + (gt_f * gpost_ref[...]) * _rms(acc)


def _ffn(x, mod, g_pre, g_post, w_up, conv_w, conv_b, w_down, layer):
    bsz, s, d = x.shape
    t = FFN_T
    resident = lambda a: pl.BlockSpec(a.shape, lambda b, i: (0,) * a.ndim,
                                      pipeline_mode=pl.Buffered(1))
    return pl.pallas_call(
        _ffn_kernel,
        grid=(bsz, s // t),
        in_specs=[
            pl.BlockSpec((None, t, d), lambda b, i: (b, i, 0)),
            pl.BlockSpec(mod.shape, lambda b, i: (0, 0)),
            resident(g_pre), resident(g_post), resident(w_up),
            pl.BlockSpec((None,) + conv_w.shape[1:], lambda b, i: (layer, 0, 0),
                         pipeline_mode=pl.Buffered(1)),
            resident(conv_b), resident(w_down),
        ],
        out_specs=pl.BlockSpec((None, t, d), lambda b, i: (b, i, 0)),
        out_shape=jax.ShapeDtypeStruct(x.shape, x.dtype),
        scratch_shapes=[
            pltpu.VMEM((FFN_SLOTS * (FFN_FC // LANES), SUBLANES + t, LANES), F32),
            pltpu.VMEM((SUBLANES, w_up.shape[1]), F32),
        ],
        compiler_params=pltpu.CompilerParams(
            dimension_semantics=("arbitrary", "arbitrary"), vmem_limit_bytes=VMEM_LIMIT),
        name="conv_ffn",
    )(x, mod, g_pre, g_post, w_up, conv_w, conv_b, w_down)


def _gate_weights(w_rgate, w_igate):
    per_tile = GATE_TILE // LRU_HEAD_DIM
    n_tiles = LRU_WIDTH // GATE_TILE
    eye = jnp.eye(per_tile, dtype=F32)

    def block_diag(w):
        w4 = w.reshape(n_tiles, per_tile, LRU_HEAD_DIM, LRU_HEAD_DIM)
        return (w4[:, :, :, None, :] * eye[None, :, None, :, None]).reshape(
            n_tiles, GATE_TILE, GATE_TILE)

    return _pack_rows_xla(jnp.concatenate([block_diag(w_rgate), block_diag(w_igate)], axis=2))


def kernel(x, c, w_ada, b_ada, g_mix_pre, g_mix_post, w_in, conv_w, conv_b, w_rgate, b_rgate, w_igate, b_igate, lru_a, v_norm_g, v_norm_b, w_spatial, b_spatial, g_lru_out, g_gmlp_out, w_out, g_ffn_pre, g_ffn_post, w_up, ffn_conv_w, ffn_conv_b, w_down):
    depth = w_ada.shape[0]
    bsz, s, d = x.shape
    for l in range(depth):
        mod = _modulation(c, w_ada, b_ada[l:l + 1], l)
        row = lambda a: a[l:l + 1]
        w_gate = _gate_weights(w_rgate[l:l + 1], w_igate[l:l + 1])
        b_gate = (b_rgate[l:l + 1].reshape(1, -1), b_igate[l:l + 1].reshape(1, -1))
        b_sp_full = jnp.broadcast_to(
            jnp.transpose(b_spatial[l:l + 1], (2, 0, 1))[..., None],
            (GMLP_BLOCK, 1, GMLP_GROUPS, GMLP_GROUP_DIM)).reshape(GMLP_BLOCK, GMLP_WIDTH)
        x, (w_up_p, w_down_p) = _mixer(
                   x, mod, row(g_mix_pre), row(g_mix_post), _pack_rows(w_in, l),
                   conv_w, row(conv_b), w_gate, *b_gate, row(lru_a),
                   row(v_norm_g), row(v_norm_b), w_spatial, b_sp_full,
                   row(g_lru_out), row(g_gmlp_out), _pack_rows(w_out, l),
                   later_weights=[w_up, w_down], layer=l)
        x = _ffn(x, mod, row(g_ffn_pre), row(g_ffn_post), w_up_p,
                 ffn_conv_w, row(ffn_conv_b), w_down_p, layer=l)
    return x
```

```python
import functools

import jax
import jax.numpy as jnp
from jax import lax
from jax.experimental import pallas as pl
from jax.experimental.pallas import tpu as pltpu

CHUNK = 64
LRU_WIDTH = 512
LRU_HEADS = 8
LRU_HEAD_DIM = LRU_WIDTH // LRU_HEADS
LRU_CONV_WIDTH = 4
LRU_C = 8.0
GMLP_WIDTH = 512
GMLP_GROUPS = 4
GMLP_GROUP_DIM = GMLP_WIDTH // GMLP_GROUPS
GMLP_BLOCK = 128
FFN_CONV_WIDTH = 3
N_MOD = 6
EPS = 1e-6

SUBLANES = 8
LANES = 128
GATE_TILE = 256

MOD_ROWS = 128
MIX_T = 256
MIX_NB = 4
FFN_T = 512
FFN_FC = 1024
FFN_SLOTS = 4
PACK_BLOCK_BYTES = 4 * 1024 * 1024
VMEM_LIMIT = 56 * 1024 * 1024

F32 = jnp.float32
BF16 = jnp.bfloat16
U32 = jnp.uint32


def _dot(a, b):
    return jnp.dot(a, b, preferred_element_type=F32)


def _pack_rows_xla(w):
    k, n = w.shape[-2:]
    wb = w.astype(BF16).reshape(*w.shape[:-2], k // 2, 2, n)
    return lax.bitcast_convert_type(jnp.swapaxes(wb, -1, -2), U32)


def _pack_kernel(w_ref, o_ref):
    o_ref[...] = pltpu.bitcast(w_ref[...].astype(BF16), U32)


def _pack_rows(w, layer):
    _, k, n = w.shape
    bk = k
    while bk * n * 4 > PACK_BLOCK_BYTES and bk % (4 * SUBLANES) == 0:
        bk //= 2
    assert k % bk == 0
    return pl.pallas_call(
        _pack_kernel,
        grid=(k // bk,),
        in_specs=[pl.BlockSpec((None, bk, n), lambda i: (layer, i, 0))],
        out_specs=pl.BlockSpec((bk // 2, n), lambda i: (i, 0)),
        out_shape=jax.ShapeDtypeStruct((k // 2, n), U32),
        compiler_params=pltpu.CompilerParams(
            dimension_semantics=("arbitrary",), vmem_limit_bytes=VMEM_LIMIT),
        name="pack_weight",
    )(w)


def _as_bf16(packed):
    return pltpu.bitcast(packed, BF16)


def _rms(x):
    return x * lax.rsqrt(jnp.mean(x * x, axis=-1, keepdims=True) + EPS)


def _gelu(x):
    c0 = 0.7978845608028654
    hx = 0.5 * x
    return hx + hx * jnp.tanh(x * (c0 + (c0 * 0.044715) * (x * x)))


def _gelu2_times(x, v):
    c0 = 0.7978845608028654
    z = x * (c0 + (c0 * 0.044715) * (x * x))
    return (x * v) * (1.0 + jnp.tanh(z))


def _rows(ref, lead, start, size):
    return ref[pl.ds(lead, 1, stride=2), pl.ds(start, size), :][0]


def _put_rows(ref, lead, start, val):
    ref[pl.ds(lead, 1, stride=2), pl.ds(start, val.shape[0]), :] = val[None]


def _linear_scan(a, b, state_ref, a_scr, b_scr, h_scr):
    t = a[0].shape[0]
    seg = t // SUBLANES
    pitch = seg + 1 - seg % 2
    n = len(a)
    for c in range(n):
        for s in range(SUBLANES):
            _put_rows(a_scr, c, s * pitch, a[c][s * seg:(s + 1) * seg])
            _put_rows(b_scr, c, s * pitch, b[c][s * seg:(s + 1) * seg])
    step = lambda ref, c, j: ref[c, pl.ds(j, SUBLANES, stride=pitch), :]
    row = lax.broadcasted_iota(jnp.int32, (SUBLANES, LANES), 0)
    shifted = lambda v, sh, fill: jnp.where(row >= sh, pltpu.roll(v, sh, axis=0), fill)

    prod = [None] * n
    end = [None] * n
    for j in range(seg):
        for c in range(n):
            aj, bj = step(a_scr, c, j), step(b_scr, c, j)
            prod[c] = aj if j == 0 else aj * prod[c]
            end[c] = bj if j == 0 else aj * end[c] + bj

    h = []
    for c in range(n):
        cols = slice(c * LANES, (c + 1) * LANES)
        h0 = state_ref[SUBLANES - 1:SUBLANES, cols]
        p = prod[c]
        e = end[c] + jnp.where(row == 0, p * h0, 0.0)
        for sh in (1, 2, 4):
            e = e + p * shifted(e, sh, 0.0)
            if sh < 4:
                p = p * shifted(p, sh, 1.0)
        state_ref[:, cols] = e
        h.append(shifted(e, 1, h0))
    for j in range(seg):
        for c in range(n):
            h[c] = step(a_scr, c, j) * h[c] + step(b_scr, c, j)
            h_scr[c, pl.ds(j, SUBLANES, stride=pitch), :] = h[c]
    return [jnp.concatenate([_rows(h_scr, c, s * pitch, seg) for s in range(SUBLANES)], axis=0)
            for c in range(n)]


def _conv_via_scratch(scr_ref, lead0, tail, x, w, b):
    k_width = w.shape[0]
    t, c = x.shape
    outs = []
    for g in range(c // LANES):
        cols = slice(g * LANES, (g + 1) * LANES)
        scr_ref[lead0 + g, 0:SUBLANES, :] = tail[:, cols]
        scr_ref[lead0 + g, SUBLANES:SUBLANES + t, :] = x[:, cols]
    for g in range(c // LANES):
        cols = slice(g * LANES, (g + 1) * LANES)
        out = x[:, cols] * w[k_width - 1:k_width, cols] + b[:, cols]
        for k in range(k_width - 1):
            shifted = _rows(scr_ref, lead0 + g, SUBLANES - (k_width - 1 - k), t)
            out = out + shifted * w[k:k + 1, cols]
        outs.append(out)
    return outs


def _mod_kernel(c_ref, w_ref, b_ref, o_ref):
    c = c_ref[...]
    rows = c.shape[0]
    c_act = c * jax.nn.sigmoid(c)
    pad = (-rows) % SUBLANES
    if pad:
        c_act = jnp.concatenate([c_act, jnp.zeros((pad, c.shape[1]), F32)], axis=0)
    part = _dot(c_act.astype(BF16), w_ref[...].astype(BF16))[:rows]

    @pl.when(pl.program_id(0) == 0)
    def _():
        o_ref[...] = part + b_ref[...]

    @pl.when(pl.program_id(0) > 0)
    def _():
        o_ref[...] += part


def _modulation(c, w_ada, b_ada, layer):
    rows, d = c.shape
    n = w_ada.shape[2]
    bk = MOD_ROWS
    assert d % bk == 0
    return pl.pallas_call(
        _mod_kernel,
        grid=(d // bk,),
        in_specs=[
            pl.BlockSpec((rows, bk), lambda k: (0, k)),
            pl.BlockSpec((None, bk, n), lambda k: (layer, k, 0)),
            pl.BlockSpec((1, n), lambda k: (0, 0)),
        ],
        out_specs=pl.BlockSpec((rows, n), lambda k: (0, 0)),
        out_shape=jax.ShapeDtypeStruct((rows, n), F32),
        compiler_params=pltpu.CompilerParams(
            dimension_semantics=("arbitrary",), vmem_limit_bytes=VMEM_LIMIT),
        name="adaln_mod",
    )(c, w_ada, b_ada)


def _mixer_kernel(x_ref, mod_ref, gpre_ref, gpost_ref, win_ref, cw_ref, cb_ref,
                  wgate_ref, brg_ref, big_ref, lrua_ref, vng_ref, vnb_ref, wsp_ref, bsp_ref,
                  glru_ref, ggmlp_ref, wout_ref, *rest, n_later):
    later_in, o_ref, later_out = rest[:n_later], rest[n_later], rest[n_later + 1:2 * n_later + 1]
    scratch = rest[2 * n_later + 1:]
    for w_ref, p_ref in zip(later_in, later_out):
        p_ref[...] = pltpu.bitcast(w_ref[...].astype(BF16), U32)
    nb_tiles = x_ref.shape[0]
    tail_ref, state_ref = scratch[4 * nb_tiles:]

    @pl.when(pl.program_id(1) == 0)
    def _():
        tail_ref[...] = jnp.zeros_like(tail_ref)
        state_ref[...] = jnp.zeros_like(state_ref)

    mod_row = lambda k: mod_ref[pl.ds(pl.program_id(0) * nb_tiles + k, 1), :]
    project = lambda k: _mix_in(x_ref.at[k], mod_row(k), gpre_ref, win_ref)
    z = project(0)
    for k in range(nb_tiles):
        conv_scr, a_scr, b_scr, h_scr = scratch[k:4 * nb_tiles:nb_tiles]
        xc, gates = _mix_gates(z[0], cw_ref, cb_ref, wgate_ref, conv_scr, tail_ref.at[k])
        sp_groups = _mix_positions(z[3], vng_ref, vnb_ref, wsp_ref)
        z_next = project(k + 1) if k + 1 < nb_tiles else None
        y = _mix_body(z, xc, gates, sp_groups, brg_ref, big_ref, lrua_ref, bsp_ref,
                      glru_ref, ggmlp_ref, a_scr, b_scr, h_scr, state_ref.at[k])
        _mix_out(y, x_ref.at[k], mod_row(k), gpost_ref, wout_ref, o_ref.at[k])
        z = z_next


def _mix_in(x_ref, mod, gpre_ref, win_ref):
    d = x_ref.shape[1]
    sh_m = mod[:, 0:d]
    sc_m = mod[:, d:2 * d]
    hb = (_rms(x_ref[...]) * (gpre_ref[...] * (1.0 + sc_m)) + sh_m).astype(BF16)
    widths = (LRU_WIDTH, LRU_WIDTH, GMLP_WIDTH, GMLP_WIDTH)
    starts = [sum(widths[:q]) for q in range(len(widths))]
    return [_dot(hb, _as_bf16(win_ref[:, c0:c0 + w])) for c0, w in zip(starts, widths)]


def _mix_gates(lru_x, cw_ref, cb_ref, wgate_ref, conv_scr, tail_ref):
    t = lru_x.shape[0]
    tail = tail_ref[...]
    tail_ref[...] = lru_x[t - SUBLANES:]
    xc = jnp.concatenate(
        _conv_via_scratch(conv_scr, 0, tail, lru_x, cw_ref[...], cb_ref[...]), axis=1)
    xcb = xc.astype(BF16)
    gates = [_dot(xcb[:, j * GATE_TILE:(j + 1) * GATE_TILE], _as_bf16(wgate_ref[j]))
             for j in range(LRU_WIDTH // GATE_TILE)]
    return xc, gates


def _mix_out(y, x_ref, mod, gpost_ref, wout_ref, o_ref):
    d = x_ref.shape[1]
    gt_m = mod[:, 2 * d:3 * d]
    y = _dot(y, _as_bf16(wout_ref[...]))
    o_ref[...] = x_ref[...] + (gt_m * gpost_ref[...]) * _rms(y)


def _mix_positions(g_v, vng_ref, vnb_ref, wsp_ref):
    t = g_v.shape[0]
    gv = _gelu(g_v)
    mu = jnp.mean(gv, axis=-1, keepdims=True)
    cen = gv - mu
    var = jnp.mean(cen * cen, axis=-1, keepdims=True)
    v = cen * lax.rsqrt(var + EPS) * vng_ref[...] + vnb_ref[...]
    vb = v.astype(BF16)
    nb = t // GMLP_BLOCK
    pi = lax.broadcasted_iota(jnp.int32, (GMLP_BLOCK, GMLP_BLOCK), 0) // CHUNK
    pj = lax.broadcasted_iota(jnp.int32, (GMLP_BLOCK, GMLP_BLOCK), 1) // CHUNK
    mask = pj <= pi
    sp_groups = []
    for g in range(GMLP_GROUPS):
        ws = jnp.where(mask, 0.5 * wsp_ref[g], 0.0).astype(BF16)
        cols = slice(g * GMLP_GROUP_DIM, (g + 1) * GMLP_GROUP_DIM)
        rhs = jnp.concatenate(
            [vb[n * GMLP_BLOCK:(n + 1) * GMLP_BLOCK, cols] for n in range(nb)], axis=1)
        sp_groups.append(_dot(ws, rhs))
    return sp_groups


def _mix_body(z, xc, gates, sp_groups, brg_ref, big_ref, lrua_ref, bsp_ref,
              glru_ref, ggmlp_ref, a_scr, b_scr, h_scr, state_ref):
    _, lru_gate, g_u, _ = z
    t = lru_gate.shape[0]
    slabs = lambda v: [v[:, c * LANES:(c + 1) * LANES] for c in range(v.shape[1] // LANES)]

    r_pre = jnp.concatenate([g[:, :GATE_TILE] for g in gates], axis=1)
    i_pre = jnp.concatenate([g[:, GATE_TILE:] for g in gates], axis=1)
    r_gate = jax.nn.sigmoid(r_pre + brg_ref[...])
    i_gate = jax.nn.sigmoid(i_pre + big_ref[...])
    neg_a = -lrua_ref[...]
    softplus = jnp.maximum(neg_a, 0.0) + jnp.log1p(jnp.exp(-jnp.abs(neg_a)))
    log_a = r_gate * ((-LRU_C) * softplus)
    a = jnp.exp(log_a)
    th = jnp.tanh(log_a)
    q = (-2.0 * th) / (1.0 - th)
    mult = jnp.where(q > 0.0, q * lax.rsqrt(q), 0.0)
    bx = (mult * xc) * i_gate
    hs = _linear_scan(slabs(a), slabs(bx), state_ref, a_scr, b_scr, h_scr)
    y_lru = 0.5 * _gelu2_times(lru_gate, jnp.concatenate(hs, axis=1))
    yl = _rms(y_lru) * glru_ref[...]

    nb = t // GMLP_BLOCK
    bsp = 0.5 * bsp_ref[...]
    sp_rows = []
    for n in range(nb):
        blk = jnp.concatenate(
            [sg[:, n * GMLP_GROUP_DIM:(n + 1) * GMLP_GROUP_DIM] for sg in sp_groups], axis=1)
        sp_rows.append(blk + bsp)
    sp_half = jnp.concatenate(sp_rows, axis=0)
    y_gmlp = _gelu2_times(g_u, sp_half)
    yg = _rms(y_gmlp) * ggmlp_ref[...]

    return jnp.concatenate([yl, yg], axis=1).astype(BF16)


def _mixer(x, mod, g_pre, g_post, w_in, conv_w, conv_b, w_gate, b_rgate, b_igate, lru_a,
           v_norm_g, v_norm_b, w_spatial, b_sp_full, g_lru_out, g_gmlp_out, w_out, later_weights,
           layer):
    bsz, s, d = x.shape
    t = MIX_T
    nb = MIX_NB
    assert bsz % nb == 0 and s % t == 0
    n_lg = LRU_WIDTH // LANES
    scan_rows = SUBLANES * (t // SUBLANES + 1)
    full = lambda a: pl.BlockSpec(a.shape, lambda b, i: (0,) * a.ndim)
    of_layer = lambda a: pl.BlockSpec((None,) + a.shape[1:],
                                      lambda b, i: (layer,) + (0,) * (a.ndim - 1))
    in_arrays = [g_pre, g_post, w_in, conv_w, conv_b, w_gate, b_rgate, b_igate, lru_a,
                 v_norm_g, v_norm_b, w_spatial, b_sp_full, g_lru_out, g_gmlp_out, w_out]
    in_array_specs = [of_layer(a) if a is conv_w or a is w_spatial else full(a)
                      for a in in_arrays]
    nt = s // t
    n_steps = (bsz // nb) * nt
    step = lambda b, i: (b * nt + i, 0)
    later_in, later_out, later_shapes = [], [], []
    for w in later_weights:
        _, k, n = w.shape
        assert k % (n_steps * 2 * SUBLANES) == 0
        later_in.append(pl.BlockSpec((None, k // n_steps, n),
                                     lambda b, i: (layer, b * nt + i, 0)))
        later_out.append(pl.BlockSpec((k // n_steps // 2, n), step))
        later_shapes.append(jax.ShapeDtypeStruct((k // 2, n), U32))
    out = pl.pallas_call(
        functools.partial(_mixer_kernel, n_later=len(later_weights)),
        grid=(bsz // nb, nt),
        in_specs=[
            pl.BlockSpec((nb, t, d), lambda b, i: (b, i, 0)),
            pl.BlockSpec(mod.shape, lambda b, i: (0, 0)),
        ] + in_array_specs + later_in,
        out_specs=[pl.BlockSpec((nb, t, d), lambda b, i: (b, i, 0))] + later_out,
        out_shape=[jax.ShapeDtypeStruct(x.shape, x.dtype)] + later_shapes,
        scratch_shapes=(
            [pltpu.VMEM((n_lg, SUBLANES + t, LANES), F32)] * nb
            + [pltpu.VMEM((n_lg, scan_rows, LANES), F32)] * (3 * nb)
            + [pltpu.VMEM((nb, SUBLANES, LRU_WIDTH), F32)] * 2),
        compiler_params=pltpu.CompilerParams(
            dimension_semantics=("arbitrary", "arbitrary"), vmem_limit_bytes=VMEM_LIMIT),
        name="token_mixer",
    )(x, mod, *in_arrays, *later_weights)
    return out[0], out[1:]


def _ffn_kernel(x_ref, mod_ref, gpre_ref, gpost_ref, wup_ref, cw_ref, cb_ref, wd_ref,
                o_ref, scr_ref, tail_ref):
    t_idx = pl.program_id(1)
    t, d = x_ref.shape
    d_ff = 2 * wd_ref.shape[0]
    fc = FFN_FC
    lg = fc // LANES

    @pl.when(t_idx == 0)
    def _():
        tail_ref[...] = jnp.zeros_like(tail_ref)

    mod = mod_ref[pl.ds(pl.program_id(0), 1), :]
    sh_f = mod[:, 3 * d:4 * d]
    sc_f = mod[:, 4 * d:5 * d]
    gt_f = mod[:, 5 * d:6 * d]
    x = x_ref[...]
    hb = (_rms(x) * (gpre_ref[...] * (1.0 + sc_f)) + sh_f).astype(BF16)

    n_chunks = d_ff // fc
    halves = ((0, 1.0), (d_ff, 0.5))
    chunk_cols = lambda base, j: slice(base + j * fc, base + (j + 1) * fc)
    up_proj = lambda j: [_dot(hb, _as_bf16(wup_ref[:, chunk_cols(base, j)])) for base, _ in halves]

    def gated(j, ups):
        conv = []
        for k, (base, scale) in enumerate(halves):
            cols = chunk_cols(base, j)
            tail = tail_ref[:, cols]
            tail_ref[:, cols] = ups[k][t - SUBLANES:]
            slot = (2 * j + k) % FFN_SLOTS
            conv.append(_conv_via_scratch(scr_ref, slot * lg, tail, ups[k],
                                          cw_ref[:, cols] * scale, cb_ref[:, cols] * scale))
        return jnp.concatenate(
            [_gelu2_times(cg, cv) for cg, cv in zip(*conv)], axis=1).astype(BF16)

    acc = None
    ups = up_proj(0)
    for j in range(n_chunks):
        ups_next = up_proj(j + 1) if j + 1 < n_chunks else None
        part = _dot(gated(j, ups), _as_bf16(wd_ref[j * fc // 2:(j + 1) * fc // 2, :]))
        acc = part if acc is None else acc + part
        ups = ups_next
    o_ref[...] = x + (gt_f * gpost_ref[...]) * _rms(acc)


def _ffn(x, mod, g_pre, g_post, w_up, conv_w, conv_b, w_down, layer):
    bsz, s, d = x.shape
    t = FFN_T
    resident = lambda a: pl.BlockSpec(a.shape, lambda b, i: (0,) * a.ndim,
                                      pipeline_mode=pl.Buffered(1))
    return pl.pallas_call(
        _ffn_kernel,
        grid=(bsz, s // t),
        in_specs=[
            pl.BlockSpec((None, t, d), lambda b, i: (b, i, 0)),
            pl.BlockSpec(mod.shape, lambda b, i: (0, 0)),
            resident(g_pre), resident(g_post), resident(w_up),
            pl.BlockSpec((None,) + conv_w.shape[1:], lambda b, i: (layer, 0, 0),
                         pipeline_mode=pl.Buffered(1)),
            resident(conv_b), resident(w_down),
        ],
        out_specs=pl.BlockSpec((None, t, d), lambda b, i: (b, i, 0)),
        out_shape=jax.ShapeDtypeStruct(x.shape, x.dtype),
        scratch_shapes=[
            pltpu.VMEM((FFN_SLOTS * (FFN_FC // LANES), SUBLANES + t, LANES), F32),
            pltpu.VMEM((SUBLANES, w_up.shape[1]), F32),
        ],
        compiler_params=pltpu.CompilerParams(
            dimension_semantics=("arbitrary", "arbitrary"), vmem_limit_bytes=VMEM_LIMIT),
        name="conv_ffn",
    )(x, mod, g_pre, g_post, w_up, conv_w, conv_b, w_down)


def _gate_weights(w_rgate, w_igate):
    per_tile = GATE_TILE // LRU_HEAD_DIM
    n_tiles = LRU_WIDTH // GATE_TILE
    eye = jnp.eye(per_tile, dtype=F32)

    def block_diag(w):
        w4 = w.reshape(n_tiles, per_tile, LRU_HEAD_DIM, LRU_HEAD_DIM)
        return (w4[:, :, :, None, :] * eye[None, :, None, :, None]).reshape(
            n_tiles, GATE_TILE, GATE_TILE)

    return _pack_rows_xla(jnp.concatenate([block_diag(w_rgate), block_diag(w_igate)], axis=2))


def kernel(x, c, w_ada, b_ada, g_mix_pre, g_mix_post, w_in, conv_w, conv_b, w_rgate, b_rgate, w_igate, b_igate, lru_a, v_norm_g, v_norm_b, w_spatial, b_spatial, g_lru_out, g_gmlp_out, w_out, g_ffn_pre, g_ffn_post, w_up, ffn_conv_w, ffn_conv_b, w_down):
    depth = w_ada.shape[0]
    bsz, s, d = x.shape
    for l in range(depth):
        mod = _modulation(c, w_ada, b_ada[l:l + 1], l)
        row = lambda a: a[l:l + 1]
        w_gate = _gate_weights(w_rgate[l:l + 1], w_igate[l:l + 1])
        b_gate = (b_rgate[l:l + 1].reshape(1, -1), b_igate[l:l + 1].reshape(1, -1))
        b_sp_full = jnp.broadcast_to(
            jnp.transpose(b_spatial[l:l + 1], (2, 0, 1))[..., None],
            (GMLP_BLOCK, 1, GMLP_GROUPS, GMLP_GROUP_DIM)).reshape(GMLP_BLOCK, GMLP_WIDTH)
        x, (w_up_p, w_down_p) = _mixer(
                   x, mod, row(g_mix_pre), row(g_mix_post), _pack_rows(w_in, l),
                   conv_w, row(conv_b), w_gate, *b_gate, row(lru_a),
                   row(v_norm_g), row(v_norm_b), w_spatial, b_sp_full,
                   row(g_lru_out), row(g_gmlp_out), _pack_rows(w_out, l),
                   later_weights=[w_up, w_down], layer=l)
        x = _ffn(x, mod, row(g_ffn_pre), row(g_ffn_post), w_up_p,
                 ffn_conv_w, row(ffn_conv_b), w_down_p, layer=l)
    return x
```

```python
import functools

import jax
import jax.numpy as jnp
from jax import lax
from jax.experimental import pallas as pl
from jax.experimental.pallas import tpu as pltpu

CHUNK = 64
LRU_WIDTH = 512
LRU_HEADS = 8
LRU_HEAD_DIM = LRU_WIDTH // LRU_HEADS
LRU_CONV_WIDTH = 4
LRU_C = 8.0
GMLP_WIDTH = 512
GMLP_GROUPS = 4
GMLP_GROUP_DIM = GMLP_WIDTH // GMLP_GROUPS
GMLP_BLOCK = 128
FFN_CONV_WIDTH = 3
N_MOD = 6
EPS = 1e-6

SUBLANES = 8
LANES = 128
GATE_TILE = 256

MOD_ROWS = 128
MIX_T = 256
MIX_NB = 4
FFN_T = 512
FFN_FC = 1024
FFN_SLOTS = 4
PACK_BLOCK_BYTES = 4 * 1024 * 1024
VMEM_LIMIT = 56 * 1024 * 1024

F32 = jnp.float32
BF16 = jnp.bfloat16
U32 = jnp.uint32


def _dot(a, b):
    return jnp.dot(a, b, preferred_element_type=F32)


def _pack_rows_xla(w):
    k, n = w.shape[-2:]
    wb = w.astype(BF16).reshape(*w.shape[:-2], k // 2, 2, n)
    return lax.bitcast_convert_type(jnp.swapaxes(wb, -1, -2), U32)


def _pack_kernel(w_ref, o_ref):
    o_ref[...] = pltpu.bitcast(w_ref[...].astype(BF16), U32)


def _pack_rows(w, layer):
    _, k, n = w.shape
    bk = k
    while bk * n * 4 > PACK_BLOCK_BYTES and bk % (4 * SUBLANES) == 0:
        bk //= 2
    assert k % bk == 0
    return pl.pallas_call(
        _pack_kernel,
        grid=(k // bk,),
        in_specs=[pl.BlockSpec((None, bk, n), lambda i: (layer, i, 0))],
        out_specs=pl.BlockSpec((bk // 2, n), lambda i: (i, 0)),
        out_shape=jax.ShapeDtypeStruct((k // 2, n), U32),
        compiler_params=pltpu.CompilerParams(
            dimension_semantics=("arbitrary",), vmem_limit_bytes=VMEM_LIMIT),
        name="pack_weight",
    )(w)


def _as_bf16(packed):
    return pltpu.bitcast(packed, BF16)


def _rms(x):
    return x * lax.rsqrt(jnp.mean(x * x, axis=-1, keepdims=True) + EPS)


def _gelu(x):
    c0 = 0.7978845608028654
    hx = 0.5 * x
    return hx + hx * jnp.tanh(x * (c0 + (c0 * 0.044715) * (x * x)))


def _gelu2_times(x, v):
    c0 = 0.7978845608028654
    z = x * (c0 + (c0 * 0.044715) * (x * x))
    return (x * v) * (1.0 + jnp.tanh(z))


def _rows(ref, lead, start, size):
    return ref[pl.ds(lead, 1, stride=2), pl.ds(start, size), :][0]


def _put_rows(ref, lead, start, val):
    ref[pl.ds(lead, 1, stride=2), pl.ds(start, val.shape[0]), :] = val[None]


def _linear_scan(a, b, state_ref, a_scr, b_scr, h_scr):
    t = a[0].shape[0]
    seg = t // SUBLANES
    pitch = seg + 1 - seg % 2
    n = len(a)
    for c in range(n):
        for s in range(SUBLANES):
            _put_rows(a_scr, c, s * pitch, a[c][s * seg:(s + 1) * seg])
            _put_rows(b_scr, c, s * pitch, b[c][s * seg:(s + 1) * seg])
    step = lambda ref, c, j: ref[c, pl.ds(j, SUBLANES, stride=pitch), :]
    row = lax.broadcasted_iota(jnp.int32, (SUBLANES, LANES), 0)
    shifted = lambda v, sh, fill: jnp.where(row >= sh, pltpu.roll(v, sh, axis=0), fill)

    prod = [None] * n
    end = [None] * n
    for j in range(seg):
        for c in range(n):
            aj, bj = step(a_scr, c, j), step(b_scr, c, j)
            prod[c] = aj if j == 0 else aj * prod[c]
            end[c] = bj if j == 0 else aj * end[c] + bj

    h = []
    for c in range(n):
        cols = slice(c * LANES, (c + 1) * LANES)
        h0 = state_ref[SUBLANES - 1:SUBLANES, cols]
        p = prod[c]
        e = end[c] + jnp.where(row == 0, p * h0, 0.0)
        for sh in (1, 2, 4):
            e = e + p * shifted(e, sh, 0.0)
            if sh < 4:
                p = p * shifted(p, sh, 1.0)
        state_ref[:, cols] = e
        h.append(shifted(e, 1, h0))
    for j in range(seg):
        for c in range(n):
            h[c] = step(a_scr, c, j) * h[c] + step(b_scr, c, j)
            h_scr[c, pl.ds(j, SUBLANES, stride=pitch), :] = h[c]
    return [jnp.concatenate([_rows(h_scr, c, s * pitch, seg) for s in range(SUBLANES)], axis=0)
            for c in range(n)]


def _conv_via_scratch(scr_ref, lead0, tail, x, w, b):
    k_width = w.shape[0]
    t, c = x.shape
    outs = []
    for g in range(c // LANES):
        cols = slice(g * LANES, (g + 1) * LANES)
        scr_ref[lead0 + g, 0:SUBLANES, :] = tail[:, cols]
        scr_ref[lead0 + g, SUBLANES:SUBLANES + t, :] = x[:, cols]
    for g in range(c // LANES):
        cols = slice(g * LANES, (g + 1) * LANES)
        out = x[:, cols] * w[k_width - 1:k_width, cols] + b[:, cols]
        for k in range(k_width - 1):
            shifted = _rows(scr_ref, lead0 + g, SUBLANES - (k_width - 1 - k), t)
            out = out + shifted * w[k:k + 1, cols]
        outs.append(out)
    return outs


def _mod_kernel(c_ref, w_ref, b_ref, o_ref):
    c = c_ref[...]
    rows = c.shape[0]
    c_act = c * jax.nn.sigmoid(c)
    pad = (-rows) % SUBLANES
    if pad:
        c_act = jnp.concatenate([c_act, jnp.zeros((pad, c.shape[1]), F32)], axis=0)
    part = _dot(c_act.astype(BF16), w_ref[...].astype(BF16))[:rows]

    @pl.when(pl.program_id(0) == 0)
    def _():
        o_ref[...] = part + b_ref[...]

    @pl.when(pl.program_id(0) > 0)
    def _():
        o_ref[...] += part


def _modulation(c, w_ada, b_ada, layer):
    rows, d = c.shape
    n = w_ada.shape[2]
    bk = MOD_ROWS
    assert d % bk == 0
    return pl.pallas_call(
        _mod_kernel,
        grid=(d // bk,),
        in_specs=[
            pl.BlockSpec((rows, bk), lambda k: (0, k)),
            pl.BlockSpec((None, bk, n), lambda k: (layer, k, 0)),
            pl.BlockSpec((1, n), lambda k: (0, 0)),
        ],
        out_specs=pl.BlockSpec((rows, n), lambda k: (0, 0)),
        out_shape=jax.ShapeDtypeStruct((rows, n), F32),
        compiler_params=pltpu.CompilerParams(
            dimension_semantics=("arbitrary",), vmem_limit_bytes=VMEM_LIMIT),
        name="adaln_mod",
    )(c, w_ada, b_ada)


def _mixer_kernel(x_ref, mod_ref, gpre_ref, gpost_ref, win_ref, cw_ref, cb_ref,
                  wgate_ref, brg_ref, big_ref, lrua_ref, vng_ref, vnb_ref, wsp_ref, bsp_ref,
                  glru_ref, ggmlp_ref, wout_ref, *rest, n_later):
    later_in, o_ref, later_out = rest[:n_later], rest[n_later], rest[n_later + 1:2 * n_later + 1]
    scratch = rest[2 * n_later + 1:]
    for w_ref, p_ref in zip(later_in, later_out):
        p_ref[...] = pltpu.bitcast(w_ref[...].astype(BF16), U32)
    nb_tiles = x_ref.shape[0]
    tail_ref, state_ref = scratch[4 * nb_tiles:]

    @pl.when(pl.program_id(1) == 0)
    def _():
        tail_ref[...] = jnp.zeros_like(tail_ref)
        state_ref[...] = jnp.zeros_like(state_ref)

    mod_row = lambda k: mod_ref[pl.ds(pl.program_id(0) * nb_tiles + k, 1), :]
    project = lambda k: _mix_in(x_ref.at[k], mod_row(k), gpre_ref, win_ref)
    def front(z_tile, k):
        xc, gates = _mix_gates(z_tile[0], cw_ref, cb_ref, wgate_ref, scratch[k], tail_ref.at[k])
        return xc, gates, _mix_positions(z_tile[3], vng_ref, vnb_ref, wsp_ref)

    z = project(0)
    xc, gates, sp_groups = front(z, 0)
    for k in range(nb_tiles):
        a_scr, b_scr, h_scr = scratch[nb_tiles + k:4 * nb_tiles:nb_tiles]
        z_next = project(k + 1) if k + 1 < nb_tiles else None
        y = _mix_body(z, xc, gates, sp_groups, brg_ref, big_ref, lrua_ref, bsp_ref,
                      glru_ref, ggmlp_ref, a_scr, b_scr, h_scr, state_ref.at[k])
        if z_next:
            xc, gates, sp_groups = front(z_next, k + 1)
        _mix_out(y, x_ref.at[k], mod_row(k), gpost_ref, wout_ref, o_ref.at[k])
        z = z_next


def _mix_in(x_ref, mod, gpre_ref, win_ref):
    d = x_ref.shape[1]
    sh_m = mod[:, 0:d]
    sc_m = mod[:, d:2 * d]
    hb = (_rms(x_ref[...]) * (gpre_ref[...] * (1.0 + sc_m)) + sh_m).astype(BF16)
    widths = (LRU_WIDTH, LRU_WIDTH, GMLP_WIDTH, GMLP_WIDTH)
    starts = [sum(widths[:q]) for q in range(len(widths))]
    return [_dot(hb, _as_bf16(win_ref[:, c0:c0 + w])) for c0, w in zip(starts, widths)]


def _mix_gates(lru_x, cw_ref, cb_ref, wgate_ref, conv_scr, tail_ref):
    t = lru_x.shape[0]
    tail = tail_ref[...]
    tail_ref[...] = lru_x[t - SUBLANES:]
    xc = jnp.concatenate(
        _conv_via_scratch(conv_scr, 0, tail, lru_x, cw_ref[...], cb_ref[...]), axis=1)
    xcb = xc.astype(BF16)
    gates = [_dot(xcb[:, j * GATE_TILE:(j + 1) * GATE_TILE], _as_bf16(wgate_ref[j]))
             for j in range(LRU_WIDTH // GATE_TILE)]
    return xc, gates


def _mix_out(y, x_ref, mod, gpost_ref, wout_ref, o_ref):
    d = x_ref.shape[1]
    gt_m = mod[:, 2 * d:3 * d]
    y = _dot(y, _as_bf16(wout_ref[...]))
    o_ref[...] = x_ref[...] + (gt_m * gpost_ref[...]) * _rms(y)


def _mix_positions(g_v, vng_ref, vnb_ref, wsp_ref):
    t = g_v.shape[0]
    gv = _gelu(g_v)
    mu = jnp.mean(gv, axis=-1, keepdims=True)
    cen = gv - mu
    var = jnp.mean(cen * cen, axis=-1, keepdims=True)
    v = cen * lax.rsqrt(var + EPS) * vng_ref[...] + vnb_ref[...]
    vb = v.astype(BF16)
    nb = t // GMLP_BLOCK
    pi = lax.broadcasted_iota(jnp.int32, (GMLP_BLOCK, GMLP_BLOCK), 0) // CHUNK
    pj = lax.broadcasted_iota(jnp.int32, (GMLP_BLOCK, GMLP_BLOCK), 1) // CHUNK
    mask = pj <= pi
    sp_groups = []
    for g in range(GMLP_GROUPS):
        ws = jnp.where(mask, 0.5 * wsp_ref[g], 0.0).astype(BF16)
        cols = slice(g * GMLP_GROUP_DIM, (g + 1) * GMLP_GROUP_DIM)
        rhs = jnp.concatenate(
            [vb[n * GMLP_BLOCK:(n + 1) * GMLP_BLOCK, cols] for n in range(nb)], axis=1)
        sp_groups.append(_dot(ws, rhs))
    return sp_groups


def _mix_body(z, xc, gates, sp_groups, brg_ref, big_ref, lrua_ref, bsp_ref,
              glru_ref, ggmlp_ref, a_scr, b_scr, h_scr, state_ref):
    _, lru_gate, g_u, _ = z
    t = lru_gate.shape[0]
    slabs = lambda v: [v[:, c * LANES:(c + 1) * LANES] for c in range(v.shape[1] // LANES)]

    r_pre = jnp.concatenate([g[:, :GATE_TILE] for g in gates], axis=1)
    i_pre = jnp.concatenate([g[:, GATE_TILE:] for g in gates], axis=1)
    r_gate = jax.nn.sigmoid(r_pre + brg_ref[...])
    i_gate = jax.nn.sigmoid(i_pre + big_ref[...])
    neg_a = -lrua_ref[...]
    softplus = jnp.maximum(neg_a, 0.0) + jnp.log1p(jnp.exp(-jnp.abs(neg_a)))
    log_a = r_gate * ((-LRU_C) * softplus)
    a = jnp.exp(log_a)
    th = jnp.tanh(log_a)
    q = (-2.0 * th) / (1.0 - th)
    mult = jnp.where(q > 0.0, q * lax.rsqrt(q), 0.0)
    bx = (mult * xc) * i_gate
    hs = _linear_scan(slabs(a), slabs(bx), state_ref, a_scr, b_scr, h_scr)
    y_lru = 0.5 * _gelu2_times(lru_gate, jnp.concatenate(hs, axis=1))
    yl = _rms(y_lru) * glru_ref[...]

    nb = t // GMLP_BLOCK
    bsp = 0.5 * bsp_ref[...]
    sp_rows = []
    for n in range(nb):
        blk = jnp.concatenate(
            [sg[:, n * GMLP_GROUP_DIM:(n + 1) * GMLP_GROUP_DIM] for sg in sp_groups], axis=1)
        sp_rows.append(blk + bsp)
    sp_half = jnp.concatenate(sp_rows, axis=0)
    y_gmlp = _gelu2_times(g_u, sp_half)
    yg = _rms(y_gmlp) * ggmlp_ref[...]

    return jnp.concatenate([yl, yg], axis=1).astype(BF16)


def _mixer(x, mod, g_pre, g_post, w_in, conv_w, conv_b, w_gate, b_rgate, b_igate, lru_a,
           v_norm_g, v_norm_b, w_spatial, b_sp_full, g_lru_out, g_gmlp_out, w_out, later_weights,
           layer):
    bsz, s, d = x.shape
    t = MIX_T
    nb = MIX_NB
    assert bsz % nb == 0 and s % t == 0
    n_lg = LRU_WIDTH // LANES
    scan_rows = SUBLANES * (t // SUBLANES + 1)
    full = lambda a: pl.BlockSpec(a.shape, lambda b, i: (0,) * a.ndim)
    of_layer = lambda a: pl.BlockSpec((None,) + a.shape[1:],
                                      lambda b, i: (layer,) + (0,) * (a.ndim - 1))
    in_arrays = [g_pre, g_post, w_in, conv_w, conv_b, w_gate, b_rgate, b_igate, lru_a,
                 v_norm_g, v_norm_b, w_spatial, b_sp_full, g_lru_out, g_gmlp_out, w_out]
    in_array_specs = [of_layer(a) if a is conv_w or a is w_spatial else full(a)
                      for a in in_arrays]
    nt = s // t
    n_steps = (bsz // nb) * nt
    step = lambda b, i: (b * nt + i, 0)
    later_in, later_out, later_shapes = [], [], []
    for w in later_weights:
        _, k, n = w.shape
        assert k % (n_steps * 2 * SUBLANES) == 0
        later_in.append(pl.BlockSpec((None, k // n_steps, n),
                                     lambda b, i: (layer, b * nt + i, 0)))
        later_out.append(pl.BlockSpec((k // n_steps // 2, n), step))
        later_shapes.append(jax.ShapeDtypeStruct((k // 2, n), U32))
    out = pl.pallas_call(
        functools.partial(_mixer_kernel, n_later=len(later_weights)),
        grid=(bsz // nb, nt),
        in_specs=[
            pl.BlockSpec((nb, t, d), lambda b, i: (b, i, 0)),
            pl.BlockSpec(mod.shape, lambda b, i: (0, 0)),
        ] + in_array_specs + later_in,
        out_specs=[pl.BlockSpec((nb, t, d), lambda b, i: (b, i, 0))] + later_out,
        out_shape=[jax.ShapeDtypeStruct(x.shape, x.dtype)] + later_shapes,
        scratch_shapes=(
            [pltpu.VMEM((n_lg, SUBLANES + t, LANES), F32)] * nb
            + [pltpu.VMEM((n_lg, scan_rows, LANES), F32)] * (3 * nb)
            + [pltpu.VMEM((nb, SUBLANES, LRU_WIDTH), F32)] * 2),
        compiler_params=pltpu.CompilerParams(
            dimension_semantics=("arbitrary", "arbitrary"), vmem_limit_bytes=VMEM_LIMIT),
        name="token_mixer",
    )(x, mod, *in_arrays, *later_weights)
    return out[0], out[1:]


def _ffn_kernel(x_ref, mod_ref, gpre_ref, gpost_ref, wup_ref, cw_ref, cb_ref, wd_ref,
                o_ref, scr_ref, tail_ref):
    t_idx = pl.program_id(1)
    t, d = x_ref.shape
    d_ff = 2 * wd_ref.shape[0]
    fc = FFN_FC
    lg = fc // LANES

    @pl.when(t_idx == 0)
    def _():
        tail_ref[...] = jnp.zeros_like(tail_ref)

    mod = mod_ref[pl.ds(pl.program_id(0), 1), :]
    sh_f = mod[:, 3 * d:4 * d]
    sc_f = mod[:, 4 * d:5 * d]
    gt_f = mod[:, 5 * d:6 * d]
    x = x_ref[...]
    hb = (_rms(x) * (gpre_ref[...] * (1.0 + sc_f)) + sh_f).astype(BF16)

    n_chunks = d_ff // fc
    halves = ((0, 1.0), (d_ff, 0.5))
    chunk_cols = lambda base, j: slice(base + j * fc, base + (j + 1) * fc)
    up_proj = lambda j: [_dot(hb, _as_bf16(wup_ref[:, chunk_cols(base, j)])) for base, _ in halves]

    def gated(j, ups):
        conv = []
        for k, (base, scale) in enumerate(halves):
            cols = chunk_cols(base, j)
            tail = tail_ref[:, cols]
            tail_ref[:, cols] = ups[k][t - SUBLANES:]
            slot = (2 * j + k) % FFN_SLOTS
            conv.append(_conv_via_scratch(scr_ref, slot * lg, tail, ups[k],
                                          cw_ref[:, cols] * scale, cb_ref[:, cols] * scale))
        return jnp.concatenate(
            [_gelu2_times(cg, cv) for cg, cv in zip(*conv)], axis=1).astype(BF16)

    acc = None
    ups = up_proj(0)
    for j in range(n_chunks):
        ups_next = up_proj(j + 1) if j + 1 < n_chunks else None
        part = _dot(gated(j, ups), _as_bf16(wd_ref[j * fc // 2:(j + 1) * fc // 2, :]))
        acc = part if acc is None else acc + part
        ups = ups_next
    o_ref[...] = x + (gt_f * gpost_ref[...]) * _rms(acc)


def _ffn(x, mod, g_pre, g_post, w_up, conv_w, conv_b, w_down, layer):
    bsz, s, d = x.shape
    t = FFN_T
    resident = lambda a: pl.BlockSpec(a.shape, lambda b, i: (0,) * a.ndim,
                                      pipeline_mode=pl.Buffered(1))
    return pl.pallas_call(
        _ffn_kernel,
        grid=(bsz, s // t),
        in_specs=[
            pl.BlockSpec((None, t, d), lambda b, i: (b, i, 0)),
            pl.BlockSpec(mod.shape, lambda b, i: (0, 0)),
            resident(g_pre), resident(g_post), resident(w_up),
            pl.BlockSpec((None,) + conv_w.shape[1:], lambda b, i: (layer, 0, 0),
                         pipeline_mode=pl.Buffered(1)),
            resident(conv_b), resident(w_down),
        ],
        out_specs=pl.BlockSpec((None, t, d), lambda b, i: (b, i, 0)),
        out_shape=jax.ShapeDtypeStruct(x.shape, x.dtype),
        scratch_shapes=[
            pltpu.VMEM((FFN_SLOTS * (FFN_FC // LANES), SUBLANES + t, LANES), F32),
            pltpu.VMEM((SUBLANES, w_up.shape[1]), F32),
        ],
        compiler_params=pltpu.CompilerParams(
            dimension_semantics=("arbitrary", "arbitrary"), vmem_limit_bytes=VMEM_LIMIT),
        name="conv_ffn",
    )(x, mod, g_pre, g_post, w_up, conv_w, conv_b, w_down)


def _gate_weights(w_rgate, w_igate):
    per_tile = GATE_TILE // LRU_HEAD_DIM
    n_tiles = LRU_WIDTH // GATE_TILE
    eye = jnp.eye(per_tile, dtype=F32)

    def block_diag(w):
        w4 = w.reshape(n_tiles, per_tile, LRU_HEAD_DIM, LRU_HEAD_DIM)
        return (w4[:, :, :, None, :] * eye[None, :, None, :, None]).reshape(
            n_tiles, GATE_TILE, GATE_TILE)

    return _pack_rows_xla(jnp.concatenate([block_diag(w_rgate), block_diag(w_igate)], axis=2))


def kernel(x, c, w_ada, b_ada, g_mix_pre, g_mix_post, w_in, conv_w, conv_b, w_rgate, b_rgate, w_igate, b_igate, lru_a, v_norm_g, v_norm_b, w_spatial, b_spatial, g_lru_out, g_gmlp_out, w_out, g_ffn_pre, g_ffn_post, w_up, ffn_conv_w, ffn_conv_b, w_down):
    depth = w_ada.shape[0]
    bsz, s, d = x.shape
    for l in range(depth):
        mod = _modulation(c, w_ada, b_ada[l:l + 1], l)
        row = lambda a: a[l:l + 1]
        w_gate = _gate_weights(w_rgate[l:l + 1], w_igate[l:l + 1])
        b_gate = (b_rgate[l:l + 1].reshape(1, -1), b_igate[l:l + 1].reshape(1, -1))
        b_sp_full = jnp.broadcast_to(
            jnp.transpose(b_spatial[l:l + 1], (2, 0, 1))[..., None],
            (GMLP_BLOCK, 1, GMLP_GROUPS, GMLP_GROUP_DIM)).reshape(GMLP_BLOCK, GMLP_WIDTH)
        x, (w_up_p, w_down_p) = _mixer(
                   x, mod, row(g_mix_pre), row(g_mix_post), _pack_rows(w_in, l),
                   conv_w, row(conv_b), w_gate, *b_gate, row(lru_a),
                   row(v_norm_g), row(v_norm_b), w_spatial, b_sp_full,
                   row(g_lru_out), row(g_gmlp_out), _pack_rows(w_out, l),
                   later_weights=[w_up, w_down], layer=l)
        x = _ffn(x, mod, row(g_ffn_pre), row(g_ffn_post), w_up_p,
                 ffn_conv_w, row(ffn_conv_b), w_down_p, layer=l)
    return x
```

```python
import functools

import jax
import jax.numpy as jnp
from jax import lax
from jax.experimental import pallas as pl
from jax.experimental.pallas import tpu as pltpu

CHUNK = 64
LRU_WIDTH = 512
LRU_HEADS = 8
LRU_HEAD_DIM = LRU_WIDTH // LRU_HEADS
LRU_CONV_WIDTH = 4
LRU_C = 8.0
GMLP_WIDTH = 512
GMLP_GROUPS = 4
GMLP_GROUP_DIM = GMLP_WIDTH // GMLP_GROUPS
GMLP_BLOCK = 128
FFN_CONV_WIDTH = 3
N_MOD = 6
EPS = 1e-6

SUBLANES = 8
LANES = 128
GATE_TILE = 256

MOD_ROWS = 128
MIX_T = 256
MIX_NB = 4
FFN_T = 512
FFN_FC = 1024
FFN_SLOTS = 4
VMEM_LIMIT = 60 * 1024 * 1024

F32 = jnp.float32
BF16 = jnp.bfloat16
U32 = jnp.uint32


def _dot(a, b):
    return jnp.dot(a, b, preferred_element_type=F32)


def _pack_rows_xla(w):
    k, n = w.shape[-2:]
    wb = w.astype(BF16).reshape(*w.shape[:-2], k // 2, 2, n)
    return lax.bitcast_convert_type(jnp.swapaxes(wb, -1, -2), U32)


def _pack_rows(w):
    return pltpu.bitcast(w.astype(BF16), U32)


def _as_bf16(packed):
    return pltpu.bitcast(packed, BF16)


def _rms(x):
    return x * lax.rsqrt(jnp.mean(x * x, axis=-1, keepdims=True) + EPS)


def _gelu(x):
    c0 = 0.7978845608028654
    hx = 0.5 * x
    return hx + hx * jnp.tanh(x * (c0 + (c0 * 0.044715) * (x * x)))


def _gelu2_times(x, v):
    c0 = 0.7978845608028654
    z = x * (c0 + (c0 * 0.044715) * (x * x))
    return (x * v) * (1.0 + jnp.tanh(z))


def _rows(ref, lead, start, size):
    return ref[pl.ds(lead, 1, stride=2), pl.ds(start, size), :][0]


def _put_rows(ref, lead, start, val):
    ref[pl.ds(lead, 1, stride=2), pl.ds(start, val.shape[0]), :] = val[None]


def _linear_scan(a, b, state_ref, a_scr, b_scr, h_scr):
    t = a[0].shape[0]
    seg = t // SUBLANES
    pitch = seg + 1 - seg % 2
    n = len(a)
    for c in range(n):
        for s in range(SUBLANES):
            _put_rows(a_scr, c, s * pitch, a[c][s * seg:(s + 1) * seg])
            _put_rows(b_scr, c, s * pitch, b[c][s * seg:(s + 1) * seg])
    step = lambda ref, c, j: ref[c, pl.ds(j, SUBLANES, stride=pitch), :]
    row = lax.broadcasted_iota(jnp.int32, (SUBLANES, LANES), 0)
    shifted = lambda v, sh, fill: jnp.where(row >= sh, pltpu.roll(v, sh, axis=0), fill)

    prod = [None] * n
    end = [None] * n
    for j in range(seg):
        for c in range(n):
            aj, bj = step(a_scr, c, j), step(b_scr, c, j)
            prod[c] = aj if j == 0 else aj * prod[c]
            end[c] = bj if j == 0 else aj * end[c] + bj

    h = []
    for c in range(n):
        cols = slice(c * LANES, (c + 1) * LANES)
        h0 = state_ref[SUBLANES - 1:SUBLANES, cols]
        p = prod[c]
        e = end[c] + jnp.where(row == 0, p * h0, 0.0)
        for sh in (1, 2, 4):
            e = e + p * shifted(e, sh, 0.0)
            if sh < 4:
                p = p * shifted(p, sh, 1.0)
        state_ref[:, cols] = e
        h.append(shifted(e, 1, h0))
    for j in range(seg):
        for c in range(n):
            h[c] = step(a_scr, c, j) * h[c] + step(b_scr, c, j)
            h_scr[c, pl.ds(j, SUBLANES, stride=pitch), :] = h[c]
    return [jnp.concatenate([_rows(h_scr, c, s * pitch, seg) for s in range(SUBLANES)], axis=0)
            for c in range(n)]


def _conv_via_scratch(scr_ref, lead0, tail, x, w, b):
    k_width = w.shape[0]
    t, c = x.shape
    outs = []
    for g in range(c // LANES):
        cols = slice(g * LANES, (g + 1) * LANES)
        scr_ref[lead0 + g, 0:SUBLANES, :] = tail[:, cols]
        scr_ref[lead0 + g, SUBLANES:SUBLANES + t, :] = x[:, cols]
    for g in range(c // LANES):
        cols = slice(g * LANES, (g + 1) * LANES)
        out = x[:, cols] * w[k_width - 1:k_width, cols] + b[:, cols]
        for k in range(k_width - 1):
            shifted = _rows(scr_ref, lead0 + g, SUBLANES - (k_width - 1 - k), t)
            out = out + shifted * w[k:k + 1, cols]
        outs.append(out)
    return outs


def _mod_kernel(c_ref, w_ref, b_ref, o_ref):
    c = c_ref[...]
    rows = c.shape[0]
    c_act = c * jax.nn.sigmoid(c)
    pad = (-rows) % SUBLANES
    if pad:
        c_act = jnp.concatenate([c_act, jnp.zeros((pad, c.shape[1]), F32)], axis=0)
    part = _dot(c_act.astype(BF16), w_ref[...].astype(BF16))[:rows]

    @pl.when(pl.program_id(0) == 0)
    def _():
        o_ref[...] = part + b_ref[...]

    @pl.when(pl.program_id(0) > 0)
    def _():
        o_ref[...] += part


def _modulation(c, w_ada, b_ada, layer):
    rows, d = c.shape
    n = w_ada.shape[2]
    bk = MOD_ROWS
    assert d % bk == 0
    return pl.pallas_call(
        _mod_kernel,
        grid=(d // bk,),
        in_specs=[
            pl.BlockSpec((rows, bk), lambda k: (0, k)),
            pl.BlockSpec((None, bk, n), lambda k: (layer, k, 0)),
            pl.BlockSpec((1, n), lambda k: (0, 0)),
        ],
        out_specs=pl.BlockSpec((rows, n), lambda k: (0, 0)),
        out_shape=jax.ShapeDtypeStruct((rows, n), F32),
        compiler_params=pltpu.CompilerParams(
            dimension_semantics=("arbitrary",), vmem_limit_bytes=VMEM_LIMIT),
        name="adaln_mod",
    )(c, w_ada, b_ada)


def _mixer_kernel(x_ref, mod_ref, gpre_ref, gpost_ref, win_ref, cw_ref, cb_ref,
                  wgate_ref, brg_ref, big_ref, lrua_ref, vng_ref, vnb_ref, wsp_ref, bsp_ref,
                  glru_ref, ggmlp_ref, wout_ref, *rest, n_later):
    later_in, o_ref, later_out = rest[:n_later], rest[n_later], rest[n_later + 1:2 * n_later + 1]
    scratch = rest[2 * n_later + 1:]
    for w_ref, p_ref in zip(later_in, later_out):
        p_ref[...] = _pack_rows(w_ref[...])
    nb_tiles = x_ref.shape[0]
    tail_ref, state_ref, win_packed_ref, wout_packed_ref = scratch[4 * nb_tiles:]

    @pl.when((pl.program_id(0) == 0) & (pl.program_id(1) == 0))
    def _():
        win_packed_ref[...] = _pack_rows(win_ref[...])
        wout_packed_ref[...] = _pack_rows(wout_ref[...])

    @pl.when(pl.program_id(1) == 0)
    def _():
        tail_ref[...] = jnp.zeros_like(tail_ref)
        state_ref[...] = jnp.zeros_like(state_ref)

    mod_row = lambda k: mod_ref[pl.ds(pl.program_id(0) * nb_tiles + k, 1), :]
    project = lambda k: _mix_in(x_ref.at[k], mod_row(k), gpre_ref, win_packed_ref)
    z = project(0)
    for k in range(nb_tiles):
        conv_scr, a_scr, b_scr, h_scr = scratch[k:4 * nb_tiles:nb_tiles]
        xc, gates = _mix_gates(z[0], cw_ref, cb_ref, wgate_ref, conv_scr, tail_ref.at[k])
        sp_groups = _mix_positions(z[3], vng_ref, vnb_ref, wsp_ref)
        z_next = project(k + 1) if k + 1 < nb_tiles else None
        y = _mix_body(z, xc, gates, sp_groups, brg_ref, big_ref, lrua_ref, bsp_ref,
                      glru_ref, ggmlp_ref, a_scr, b_scr, h_scr, state_ref.at[k])
        _mix_out(y, x_ref.at[k], mod_row(k), gpost_ref, wout_packed_ref, o_ref.at[k])
        z = z_next


def _mix_in(x_ref, mod, gpre_ref, win_ref):
    d = x_ref.shape[1]
    sh_m = mod[:, 0:d]
    sc_m = mod[:, d:2 * d]
    hb = (_rms(x_ref[...]) * (gpre_ref[...] * (1.0 + sc_m)) + sh_m).astype(BF16)
    widths = (LRU_WIDTH, LRU_WIDTH, GMLP_WIDTH, GMLP_WIDTH)
    starts = [sum(widths[:q]) for q in range(len(widths))]
    return [_dot(hb, _as_bf16(win_ref[:, c0:c0 + w])) for c0, w in zip(starts, widths)]


def _mix_gates(lru_x, cw_ref, cb_ref, wgate_ref, conv_scr, tail_ref):
    t = lru_x.shape[0]
    tail = tail_ref[...]
    tail_ref[...] = lru_x[t - SUBLANES:]
    xc = jnp.concatenate(
        _conv_via_scratch(conv_scr, 0, tail, lru_x, cw_ref[...], cb_ref[...]), axis=1)
    xcb = xc.astype(BF16)
    gates = [_dot(xcb[:, j * GATE_TILE:(j + 1) * GATE_TILE], _as_bf16(wgate_ref[j]))
             for j in range(LRU_WIDTH // GATE_TILE)]
    return xc, gates


def _mix_out(y, x_ref, mod, gpost_ref, wout_ref, o_ref):
    d = x_ref.shape[1]
    gt_m = mod[:, 2 * d:3 * d]
    y = _dot(y, _as_bf16(wout_ref[...]))
    o_ref[...] = x_ref[...] + (gt_m * gpost_ref[...]) * _rms(y)


def _mix_positions(g_v, vng_ref, vnb_ref, wsp_ref):
    t = g_v.shape[0]
    gv = _gelu(g_v)
    mu = jnp.mean(gv, axis=-1, keepdims=True)
    cen = gv - mu
    var = jnp.mean(cen * cen, axis=-1, keepdims=True)
    v = cen * lax.rsqrt(var + EPS) * vng_ref[...] + vnb_ref[...]
    vb = v.astype(BF16)
    nb = t // GMLP_BLOCK
    pi = lax.broadcasted_iota(jnp.int32, (GMLP_BLOCK, GMLP_BLOCK), 0) // CHUNK
    pj = lax.broadcasted_iota(jnp.int32, (GMLP_BLOCK, GMLP_BLOCK), 1) // CHUNK
    mask = pj <= pi
    sp_groups = []
    for g in range(GMLP_GROUPS):
        ws = jnp.where(mask, 0.5 * wsp_ref[g], 0.0).astype(BF16)
        cols = slice(g * GMLP_GROUP_DIM, (g + 1) * GMLP_GROUP_DIM)
        rhs = jnp.concatenate(
            [vb[n * GMLP_BLOCK:(n + 1) * GMLP_BLOCK, cols] for n in range(nb)], axis=1)
        sp_groups.append(_dot(ws, rhs))
    return sp_groups


def _mix_body(z, xc, gates, sp_groups, brg_ref, big_ref, lrua_ref, bsp_ref,
              glru_ref, ggmlp_ref, a_scr, b_scr, h_scr, state_ref):
    _, lru_gate, g_u, _ = z
    t = lru_gate.shape[0]
    slabs = lambda v: [v[:, c * LANES:(c + 1) * LANES] for c in range(v.shape[1] // LANES)]

    r_pre = jnp.concatenate([g[:, :GATE_TILE] for g in gates], axis=1)
    i_pre = jnp.concatenate([g[:, GATE_TILE:] for g in gates], axis=1)
    r_gate = jax.nn.sigmoid(r_pre + brg_ref[...])
    i_gate = jax.nn.sigmoid(i_pre + big_ref[...])
    neg_a = -lrua_ref[...]
    softplus = jnp.maximum(neg_a, 0.0) + jnp.log1p(jnp.exp(-jnp.abs(neg_a)))
    log_a = r_gate * ((-LRU_C) * softplus)
    a = jnp.exp(log_a)
    th = jnp.tanh(log_a)
    q = (-2.0 * th) / (1.0 - th)
    mult = jnp.where(q > 0.0, q * lax.rsqrt(q), 0.0)
    bx = (mult * xc) * i_gate
    hs = _linear_scan(slabs(a), slabs(bx), state_ref, a_scr, b_scr, h_scr)
    y_lru = 0.5 * _gelu2_times(lru_gate, jnp.concatenate(hs, axis=1))
    yl = _rms(y_lru) * glru_ref[...]

    nb = t // GMLP_BLOCK
    bsp = 0.5 * bsp_ref[...]
    sp_rows = []
    for n in range(nb):
        blk = jnp.concatenate(
            [sg[:, n * GMLP_GROUP_DIM:(n + 1) * GMLP_GROUP_DIM] for sg in sp_groups], axis=1)
        sp_rows.append(blk + bsp)
    sp_half = jnp.concatenate(sp_rows, axis=0)
    y_gmlp = _gelu2_times(g_u, sp_half)
    yg = _rms(y_gmlp) * ggmlp_ref[...]

    return jnp.concatenate([yl, yg], axis=1).astype(BF16)


def _mixer(x, mod, g_pre, g_post, w_in, conv_w, conv_b, w_gate, b_rgate, b_igate, lru_a,
           v_norm_g, v_norm_b, w_spatial, b_sp_full, g_lru_out, g_gmlp_out, w_out, later_weights,
           layer):
    bsz, s, d = x.shape
    t = MIX_T
    nb = MIX_NB
    assert bsz % nb == 0 and s % t == 0
    n_lg = LRU_WIDTH // LANES
    scan_rows = SUBLANES * (t // SUBLANES + 1)
    full = lambda a: pl.BlockSpec(a.shape, lambda b, i: (0,) * a.ndim)
    of_layer = lambda a: pl.BlockSpec((None,) + a.shape[1:],
                                      lambda b, i: (layer,) + (0,) * (a.ndim - 1))
    in_arrays = [g_pre, g_post, w_in, conv_w, conv_b, w_gate, b_rgate, b_igate, lru_a,
                 v_norm_g, v_norm_b, w_spatial, b_sp_full, g_lru_out, g_gmlp_out, w_out]
    resident = lambda a: pl.BlockSpec((None,) + a.shape[1:], lambda b, i: (layer, 0, 0),
                                      pipeline_mode=pl.Buffered(1))
    in_array_specs = [resident(a) if a is w_in or a is w_out
                      else of_layer(a) if a is conv_w or a is w_spatial else full(a)
                      for a in in_arrays]
    packed = lambda a: pltpu.VMEM((a.shape[1] // 2, a.shape[2]), U32)
    nt = s // t
    n_steps = (bsz // nb) * nt
    step = lambda b, i: (b * nt + i, 0)
    later_in, later_out, later_shapes = [], [], []
    for w in later_weights:
        _, k, n = w.shape
        assert k % (n_steps * 2 * SUBLANES) == 0
        later_in.append(pl.BlockSpec((None, k // n_steps, n),
                                     lambda b, i: (layer, b * nt + i, 0)))
        later_out.append(pl.BlockSpec((k // n_steps // 2, n), step))
        later_shapes.append(jax.ShapeDtypeStruct((k // 2, n), U32))
    out = pl.pallas_call(
        functools.partial(_mixer_kernel, n_later=len(later_weights)),
        grid=(bsz // nb, nt),
        in_specs=[
            pl.BlockSpec((nb, t, d), lambda b, i: (b, i, 0)),
            pl.BlockSpec(mod.shape, lambda b, i: (0, 0)),
        ] + in_array_specs + later_in,
        out_specs=[pl.BlockSpec((nb, t, d), lambda b, i: (b, i, 0))] + later_out,
        out_shape=[jax.ShapeDtypeStruct(x.shape, x.dtype)] + later_shapes,
        scratch_shapes=(
            [pltpu.VMEM((n_lg, SUBLANES + t, LANES), F32)] * nb
            + [pltpu.VMEM((n_lg, scan_rows, LANES), F32)] * (3 * nb)
            + [pltpu.VMEM((nb, SUBLANES, LRU_WIDTH), F32)] * 2
            + [packed(w_in), packed(w_out)]),
        compiler_params=pltpu.CompilerParams(
            dimension_semantics=("arbitrary", "arbitrary"), vmem_limit_bytes=VMEM_LIMIT),
        name="token_mixer",
    )(x, mod, *in_arrays, *later_weights)
    return out[0], out[1:]


def _ffn_kernel(x_ref, mod_ref, gpre_ref, gpost_ref, wup_ref, cw_ref, cb_ref, wd_ref,
                o_ref, scr_ref, tail_ref):
    t_idx = pl.program_id(1)
    t, d = x_ref.shape
    d_ff = 2 * wd_ref.shape[0]
    fc = FFN_FC
    lg = fc // LANES

    @pl.when(t_idx == 0)
    def _():
        tail_ref[...] = jnp.zeros_like(tail_ref)

    mod = mod_ref[pl.ds(pl.program_id(0), 1), :]
    sh_f = mod[:, 3 * d:4 * d]
    sc_f = mod[:, 4 * d:5 * d]
    gt_f = mod[:, 5 * d:6 * d]
    x = x_ref[...]
    hb = (_rms(x) * (gpre_ref[...] * (1.0 + sc_f)) + sh_f).astype(BF16)

    n_chunks = d_ff // fc
    halves = ((0, 1.0), (d_ff, 0.5))
    chunk_cols = lambda base, j: slice(base + j * fc, base + (j + 1) * fc)
    up_proj = lambda j: [_dot(hb, _as_bf16(wup_ref[:, chunk_cols(base, j)])) for base, _ in halves]

    def gated(j, ups):
        conv = []
        for k, (base, scale) in enumerate(halves):
            cols = chunk_cols(base, j)
            tail = tail_ref[:, cols]
            tail_ref[:, cols] = ups[k][t - SUBLANES:]
            slot = (2 * j + k) % FFN_SLOTS
            conv.append(_conv_via_scratch(scr_ref, slot * lg, tail, ups[k],
                                          cw_ref[:, cols] * scale, cb_ref[:, cols] * scale))
        return jnp.concatenate(
            [_gelu2_times(cg, cv) for cg, cv in zip(*conv)], axis=1).astype(BF16)

    acc = None
    ups = up_proj(0)
    for j in range(n_chunks):
        ups_next = up_proj(j + 1) if j + 1 < n_chunks else None
        part = _dot(gated(j, ups), _as_bf16(wd_ref[j * fc // 2:(j + 1) * fc // 2, :]))
        acc = part if acc is None else acc + part
        ups = ups_next
    o_ref[...] = x + (gt_f * gpost_ref[...]) * _rms(acc)


def _ffn(x, mod, g_pre, g_post, w_up, conv_w, conv_b, w_down, layer):
    bsz, s, d = x.shape
    t = FFN_T
    resident = lambda a: pl.BlockSpec(a.shape, lambda b, i: (0,) * a.ndim,
                                      pipeline_mode=pl.Buffered(1))
    return pl.pallas_call(
        _ffn_kernel,
        grid=(bsz, s // t),
        in_specs=[
            pl.BlockSpec((None, t, d), lambda b, i: (b, i, 0)),
            pl.BlockSpec(mod.shape, lambda b, i: (0, 0)),
            resident(g_pre), resident(g_post), resident(w_up),
            pl.BlockSpec((None,) + conv_w.shape[1:], lambda b, i: (layer, 0, 0),
                         pipeline_mode=pl.Buffered(1)),
            resident(conv_b), resident(w_down),
        ],
        out_specs=pl.BlockSpec((None, t, d), lambda b, i: (b, i, 0)),
        out_shape=jax.ShapeDtypeStruct(x.shape, x.dtype),
        scratch_shapes=[
            pltpu.VMEM((FFN_SLOTS * (FFN_FC // LANES), SUBLANES + t, LANES), F32),
            pltpu.VMEM((SUBLANES, w_up.shape[1]), F32),
        ],
        compiler_params=pltpu.CompilerParams(
            dimension_semantics=("arbitrary", "arbitrary"), vmem_limit_bytes=VMEM_LIMIT),
        name="conv_ffn",
    )(x, mod, g_pre, g_post, w_up, conv_w, conv_b, w_down)


def _gate_weights(w_rgate, w_igate):
    per_tile = GATE_TILE // LRU_HEAD_DIM
    n_tiles = LRU_WIDTH // GATE_TILE
    eye = jnp.eye(per_tile, dtype=F32)

    def block_diag(w):
        w4 = w.reshape(n_tiles, per_tile, LRU_HEAD_DIM, LRU_HEAD_DIM)
        return (w4[:, :, :, None, :] * eye[None, :, None, :, None]).reshape(
            n_tiles, GATE_TILE, GATE_TILE)

    return _pack_rows_xla(jnp.concatenate([block_diag(w_rgate), block_diag(w_igate)], axis=2))


def kernel(x, c, w_ada, b_ada, g_mix_pre, g_mix_post, w_in, conv_w, conv_b, w_rgate, b_rgate, w_igate, b_igate, lru_a, v_norm_g, v_norm_b, w_spatial, b_spatial, g_lru_out, g_gmlp_out, w_out, g_ffn_pre, g_ffn_post, w_up, ffn_conv_w, ffn_conv_b, w_down):
    depth = w_ada.shape[0]
    bsz, s, d = x.shape
    for l in range(depth):
        mod = _modulation(c, w_ada, b_ada[l:l + 1], l)
        row = lambda a: a[l:l + 1]
        w_gate = _gate_weights(w_rgate[l:l + 1], w_igate[l:l + 1])
        b_gate = (b_rgate[l:l + 1].reshape(1, -1), b_igate[l:l + 1].reshape(1, -1))
        b_sp_full = jnp.broadcast_to(
            jnp.transpose(b_spatial[l:l + 1], (2, 0, 1))[..., None],
            (GMLP_BLOCK, 1, GMLP_GROUPS, GMLP_GROUP_DIM)).reshape(GMLP_BLOCK, GMLP_WIDTH)
        x, (w_up_p, w_down_p) = _mixer(
                   x, mod, row(g_mix_pre), row(g_mix_post), w_in,
                   conv_w, row(conv_b), w_gate, *b_gate, row(lru_a),
                   row(v_norm_g), row(v_norm_b), w_spatial, b_sp_full,
                   row(g_lru_out), row(g_gmlp_out), w_out,
                   later_weights=[w_up, w_down], layer=l)
        x = _ffn(x, mod, row(g_ffn_pre), row(g_ffn_post), w_up_p,
                 ffn_conv_w, row(ffn_conv_b), w_down_p, layer=l)
    return x
```

```python
import functools

import jax
import jax.numpy as jnp
from jax import lax
from jax.experimental import pallas as pl
from jax.experimental.pallas import tpu as pltpu

CHUNK = 64
LRU_WIDTH = 512
LRU_HEADS = 8
LRU_HEAD_DIM = LRU_WIDTH // LRU_HEADS
LRU_CONV_WIDTH = 4
LRU_C = 8.0
GMLP_WIDTH = 512
GMLP_GROUPS = 4
GMLP_GROUP_DIM = GMLP_WIDTH // GMLP_GROUPS
GMLP_BLOCK = 128
FFN_CONV_WIDTH = 3
N_MOD = 6
EPS = 1e-6

SUBLANES = 8
LANES = 128
GATE_TILE = 256

MOD_ROWS = 128
MOD_STREAMS = 4
MIX_T = 256
MIX_NB = 4
FFN_T = 512
FFN_FC = 1024
FFN_SLOTS = 4
VMEM_LIMIT = 60 * 1024 * 1024

F32 = jnp.float32
BF16 = jnp.bfloat16
U32 = jnp.uint32


def _dot(a, b):
    return jnp.dot(a, b, preferred_element_type=F32)


def _pack_rows(w):
    return pltpu.bitcast(w.astype(BF16), U32)


def _as_bf16(packed):
    return pltpu.bitcast(packed, BF16)


def _rms(x):
    return x * lax.rsqrt(jnp.mean(x * x, axis=-1, keepdims=True) + EPS)


def _gelu(x):
    c0 = 0.7978845608028654
    hx = 0.5 * x
    return hx + hx * jnp.tanh(x * (c0 + (c0 * 0.044715) * (x * x)))


def _gelu2_times(x, v):
    c0 = 0.7978845608028654
    z = x * (c0 + (c0 * 0.044715) * (x * x))
    return (x * v) * (1.0 + jnp.tanh(z))


def _rows(ref, lead, start, size):
    return ref[pl.ds(lead, 1, stride=2), pl.ds(start, size), :][0]


def _put_rows(ref, lead, start, val):
    ref[pl.ds(lead, 1, stride=2), pl.ds(start, val.shape[0]), :] = val[None]


def _linear_scan(a, b, state_ref, a_scr, b_scr, h_scr):
    t = a[0].shape[0]
    seg = t // SUBLANES
    pitch = seg + 1 - seg % 2
    n = len(a)
    for c in range(n):
        for s in range(SUBLANES):
            _put_rows(a_scr, c, s * pitch, a[c][s * seg:(s + 1) * seg])
            _put_rows(b_scr, c, s * pitch, b[c][s * seg:(s + 1) * seg])
    step = lambda ref, c, j: ref[c, pl.ds(j, SUBLANES, stride=pitch), :]
    row = lax.broadcasted_iota(jnp.int32, (SUBLANES, LANES), 0)
    shifted = lambda v, sh, fill: jnp.where(row >= sh, pltpu.roll(v, sh, axis=0), fill)

    prod = [None] * n
    end = [None] * n
    for j in range(seg):
        for c in range(n):
            aj, bj = step(a_scr, c, j), step(b_scr, c, j)
            prod[c] = aj if j == 0 else aj * prod[c]
            end[c] = bj if j == 0 else aj * end[c] + bj

    h = []
    for c in range(n):
        cols = slice(c * LANES, (c + 1) * LANES)
        h0 = state_ref[SUBLANES - 1:SUBLANES, cols]
        p = prod[c]
        e = end[c] + jnp.where(row == 0, p * h0, 0.0)
        for sh in (1, 2, 4):
            e = e + p * shifted(e, sh, 0.0)
            if sh < 4:
                p = p * shifted(p, sh, 1.0)
        state_ref[:, cols] = e
        h.append(shifted(e, 1, h0))
    for j in range(seg):
        for c in range(n):
            h[c] = step(a_scr, c, j) * h[c] + step(b_scr, c, j)
            h_scr[c, pl.ds(j, SUBLANES, stride=pitch), :] = h[c]
    return [jnp.concatenate([_rows(h_scr, c, s * pitch, seg) for s in range(SUBLANES)], axis=0)
            for c in range(n)]


def _conv_via_scratch(scr_ref, lead0, tail, x, w, b):
    k_width = w.shape[0]
    t, c = x.shape
    outs = []
    for g in range(c // LANES):
        cols = slice(g * LANES, (g + 1) * LANES)
        scr_ref[lead0 + g, 0:SUBLANES, :] = tail[:, cols]
        scr_ref[lead0 + g, SUBLANES:SUBLANES + t, :] = x[:, cols]
    for g in range(c // LANES):
        cols = slice(g * LANES, (g + 1) * LANES)
        out = x[:, cols] * w[k_width - 1:k_width, cols] + b[:, cols]
        for k in range(k_width - 1):
            shifted = _rows(scr_ref, lead0 + g, SUBLANES - (k_width - 1 - k), t)
            out = out + shifted * w[k:k + 1, cols]
        outs.append(out)
    return outs


def _mod_kernel(c_ref, *rest):
    w_refs, b_ref, o_ref = rest[:-2], rest[-2], rest[-1]
    c = c_ref[...]
    rows = c.shape[0]
    c_act = c * jax.nn.sigmoid(c)
    pad = (-rows) % SUBLANES
    if pad:
        c_act = jnp.concatenate([c_act, jnp.zeros((pad, c.shape[1]), F32)], axis=0)
    c_act = c_act.astype(BF16)
    bk = w_refs[0].shape[0]
    parts = [_dot(c_act[:, j * bk:(j + 1) * bk], w_ref[...].astype(BF16))[:rows]
             for j, w_ref in enumerate(w_refs)]

    @pl.when(pl.program_id(0) == 0)
    def _():
        o_ref[...] = functools.reduce(jnp.add, parts, b_ref[...])

    @pl.when(pl.program_id(0) > 0)
    def _():
        o_ref[...] = functools.reduce(jnp.add, parts, o_ref[...])


def _modulation(c, w_ada, b_ada, layer):
    rows, d = c.shape
    n = w_ada.shape[2]
    bk = MOD_ROWS
    ns = MOD_STREAMS
    assert d % (bk * ns) == 0
    row_block = lambda j: pl.BlockSpec((None, bk, n), lambda k: (layer, k * ns + j, 0))
    return pl.pallas_call(
        _mod_kernel,
        grid=(d // (bk * ns),),
        in_specs=[pl.BlockSpec((rows, bk * ns), lambda k: (0, k))]
                 + [row_block(j) for j in range(ns)]
                 + [pl.BlockSpec((1, n), lambda k: (0, 0))],
        out_specs=pl.BlockSpec((rows, n), lambda k: (0, 0)),
        out_shape=jax.ShapeDtypeStruct((rows, n), F32),
        compiler_params=pltpu.CompilerParams(
            dimension_semantics=("arbitrary",), vmem_limit_bytes=VMEM_LIMIT),
        name="adaln_mod",
    )(c, *[w_ada] * ns, b_ada)


def _mixer_kernel(x_ref, mod_ref, gpre_ref, gpost_ref, win_ref, cw_ref, cb_ref,
                  wgate_ref, bgate_ref, lrua_ref, vng_ref, vnb_ref, wsp_ref, bsp_ref,
                  glru_ref, ggmlp_ref, wout_ref, *rest, n_later):
    later_in, o_ref, later_out = rest[:n_later], rest[n_later], rest[n_later + 1:2 * n_later + 1]
    scratch = rest[2 * n_later + 1:]
    for w_ref, p_ref in zip(later_in, later_out):
        p_ref[...] = _pack_rows(w_ref[...])
    nb_tiles = x_ref.shape[0]
    tail_ref, state_ref, win_packed_ref, wgate_packed_ref, wout_packed_ref = scratch[4 * nb_tiles:]

    @pl.when((pl.program_id(0) == 0) & (pl.program_id(1) == 0))
    def _():
        win_packed_ref[...] = _pack_rows(win_ref[...])
        for j in range(wgate_ref.shape[0]):
            wgate_packed_ref[j] = _pack_rows(wgate_ref[j])
        wout_packed_ref[...] = _pack_rows(wout_ref[...])

    @pl.when(pl.program_id(1) == 0)
    def _():
        tail_ref[...] = jnp.zeros_like(tail_ref)
        state_ref[...] = jnp.zeros_like(state_ref)

    mod_row = lambda k: mod_ref[pl.ds(pl.program_id(0) * nb_tiles + k, 1), :]
    project = lambda k: _mix_in(x_ref.at[k], mod_row(k), gpre_ref, win_packed_ref)
    z = project(0)
    for k in range(nb_tiles):
        conv_scr, a_scr, b_scr, h_scr = scratch[k:4 * nb_tiles:nb_tiles]
        xc, gates = _mix_gates(z[0], cw_ref, cb_ref, wgate_packed_ref, conv_scr, tail_ref.at[k])
        sp_groups = _mix_positions(z[3], vng_ref, vnb_ref, wsp_ref)
        z_next = project(k + 1) if k + 1 < nb_tiles else None
        y = _mix_body(z, xc, gates, sp_groups, bgate_ref, lrua_ref, bsp_ref,
                      glru_ref, ggmlp_ref, a_scr, b_scr, h_scr, state_ref.at[k])
        _mix_out(y, x_ref.at[k], mod_row(k), gpost_ref, wout_packed_ref, o_ref.at[k])
        z = z_next


def _mix_in(x_ref, mod, gpre_ref, win_ref):
    d = x_ref.shape[1]
    sh_m = mod[:, 0:d]
    sc_m = mod[:, d:2 * d]
    hb = (_rms(x_ref[...]) * (gpre_ref[...] * (1.0 + sc_m)) + sh_m).astype(BF16)
    widths = (LRU_WIDTH, LRU_WIDTH, GMLP_WIDTH, GMLP_WIDTH)
    starts = [sum(widths[:q]) for q in range(len(widths))]
    return [_dot(hb, _as_bf16(win_ref[:, c0:c0 + w])) for c0, w in zip(starts, widths)]


def _mix_gates(lru_x, cw_ref, cb_ref, wgate_ref, conv_scr, tail_ref):
    t = lru_x.shape[0]
    tail = tail_ref[...]
    tail_ref[...] = lru_x[t - SUBLANES:]
    xc = jnp.concatenate(
        _conv_via_scratch(conv_scr, 0, tail, lru_x, cw_ref[...], cb_ref[...]), axis=1)
    xcb = xc.astype(BF16)
    gates = [_dot(xcb[:, j * GATE_TILE:(j + 1) * GATE_TILE], _as_bf16(wgate_ref[j]))
             for j in range(LRU_WIDTH // GATE_TILE)]
    return xc, gates


def _mix_out(y, x_ref, mod, gpost_ref, wout_ref, o_ref):
    d = x_ref.shape[1]
    gt_m = mod[:, 2 * d:3 * d]
    y = _dot(y, _as_bf16(wout_ref[...]))
    o_ref[...] = x_ref[...] + (gt_m * gpost_ref[...]) * _rms(y)


def _mix_positions(g_v, vng_ref, vnb_ref, wsp_ref):
    t = g_v.shape[0]
    gv = _gelu(g_v)
    mu = jnp.mean(gv, axis=-1, keepdims=True)
    cen = gv - mu
    var = jnp.mean(cen * cen, axis=-1, keepdims=True)
    v = cen * lax.rsqrt(var + EPS) * vng_ref[...] + vnb_ref[...]
    vb = v.astype(BF16)
    nb = t // GMLP_BLOCK
    pi = lax.broadcasted_iota(jnp.int32, (GMLP_BLOCK, GMLP_BLOCK), 0) // CHUNK
    pj = lax.broadcasted_iota(jnp.int32, (GMLP_BLOCK, GMLP_BLOCK), 1) // CHUNK
    mask = pj <= pi
    sp_groups = []
    for g in range(GMLP_GROUPS):
        ws = jnp.where(mask, 0.5 * wsp_ref[g], 0.0).astype(BF16)
        cols = slice(g * GMLP_GROUP_DIM, (g + 1) * GMLP_GROUP_DIM)
        rhs = jnp.concatenate(
            [vb[n * GMLP_BLOCK:(n + 1) * GMLP_BLOCK, cols] for n in range(nb)], axis=1)
        sp_groups.append(_dot(ws, rhs))
    return sp_groups


def _mix_body(z, xc, gates, sp_groups, bgate_ref, lrua_ref, bsp_ref,
              glru_ref, ggmlp_ref, a_scr, b_scr, h_scr, state_ref):
    _, lru_gate, g_u, _ = z
    t = lru_gate.shape[0]
    slabs = lambda v: [v[:, c * LANES:(c + 1) * LANES] for c in range(v.shape[1] // LANES)]

    r_pre = jnp.concatenate([g[:, :GATE_TILE] for g in gates], axis=1)
    i_pre = jnp.concatenate([g[:, GATE_TILE:] for g in gates], axis=1)
    r_gate = jax.nn.sigmoid(r_pre + bgate_ref[0:1, :])
    i_gate = jax.nn.sigmoid(i_pre + bgate_ref[1:2, :])
    neg_a = -lrua_ref[...]
    softplus = jnp.maximum(neg_a, 0.0) + jnp.log1p(jnp.exp(-jnp.abs(neg_a)))
    log_a = r_gate * ((-LRU_C) * softplus)
    a = jnp.exp(log_a)
    th = jnp.tanh(log_a)
    q = (-2.0 * th) / (1.0 - th)
    mult = jnp.where(q > 0.0, q * lax.rsqrt(q), 0.0)
    bx = (mult * xc) * i_gate
    hs = _linear_scan(slabs(a), slabs(bx), state_ref, a_scr, b_scr, h_scr)
    y_lru = 0.5 * _gelu2_times(lru_gate, jnp.concatenate(hs, axis=1))
    yl = _rms(y_lru) * glru_ref[...]

    nb = t // GMLP_BLOCK
    bsp = 0.5 * bsp_ref[...]
    sp_rows = []
    for n in range(nb):
        blk = jnp.concatenate(
            [sg[:, n * GMLP_GROUP_DIM:(n + 1) * GMLP_GROUP_DIM] for sg in sp_groups], axis=1)
        sp_rows.append(blk + bsp)
    sp_half = jnp.concatenate(sp_rows, axis=0)
    y_gmlp = _gelu2_times(g_u, sp_half)
    yg = _rms(y_gmlp) * ggmlp_ref[...]

    return jnp.concatenate([yl, yg], axis=1).astype(BF16)


def _mixer(x, mod, g_pre, g_post, w_in, conv_w, conv_b, w_gate, b_gate, lru_a,
           v_norm_g, v_norm_b, w_spatial, b_sp_full, g_lru_out, g_gmlp_out, w_out, later_weights,
           layer):
    bsz, s, d = x.shape
    t = MIX_T
    nb = MIX_NB
    assert bsz % nb == 0 and s % t == 0
    n_lg = LRU_WIDTH // LANES
    scan_rows = SUBLANES * (t // SUBLANES + 1)
    full = lambda a: pl.BlockSpec(a.shape, lambda b, i: (0,) * a.ndim)
    of_layer = lambda a: pl.BlockSpec((None,) + a.shape[1:],
                                      lambda b, i: (layer,) + (0,) * (a.ndim - 1))
    in_arrays = [g_pre, g_post, w_in, conv_w, conv_b, w_gate, b_gate, lru_a,
                 v_norm_g, v_norm_b, w_spatial, b_sp_full, g_lru_out, g_gmlp_out, w_out]
    resident = lambda a: pl.BlockSpec((None,) + a.shape[1:], lambda b, i: (layer, 0, 0),
                                      pipeline_mode=pl.Buffered(1))
    in_array_specs = [resident(a) if a is w_in or a is w_out
                      else of_layer(a) if a is conv_w or a is w_spatial else full(a)
                      for a in in_arrays]
    packed = lambda shape: pltpu.VMEM(shape[:-2] + (shape[-2] // 2, shape[-1]), U32)
    nt = s // t
    n_steps = (bsz // nb) * nt
    step = lambda b, i: (b * nt + i, 0)
    later_in, later_out, later_shapes = [], [], []
    for w in later_weights:
        _, k, n = w.shape
        assert k % (n_steps * 2 * SUBLANES) == 0
        later_in.append(pl.BlockSpec((None, k // n_steps, n),
                                     lambda b, i: (layer, b * nt + i, 0)))
        later_out.append(pl.BlockSpec((k // n_steps // 2, n), step))
        later_shapes.append(jax.ShapeDtypeStruct((k // 2, n), U32))
    out = pl.pallas_call(
        functools.partial(_mixer_kernel, n_later=len(later_weights)),
        grid=(bsz // nb, nt),
        in_specs=[
            pl.BlockSpec((nb, t, d), lambda b, i: (b, i, 0)),
            pl.BlockSpec(mod.shape, lambda b, i: (0, 0)),
        ] + in_array_specs + later_in,
        out_specs=[pl.BlockSpec((nb, t, d), lambda b, i: (b, i, 0))] + later_out,
        out_shape=[jax.ShapeDtypeStruct(x.shape, x.dtype)] + later_shapes,
        scratch_shapes=(
            [pltpu.VMEM((n_lg, SUBLANES + t, LANES), F32)] * nb
            + [pltpu.VMEM((n_lg, scan_rows, LANES), F32)] * (3 * nb)
            + [pltpu.VMEM((nb, SUBLANES, LRU_WIDTH), F32)] * 2
            + [packed(w_in.shape[1:]), packed(w_gate.shape), packed(w_out.shape[1:])]),
        compiler_params=pltpu.CompilerParams(
            dimension_semantics=("arbitrary", "arbitrary"), vmem_limit_bytes=VMEM_LIMIT),
        name="token_mixer",
    )(x, mod, *in_arrays, *later_weights)
    return out[0], out[1:]


def _ffn_kernel(x_ref, mod_ref, gpre_ref, gpost_ref, wup_ref, cw_ref, cb_ref, wd_ref,
                o_ref, scr_ref, tail_ref):
    t_idx = pl.program_id(1)
    t, d = x_ref.shape
    d_ff = 2 * wd_ref.shape[0]
    fc = FFN_FC
    lg = fc // LANES

    @pl.when(t_idx == 0)
    def _():
        tail_ref[...] = jnp.zeros_like(tail_ref)

    mod = mod_ref[pl.ds(pl.program_id(0), 1), :]
    sh_f = mod[:, 3 * d:4 * d]
    sc_f = mod[:, 4 * d:5 * d]
    gt_f = mod[:, 5 * d:6 * d]
    x = x_ref[...]
    hb = (_rms(x) * (gpre_ref[...] * (1.0 + sc_f)) + sh_f).astype(BF16)

    n_chunks = d_ff // fc
    halves = ((0, 1.0), (d_ff, 0.5))
    chunk_cols = lambda base, j: slice(base + j * fc, base + (j + 1) * fc)
    up_proj = lambda j: [_dot(hb, _as_bf16(wup_ref[:, chunk_cols(base, j)])) for base, _ in halves]

    def gated(j, ups):
        conv = []
        for k, (base, scale) in enumerate(halves):
            cols = chunk_cols(base, j)
            tail = tail_ref[:, cols]
            tail_ref[:, cols] = ups[k][t - SUBLANES:]
            slot = (2 * j + k) % FFN_SLOTS
            conv.append(_conv_via_scratch(scr_ref, slot * lg, tail, ups[k],
                                          cw_ref[:, cols] * scale, cb_ref[:, cols] * scale))
        return jnp.concatenate(
            [_gelu2_times(cg, cv) for cg, cv in zip(*conv)], axis=1).astype(BF16)

    acc = None
    ups = up_proj(0)
    for j in range(n_chunks):
        ups_next = up_proj(j + 1) if j + 1 < n_chunks else None
        part = _dot(gated(j, ups), _as_bf16(wd_ref[j * fc // 2:(j + 1) * fc // 2, :]))
        acc = part if acc is None else acc + part
        ups = ups_next
    o_ref[...] = x + (gt_f * gpost_ref[...]) * _rms(acc)


def _ffn(x, mod, g_pre, g_post, w_up, conv_w, conv_b, w_down, layer):
    bsz, s, d = x.shape
    t = FFN_T
    resident = lambda a: pl.BlockSpec(a.shape, lambda b, i: (0,) * a.ndim,
                                      pipeline_mode=pl.Buffered(1))
    return pl.pallas_call(
        _ffn_kernel,
        grid=(bsz, s // t),
        in_specs=[
            pl.BlockSpec((None, t, d), lambda b, i: (b, i, 0)),
            pl.BlockSpec(mod.shape, lambda b, i: (0, 0)),
            resident(g_pre), resident(g_post), resident(w_up),
            pl.BlockSpec((None,) + conv_w.shape[1:], lambda b, i: (layer, 0, 0),
                         pipeline_mode=pl.Buffered(1)),
            resident(conv_b), resident(w_down),
        ],
        out_specs=pl.BlockSpec((None, t, d), lambda b, i: (b, i, 0)),
        out_shape=jax.ShapeDtypeStruct(x.shape, x.dtype),
        scratch_shapes=[
            pltpu.VMEM((FFN_SLOTS * (FFN_FC // LANES), SUBLANES + t, LANES), F32),
            pltpu.VMEM((SUBLANES, w_up.shape[1]), F32),
        ],
        compiler_params=pltpu.CompilerParams(
            dimension_semantics=("arbitrary", "arbitrary"), vmem_limit_bytes=VMEM_LIMIT),
        name="conv_ffn",
    )(x, mod, g_pre, g_post, w_up, conv_w, conv_b, w_down)


def _gate_weights(w_rgate, w_igate):
    per_tile = GATE_TILE // LRU_HEAD_DIM
    n_tiles = LRU_WIDTH // GATE_TILE

    def head_rows(w, head, slot):
        return jnp.pad(w[0, head], ((0, 0), (slot * LRU_HEAD_DIM, (per_tile - 1 - slot) * LRU_HEAD_DIM)))

    return jnp.stack([
        jnp.concatenate([
            jnp.concatenate([head_rows(w, t * per_tile + a, a) for w in (w_rgate, w_igate)], axis=1)
            for a in range(per_tile)], axis=0)
        for t in range(n_tiles)])


def kernel(x, c, w_ada, b_ada, g_mix_pre, g_mix_post, w_in, conv_w, conv_b, w_rgate, b_rgate, w_igate, b_igate, lru_a, v_norm_g, v_norm_b, w_spatial, b_spatial, g_lru_out, g_gmlp_out, w_out, g_ffn_pre, g_ffn_post, w_up, ffn_conv_w, ffn_conv_b, w_down):
    depth = w_ada.shape[0]
    bsz, s, d = x.shape
    for l in range(depth):
        mod = _modulation(c, w_ada, b_ada[l:l + 1], l)
        row = lambda a: a[l:l + 1]
        w_gate = _gate_weights(w_rgate[l:l + 1], w_igate[l:l + 1])
        b_gate = jnp.stack([b_rgate[l], b_igate[l]]).reshape(2, -1)
        b_sp_full = jnp.broadcast_to(
            jnp.transpose(b_spatial[l:l + 1], (2, 0, 1))[..., None],
            (GMLP_BLOCK, 1, GMLP_GROUPS, GMLP_GROUP_DIM)).reshape(GMLP_BLOCK, GMLP_WIDTH)
        x, (w_up_p, w_down_p) = _mixer(
                   x, mod, row(g_mix_pre), row(g_mix_post), w_in,
                   conv_w, row(conv_b), w_gate, b_gate, row(lru_a),
                   row(v_norm_g), row(v_norm_b), w_spatial, b_sp_full,
                   row(g_lru_out), row(g_gmlp_out), w_out,
                   later_weights=[w_up, w_down], layer=l)
        x = _ffn(x, mod, row(g_ffn_pre), row(g_ffn_post), w_up_p,
                 ffn_conv_w, row(ffn_conv_b), w_down_p, layer=l)
    return x
```

```python
import functools

import jax
import jax.numpy as jnp
from jax import lax
from jax.experimental import pallas as pl
from jax.experimental.pallas import tpu as pltpu

CHUNK = 64
LRU_WIDTH = 512
LRU_HEADS = 8
LRU_HEAD_DIM = LRU_WIDTH // LRU_HEADS
LRU_CONV_WIDTH = 4
LRU_C = 8.0
GMLP_WIDTH = 512
GMLP_GROUPS = 4
GMLP_GROUP_DIM = GMLP_WIDTH // GMLP_GROUPS
GMLP_BLOCK = 128
FFN_CONV_WIDTH = 3
N_MOD = 6
EPS = 1e-6

SUBLANES = 8
LANES = 128
GATE_TILE = 256

MOD_ROWS = 128
MOD_STREAMS = 2
MIX_T = 256
MIX_NB = 4
FFN_T = 512
FFN_FC = 1024
FFN_SLOTS = 4
VMEM_LIMIT = 60 * 1024 * 1024

F32 = jnp.float32
BF16 = jnp.bfloat16
U32 = jnp.uint32


def _dot(a, b):
    return jnp.dot(a, b, preferred_element_type=F32)


def _pack_rows(w):
    return pltpu.bitcast(w.astype(BF16), U32)


def _as_bf16(packed):
    return pltpu.bitcast(packed, BF16)


def _rms(x):
    return x * lax.rsqrt(jnp.mean(x * x, axis=-1, keepdims=True) + EPS)


def _gelu(x):
    c0 = 0.7978845608028654
    hx = 0.5 * x
    return hx + hx * jnp.tanh(x * (c0 + (c0 * 0.044715) * (x * x)))


def _gelu2_times(x, v):
    c0 = 0.7978845608028654
    z = x * (c0 + (c0 * 0.044715) * (x * x))
    return (x * v) * (1.0 + jnp.tanh(z))


def _rows(ref, lead, start, size):
    return ref[pl.ds(lead, 1, stride=2), pl.ds(start, size), :][0]


def _put_rows(ref, lead, start, val):
    ref[pl.ds(lead, 1, stride=2), pl.ds(start, val.shape[0]), :] = val[None]


def _linear_scan(a, b, state_ref, a_scr, b_scr, h_scr):
    t = a[0].shape[0]
    seg = t // SUBLANES
    pitch = seg + 1 - seg % 2
    n = len(a)
    for c in range(n):
        for s in range(SUBLANES):
            _put_rows(a_scr, c, s * pitch, a[c][s * seg:(s + 1) * seg])
            _put_rows(b_scr, c, s * pitch, b[c][s * seg:(s + 1) * seg])
    step = lambda ref, c, j: ref[c, pl.ds(j, SUBLANES, stride=pitch), :]
    row = lax.broadcasted_iota(jnp.int32, (SUBLANES, LANES), 0)
    shifted = lambda v, sh, fill: jnp.where(row >= sh, pltpu.roll(v, sh, axis=0), fill)

    prod = [None] * n
    end = [None] * n
    for j in range(seg):
        for c in range(n):
            aj, bj = step(a_scr, c, j), step(b_scr, c, j)
            prod[c] = aj if j == 0 else aj * prod[c]
            end[c] = bj if j == 0 else aj * end[c] + bj

    h = []
    for c in range(n):
        cols = slice(c * LANES, (c + 1) * LANES)
        h0 = state_ref[SUBLANES - 1:SUBLANES, cols]
        p = prod[c]
        e = end[c] + jnp.where(row == 0, p * h0, 0.0)
        for sh in (1, 2, 4):
            e = e + p * shifted(e, sh, 0.0)
            if sh < 4:
                p = p * shifted(p, sh, 1.0)
        state_ref[:, cols] = e
        h.append(shifted(e, 1, h0))
    for j in range(seg):
        for c in range(n):
            h[c] = step(a_scr, c, j) * h[c] + step(b_scr, c, j)
            h_scr[c, pl.ds(j, SUBLANES, stride=pitch), :] = h[c]
    return [jnp.concatenate([_rows(h_scr, c, s * pitch, seg) for s in range(SUBLANES)], axis=0)
            for c in range(n)]


def _conv_via_scratch(scr_ref, lead0, tail, x, w, b):
    k_width = w.shape[0]
    t, c = x.shape
    outs = []
    for g in range(c // LANES):
        cols = slice(g * LANES, (g + 1) * LANES)
        scr_ref[lead0 + g, 0:SUBLANES, :] = tail[:, cols]
        scr_ref[lead0 + g, SUBLANES:SUBLANES + t, :] = x[:, cols]
    for g in range(c // LANES):
        cols = slice(g * LANES, (g + 1) * LANES)
        out = x[:, cols] * w[k_width - 1:k_width, cols] + b[:, cols]
        for k in range(k_width - 1):
            shifted = _rows(scr_ref, lead0 + g, SUBLANES - (k_width - 1 - k), t)
            out = out + shifted * w[k:k + 1, cols]
        outs.append(out)
    return outs


def _mod_kernel(c_ref, *rest):
    w_refs, b_ref, o_ref = rest[:-2], rest[-2], rest[-1]
    c = c_ref[...]
    rows = c.shape[0]
    c_act = c * jax.nn.sigmoid(c)
    pad = (-rows) % SUBLANES
    if pad:
        c_act = jnp.concatenate([c_act, jnp.zeros((pad, c.shape[1]), F32)], axis=0)
    c_act = c_act.astype(BF16)
    bk = w_refs[0].shape[0]
    parts = [_dot(c_act[:, j * bk:(j + 1) * bk], w_ref[...].astype(BF16))[:rows]
             for j, w_ref in enumerate(w_refs)]

    @pl.when(pl.program_id(0) == 0)
    def _():
        o_ref[...] = functools.reduce(jnp.add, parts, b_ref[...])

    @pl.when(pl.program_id(0) > 0)
    def _():
        o_ref[...] = functools.reduce(jnp.add, parts, o_ref[...])


def _modulation(c, w_ada, b_ada, layer):
    rows, d = c.shape
    n = w_ada.shape[2]
    bk = MOD_ROWS
    ns = MOD_STREAMS
    assert d % (bk * ns) == 0
    row_block = lambda j: pl.BlockSpec((None, bk, n), lambda k: (layer, k * ns + j, 0))
    return pl.pallas_call(
        _mod_kernel,
        grid=(d // (bk * ns),),
        in_specs=[pl.BlockSpec((rows, bk * ns), lambda k: (0, k))]
                 + [row_block(j) for j in range(ns)]
                 + [pl.BlockSpec((1, n), lambda k: (0, 0))],
        out_specs=pl.BlockSpec((rows, n), lambda k: (0, 0)),
        out_shape=jax.ShapeDtypeStruct((rows, n), F32),
        compiler_params=pltpu.CompilerParams(
            dimension_semantics=("arbitrary",), vmem_limit_bytes=VMEM_LIMIT),
        name="adaln_mod",
    )(c, *[w_ada] * ns, b_ada)


def _mixer_kernel(x_ref, mod_ref, gpre_ref, gpost_ref, win_ref, cw_ref, cb_ref,
                  wgate_ref, bgate_ref, lrua_ref, vng_ref, vnb_ref, wsp_ref, bsp_ref,
                  glru_ref, ggmlp_ref, wout_ref, *rest, n_later):
    later_in, o_ref, later_out = rest[:n_later], rest[n_later], rest[n_later + 1:2 * n_later + 1]
    scratch = rest[2 * n_later + 1:]
    for w_ref, p_ref in zip(later_in, later_out):
        p_ref[...] = _pack_rows(w_ref[...])
    nb_tiles = x_ref.shape[0]
    (tail_ref, state_ref, win_packed_ref, wgate_packed_ref, wout_packed_ref,
     bsp_full_ref) = scratch[4 * nb_tiles:]

    @pl.when((pl.program_id(0) == 0) & (pl.program_id(1) == 0))
    def _():
        win_packed_ref[...] = _pack_rows(win_ref[...])
        for j in range(wgate_ref.shape[0]):
            wgate_packed_ref[j] = _pack_rows(wgate_ref[j])
        wout_packed_ref[...] = _pack_rows(wout_ref[...])
        diag = (lax.broadcasted_iota(jnp.int32, (GMLP_BLOCK, GMLP_BLOCK), 0)
                == lax.broadcasted_iota(jnp.int32, (GMLP_BLOCK, GMLP_BLOCK), 1))
        for g in range(GMLP_GROUPS):
            col = jnp.sum(jnp.where(diag, bsp_ref[g:g + 1, :], -0.0), axis=1, keepdims=True)
            bsp_full_ref[:, g * GMLP_GROUP_DIM:(g + 1) * GMLP_GROUP_DIM] = jnp.broadcast_to(
                col, (GMLP_BLOCK, GMLP_GROUP_DIM))

    @pl.when(pl.program_id(1) == 0)
    def _():
        tail_ref[...] = jnp.zeros_like(tail_ref)
        state_ref[...] = jnp.zeros_like(state_ref)

    mod_row = lambda k: mod_ref[pl.ds(pl.program_id(0) * nb_tiles + k, 1), :]
    project = lambda k: _mix_in(x_ref.at[k], mod_row(k), gpre_ref, win_packed_ref)
    z = project(0)
    for k in range(nb_tiles):
        conv_scr, a_scr, b_scr, h_scr = scratch[k:4 * nb_tiles:nb_tiles]
        xc, gates = _mix_gates(z[0], cw_ref, cb_ref, wgate_packed_ref, conv_scr, tail_ref.at[k])
        sp_groups = _mix_positions(z[3], vng_ref, vnb_ref, wsp_ref)
        z_next = project(k + 1) if k + 1 < nb_tiles else None
        y = _mix_body(z, xc, gates, sp_groups, bgate_ref, lrua_ref, bsp_full_ref,
                      glru_ref, ggmlp_ref, a_scr, b_scr, h_scr, state_ref.at[k])
        _mix_out(y, x_ref.at[k], mod_row(k), gpost_ref, wout_packed_ref, o_ref.at[k])
        z = z_next


def _mix_in(x_ref, mod, gpre_ref, win_ref):
    d = x_ref.shape[1]
    sh_m = mod[:, 0:d]
    sc_m = mod[:, d:2 * d]
    hb = (_rms(x_ref[...]) * (gpre_ref[...] * (1.0 + sc_m)) + sh_m).astype(BF16)
    widths = (LRU_WIDTH, LRU_WIDTH, GMLP_WIDTH, GMLP_WIDTH)
    starts = [sum(widths[:q]) for q in range(len(widths))]
    return [_dot(hb, _as_bf16(win_ref[:, c0:c0 + w])) for c0, w in zip(starts, widths)]


def _mix_gates(lru_x, cw_ref, cb_ref, wgate_ref, conv_scr, tail_ref):
    t = lru_x.shape[0]
    tail = tail_ref[...]
    tail_ref[...] = lru_x[t - SUBLANES:]
    xc = jnp.concatenate(
        _conv_via_scratch(conv_scr, 0, tail, lru_x, cw_ref[...], cb_ref[...]), axis=1)
    xcb = xc.astype(BF16)
    gates = [_dot(xcb[:, j * GATE_TILE:(j + 1) * GATE_TILE], _as_bf16(wgate_ref[j]))
             for j in range(LRU_WIDTH // GATE_TILE)]
    return xc, gates


def _mix_out(y, x_ref, mod, gpost_ref, wout_ref, o_ref):
    d = x_ref.shape[1]
    gt_m = mod[:, 2 * d:3 * d]
    y = _dot(y, _as_bf16(wout_ref[...]))
    o_ref[...] = x_ref[...] + (gt_m * gpost_ref[...]) * _rms(y)


def _mix_positions(g_v, vng_ref, vnb_ref, wsp_ref):
    t = g_v.shape[0]
    gv = _gelu(g_v)
    mu = jnp.mean(gv, axis=-1, keepdims=True)
    cen = gv - mu
    var = jnp.mean(cen * cen, axis=-1, keepdims=True)
    v = cen * lax.rsqrt(var + EPS) * vng_ref[...] + vnb_ref[...]
    vb = v.astype(BF16)
    nb = t // GMLP_BLOCK
    pi = lax.broadcasted_iota(jnp.int32, (GMLP_BLOCK, GMLP_BLOCK), 0) // CHUNK
    pj = lax.broadcasted_iota(jnp.int32, (GMLP_BLOCK, GMLP_BLOCK), 1) // CHUNK
    mask = pj <= pi
    sp_groups = []
    for g in range(GMLP_GROUPS):
        ws = jnp.where(mask, 0.5 * wsp_ref[g], 0.0).astype(BF16)
        cols = slice(g * GMLP_GROUP_DIM, (g + 1) * GMLP_GROUP_DIM)
        rhs = jnp.concatenate(
            [vb[n * GMLP_BLOCK:(n + 1) * GMLP_BLOCK, cols] for n in range(nb)], axis=1)
        sp_groups.append(_dot(ws, rhs))
    return sp_groups


def _mix_body(z, xc, gates, sp_groups, bgate_ref, lrua_ref, bsp_ref,
              glru_ref, ggmlp_ref, a_scr, b_scr, h_scr, state_ref):
    _, lru_gate, g_u, _ = z
    t = lru_gate.shape[0]
    slabs = lambda v: [v[:, c * LANES:(c + 1) * LANES] for c in range(v.shape[1] // LANES)]

    r_pre = jnp.concatenate([g[:, :GATE_TILE] for g in gates], axis=1)
    i_pre = jnp.concatenate([g[:, GATE_TILE:] for g in gates], axis=1)
    r_gate = jax.nn.sigmoid(r_pre + bgate_ref[0:1, :])
    i_gate = jax.nn.sigmoid(i_pre + bgate_ref[1:2, :])
    neg_a = -lrua_ref[...]
    softplus = jnp.maximum(neg_a, 0.0) + jnp.log1p(jnp.exp(-jnp.abs(neg_a)))
    log_a = r_gate * ((-LRU_C) * softplus)
    a = jnp.exp(log_a)
    th = jnp.tanh(log_a)
    q = (-2.0 * th) / (1.0 - th)
    mult = jnp.where(q > 0.0, q * lax.rsqrt(q), 0.0)
    bx = (mult * xc) * i_gate
    hs = _linear_scan(slabs(a), slabs(bx), state_ref, a_scr, b_scr, h_scr)
    y_lru = 0.5 * _gelu2_times(lru_gate, jnp.concatenate(hs, axis=1))
    yl = _rms(y_lru) * glru_ref[...]

    nb = t // GMLP_BLOCK
    bsp = 0.5 * bsp_ref[...]
    sp_rows = []
    for n in range(nb):
        blk = jnp.concatenate(
            [sg[:, n * GMLP_GROUP_DIM:(n + 1) * GMLP_GROUP_DIM] for sg in sp_groups], axis=1)
        sp_rows.append(blk + bsp)
    sp_half = jnp.concatenate(sp_rows, axis=0)
    y_gmlp = _gelu2_times(g_u, sp_half)
    yg = _rms(y_gmlp) * ggmlp_ref[...]

    return jnp.concatenate([yl, yg], axis=1).astype(BF16)


def _mixer(x, mod, g_pre, g_post, w_in, conv_w, conv_b, w_gate, b_gate, lru_a,
           v_norm_g, v_norm_b, w_spatial, b_spatial, g_lru_out, g_gmlp_out, w_out, later_weights,
           layer):
    bsz, s, d = x.shape
    t = MIX_T
    nb = MIX_NB
    assert bsz % nb == 0 and s % t == 0
    n_lg = LRU_WIDTH // LANES
    scan_rows = SUBLANES * (t // SUBLANES + 1)
    full = lambda a: pl.BlockSpec(a.shape, lambda b, i: (0,) * a.ndim)
    of_layer = lambda a: pl.BlockSpec((None,) + a.shape[1:],
                                      lambda b, i: (layer,) + (0,) * (a.ndim - 1))
    in_arrays = [g_pre, g_post, w_in, conv_w, conv_b, w_gate, b_gate, lru_a,
                 v_norm_g, v_norm_b, w_spatial, b_spatial, g_lru_out, g_gmlp_out, w_out]
    resident = lambda a: pl.BlockSpec((None,) + a.shape[1:], lambda b, i: (layer, 0, 0),
                                      pipeline_mode=pl.Buffered(1))
    in_array_specs = [resident(a) if a is w_in or a is w_out
                      else of_layer(a) if a is conv_w or a is w_spatial or a is b_spatial
                      else full(a)
                      for a in in_arrays]
    packed = lambda shape: pltpu.VMEM(shape[:-2] + (shape[-2] // 2, shape[-1]), U32)
    nt = s // t
    n_steps = (bsz // nb) * nt
    step = lambda b, i: (b * nt + i, 0)
    later_in, later_out, later_shapes = [], [], []
    for w in later_weights:
        _, k, n = w.shape
        assert k % (n_steps * 2 * SUBLANES) == 0
        later_in.append(pl.BlockSpec((None, k // n_steps, n),
                                     lambda b, i: (layer, b * nt + i, 0)))
        later_out.append(pl.BlockSpec((k // n_steps // 2, n), step))
        later_shapes.append(jax.ShapeDtypeStruct((k // 2, n), U32))
    out = pl.pallas_call(
        functools.partial(_mixer_kernel, n_later=len(later_weights)),
        grid=(bsz // nb, nt),
        in_specs=[
            pl.BlockSpec((nb, t, d), lambda b, i: (b, i, 0)),
            pl.BlockSpec(mod.shape, lambda b, i: (0, 0)),
        ] + in_array_specs + later_in,
        out_specs=[pl.BlockSpec((nb, t, d), lambda b, i: (b, i, 0))] + later_out,
        out_shape=[jax.ShapeDtypeStruct(x.shape, x.dtype)] + later_shapes,
        scratch_shapes=(
            [pltpu.VMEM((n_lg, SUBLANES + t, LANES), F32)] * nb
            + [pltpu.VMEM((n_lg, scan_rows, LANES), F32)] * (3 * nb)
            + [pltpu.VMEM((nb, SUBLANES, LRU_WIDTH), F32)] * 2
            + [packed(w_in.shape[1:]), packed(w_gate.shape), packed(w_out.shape[1:])]
            + [pltpu.VMEM((GMLP_BLOCK, GMLP_WIDTH), F32)]),
        compiler_params=pltpu.CompilerParams(
            dimension_semantics=("arbitrary", "arbitrary"), vmem_limit_bytes=VMEM_LIMIT),
        name="token_mixer",
    )(x, mod, *in_arrays, *later_weights)
    return out[0], out[1:]


def _ffn_kernel(x_ref, mod_ref, gpre_ref, gpost_ref, wup_ref, cw_ref, cb_ref, wd_ref,
                o_ref, scr_ref, tail_ref):
    t_idx = pl.program_id(1)
    t, d = x_ref.shape
    d_ff = 2 * wd_ref.shape[0]
    fc = FFN_FC
    lg = fc // LANES

    @pl.when(t_idx == 0)
    def _():
        tail_ref[...] = jnp.zeros_like(tail_ref)

    mod = mod_ref[pl.ds(pl.program_id(0), 1), :]
    sh_f = mod[:, 3 * d:4 * d]
    sc_f = mod[:, 4 * d:5 * d]
    gt_f = mod[:, 5 * d:6 * d]
    x = x_ref[...]
    hb = (_rms(x) * (gpre_ref[...] * (1.0 + sc_f)) + sh_f).astype(BF16)

    n_chunks = d_ff // fc
    halves = ((0, 1.0), (d_ff, 0.5))
    chunk_cols = lambda base, j: slice(base + j * fc, base + (j + 1) * fc)
    up_proj = lambda j: [_dot(hb, _as_bf16(wup_ref[:, chunk_cols(base, j)])) for base, _ in halves]

    def gated(j, ups):
        conv = []
        for k, (base, scale) in enumerate(halves):
            cols = chunk_cols(base, j)
            tail = tail_ref[:, cols]
            tail_ref[:, cols] = ups[k][t - SUBLANES:]
            slot = (2 * j + k) % FFN_SLOTS
            conv.append(_conv_via_scratch(scr_ref, slot * lg, tail, ups[k],
                                          cw_ref[:, cols] * scale, cb_ref[:, cols] * scale))
        return jnp.concatenate(
            [_gelu2_times(cg, cv) for cg, cv in zip(*conv)], axis=1).astype(BF16)

    acc = None
    ups = up_proj(0)
    for j in range(n_chunks):
        ups_next = up_proj(j + 1) if j + 1 < n_chunks else None
        part = _dot(gated(j, ups), _as_bf16(wd_ref[j * fc // 2:(j + 1) * fc // 2, :]))
        acc = part if acc is None else acc + part
        ups = ups_next
    o_ref[...] = x + (gt_f * gpost_ref[...]) * _rms(acc)


def _ffn(x, mod, g_pre, g_post, w_up, conv_w, conv_b, w_down, layer):
    bsz, s, d = x.shape
    t = FFN_T
    resident = lambda a: pl.BlockSpec(a.shape, lambda b, i: (0,) * a.ndim,
                                      pipeline_mode=pl.Buffered(1))
    return pl.pallas_call(
        _ffn_kernel,
        grid=(bsz, s // t),
        in_specs=[
            pl.BlockSpec((None, t, d), lambda b, i: (b, i, 0)),
            pl.BlockSpec(mod.shape, lambda b, i: (0, 0)),
            resident(g_pre), resident(g_post), resident(w_up),
            pl.BlockSpec((None,) + conv_w.shape[1:], lambda b, i: (layer, 0, 0),
                         pipeline_mode=pl.Buffered(1)),
            resident(conv_b), resident(w_down),
        ],
        out_specs=pl.BlockSpec((None, t, d), lambda b, i: (b, i, 0)),
        out_shape=jax.ShapeDtypeStruct(x.shape, x.dtype),
        scratch_shapes=[
            pltpu.VMEM((FFN_SLOTS * (FFN_FC // LANES), SUBLANES + t, LANES), F32),
            pltpu.VMEM((SUBLANES, w_up.shape[1]), F32),
        ],
        compiler_params=pltpu.CompilerParams(
            dimension_semantics=("arbitrary", "arbitrary"), vmem_limit_bytes=VMEM_LIMIT),
        name="conv_ffn",
    )(x, mod, g_pre, g_post, w_up, conv_w, conv_b, w_down)


def _gate_weights(w_rgate, w_igate):
    per_tile = GATE_TILE // LRU_HEAD_DIM
    n_tiles = LRU_WIDTH // GATE_TILE

    def head_rows(w, head, slot):
        return jnp.pad(w[0, head], ((0, 0), (slot * LRU_HEAD_DIM, (per_tile - 1 - slot) * LRU_HEAD_DIM)))

    return jnp.stack([
        jnp.concatenate([
            jnp.concatenate([head_rows(w, t * per_tile + a, a) for w in (w_rgate, w_igate)], axis=1)
            for a in range(per_tile)], axis=0)
        for t in range(n_tiles)])


def kernel(x, c, w_ada, b_ada, g_mix_pre, g_mix_post, w_in, conv_w, conv_b, w_rgate, b_rgate, w_igate, b_igate, lru_a, v_norm_g, v_norm_b, w_spatial, b_spatial, g_lru_out, g_gmlp_out, w_out, g_ffn_pre, g_ffn_post, w_up, ffn_conv_w, ffn_conv_b, w_down):
    depth = w_ada.shape[0]
    bsz, s, d = x.shape
    for l in range(depth):
        mod = _modulation(c, w_ada, b_ada[l:l + 1], l)
        row = lambda a: a[l:l + 1]
        w_gate = _gate_weights(w_rgate[l:l + 1], w_igate[l:l + 1])
        b_gate = jnp.stack([b_rgate[l], b_igate[l]]).reshape(2, -1)
        x, (w_up_p, w_down_p) = _mixer(
                   x, mod, row(g_mix_pre), row(g_mix_post), w_in,
                   conv_w, row(conv_b), w_gate, b_gate, row(lru_a),
                   row(v_norm_g), row(v_norm_b), w_spatial, b_spatial,
                   row(g_lru_out), row(g_gmlp_out), w_out,
                   later_weights=[w_up, w_down], layer=l)
        x = _ffn(x, mod, row(g_ffn_pre), row(g_ffn_post), w_up_p,
                 ffn_conv_w, row(ffn_conv_b), w_down_p, layer=l)
    return x
```

```python
import functools

import jax
import jax.numpy as jnp
from jax import lax
from jax.experimental import pallas as pl
from jax.experimental.pallas import tpu as pltpu

CHUNK = 64
LRU_WIDTH = 512
LRU_HEADS = 8
LRU_HEAD_DIM = LRU_WIDTH // LRU_HEADS
LRU_CONV_WIDTH = 4
LRU_C = 8.0
GMLP_WIDTH = 512
GMLP_GROUPS = 4
GMLP_GROUP_DIM = GMLP_WIDTH // GMLP_GROUPS
GMLP_BLOCK = 128
FFN_CONV_WIDTH = 3
N_MOD = 6
EPS = 1e-6

SUBLANES = 8
LANES = 128
GATE_TILE = 256

MOD_ROWS = 128
MOD_STREAMS = 2
MIX_T = 256
MIX_NB = 4
FFN_T = 512
FFN_FC = 1024
FFN_SLOTS = 4
VMEM_LIMIT = 60 * 1024 * 1024

F32 = jnp.float32
BF16 = jnp.bfloat16
U32 = jnp.uint32


def _dot(a, b):
    return jnp.dot(a, b, preferred_element_type=F32)


def _pack_rows(w):
    return pltpu.bitcast(w.astype(BF16), U32)


def _as_bf16(packed):
    return pltpu.bitcast(packed, BF16)


def _rms(x):
    return x * lax.rsqrt(jnp.mean(x * x, axis=-1, keepdims=True) + EPS)


def _gelu(x):
    c0 = 0.7978845608028654
    hx = 0.5 * x
    return hx + hx * jnp.tanh(x * (c0 + (c0 * 0.044715) * (x * x)))


def _gelu2_times(x, v):
    c0 = 0.7978845608028654
    z = x * (c0 + (c0 * 0.044715) * (x * x))
    return (x * v) * (1.0 + jnp.tanh(z))


def _rows(ref, lead, start, size):
    return ref[pl.ds(lead, 1, stride=2), pl.ds(start, size), :][0]


def _put_rows(ref, lead, start, val):
    ref[pl.ds(lead, 1, stride=2), pl.ds(start, val.shape[0]), :] = val[None]


def _linear_scan(a, b, state_ref, a_scr, b_scr, h_scr):
    t = a[0].shape[0]
    seg = t // SUBLANES
    pitch = seg + 1 - seg % 2
    n = len(a)
    for c in range(n):
        for s in range(SUBLANES):
            _put_rows(a_scr, c, s * pitch, a[c][s * seg:(s + 1) * seg])
            _put_rows(b_scr, c, s * pitch, b[c][s * seg:(s + 1) * seg])
    step = lambda ref, c, j: ref[c, pl.ds(j, SUBLANES, stride=pitch), :]
    row = lax.broadcasted_iota(jnp.int32, (SUBLANES, LANES), 0)
    shifted = lambda v, sh, fill: jnp.where(row >= sh, pltpu.roll(v, sh, axis=0), fill)

    prod = [None] * n
    end = [None] * n
    for j in range(seg):
        for c in range(n):
            aj, bj = step(a_scr, c, j), step(b_scr, c, j)
            prod[c] = aj if j == 0 else aj * prod[c]
            end[c] = bj if j == 0 else aj * end[c] + bj

    h = []
    for c in range(n):
        cols = slice(c * LANES, (c + 1) * LANES)
        h0 = state_ref[SUBLANES - 1:SUBLANES, cols]
        p = prod[c]
        e = end[c] + jnp.where(row == 0, p * h0, 0.0)
        for sh in (1, 2, 4):
            e = e + p * shifted(e, sh, 0.0)
            if sh < 4:
                p = p * shifted(p, sh, 1.0)
        state_ref[:, cols] = e
        h.append(shifted(e, 1, h0))
    for j in range(seg):
        for c in range(n):
            h[c] = step(a_scr, c, j) * h[c] + step(b_scr, c, j)
            h_scr[c, pl.ds(j, SUBLANES, stride=pitch), :] = h[c]
    return [jnp.concatenate([_rows(h_scr, c, s * pitch, seg) for s in range(SUBLANES)], axis=0)
            for c in range(n)]


def _conv_via_scratch(scr_ref, lead0, tail, x, w, b):
    k_width = w.shape[0]
    t, c = x.shape
    outs = []
    for g in range(c // LANES):
        cols = slice(g * LANES, (g + 1) * LANES)
        scr_ref[lead0 + g, 0:SUBLANES, :] = tail[:, cols]
        scr_ref[lead0 + g, SUBLANES:SUBLANES + t, :] = x[:, cols]
    for g in range(c // LANES):
        cols = slice(g * LANES, (g + 1) * LANES)
        out = x[:, cols] * w[k_width - 1:k_width, cols] + b[:, cols]
        for k in range(k_width - 1):
            shifted = _rows(scr_ref, lead0 + g, SUBLANES - (k_width - 1 - k), t)
            out = out + shifted * w[k:k + 1, cols]
        outs.append(out)
    return outs


def _mod_kernel(c_ref, *rest):
    w_refs, b_ref, o_ref = rest[:-2], rest[-2], rest[-1]
    c = c_ref[...]
    rows = c.shape[0]
    c_act = c * jax.nn.sigmoid(c)
    pad = (-rows) % SUBLANES
    if pad:
        c_act = jnp.concatenate([c_act, jnp.zeros((pad, c.shape[1]), F32)], axis=0)
    c_act = c_act.astype(BF16)
    bk = w_refs[0].shape[0]
    parts = [_dot(c_act[:, j * bk:(j + 1) * bk], w_ref[...].astype(BF16))[:rows]
             for j, w_ref in enumerate(w_refs)]

    @pl.when(pl.program_id(0) == 0)
    def _():
        o_ref[...] = functools.reduce(jnp.add, parts, b_ref[...])

    @pl.when(pl.program_id(0) > 0)
    def _():
        o_ref[...] = functools.reduce(jnp.add, parts, o_ref[...])


def _modulation(c, w_ada, b_ada, layer):
    rows, d = c.shape
    n = w_ada.shape[2]
    bk = MOD_ROWS
    ns = MOD_STREAMS
    assert d % (bk * ns) == 0
    row_block = lambda j: pl.BlockSpec((None, bk, n), lambda k: (layer, k * ns + j, 0))
    return pl.pallas_call(
        _mod_kernel,
        grid=(d // (bk * ns),),
        in_specs=[pl.BlockSpec((rows, bk * ns), lambda k: (0, k))]
                 + [row_block(j) for j in range(ns)]
                 + [pl.BlockSpec((1, n), lambda k: (0, 0))],
        out_specs=pl.BlockSpec((rows, n), lambda k: (0, 0)),
        out_shape=jax.ShapeDtypeStruct((rows, n), F32),
        compiler_params=pltpu.CompilerParams(
            dimension_semantics=("arbitrary",), vmem_limit_bytes=VMEM_LIMIT),
        name="adaln_mod",
    )(c, *[w_ada] * ns, b_ada)


def _mixer_kernel(x_ref, mod_ref, gpre_ref, gpost_ref, win_ref, cw_ref, cb_ref,
                  wgate_ref, bgate_ref, lrua_ref, vng_ref, vnb_ref, wsp_ref, bsp_ref,
                  glru_ref, ggmlp_ref, wout_ref, *rest, n_later):
    later_in, o_ref, later_out = rest[:n_later], rest[n_later], rest[n_later + 1:2 * n_later + 1]
    scratch = rest[2 * n_later + 1:]
    for w_ref, p_ref in zip(later_in, later_out):
        p_ref[...] = _pack_rows(w_ref[...])
    nb_tiles = x_ref.shape[0]
    (tail_ref, state_ref, win_packed_ref, wgate_packed_ref, wout_packed_ref,
     bsp_full_ref) = scratch[4 * nb_tiles:]

    @pl.when((pl.program_id(0) == 0) & (pl.program_id(1) == 0))
    def _():
        win_packed_ref[...] = _pack_rows(win_ref[...])
        for j in range(wgate_ref.shape[0]):
            wgate_packed_ref[j] = _pack_rows(wgate_ref[j])
        wout_packed_ref[...] = _pack_rows(wout_ref[...])
        diag = (lax.broadcasted_iota(jnp.int32, (GMLP_BLOCK, GMLP_BLOCK), 0)
                == lax.broadcasted_iota(jnp.int32, (GMLP_BLOCK, GMLP_BLOCK), 1))
        for g in range(GMLP_GROUPS):
            col = jnp.sum(jnp.where(diag, bsp_ref[g:g + 1, :], -0.0), axis=1, keepdims=True)
            bsp_full_ref[:, g * GMLP_GROUP_DIM:(g + 1) * GMLP_GROUP_DIM] = jnp.broadcast_to(
                col, (GMLP_BLOCK, GMLP_GROUP_DIM))

    @pl.when(pl.program_id(1) == 0)
    def _():
        tail_ref[...] = jnp.zeros_like(tail_ref)
        state_ref[...] = jnp.zeros_like(state_ref)

    mod_row = lambda k: mod_ref[pl.ds(pl.program_id(0) * nb_tiles + k, 1), :]
    project = lambda k: _mix_in(x_ref.at[k], mod_row(k), gpre_ref, win_packed_ref)
    z = project(0)
    for k in range(nb_tiles):
        conv_scr, a_scr, b_scr, h_scr = scratch[k:4 * nb_tiles:nb_tiles]
        xc, gates = _mix_gates(z[0], cw_ref, cb_ref, wgate_packed_ref, conv_scr, tail_ref.at[k])
        sp_groups = _mix_positions(z[3], vng_ref, vnb_ref, wsp_ref)
        z_next = project(k + 1) if k + 1 < nb_tiles else None
        y = _mix_body(z, xc, gates, sp_groups, bgate_ref, lrua_ref, bsp_full_ref,
                      glru_ref, ggmlp_ref, a_scr, b_scr, h_scr, state_ref.at[k])
        _mix_out(y, x_ref.at[k], mod_row(k), gpost_ref, wout_packed_ref, o_ref.at[k])
        z = z_next


def _mix_in(x_ref, mod, gpre_ref, win_ref):
    d = x_ref.shape[1]
    sh_m = mod[:, 0:d]
    sc_m = mod[:, d:2 * d]
    hb = (_rms(x_ref[...]) * (gpre_ref[...] * (1.0 + sc_m)) + sh_m).astype(BF16)
    widths = (LRU_WIDTH, LRU_WIDTH, GMLP_WIDTH, GMLP_WIDTH)
    starts = [sum(widths[:q]) for q in range(len(widths))]
    return [_dot(hb, _as_bf16(win_ref[:, c0:c0 + w])) for c0, w in zip(starts, widths)]


def _mix_gates(lru_x, cw_ref, cb_ref, wgate_ref, conv_scr, tail_ref):
    t = lru_x.shape[0]
    tail = tail_ref[...]
    tail_ref[...] = lru_x[t - SUBLANES:]
    xc = jnp.concatenate(
        _conv_via_scratch(conv_scr, 0, tail, lru_x, cw_ref[...], cb_ref[...]), axis=1)
    xcb = xc.astype(BF16)
    gates = [_dot(xcb[:, j * GATE_TILE:(j + 1) * GATE_TILE], _as_bf16(wgate_ref[j]))
             for j in range(LRU_WIDTH // GATE_TILE)]
    return xc, gates


def _mix_out(y, x_ref, mod, gpost_ref, wout_ref, o_ref):
    d = x_ref.shape[1]
    gt_m = mod[:, 2 * d:3 * d]
    y = _dot(y, _as_bf16(wout_ref[...]))
    o_ref[...] = x_ref[...] + (gt_m * gpost_ref[...]) * _rms(y)


def _mix_positions(g_v, vng_ref, vnb_ref, wsp_ref):
    t = g_v.shape[0]
    gv = _gelu(g_v)
    mu = jnp.mean(gv, axis=-1, keepdims=True)
    cen = gv - mu
    var = jnp.mean(cen * cen, axis=-1, keepdims=True)
    v = cen * lax.rsqrt(var + EPS) * vng_ref[...] + vnb_ref[...]
    vb = v.astype(BF16)
    nb = t // GMLP_BLOCK
    pi = lax.broadcasted_iota(jnp.int32, (GMLP_BLOCK, GMLP_BLOCK), 0) // CHUNK
    pj = lax.broadcasted_iota(jnp.int32, (GMLP_BLOCK, GMLP_BLOCK), 1) // CHUNK
    mask = pj <= pi
    sp_groups = []
    for g in range(GMLP_GROUPS):
        ws = jnp.where(mask, 0.5 * wsp_ref[g], 0.0).astype(BF16)
        cols = slice(g * GMLP_GROUP_DIM, (g + 1) * GMLP_GROUP_DIM)
        rhs = jnp.concatenate(
            [vb[n * GMLP_BLOCK:(n + 1) * GMLP_BLOCK, cols] for n in range(nb)], axis=1)
        sp_groups.append(_dot(ws, rhs))
    return sp_groups


def _mix_body(z, xc, gates, sp_groups, bgate_ref, lrua_ref, bsp_ref,
              glru_ref, ggmlp_ref, a_scr, b_scr, h_scr, state_ref):
    _, lru_gate, g_u, _ = z
    t = lru_gate.shape[0]
    slabs = lambda v: [v[:, c * LANES:(c + 1) * LANES] for c in range(v.shape[1] // LANES)]

    r_pre = jnp.concatenate([g[:, :GATE_TILE] for g in gates], axis=1)
    i_pre = jnp.concatenate([g[:, GATE_TILE:] for g in gates], axis=1)
    r_gate = jax.nn.sigmoid(r_pre + bgate_ref[0:1, :])
    i_gate = jax.nn.sigmoid(i_pre + bgate_ref[1:2, :])
    neg_a = -lrua_ref[...]
    softplus = jnp.maximum(neg_a, 0.0) + jnp.log1p(jnp.exp(-jnp.abs(neg_a)))
    log_a = r_gate * ((-LRU_C) * softplus)
    a = jnp.exp(log_a)
    th = jnp.tanh(log_a)
    q = (-2.0 * th) / (1.0 - th)
    mult = jnp.where(q > 0.0, q * lax.rsqrt(q), 0.0)
    bx = (mult * xc) * i_gate
    hs = _linear_scan(slabs(a), slabs(bx), state_ref, a_scr, b_scr, h_scr)
    y_lru = 0.5 * _gelu2_times(lru_gate, jnp.concatenate(hs, axis=1))
    yl = _rms(y_lru) * glru_ref[...]

    nb = t // GMLP_BLOCK
    bsp = 0.5 * bsp_ref[...]
    sp_rows = []
    for n in range(nb):
        blk = jnp.concatenate(
            [sg[:, n * GMLP_GROUP_DIM:(n + 1) * GMLP_GROUP_DIM] for sg in sp_groups], axis=1)
        sp_rows.append(blk + bsp)
    sp_half = jnp.concatenate(sp_rows, axis=0)
    y_gmlp = _gelu2_times(g_u, sp_half)
    yg = _rms(y_gmlp) * ggmlp_ref[...]

    return jnp.concatenate([yl, yg], axis=1).astype(BF16)


def _mixer(x, mod, g_pre, g_post, w_in, conv_w, conv_b, w_gate, b_gate, lru_a,
           v_norm_g, v_norm_b, w_spatial, b_spatial, g_lru_out, g_gmlp_out, w_out, later_weights,
           layer):
    bsz, s, d = x.shape
    t = MIX_T
    nb = MIX_NB
    assert bsz % nb == 0 and s % t == 0
    n_lg = LRU_WIDTH // LANES
    scan_rows = SUBLANES * (t // SUBLANES + 1)
    full = lambda a: pl.BlockSpec(a.shape, lambda b, i: (0,) * a.ndim)
    of_layer = lambda a: pl.BlockSpec((None,) + a.shape[1:],
                                      lambda b, i: (layer,) + (0,) * (a.ndim - 1))
    in_arrays = [g_pre, g_post, w_in, conv_w, conv_b, w_gate, b_gate, lru_a,
                 v_norm_g, v_norm_b, w_spatial, b_spatial, g_lru_out, g_gmlp_out, w_out]
    resident = lambda a: pl.BlockSpec((None,) + a.shape[1:], lambda b, i: (layer, 0, 0),
                                      pipeline_mode=pl.Buffered(1))
    in_array_specs = [resident(a) if a is w_in or a is w_out
                      else of_layer(a) if a is conv_w or a is w_spatial or a is b_spatial
                      else full(a)
                      for a in in_arrays]
    packed = lambda shape: pltpu.VMEM(shape[:-2] + (shape[-2] // 2, shape[-1]), U32)
    nt = s // t
    n_steps = (bsz // nb) * nt
    step = lambda b, i: (b * nt + i, 0)
    later_in, later_out, later_shapes = [], [], []
    for w in later_weights:
        _, k, n = w.shape
        assert k % (n_steps * 2 * SUBLANES) == 0
        later_in.append(pl.BlockSpec((None, k // n_steps, n),
                                     lambda b, i: (layer, b * nt + i, 0)))
        later_out.append(pl.BlockSpec((k // n_steps // 2, n), step))
        later_shapes.append(jax.ShapeDtypeStruct((k // 2, n), U32))
    out = pl.pallas_call(
        functools.partial(_mixer_kernel, n_later=len(later_weights)),
        grid=(bsz // nb, nt),
        in_specs=[
            pl.BlockSpec((nb, t, d), lambda b, i: (b, i, 0)),
            pl.BlockSpec(mod.shape, lambda b, i: (0, 0)),
        ] + in_array_specs + later_in,
        out_specs=[pl.BlockSpec((nb, t, d), lambda b, i: (b, i, 0))] + later_out,
        out_shape=[jax.ShapeDtypeStruct(x.shape, x.dtype)] + later_shapes,
        scratch_shapes=(
            [pltpu.VMEM((n_lg, SUBLANES + t, LANES), F32)] * nb
            + [pltpu.VMEM((n_lg, scan_rows, LANES), F32)] * (3 * nb)
            + [pltpu.VMEM((nb, SUBLANES, LRU_WIDTH), F32)] * 2
            + [packed(w_in.shape[1:]), packed(w_gate.shape), packed(w_out.shape[1:])]
            + [pltpu.VMEM((GMLP_BLOCK, GMLP_WIDTH), F32)]),
        compiler_params=pltpu.CompilerParams(
            dimension_semantics=("arbitrary", "arbitrary"), vmem_limit_bytes=VMEM_LIMIT),
        name="token_mixer",
    )(x, mod, *in_arrays, *later_weights)
    return out[0], out[1:]


def _ffn_kernel(x_ref, mod_ref, gpre_ref, gpost_ref, wup_hbm, cw_ref, cb_ref, wd_hbm,
                o_ref, scr_ref, tail_ref, wup_ref, wd_ref, sem):
    t_idx = pl.program_id(1)
    t, d = x_ref.shape
    d_ff = 2 * wd_ref.shape[0]
    fc = FFN_FC
    lg = fc // LANES
    n_chunks = d_ff // fc
    halves = ((0, 1.0), (d_ff, 0.5))
    chunk_cols = lambda base, j: slice(base + j * fc, base + (j + 1) * fc)
    down_rows = lambda j: slice(j * fc // 2, (j + 1) * fc // 2)

    def chunk_copies(j):
        pieces = [(wup_hbm.at[:, chunk_cols(base, j)], wup_ref.at[:, chunk_cols(base, j)])
                  for base, _ in halves]
        pieces.append((wd_hbm.at[down_rows(j), :], wd_ref.at[down_rows(j), :]))
        return [pltpu.make_async_copy(src, dst, sem.at[j, k]) for k, (src, dst) in enumerate(pieces)]

    def start_chunk(j):
        for copy in chunk_copies(j):
            copy.start()

    def await_chunk(j):
        for copy in chunk_copies(j):
            copy.wait()
        if j + 1 < n_chunks:
            start_chunk(j + 1)

    def tile(before_chunk):
        mod = mod_ref[pl.ds(pl.program_id(0), 1), :]
        sh_f = mod[:, 3 * d:4 * d]
        sc_f = mod[:, 4 * d:5 * d]
        gt_f = mod[:, 5 * d:6 * d]
        x = x_ref[...]
        hb = (_rms(x) * (gpre_ref[...] * (1.0 + sc_f)) + sh_f).astype(BF16)

        def up_proj(j):
            before_chunk(j)
            return [_dot(hb, _as_bf16(wup_ref[:, chunk_cols(base, j)])) for base, _ in halves]

        def gated(j, ups):
            conv = []
            for k, (base, scale) in enumerate(halves):
                cols = chunk_cols(base, j)
                tail = tail_ref[:, cols]
                tail_ref[:, cols] = ups[k][t - SUBLANES:]
                slot = (2 * j + k) % FFN_SLOTS
                conv.append(_conv_via_scratch(scr_ref, slot * lg, tail, ups[k],
                                              cw_ref[:, cols] * scale, cb_ref[:, cols] * scale))
            return jnp.concatenate(
                [_gelu2_times(cg, cv) for cg, cv in zip(*conv)], axis=1).astype(BF16)

        acc = None
        ups = up_proj(0)
        for j in range(n_chunks):
            ups_next = up_proj(j + 1) if j + 1 < n_chunks else None
            part = _dot(gated(j, ups), _as_bf16(wd_ref[down_rows(j), :]))
            acc = part if acc is None else acc + part
            ups = ups_next
        o_ref[...] = x + (gt_f * gpost_ref[...]) * _rms(acc)

    @pl.when(t_idx == 0)
    def _():
        tail_ref[...] = jnp.zeros_like(tail_ref)

    first = (pl.program_id(0) == 0) & (t_idx == 0)

    @pl.when(first)
    def _():
        start_chunk(0)
        tile(await_chunk)

    @pl.when(jnp.logical_not(first))
    def _():
        tile(lambda j: None)


def _ffn(x, mod, g_pre, g_post, w_up, conv_w, conv_b, w_down, layer):
    bsz, s, d = x.shape
    t = FFN_T
    resident = lambda a: pl.BlockSpec(a.shape, lambda b, i: (0,) * a.ndim,
                                      pipeline_mode=pl.Buffered(1))
    in_hbm = pl.BlockSpec(memory_space=pl.ANY)
    n_chunks = 2 * w_down.shape[0] // FFN_FC
    return pl.pallas_call(
        _ffn_kernel,
        grid=(bsz, s // t),
        in_specs=[
            pl.BlockSpec((None, t, d), lambda b, i: (b, i, 0)),
            pl.BlockSpec(mod.shape, lambda b, i: (0, 0)),
            resident(g_pre), resident(g_post), in_hbm,
            pl.BlockSpec((None,) + conv_w.shape[1:], lambda b, i: (layer, 0, 0),
                         pipeline_mode=pl.Buffered(1)),
            resident(conv_b), in_hbm,
        ],
        out_specs=pl.BlockSpec((None, t, d), lambda b, i: (b, i, 0)),
        out_shape=jax.ShapeDtypeStruct(x.shape, x.dtype),
        scratch_shapes=[
            pltpu.VMEM((FFN_SLOTS * (FFN_FC // LANES), SUBLANES + t, LANES), F32),
            pltpu.VMEM((SUBLANES, w_up.shape[1]), F32),
            pltpu.VMEM(w_up.shape, U32),
            pltpu.VMEM(w_down.shape, U32),
            pltpu.SemaphoreType.DMA((n_chunks, 3)),
        ],
        compiler_params=pltpu.CompilerParams(
            dimension_semantics=("arbitrary", "arbitrary"), vmem_limit_bytes=VMEM_LIMIT),
        name="conv_ffn",
    )(x, mod, g_pre, g_post, w_up, conv_w, conv_b, w_down)


def _gate_weights(w_rgate, w_igate):
    per_tile = GATE_TILE // LRU_HEAD_DIM
    n_tiles = LRU_WIDTH // GATE_TILE

    def head_rows(w, head, slot):
        return jnp.pad(w[0, head], ((0, 0), (slot * LRU_HEAD_DIM, (per_tile - 1 - slot) * LRU_HEAD_DIM)))

    return jnp.stack([
        jnp.concatenate([
            jnp.concatenate([head_rows(w, t * per_tile + a, a) for w in (w_rgate, w_igate)], axis=1)
            for a in range(per_tile)], axis=0)
        for t in range(n_tiles)])


def kernel(x, c, w_ada, b_ada, g_mix_pre, g_mix_post, w_in, conv_w, conv_b, w_rgate, b_rgate, w_igate, b_igate, lru_a, v_norm_g, v_norm_b, w_spatial, b_spatial, g_lru_out, g_gmlp_out, w_out, g_ffn_pre, g_ffn_post, w_up, ffn_conv_w, ffn_conv_b, w_down):
    depth = w_ada.shape[0]
    bsz, s, d = x.shape
    for l in range(depth):
        mod = _modulation(c, w_ada, b_ada[l:l + 1], l)
        row = lambda a: a[l:l + 1]
        w_gate = _gate_weights(w_rgate[l:l + 1], w_igate[l:l + 1])
        b_gate = jnp.stack([b_rgate[l], b_igate[l]]).reshape(2, -1)
        x, (w_up_p, w_down_p) = _mixer(
                   x, mod, row(g_mix_pre), row(g_mix_post), w_in,
                   conv_w, row(conv_b), w_gate, b_gate, row(lru_a),
                   row(v_norm_g), row(v_norm_b), w_spatial, b_spatial,
                   row(g_lru_out), row(g_gmlp_out), w_out,
                   later_weights=[w_up, w_down], layer=l)
        x = _ffn(x, mod, row(g_ffn_pre), row(g_ffn_post), w_up_p,
                 ffn_conv_w, row(ffn_conv_b), w_down_p, layer=l)
    return x
```

```python
import functools

import jax
import jax.numpy as jnp
from jax import lax
from jax.experimental import pallas as pl
from jax.experimental.pallas import tpu as pltpu

CHUNK = 64
LRU_WIDTH = 512
LRU_HEADS = 8
LRU_HEAD_DIM = LRU_WIDTH // LRU_HEADS
LRU_CONV_WIDTH = 4
LRU_C = 8.0
GMLP_WIDTH = 512
GMLP_GROUPS = 4
GMLP_GROUP_DIM = GMLP_WIDTH // GMLP_GROUPS
GMLP_BLOCK = 128
FFN_CONV_WIDTH = 3
N_MOD = 6
EPS = 1e-6

SUBLANES = 8
LANES = 128
GATE_TILE = 256

MOD_ROWS = 128
MOD_STREAMS = 2
MIX_T = 256
MIX_NB = 4
FFN_T = 1024
FFN_FC = 512
FFN_SLOTS = 4
VMEM_LIMIT = 60 * 1024 * 1024

F32 = jnp.float32
BF16 = jnp.bfloat16
U32 = jnp.uint32


def _dot(a, b):
    return jnp.dot(a, b, preferred_element_type=F32)


def _pack_rows(w):
    return pltpu.bitcast(w.astype(BF16), U32)


def _as_bf16(packed):
    return pltpu.bitcast(packed, BF16)


def _rms(x):
    return x * lax.rsqrt(jnp.mean(x * x, axis=-1, keepdims=True) + EPS)


def _gelu(x):
    c0 = 0.7978845608028654
    hx = 0.5 * x
    return hx + hx * jnp.tanh(x * (c0 + (c0 * 0.044715) * (x * x)))


def _gelu2_times(x, v):
    c0 = 0.7978845608028654
    z = x * (c0 + (c0 * 0.044715) * (x * x))
    return (x * v) * (1.0 + jnp.tanh(z))


def _rows(ref, lead, start, size):
    return ref[pl.ds(lead, 1, stride=2), pl.ds(start, size), :][0]


def _put_rows(ref, lead, start, val):
    ref[pl.ds(lead, 1, stride=2), pl.ds(start, val.shape[0]), :] = val[None]


def _linear_scan(a, b, state_ref, a_scr, b_scr, h_scr):
    t = a[0].shape[0]
    seg = t // SUBLANES
    pitch = seg + 1 - seg % 2
    n = len(a)
    for c in range(n):
        for s in range(SUBLANES):
            _put_rows(a_scr, c, s * pitch, a[c][s * seg:(s + 1) * seg])
            _put_rows(b_scr, c, s * pitch, b[c][s * seg:(s + 1) * seg])
    step = lambda ref, c, j: ref[c, pl.ds(j, SUBLANES, stride=pitch), :]
    row = lax.broadcasted_iota(jnp.int32, (SUBLANES, LANES), 0)
    shifted = lambda v, sh, fill: jnp.where(row >= sh, pltpu.roll(v, sh, axis=0), fill)

    prod = [None] * n
    end = [None] * n
    for j in range(seg):
        for c in range(n):
            aj, bj = step(a_scr, c, j), step(b_scr, c, j)
            prod[c] = aj if j == 0 else aj * prod[c]
            end[c] = bj if j == 0 else aj * end[c] + bj

    h = []
    for c in range(n):
        cols = slice(c * LANES, (c + 1) * LANES)
        h0 = state_ref[SUBLANES - 1:SUBLANES, cols]
        p = prod[c]
        e = end[c] + jnp.where(row == 0, p * h0, 0.0)
        for sh in (1, 2, 4):
            e = e + p * shifted(e, sh, 0.0)
            if sh < 4:
                p = p * shifted(p, sh, 1.0)
        state_ref[:, cols] = e
        h.append(shifted(e, 1, h0))
    for j in range(seg):
        for c in range(n):
            h[c] = step(a_scr, c, j) * h[c] + step(b_scr, c, j)
            h_scr[c, pl.ds(j, SUBLANES, stride=pitch), :] = h[c]
    return [jnp.concatenate([_rows(h_scr, c, s * pitch, seg) for s in range(SUBLANES)], axis=0)
            for c in range(n)]


def _conv_via_scratch(scr_ref, lead0, tail, x, w, b):
    k_width = w.shape[0]
    t, c = x.shape
    outs = []
    for g in range(c // LANES):
        cols = slice(g * LANES, (g + 1) * LANES)
        scr_ref[lead0 + g, 0:SUBLANES, :] = tail[:, cols]
        scr_ref[lead0 + g, SUBLANES:SUBLANES + t, :] = x[:, cols]
    for g in range(c // LANES):
        cols = slice(g * LANES, (g + 1) * LANES)
        out = x[:, cols] * w[k_width - 1:k_width, cols] + b[:, cols]
        for k in range(k_width - 1):
            shifted = _rows(scr_ref, lead0 + g, SUBLANES - (k_width - 1 - k), t)
            out = out + shifted * w[k:k + 1, cols]
        outs.append(out)
    return outs


def _mod_kernel(c_ref, *rest):
    w_refs, b_ref, o_ref = rest[:-2], rest[-2], rest[-1]
    c = c_ref[...]
    rows = c.shape[0]
    c_act = c * jax.nn.sigmoid(c)
    pad = (-rows) % SUBLANES
    if pad:
        c_act = jnp.concatenate([c_act, jnp.zeros((pad, c.shape[1]), F32)], axis=0)
    c_act = c_act.astype(BF16)
    bk = w_refs[0].shape[0]
    parts = [_dot(c_act[:, j * bk:(j + 1) * bk], w_ref[...].astype(BF16))[:rows]
             for j, w_ref in enumerate(w_refs)]

    @pl.when(pl.program_id(0) == 0)
    def _():
        o_ref[...] = functools.reduce(jnp.add, parts, b_ref[...])

    @pl.when(pl.program_id(0) > 0)
    def _():
        o_ref[...] = functools.reduce(jnp.add, parts, o_ref[...])


def _modulation(c, w_ada, b_ada, layer):
    rows, d = c.shape
    n = w_ada.shape[2]
    bk = MOD_ROWS
    ns = MOD_STREAMS
    assert d % (bk * ns) == 0
    row_block = lambda j: pl.BlockSpec((None, bk, n), lambda k: (layer, k * ns + j, 0))
    return pl.pallas_call(
        _mod_kernel,
        grid=(d // (bk * ns),),
        in_specs=[pl.BlockSpec((rows, bk * ns), lambda k: (0, k))]
                 + [row_block(j) for j in range(ns)]
                 + [pl.BlockSpec((1, n), lambda k: (0, 0))],
        out_specs=pl.BlockSpec((rows, n), lambda k: (0, 0)),
        out_shape=jax.ShapeDtypeStruct((rows, n), F32),
        compiler_params=pltpu.CompilerParams(
            dimension_semantics=("arbitrary",), vmem_limit_bytes=VMEM_LIMIT),
        name="adaln_mod",
    )(c, *[w_ada] * ns, b_ada)


def _mixer_kernel(x_ref, mod_ref, gpre_ref, gpost_ref, win_ref, cw_ref, cb_ref,
                  wgate_ref, bgate_ref, lrua_ref, vng_ref, vnb_ref, wsp_ref, bsp_ref,
                  glru_ref, ggmlp_ref, wout_ref, *rest, n_later):
    later_in, o_ref, later_out = rest[:n_later], rest[n_later], rest[n_later + 1:2 * n_later + 1]
    scratch = rest[2 * n_later + 1:]
    for w_ref, p_ref in zip(later_in, later_out):
        p_ref[...] = _pack_rows(w_ref[...])
    nb_tiles = x_ref.shape[0]
    (tail_ref, state_ref, win_packed_ref, wgate_packed_ref, wout_packed_ref,
     bsp_full_ref) = scratch[4 * nb_tiles:]

    @pl.when((pl.program_id(0) == 0) & (pl.program_id(1) == 0))
    def _():
        win_packed_ref[...] = _pack_rows(win_ref[...])
        for j in range(wgate_ref.shape[0]):
            wgate_packed_ref[j] = _pack_rows(wgate_ref[j])
        wout_packed_ref[...] = _pack_rows(wout_ref[...])
        diag = (lax.broadcasted_iota(jnp.int32, (GMLP_BLOCK, GMLP_BLOCK), 0)
                == lax.broadcasted_iota(jnp.int32, (GMLP_BLOCK, GMLP_BLOCK), 1))
        for g in range(GMLP_GROUPS):
            col = jnp.sum(jnp.where(diag, bsp_ref[g:g + 1, :], -0.0), axis=1, keepdims=True)
            bsp_full_ref[:, g * GMLP_GROUP_DIM:(g + 1) * GMLP_GROUP_DIM] = jnp.broadcast_to(
                col, (GMLP_BLOCK, GMLP_GROUP_DIM))

    @pl.when(pl.program_id(1) == 0)
    def _():
        tail_ref[...] = jnp.zeros_like(tail_ref)
        state_ref[...] = jnp.zeros_like(state_ref)

    mod_row = lambda k: mod_ref[pl.ds(pl.program_id(0) * nb_tiles + k, 1), :]
    project = lambda k: _mix_in(x_ref.at[k], mod_row(k), gpre_ref, win_packed_ref)
    z = project(0)
    for k in range(nb_tiles):
        conv_scr, a_scr, b_scr, h_scr = scratch[k:4 * nb_tiles:nb_tiles]
        xc, gates = _mix_gates(z[0], cw_ref, cb_ref, wgate_packed_ref, conv_scr, tail_ref.at[k])
        sp_groups = _mix_positions(z[3], vng_ref, vnb_ref, wsp_ref)
        z_next = project(k + 1) if k + 1 < nb_tiles else None
        y = _mix_body(z, xc, gates, sp_groups, bgate_ref, lrua_ref, bsp_full_ref,
                      glru_ref, ggmlp_ref, a_scr, b_scr, h_scr, state_ref.at[k])
        _mix_out(y, x_ref.at[k], mod_row(k), gpost_ref, wout_packed_ref, o_ref.at[k])
        z = z_next


def _mix_in(x_ref, mod, gpre_ref, win_ref):
    d = x_ref.shape[1]
    sh_m = mod[:, 0:d]
    sc_m = mod[:, d:2 * d]
    hb = (_rms(x_ref[...]) * (gpre_ref[...] * (1.0 + sc_m)) + sh_m).astype(BF16)
    widths = (LRU_WIDTH, LRU_WIDTH, GMLP_WIDTH, GMLP_WIDTH)
    starts = [sum(widths[:q]) for q in range(len(widths))]
    return [_dot(hb, _as_bf16(win_ref[:, c0:c0 + w])) for c0, w in zip(starts, widths)]


def _mix_gates(lru_x, cw_ref, cb_ref, wgate_ref, conv_scr, tail_ref):
    t = lru_x.shape[0]
    tail = tail_ref[...]
    tail_ref[...] = lru_x[t - SUBLANES:]
    xc = jnp.concatenate(
        _conv_via_scratch(conv_scr, 0, tail, lru_x, cw_ref[...], cb_ref[...]), axis=1)
    xcb = xc.astype(BF16)
    gates = [_dot(xcb[:, j * GATE_TILE:(j + 1) * GATE_TILE], _as_bf16(wgate_ref[j]))
             for j in range(LRU_WIDTH // GATE_TILE)]
    return xc, gates


def _mix_out(y, x_ref, mod, gpost_ref, wout_ref, o_ref):
    d = x_ref.shape[1]
    gt_m = mod[:, 2 * d:3 * d]
    y = _dot(y, _as_bf16(wout_ref[...]))
    o_ref[...] = x_ref[...] + (gt_m * gpost_ref[...]) * _rms(y)


def _mix_positions(g_v, vng_ref, vnb_ref, wsp_ref):
    t = g_v.shape[0]
    gv = _gelu(g_v)
    mu = jnp.mean(gv, axis=-1, keepdims=True)
    cen = gv - mu
    var = jnp.mean(cen * cen, axis=-1, keepdims=True)
    v = cen * lax.rsqrt(var + EPS) * vng_ref[...] + vnb_ref[...]
    vb = v.astype(BF16)
    nb = t // GMLP_BLOCK
    pi = lax.broadcasted_iota(jnp.int32, (GMLP_BLOCK, GMLP_BLOCK), 0) // CHUNK
    pj = lax.broadcasted_iota(jnp.int32, (GMLP_BLOCK, GMLP_BLOCK), 1) // CHUNK
    mask = pj <= pi
    sp_groups = []
    for g in range(GMLP_GROUPS):
        ws = jnp.where(mask, 0.5 * wsp_ref[g], 0.0).astype(BF16)
        cols = slice(g * GMLP_GROUP_DIM, (g + 1) * GMLP_GROUP_DIM)
        rhs = jnp.concatenate(
            [vb[n * GMLP_BLOCK:(n + 1) * GMLP_BLOCK, cols] for n in range(nb)], axis=1)
        sp_groups.append(_dot(ws, rhs))
    return sp_groups


def _mix_body(z, xc, gates, sp_groups, bgate_ref, lrua_ref, bsp_ref,
              glru_ref, ggmlp_ref, a_scr, b_scr, h_scr, state_ref):
    _, lru_gate, g_u, _ = z
    t = lru_gate.shape[0]
    slabs = lambda v: [v[:, c * LANES:(c + 1) * LANES] for c in range(v.shape[1] // LANES)]

    r_pre = jnp.concatenate([g[:, :GATE_TILE] for g in gates], axis=1)
    i_pre = jnp.concatenate([g[:, GATE_TILE:] for g in gates], axis=1)
    r_gate = jax.nn.sigmoid(r_pre + bgate_ref[0:1, :])
    i_gate = jax.nn.sigmoid(i_pre + bgate_ref[1:2, :])
    neg_a = -lrua_ref[...]
    softplus = jnp.maximum(neg_a, 0.0) + jnp.log1p(jnp.exp(-jnp.abs(neg_a)))
    log_a = r_gate * ((-LRU_C) * softplus)
    a = jnp.exp(log_a)
    th = jnp.tanh(log_a)
    q = (-2.0 * th) / (1.0 - th)
    mult = jnp.where(q > 0.0, q * lax.rsqrt(q), 0.0)
    bx = (mult * xc) * i_gate
    hs = _linear_scan(slabs(a), slabs(bx), state_ref, a_scr, b_scr, h_scr)
    y_lru = 0.5 * _gelu2_times(lru_gate, jnp.concatenate(hs, axis=1))
    yl = _rms(y_lru) * glru_ref[...]

    nb = t // GMLP_BLOCK
    bsp = 0.5 * bsp_ref[...]
    sp_rows = []
    for n in range(nb):
        blk = jnp.concatenate(
            [sg[:, n * GMLP_GROUP_DIM:(n + 1) * GMLP_GROUP_DIM] for sg in sp_groups], axis=1)
        sp_rows.append(blk + bsp)
    sp_half = jnp.concatenate(sp_rows, axis=0)
    y_gmlp = _gelu2_times(g_u, sp_half)
    yg = _rms(y_gmlp) * ggmlp_ref[...]

    return jnp.concatenate([yl, yg], axis=1).astype(BF16)


def _mixer(x, mod, g_pre, g_post, w_in, conv_w, conv_b, w_gate, b_gate, lru_a,
           v_norm_g, v_norm_b, w_spatial, b_spatial, g_lru_out, g_gmlp_out, w_out, later_weights,
           layer):
    bsz, s, d = x.shape
    t = MIX_T
    nb = MIX_NB
    assert bsz % nb == 0 and s % t == 0
    n_lg = LRU_WIDTH // LANES
    scan_rows = SUBLANES * (t // SUBLANES + 1)
    full = lambda a: pl.BlockSpec(a.shape, lambda b, i: (0,) * a.ndim)
    of_layer = lambda a: pl.BlockSpec((None,) + a.shape[1:],
                                      lambda b, i: (layer,) + (0,) * (a.ndim - 1))
    in_arrays = [g_pre, g_post, w_in, conv_w, conv_b, w_gate, b_gate, lru_a,
                 v_norm_g, v_norm_b, w_spatial, b_spatial, g_lru_out, g_gmlp_out, w_out]
    resident = lambda a: pl.BlockSpec((None,) + a.shape[1:], lambda b, i: (layer, 0, 0),
                                      pipeline_mode=pl.Buffered(1))
    in_array_specs = [resident(a) if a is w_in or a is w_out
                      else of_layer(a) if a is conv_w or a is w_spatial or a is b_spatial
                      else full(a)
                      for a in in_arrays]
    packed = lambda shape: pltpu.VMEM(shape[:-2] + (shape[-2] // 2, shape[-1]), U32)
    nt = s // t
    n_steps = (bsz // nb) * nt
    step = lambda b, i: (b * nt + i, 0)
    later_in, later_out, later_shapes = [], [], []
    for w in later_weights:
        _, k, n = w.shape
        assert k % (n_steps * 2 * SUBLANES) == 0
        later_in.append(pl.BlockSpec((None, k // n_steps, n),
                                     lambda b, i: (layer, b * nt + i, 0)))
        later_out.append(pl.BlockSpec((k // n_steps // 2, n), step))
        later_shapes.append(jax.ShapeDtypeStruct((k // 2, n), U32))
    out = pl.pallas_call(
        functools.partial(_mixer_kernel, n_later=len(later_weights)),
        grid=(bsz // nb, nt),
        in_specs=[
            pl.BlockSpec((nb, t, d), lambda b, i: (b, i, 0)),
            pl.BlockSpec(mod.shape, lambda b, i: (0, 0)),
        ] + in_array_specs + later_in,
        out_specs=[pl.BlockSpec((nb, t, d), lambda b, i: (b, i, 0))] + later_out,
        out_shape=[jax.ShapeDtypeStruct(x.shape, x.dtype)] + later_shapes,
        scratch_shapes=(
            [pltpu.VMEM((n_lg, SUBLANES + t, LANES), F32)] * nb
            + [pltpu.VMEM((n_lg, scan_rows, LANES), F32)] * (3 * nb)
            + [pltpu.VMEM((nb, SUBLANES, LRU_WIDTH), F32)] * 2
            + [packed(w_in.shape[1:]), packed(w_gate.shape), packed(w_out.shape[1:])]
            + [pltpu.VMEM((GMLP_BLOCK, GMLP_WIDTH), F32)]),
        compiler_params=pltpu.CompilerParams(
            dimension_semantics=("arbitrary", "arbitrary"), vmem_limit_bytes=VMEM_LIMIT),
        name="token_mixer",
    )(x, mod, *in_arrays, *later_weights)
    return out[0], out[1:]


def _ffn_kernel(x_ref, mod_ref, gpre_ref, gpost_ref, wup_ref, cw_ref, cb_ref, wd_ref,
                o_ref, scr_ref, tail_ref):
    t_idx = pl.program_id(1)
    t, d = x_ref.shape
    d_ff = 2 * wd_ref.shape[0]
    fc = FFN_FC
    lg = fc // LANES

    @pl.when(t_idx == 0)
    def _():
        tail_ref[...] = jnp.zeros_like(tail_ref)

    mod = mod_ref[pl.ds(pl.program_id(0), 1), :]
    sh_f = mod[:, 3 * d:4 * d]
    sc_f = mod[:, 4 * d:5 * d]
    gt_f = mod[:, 5 * d:6 * d]
    x = x_ref[...]
    hb = (_rms(x) * (gpre_ref[...] * (1.0 + sc_f)) + sh_f).astype(BF16)

    n_chunks = d_ff // fc
    halves = ((0, 1.0), (d_ff, 0.5))
    chunk_cols = lambda base, j: slice(base + j * fc, base + (j + 1) * fc)
    up_proj = lambda j: [_dot(hb, _as_bf16(wup_ref[:, chunk_cols(base, j)])) for base, _ in halves]

    def gated(j, ups):
        conv = []
        for k, (base, scale) in enumerate(halves):
            cols = chunk_cols(base, j)
            tail = tail_ref[:, cols]
            tail_ref[:, cols] = ups[k][t - SUBLANES:]
            slot = (2 * j + k) % FFN_SLOTS
            conv.append(_conv_via_scratch(scr_ref, slot * lg, tail, ups[k],
                                          cw_ref[:, cols] * scale, cb_ref[:, cols] * scale))
        return jnp.concatenate(
            [_gelu2_times(cg, cv) for cg, cv in zip(*conv)], axis=1).astype(BF16)

    acc = None
    ups = up_proj(0)
    for j in range(n_chunks):
        ups_next = up_proj(j + 1) if j + 1 < n_chunks else None
        part = _dot(gated(j, ups), _as_bf16(wd_ref[j * fc // 2:(j + 1) * fc // 2, :]))
        acc = part if acc is None else acc + part
        ups = ups_next
    o_ref[...] = x + (gt_f * gpost_ref[...]) * _rms(acc)


def _ffn(x, mod, g_pre, g_post, w_up, conv_w, conv_b, w_down, layer):
    bsz, s, d = x.shape
    t = FFN_T
    resident = lambda a: pl.BlockSpec(a.shape, lambda b, i: (0,) * a.ndim,
                                      pipeline_mode=pl.Buffered(1))
    return pl.pallas_call(
        _ffn_kernel,
        grid=(bsz, s // t),
        in_specs=[
            pl.BlockSpec((None, t, d), lambda b, i: (b, i, 0)),
            pl.BlockSpec(mod.shape, lambda b, i: (0, 0)),
            resident(g_pre), resident(g_post), resident(w_up),
            pl.BlockSpec((None,) + conv_w.shape[1:], lambda b, i: (layer, 0, 0),
                         pipeline_mode=pl.Buffered(1)),
            resident(conv_b), resident(w_down),
        ],
        out_specs=pl.BlockSpec((None, t, d), lambda b, i: (b, i, 0)),
        out_shape=jax.ShapeDtypeStruct(x.shape, x.dtype),
        scratch_shapes=[
            pltpu.VMEM((FFN_SLOTS * (FFN_FC // LANES), SUBLANES + t, LANES), F32),
            pltpu.VMEM((SUBLANES, w_up.shape[1]), F32),
        ],
        compiler_params=pltpu.CompilerParams(
            dimension_semantics=("arbitrary", "arbitrary"), vmem_limit_bytes=VMEM_LIMIT),
        name="conv_ffn",
    )(x, mod, g_pre, g_post, w_up, conv_w, conv_b, w_down)


def _gate_weights(w_rgate, w_igate):
    per_tile = GATE_TILE // LRU_HEAD_DIM
    n_tiles = LRU_WIDTH // GATE_TILE

    def head_rows(w, head, slot):
        return jnp.pad(w[0, head], ((0, 0), (slot * LRU_HEAD_DIM, (per_tile - 1 - slot) * LRU_HEAD_DIM)))

    return jnp.stack([
        jnp.concatenate([
            jnp.concatenate([head_rows(w, t * per_tile + a, a) for w in (w_rgate, w_igate)], axis=1)
            for a in range(per_tile)], axis=0)
        for t in range(n_tiles)])


def kernel(x, c, w_ada, b_ada, g_mix_pre, g_mix_post, w_in, conv_w, conv_b, w_rgate, b_rgate, w_igate, b_igate, lru_a, v_norm_g, v_norm_b, w_spatial, b_spatial, g_lru_out, g_gmlp_out, w_out, g_ffn_pre, g_ffn_post, w_up, ffn_conv_w, ffn_conv_b, w_down):
    depth = w_ada.shape[0]
    bsz, s, d = x.shape
    for l in range(depth):
        mod = _modulation(c, w_ada, b_ada[l:l + 1], l)
        row = lambda a: a[l:l + 1]
        w_gate = _gate_weights(w_rgate[l:l + 1], w_igate[l:l + 1])
        b_gate = jnp.stack([b_rgate[l], b_igate[l]]).reshape(2, -1)
        x, (w_up_p, w_down_p) = _mixer(
                   x, mod, row(g_mix_pre), row(g_mix_post), w_in,
                   conv_w, row(conv_b), w_gate, b_gate, row(lru_a),
                   row(v_norm_g), row(v_norm_b), w_spatial, b_spatial,
                   row(g_lru_out), row(g_gmlp_out), w_out,
                   later_weights=[w_up, w_down], layer=l)
        x = _ffn(x, mod, row(g_ffn_pre), row(g_ffn_post), w_up_p,
                 ffn_conv_w, row(ffn_conv_b), w_down_p, layer=l)
    return x
```

```python
import functools

import jax
import jax.numpy as jnp
from jax import lax
from jax.experimental import pallas as pl
from jax.experimental.pallas import tpu as pltpu

CHUNK = 64
LRU_WIDTH = 512
LRU_HEADS = 8
LRU_HEAD_DIM = LRU_WIDTH // LRU_HEADS
LRU_CONV_WIDTH = 4
LRU_C = 8.0
GMLP_WIDTH = 512
GMLP_GROUPS = 4
GMLP_GROUP_DIM = GMLP_WIDTH // GMLP_GROUPS
GMLP_BLOCK = 128
FFN_CONV_WIDTH = 3
N_MOD = 6
EPS = 1e-6

SUBLANES = 8
LANES = 128
GATE_TILE = 256

MOD_ROWS = 128
MOD_STREAMS = 2
MIX_T = 256
MIX_NB = 4
FFN_T = 512
FFN_FC = 1024
FFN_SLOTS = 4
VMEM_LIMIT = 62 * 1024 * 1024

F32 = jnp.float32
BF16 = jnp.bfloat16
U32 = jnp.uint32


def _dot(a, b):
    return jnp.dot(a, b, preferred_element_type=F32)


def _pack_rows(w):
    return pltpu.bitcast(w.astype(BF16), U32)


def _as_bf16(packed):
    return pltpu.bitcast(packed, BF16)


def _rms(x):
    return x * lax.rsqrt(jnp.mean(x * x, axis=-1, keepdims=True) + EPS)


def _gelu(x):
    c0 = 0.7978845608028654
    hx = 0.5 * x
    return hx + hx * jnp.tanh(x * (c0 + (c0 * 0.044715) * (x * x)))


def _gelu2_times(x, v):
    c0 = 0.7978845608028654
    z = x * (c0 + (c0 * 0.044715) * (x * x))
    return (x * v) * (1.0 + jnp.tanh(z))


def _rows(ref, lead, start, size):
    return ref[pl.ds(lead, 1, stride=2), pl.ds(start, size), :][0]


def _put_rows(ref, lead, start, val):
    ref[pl.ds(lead, 1, stride=2), pl.ds(start, val.shape[0]), :] = val[None]


def _linear_scan(a, b, state_ref, a_scr, b_scr, h_scr):
    t = a[0].shape[0]
    seg = t // SUBLANES
    pitch = seg + 1 - seg % 2
    n = len(a)
    for c in range(n):
        for s in range(SUBLANES):
            _put_rows(a_scr, c, s * pitch, a[c][s * seg:(s + 1) * seg])
            _put_rows(b_scr, c, s * pitch, b[c][s * seg:(s + 1) * seg])
    step = lambda ref, c, j: ref[c, pl.ds(j, SUBLANES, stride=pitch), :]
    row = lax.broadcasted_iota(jnp.int32, (SUBLANES, LANES), 0)
    shifted = lambda v, sh, fill: jnp.where(row >= sh, pltpu.roll(v, sh, axis=0), fill)

    prod = [None] * n
    end = [None] * n
    for j in range(seg):
        for c in range(n):
            aj, bj = step(a_scr, c, j), step(b_scr, c, j)
            prod[c] = aj if j == 0 else aj * prod[c]
            end[c] = bj if j == 0 else aj * end[c] + bj

    h = []
    for c in range(n):
        cols = slice(c * LANES, (c + 1) * LANES)
        h0 = state_ref[SUBLANES - 1:SUBLANES, cols]
        p = prod[c]
        e = end[c] + jnp.where(row == 0, p * h0, 0.0)
        for sh in (1, 2, 4):
            e = e + p * shifted(e, sh, 0.0)
            if sh < 4:
                p = p * shifted(p, sh, 1.0)
        state_ref[:, cols] = e
        h.append(shifted(e, 1, h0))
    for j in range(seg):
        for c in range(n):
            h[c] = step(a_scr, c, j) * h[c] + step(b_scr, c, j)
            h_scr[c, pl.ds(j, SUBLANES, stride=pitch), :] = h[c]
    return [jnp.concatenate([_rows(h_scr, c, s * pitch, seg) for s in range(SUBLANES)], axis=0)
            for c in range(n)]


def _conv_via_scratch(scr_ref, lead0, tail, x, w, b):
    k_width = w.shape[0]
    t, c = x.shape
    outs = []
    for g in range(c // LANES):
        cols = slice(g * LANES, (g + 1) * LANES)
        scr_ref[lead0 + g, 0:SUBLANES, :] = tail[:, cols]
        scr_ref[lead0 + g, SUBLANES:SUBLANES + t, :] = x[:, cols]
    for g in range(c // LANES):
        cols = slice(g * LANES, (g + 1) * LANES)
        out = x[:, cols] * w[k_width - 1:k_width, cols] + b[:, cols]
        for k in range(k_width - 1):
            shifted = _rows(scr_ref, lead0 + g, SUBLANES - (k_width - 1 - k), t)
            out = out + shifted * w[k:k + 1, cols]
        outs.append(out)
    return outs


def _mod_parts(c_ref, w_refs):
    c = c_ref[...]
    rows = c.shape[0]
    c_act = c * jax.nn.sigmoid(c)
    pad = (-rows) % SUBLANES
    if pad:
        c_act = jnp.concatenate([c_act, jnp.zeros((pad, c.shape[1]), F32)], axis=0)
    c_act = c_act.astype(BF16)
    bk = w_refs[0].shape[0]
    return [_dot(c_act[:, j * bk:(j + 1) * bk], w_ref[...].astype(BF16))[:rows]
            for j, w_ref in enumerate(w_refs)]


def _mod_kernel(c_ref, *rest):
    w_refs, b_ref, o_ref = rest[:-2], rest[-2], rest[-1]
    parts = _mod_parts(c_ref, w_refs)

    @pl.when(pl.program_id(0) == 0)
    def _():
        o_ref[...] = functools.reduce(jnp.add, parts, b_ref[...])

    @pl.when(pl.program_id(0) > 0)
    def _():
        o_ref[...] = functools.reduce(jnp.add, parts, o_ref[...])


def _modulation(c, w_ada, b_ada, layer):
    rows, d = c.shape
    n = w_ada.shape[2] // 2
    bk = MOD_ROWS
    ns = MOD_STREAMS
    assert d % (bk * ns) == 0
    row_block = lambda j: pl.BlockSpec((None, bk, n), lambda k: (layer, k * ns + j, 0))
    return pl.pallas_call(
        _mod_kernel,
        grid=(d // (bk * ns),),
        in_specs=[pl.BlockSpec((rows, bk * ns), lambda k: (0, k))]
                 + [row_block(j) for j in range(ns)]
                 + [pl.BlockSpec((1, n), lambda k: (0, 0))],
        out_specs=pl.BlockSpec((rows, n), lambda k: (0, 0)),
        out_shape=jax.ShapeDtypeStruct((rows, n), F32),
        compiler_params=pltpu.CompilerParams(
            dimension_semantics=("arbitrary",), vmem_limit_bytes=VMEM_LIMIT),
        name="adaln_mod",
    )(c, *[w_ada] * ns, b_ada)


def _mixer_kernel(x_ref, mod_ref, gpre_ref, gpost_ref, win_ref, cw_ref, cb_ref,
                  wgate_ref, bgate_ref, lrua_ref, vng_ref, vnb_ref, wsp_ref, bsp_ref,
                  glru_ref, ggmlp_ref, wout_ref, c_ref, wada_ref, bada_ref, *rest,
                  n_later, mod_hold):
    later_in, o_ref, later_out = rest[:n_later], rest[n_later], rest[n_later + 1:2 * n_later + 1]
    mod_later_ref = rest[2 * n_later + 1]
    scratch = rest[2 * n_later + 2:]
    for w_ref, p_ref in zip(later_in, later_out):
        p_ref[...] = _pack_rows(w_ref[...])

    step = pl.program_id(0) * pl.num_programs(1) + pl.program_id(1)

    @pl.when(step == 0)
    def _():
        mod_later_ref[...] = bada_ref[...] + _mod_parts(c_ref, [wada_ref])[0]

    @pl.when((step > 0) & (step % mod_hold == 0))
    def _():
        mod_later_ref[...] += _mod_parts(c_ref, [wada_ref])[0]
    nb_tiles = x_ref.shape[0]
    (tail_ref, state_ref, win_packed_ref, wgate_packed_ref, wout_packed_ref,
     bsp_full_ref) = scratch[4 * nb_tiles:]

    @pl.when((pl.program_id(0) == 0) & (pl.program_id(1) == 0))
    def _():
        win_packed_ref[...] = _pack_rows(win_ref[...])
        for j in range(wgate_ref.shape[0]):
            wgate_packed_ref[j] = _pack_rows(wgate_ref[j])
        wout_packed_ref[...] = _pack_rows(wout_ref[...])
        diag = (lax.broadcasted_iota(jnp.int32, (GMLP_BLOCK, GMLP_BLOCK), 0)
                == lax.broadcasted_iota(jnp.int32, (GMLP_BLOCK, GMLP_BLOCK), 1))
        for g in range(GMLP_GROUPS):
            col = jnp.sum(jnp.where(diag, bsp_ref[g:g + 1, :], -0.0), axis=1, keepdims=True)
            bsp_full_ref[:, g * GMLP_GROUP_DIM:(g + 1) * GMLP_GROUP_DIM] = jnp.broadcast_to(
                col, (GMLP_BLOCK, GMLP_GROUP_DIM))

    @pl.when(pl.program_id(1) == 0)
    def _():
        tail_ref[...] = jnp.zeros_like(tail_ref)
        state_ref[...] = jnp.zeros_like(state_ref)

    mod_row = lambda k: mod_ref[pl.ds(pl.program_id(0) * nb_tiles + k, 1), :]
    project = lambda k: _mix_in(x_ref.at[k], mod_row(k), gpre_ref, win_packed_ref)
    z = project(0)
    for k in range(nb_tiles):
        conv_scr, a_scr, b_scr, h_scr = scratch[k:4 * nb_tiles:nb_tiles]
        xc, gates = _mix_gates(z[0], cw_ref, cb_ref, wgate_packed_ref, conv_scr, tail_ref.at[k])
        sp_groups = _mix_positions(z[3], vng_ref, vnb_ref, wsp_ref)
        z_next = project(k + 1) if k + 1 < nb_tiles else None
        y = _mix_body(z, xc, gates, sp_groups, bgate_ref, lrua_ref, bsp_full_ref,
                      glru_ref, ggmlp_ref, a_scr, b_scr, h_scr, state_ref.at[k])
        _mix_out(y, x_ref.at[k], mod_row(k), gpost_ref, wout_packed_ref, o_ref.at[k])
        z = z_next


def _mix_in(x_ref, mod, gpre_ref, win_ref):
    d = x_ref.shape[1]
    sh_m = mod[:, 0:d]
    sc_m = mod[:, d:2 * d]
    hb = (_rms(x_ref[...]) * (gpre_ref[...] * (1.0 + sc_m)) + sh_m).astype(BF16)
    widths = (LRU_WIDTH, LRU_WIDTH, GMLP_WIDTH, GMLP_WIDTH)
    starts = [sum(widths[:q]) for q in range(len(widths))]
    return [_dot(hb, _as_bf16(win_ref[:, c0:c0 + w])) for c0, w in zip(starts, widths)]


def _mix_gates(lru_x, cw_ref, cb_ref, wgate_ref, conv_scr, tail_ref):
    t = lru_x.shape[0]
    tail = tail_ref[...]
    tail_ref[...] = lru_x[t - SUBLANES:]
    xc = jnp.concatenate(
        _conv_via_scratch(conv_scr, 0, tail, lru_x, cw_ref[...], cb_ref[...]), axis=1)
    xcb = xc.astype(BF16)
    gates = [_dot(xcb[:, j * GATE_TILE:(j + 1) * GATE_TILE], _as_bf16(wgate_ref[j]))
             for j in range(LRU_WIDTH // GATE_TILE)]
    return xc, gates


def _mix_out(y, x_ref, mod, gpost_ref, wout_ref, o_ref):
    d = x_ref.shape[1]
    gt_m = mod[:, 2 * d:3 * d]
    y = _dot(y, _as_bf16(wout_ref[...]))
    o_ref[...] = x_ref[...] + (gt_m * gpost_ref[...]) * _rms(y)


def _mix_positions(g_v, vng_ref, vnb_ref, wsp_ref):
    t = g_v.shape[0]
    gv = _gelu(g_v)
    mu = jnp.mean(gv, axis=-1, keepdims=True)
    cen = gv - mu
    var = jnp.mean(cen * cen, axis=-1, keepdims=True)
    v = cen * lax.rsqrt(var + EPS) * vng_ref[...] + vnb_ref[...]
    vb = v.astype(BF16)
    nb = t // GMLP_BLOCK
    pi = lax.broadcasted_iota(jnp.int32, (GMLP_BLOCK, GMLP_BLOCK), 0) // CHUNK
    pj = lax.broadcasted_iota(jnp.int32, (GMLP_BLOCK, GMLP_BLOCK), 1) // CHUNK
    mask = pj <= pi
    sp_groups = []
    for g in range(GMLP_GROUPS):
        ws = jnp.where(mask, 0.5 * wsp_ref[g], 0.0).astype(BF16)
        cols = slice(g * GMLP_GROUP_DIM, (g + 1) * GMLP_GROUP_DIM)
        rhs = jnp.concatenate(
            [vb[n * GMLP_BLOCK:(n + 1) * GMLP_BLOCK, cols] for n in range(nb)], axis=1)
        sp_groups.append(_dot(ws, rhs))
    return sp_groups


def _mix_body(z, xc, gates, sp_groups, bgate_ref, lrua_ref, bsp_ref,
              glru_ref, ggmlp_ref, a_scr, b_scr, h_scr, state_ref):
    _, lru_gate, g_u, _ = z
    t = lru_gate.shape[0]
    slabs = lambda v: [v[:, c * LANES:(c + 1) * LANES] for c in range(v.shape[1] // LANES)]

    r_pre = jnp.concatenate([g[:, :GATE_TILE] for g in gates], axis=1)
    i_pre = jnp.concatenate([g[:, GATE_TILE:] for g in gates], axis=1)
    r_gate = jax.nn.sigmoid(r_pre + bgate_ref[0:1, :])
    i_gate = jax.nn.sigmoid(i_pre + bgate_ref[1:2, :])
    neg_a = -lrua_ref[...]
    softplus = jnp.maximum(neg_a, 0.0) + jnp.log1p(jnp.exp(-jnp.abs(neg_a)))
    log_a = r_gate * ((-LRU_C) * softplus)
    a = jnp.exp(log_a)
    th = jnp.tanh(log_a)
    q = (-2.0 * th) / (1.0 - th)
    mult = jnp.where(q > 0.0, q * lax.rsqrt(q), 0.0)
    bx = (mult * xc) * i_gate
    hs = _linear_scan(slabs(a), slabs(bx), state_ref, a_scr, b_scr, h_scr)
    y_lru = 0.5 * _gelu2_times(lru_gate, jnp.concatenate(hs, axis=1))
    yl = _rms(y_lru) * glru_ref[...]

    nb = t // GMLP_BLOCK
    bsp = 0.5 * bsp_ref[...]
    sp_rows = []
    for n in range(nb):
        blk = jnp.concatenate(
            [sg[:, n * GMLP_GROUP_DIM:(n + 1) * GMLP_GROUP_DIM] for sg in sp_groups], axis=1)
        sp_rows.append(blk + bsp)
    sp_half = jnp.concatenate(sp_rows, axis=0)
    y_gmlp = _gelu2_times(g_u, sp_half)
    yg = _rms(y_gmlp) * ggmlp_ref[...]

    return jnp.concatenate([yl, yg], axis=1).astype(BF16)


def _mixer(x, mod, g_pre, g_post, w_in, conv_w, conv_b, w_gate, b_gate, lru_a,
           v_norm_g, v_norm_b, w_spatial, b_spatial, g_lru_out, g_gmlp_out, w_out, later_weights,
           c, w_ada, b_ada, layer):
    bsz, s, d = x.shape
    t = MIX_T
    nb = MIX_NB
    assert bsz % nb == 0 and s % t == 0
    n_lg = LRU_WIDTH // LANES
    scan_rows = SUBLANES * (t // SUBLANES + 1)
    full = lambda a: pl.BlockSpec(a.shape, lambda b, i: (0,) * a.ndim)
    of_layer = lambda a: pl.BlockSpec((None,) + a.shape[1:],
                                      lambda b, i: (layer,) + (0,) * (a.ndim - 1))
    in_arrays = [g_pre, g_post, w_in, conv_w, conv_b, w_gate, b_gate, lru_a,
                 v_norm_g, v_norm_b, w_spatial, b_spatial, g_lru_out, g_gmlp_out, w_out]
    resident = lambda a: pl.BlockSpec((None,) + a.shape[1:], lambda b, i: (layer, 0, 0),
                                      pipeline_mode=pl.Buffered(1))
    in_array_specs = [resident(a) if a is w_in or a is w_out
                      else of_layer(a) if a is conv_w or a is w_spatial or a is b_spatial
                      else pl.BlockSpec(a.shape, lambda b, i: (0, 0, 0),
                                        pipeline_mode=pl.Buffered(1)) if a is w_gate
                      else full(a)
                      for a in in_arrays]
    packed = lambda shape: pltpu.VMEM(shape[:-2] + (shape[-2] // 2, shape[-1]), U32)
    nt = s // t
    n_steps = (bsz // nb) * nt
    step = lambda b, i: (b * nt + i, 0)
    later_in, later_out, later_shapes = [], [], []
    for w in later_weights:
        _, k, n = w.shape
        assert k % (n_steps * 2 * SUBLANES) == 0
        later_in.append(pl.BlockSpec((None, k // n_steps, n),
                                     lambda b, i: (layer, b * nt + i, 0)))
        later_out.append(pl.BlockSpec((k // n_steps // 2, n), step))
        later_shapes.append(jax.ShapeDtypeStruct((k // 2, n), U32))
    n_mod = w_ada.shape[2] // 2
    n_slices = w_ada.shape[1] // MOD_ROWS
    assert n_steps % n_slices == 0
    mod_hold = n_steps // n_slices
    mod_in = [
        pl.BlockSpec((c.shape[0], MOD_ROWS), lambda b, i: (0, (b * nt + i) // mod_hold)),
        pl.BlockSpec((None, MOD_ROWS, n_mod), lambda b, i: (layer, (b * nt + i) // mod_hold, 1)),
        pl.BlockSpec((1, n_mod), lambda b, i: (0, 1)),
    ]
    out = pl.pallas_call(
        functools.partial(_mixer_kernel, n_later=len(later_weights), mod_hold=mod_hold),
        grid=(bsz // nb, nt),
        in_specs=[
            pl.BlockSpec((nb, t, d), lambda b, i: (b, i, 0)),
            pl.BlockSpec(mod.shape, lambda b, i: (0, 0)),
        ] + in_array_specs + mod_in + later_in,
        out_specs=[pl.BlockSpec((nb, t, d), lambda b, i: (b, i, 0))] + later_out
                  + [pl.BlockSpec((c.shape[0], n_mod), lambda b, i: (0, 0))],
        out_shape=[jax.ShapeDtypeStruct(x.shape, x.dtype)] + later_shapes
                  + [jax.ShapeDtypeStruct((c.shape[0], n_mod), F32)],
        scratch_shapes=(
            [pltpu.VMEM((n_lg, SUBLANES + t, LANES), F32)] * nb
            + [pltpu.VMEM((n_lg, scan_rows, LANES), F32)] * (3 * nb)
            + [pltpu.VMEM((nb, SUBLANES, LRU_WIDTH), F32)] * 2
            + [packed(w_in.shape[1:]), packed(w_gate.shape), packed(w_out.shape[1:])]
            + [pltpu.VMEM((GMLP_BLOCK, GMLP_WIDTH), F32)]),
        compiler_params=pltpu.CompilerParams(
            dimension_semantics=("arbitrary", "arbitrary"), vmem_limit_bytes=VMEM_LIMIT),
        name="token_mixer",
    )(x, mod, *in_arrays, c, w_ada, b_ada, *later_weights)
    return out[0], out[1:-1], out[-1]


def _ffn_kernel(x_ref, mod_ref, gpre_ref, gpost_ref, wup_ref, cw_ref, cb_ref, wd_ref,
                o_ref, scr_ref, tail_ref):
    t_idx = pl.program_id(1)
    t, d = x_ref.shape
    d_ff = 2 * wd_ref.shape[0]
    fc = FFN_FC
    lg = fc // LANES

    @pl.when(t_idx == 0)
    def _():
        tail_ref[...] = jnp.zeros_like(tail_ref)

    mod = mod_ref[pl.ds(pl.program_id(0), 1), :]
    sh_f = mod[:, 0:d]
    sc_f = mod[:, d:2 * d]
    gt_f = mod[:, 2 * d:3 * d]
    x = x_ref[...]
    hb = (_rms(x) * (gpre_ref[...] * (1.0 + sc_f)) + sh_f).astype(BF16)

    n_chunks = d_ff // fc
    halves = ((0, 1.0), (d_ff, 0.5))
    chunk_cols = lambda base, j: slice(base + j * fc, base + (j + 1) * fc)
    up_proj = lambda j: [_dot(hb, _as_bf16(wup_ref[:, chunk_cols(base, j)])) for base, _ in halves]

    def gated(j, ups):
        conv = []
        for k, (base, scale) in enumerate(halves):
            cols = chunk_cols(base, j)
            tail = tail_ref[:, cols]
            tail_ref[:, cols] = ups[k][t - SUBLANES:]
            slot = (2 * j + k) % FFN_SLOTS
            conv.append(_conv_via_scratch(scr_ref, slot * lg, tail, ups[k],
                                          cw_ref[:, cols] * scale, cb_ref[:, cols] * scale))
        return jnp.concatenate(
            [_gelu2_times(cg, cv) for cg, cv in zip(*conv)], axis=1).astype(BF16)

    acc = None
    ups = up_proj(0)
    for j in range(n_chunks):
        ups_next = up_proj(j + 1) if j + 1 < n_chunks else None
        part = _dot(gated(j, ups), _as_bf16(wd_ref[j * fc // 2:(j + 1) * fc // 2, :]))
        acc = part if acc is None else acc + part
        ups = ups_next
    o_ref[...] = x + (gt_f * gpost_ref[...]) * _rms(acc)


def _ffn(x, mod, g_pre, g_post, w_up, conv_w, conv_b, w_down, layer):
    bsz, s, d = x.shape
    t = FFN_T
    resident = lambda a: pl.BlockSpec(a.shape, lambda b, i: (0,) * a.ndim,
                                      pipeline_mode=pl.Buffered(1))
    return pl.pallas_call(
        _ffn_kernel,
        grid=(bsz, s // t),
        in_specs=[
            pl.BlockSpec((None, t, d), lambda b, i: (b, i, 0)),
            pl.BlockSpec(mod.shape, lambda b, i: (0, 0)),
            resident(g_pre), resident(g_post), resident(w_up),
            pl.BlockSpec((None,) + conv_w.shape[1:], lambda b, i: (layer, 0, 0),
                         pipeline_mode=pl.Buffered(1)),
            resident(conv_b), resident(w_down),
        ],
        out_specs=pl.BlockSpec((None, t, d), lambda b, i: (b, i, 0)),
        out_shape=jax.ShapeDtypeStruct(x.shape, x.dtype),
        scratch_shapes=[
            pltpu.VMEM((FFN_SLOTS * (FFN_FC // LANES), SUBLANES + t, LANES), F32),
            pltpu.VMEM((SUBLANES, w_up.shape[1]), F32),
        ],
        compiler_params=pltpu.CompilerParams(
            dimension_semantics=("arbitrary", "arbitrary"), vmem_limit_bytes=VMEM_LIMIT),
        name="conv_ffn",
    )(x, mod, g_pre, g_post, w_up, conv_w, conv_b, w_down)


def _gate_weights(w_rgate, w_igate):
    per_tile = GATE_TILE // LRU_HEAD_DIM
    n_tiles = LRU_WIDTH // GATE_TILE

    def head_rows(w, head, slot):
        return jnp.pad(w[0, head], ((0, 0), (slot * LRU_HEAD_DIM, (per_tile - 1 - slot) * LRU_HEAD_DIM)))

    return jnp.stack([
        jnp.concatenate([
            jnp.concatenate([head_rows(w, t * per_tile + a, a) for w in (w_rgate, w_igate)], axis=1)
            for a in range(per_tile)], axis=0)
        for t in range(n_tiles)])


def kernel(x, c, w_ada, b_ada, g_mix_pre, g_mix_post, w_in, conv_w, conv_b, w_rgate, b_rgate, w_igate, b_igate, lru_a, v_norm_g, v_norm_b, w_spatial, b_spatial, g_lru_out, g_gmlp_out, w_out, g_ffn_pre, g_ffn_post, w_up, ffn_conv_w, ffn_conv_b, w_down):
    depth = w_ada.shape[0]
    bsz, s, d = x.shape
    for l in range(depth):
        mod = _modulation(c, w_ada, b_ada[l:l + 1], l)
        row = lambda a: a[l:l + 1]
        w_gate = _gate_weights(w_rgate[l:l + 1], w_igate[l:l + 1])
        b_gate = jnp.stack([b_rgate[l], b_igate[l]]).reshape(2, -1)
        x, (w_up_p, w_down_p), mod_ffn = _mixer(
                   x, mod, row(g_mix_pre), row(g_mix_post), w_in,
                   conv_w, row(conv_b), w_gate, b_gate, row(lru_a),
                   row(v_norm_g), row(v_norm_b), w_spatial, b_spatial,
                   row(g_lru_out), row(g_gmlp_out), w_out,
                   later_weights=[w_up, w_down], c=c, w_ada=w_ada, b_ada=b_ada[l:l + 1], layer=l)
        x = _ffn(x, mod_ffn, row(g_ffn_pre), row(g_ffn_post), w_up_p,
                 ffn_conv_w, row(ffn_conv_b), w_down_p, layer=l)
    return x
```

```python
import functools

import jax
import jax.numpy as jnp
from jax import lax
from jax.experimental import pallas as pl
from jax.experimental.pallas import tpu as pltpu

CHUNK = 64
LRU_WIDTH = 512
LRU_HEADS = 8
LRU_HEAD_DIM = LRU_WIDTH // LRU_HEADS
LRU_CONV_WIDTH = 4
LRU_C = 8.0
GMLP_WIDTH = 512
GMLP_GROUPS = 4
GMLP_GROUP_DIM = GMLP_WIDTH // GMLP_GROUPS
GMLP_BLOCK = 128
FFN_CONV_WIDTH = 3
N_MOD = 6
EPS = 1e-6

SUBLANES = 8
LANES = 128
GATE_TILE = 256

MOD_ROWS = 128
MOD_STREAMS = 2
MIX_T = 256
MIX_NB = 4
FFN_T = 512
FFN_FC = 1024
FFN_SLOTS = 4
VMEM_LIMIT = 62 * 1024 * 1024

F32 = jnp.float32
BF16 = jnp.bfloat16
U32 = jnp.uint32


def _dot(a, b):
    return jnp.dot(a, b, preferred_element_type=F32)


def _pack_rows(w):
    return pltpu.bitcast(w.astype(BF16), U32)


def _as_bf16(packed):
    return pltpu.bitcast(packed, BF16)


def _rms(x):
    return x * lax.rsqrt(jnp.mean(x * x, axis=-1, keepdims=True) + EPS)


def _gelu(x):
    c0 = 0.7978845608028654
    hx = 0.5 * x
    return hx + hx * jnp.tanh(x * (c0 + (c0 * 0.044715) * (x * x)))


def _gelu2_times(x, v):
    c0 = 0.7978845608028654
    z = x * (c0 + (c0 * 0.044715) * (x * x))
    return (x * v) * (1.0 + jnp.tanh(z))


def _rows(ref, lead, start, size):
    return ref[pl.ds(lead, 1, stride=2), pl.ds(start, size), :][0]


def _put_rows(ref, lead, start, val):
    ref[pl.ds(lead, 1, stride=2), pl.ds(start, val.shape[0]), :] = val[None]


def _linear_scan(a, b, state_ref, a_scr, b_scr, h_scr):
    t = a[0].shape[0]
    seg = t // SUBLANES
    pitch = seg + 1 - seg % 2
    n = len(a)
    for c in range(n):
        for s in range(SUBLANES):
            _put_rows(a_scr, c, s * pitch, a[c][s * seg:(s + 1) * seg])
            _put_rows(b_scr, c, s * pitch, b[c][s * seg:(s + 1) * seg])
    step = lambda ref, c, j: ref[c, pl.ds(j, SUBLANES, stride=pitch), :]
    row = lax.broadcasted_iota(jnp.int32, (SUBLANES, LANES), 0)
    shifted = lambda v, sh, fill: jnp.where(row >= sh, pltpu.roll(v, sh, axis=0), fill)

    prod = [None] * n
    end = [None] * n
    for j in range(seg):
        for c in range(n):
            aj, bj = step(a_scr, c, j), step(b_scr, c, j)
            prod[c] = aj if j == 0 else aj * prod[c]
            end[c] = bj if j == 0 else aj * end[c] + bj

    h = []
    for c in range(n):
        cols = slice(c * LANES, (c + 1) * LANES)
        h0 = state_ref[SUBLANES - 1:SUBLANES, cols]
        p = prod[c]
        e = end[c] + jnp.where(row == 0, p * h0, 0.0)
        for sh in (1, 2, 4):
            e = e + p * shifted(e, sh, 0.0)
            if sh < 4:
                p = p * shifted(p, sh, 1.0)
        state_ref[:, cols] = e
        h.append(shifted(e, 1, h0))
    for j in range(seg):
        for c in range(n):
            h[c] = step(a_scr, c, j) * h[c] + step(b_scr, c, j)
            h_scr[c, pl.ds(j, SUBLANES, stride=pitch), :] = h[c]
    return [jnp.concatenate([_rows(h_scr, c, s * pitch, seg) for s in range(SUBLANES)], axis=0)
            for c in range(n)]


def _conv_via_scratch(scr_ref, lead0, tail, x, w, b):
    k_width = w.shape[0]
    t, c = x.shape
    outs = []
    for g in range(c // LANES):
        cols = slice(g * LANES, (g + 1) * LANES)
        scr_ref[lead0 + g, 0:SUBLANES, :] = tail[:, cols]
        scr_ref[lead0 + g, SUBLANES:SUBLANES + t, :] = x[:, cols]
    for g in range(c // LANES):
        cols = slice(g * LANES, (g + 1) * LANES)
        out = x[:, cols] * w[k_width - 1:k_width, cols] + b[:, cols]
        for k in range(k_width - 1):
            shifted = _rows(scr_ref, lead0 + g, SUBLANES - (k_width - 1 - k), t)
            out = out + shifted * w[k:k + 1, cols]
        outs.append(out)
    return outs


def _mod_parts(c_ref, w_refs):
    c = c_ref[...]
    rows = c.shape[0]
    c_act = c * jax.nn.sigmoid(c)
    pad = (-rows) % SUBLANES
    if pad:
        c_act = jnp.concatenate([c_act, jnp.zeros((pad, c.shape[1]), F32)], axis=0)
    c_act = c_act.astype(BF16)
    bk = w_refs[0].shape[0]
    return [_dot(c_act[:, j * bk:(j + 1) * bk], w_ref[...].astype(BF16))[:rows]
            for j, w_ref in enumerate(w_refs)]


def _mod_kernel(c_ref, *rest):
    w_refs, b_ref, o_ref = rest[:-2], rest[-2], rest[-1]
    parts = _mod_parts(c_ref, w_refs)

    @pl.when(pl.program_id(0) == 0)
    def _():
        o_ref[...] = functools.reduce(jnp.add, parts, b_ref[...])

    @pl.when(pl.program_id(0) > 0)
    def _():
        o_ref[...] = functools.reduce(jnp.add, parts, o_ref[...])


def _modulation(c, w_ada, b_ada, layer):
    rows, d = c.shape
    n = w_ada.shape[2] // 2
    bk = MOD_ROWS
    ns = MOD_STREAMS
    assert d % (bk * ns) == 0
    row_block = lambda j: pl.BlockSpec((None, bk, n), lambda k: (layer, k * ns + j, 0))
    return pl.pallas_call(
        _mod_kernel,
        grid=(d // (bk * ns),),
        in_specs=[pl.BlockSpec((rows, bk * ns), lambda k: (0, k))]
                 + [row_block(j) for j in range(ns)]
                 + [pl.BlockSpec((1, n), lambda k: (0, 0))],
        out_specs=pl.BlockSpec((rows, n), lambda k: (0, 0)),
        out_shape=jax.ShapeDtypeStruct((rows, n), F32),
        compiler_params=pltpu.CompilerParams(
            dimension_semantics=("arbitrary",), vmem_limit_bytes=VMEM_LIMIT),
        name="adaln_mod",
    )(c, *[w_ada] * ns, b_ada)


def _mixer_kernel(x_ref, mod_ref, gpre_ref, gpost_ref, win_ref, cw_ref, cb_ref,
                  wgate_ref, bgate_ref, lrua_ref, vng_ref, vnb_ref, wsp_ref, bsp_ref,
                  glru_ref, ggmlp_ref, wout_ref, c_ref, wada_ref, bada_ref, *rest,
                  n_later, mod_hold):
    later_in, o_ref, later_out = rest[:n_later], rest[n_later], rest[n_later + 1:2 * n_later + 1]
    mod_later_ref = rest[2 * n_later + 1]
    scratch = rest[2 * n_later + 2:]
    for w_ref, p_ref in zip(later_in, later_out):
        p_ref[...] = _pack_rows(w_ref[...])

    step = pl.program_id(0) * pl.num_programs(1) + pl.program_id(1)

    @pl.when(step == 0)
    def _():
        mod_later_ref[...] = jnp.broadcast_to(bada_ref[...], mod_later_ref.shape)

    part = _mod_parts(c_ref, [wada_ref])[0]
    group = part.shape[1]
    for p in range(mod_hold):
        mod_later_ref[:, p * group:(p + 1) * group] += jnp.where(step % mod_hold == p, part, -0.0)
    nb_tiles = x_ref.shape[0]
    (tail_ref, state_ref, win_packed_ref, wgate_packed_ref, wout_packed_ref,
     bsp_full_ref) = scratch[4 * nb_tiles:]

    @pl.when((pl.program_id(0) == 0) & (pl.program_id(1) == 0))
    def _():
        win_packed_ref[...] = _pack_rows(win_ref[...])
        for j in range(wgate_ref.shape[0]):
            wgate_packed_ref[j] = _pack_rows(wgate_ref[j])
        wout_packed_ref[...] = _pack_rows(wout_ref[...])
        diag = (lax.broadcasted_iota(jnp.int32, (GMLP_BLOCK, GMLP_BLOCK), 0)
                == lax.broadcasted_iota(jnp.int32, (GMLP_BLOCK, GMLP_BLOCK), 1))
        for g in range(GMLP_GROUPS):
            col = jnp.sum(jnp.where(diag, bsp_ref[g:g + 1, :], -0.0), axis=1, keepdims=True)
            bsp_full_ref[:, g * GMLP_GROUP_DIM:(g + 1) * GMLP_GROUP_DIM] = jnp.broadcast_to(
                col, (GMLP_BLOCK, GMLP_GROUP_DIM))

    @pl.when(pl.program_id(1) == 0)
    def _():
        tail_ref[...] = jnp.zeros_like(tail_ref)
        state_ref[...] = jnp.zeros_like(state_ref)

    mod_row = lambda k: mod_ref[pl.ds(pl.program_id(0) * nb_tiles + k, 1), :]
    project = lambda k: _mix_in(x_ref.at[k], mod_row(k), gpre_ref, win_packed_ref)
    z = project(0)
    for k in range(nb_tiles):
        conv_scr, a_scr, b_scr, h_scr = scratch[k:4 * nb_tiles:nb_tiles]
        xc, gates = _mix_gates(z[0], cw_ref, cb_ref, wgate_packed_ref, conv_scr, tail_ref.at[k])
        sp_groups = _mix_positions(z[3], vng_ref, vnb_ref, wsp_ref)
        z_next = project(k + 1) if k + 1 < nb_tiles else None
        y = _mix_body(z, xc, gates, sp_groups, bgate_ref, lrua_ref, bsp_full_ref,
                      glru_ref, ggmlp_ref, a_scr, b_scr, h_scr, state_ref.at[k])
        _mix_out(y, x_ref.at[k], mod_row(k), gpost_ref, wout_packed_ref, o_ref.at[k])
        z = z_next


def _mix_in(x_ref, mod, gpre_ref, win_ref):
    d = x_ref.shape[1]
    sh_m = mod[:, 0:d]
    sc_m = mod[:, d:2 * d]
    hb = (_rms(x_ref[...]) * (gpre_ref[...] * (1.0 + sc_m)) + sh_m).astype(BF16)
    widths = (LRU_WIDTH, LRU_WIDTH, GMLP_WIDTH, GMLP_WIDTH)
    starts = [sum(widths[:q]) for q in range(len(widths))]
    return [_dot(hb, _as_bf16(win_ref[:, c0:c0 + w])) for c0, w in zip(starts, widths)]


def _mix_gates(lru_x, cw_ref, cb_ref, wgate_ref, conv_scr, tail_ref):
    t = lru_x.shape[0]
    tail = tail_ref[...]
    tail_ref[...] = lru_x[t - SUBLANES:]
    xc = jnp.concatenate(
        _conv_via_scratch(conv_scr, 0, tail, lru_x, cw_ref[...], cb_ref[...]), axis=1)
    xcb = xc.astype(BF16)
    gates = [_dot(xcb[:, j * GATE_TILE:(j + 1) * GATE_TILE], _as_bf16(wgate_ref[j]))
             for j in range(LRU_WIDTH // GATE_TILE)]
    return xc, gates


def _mix_out(y, x_ref, mod, gpost_ref, wout_ref, o_ref):
    d = x_ref.shape[1]
    gt_m = mod[:, 2 * d:3 * d]
    y = _dot(y, _as_bf16(wout_ref[...]))
    o_ref[...] = x_ref[...] + (gt_m * gpost_ref[...]) * _rms(y)


def _mix_positions(g_v, vng_ref, vnb_ref, wsp_ref):
    t = g_v.shape[0]
    gv = _gelu(g_v)
    mu = jnp.mean(gv, axis=-1, keepdims=True)
    cen = gv - mu
    var = jnp.mean(cen * cen, axis=-1, keepdims=True)
    v = cen * lax.rsqrt(var + EPS) * vng_ref[...] + vnb_ref[...]
    vb = v.astype(BF16)
    nb = t // GMLP_BLOCK
    pi = lax.broadcasted_iota(jnp.int32, (GMLP_BLOCK, GMLP_BLOCK), 0) // CHUNK
    pj = lax.broadcasted_iota(jnp.int32, (GMLP_BLOCK, GMLP_BLOCK), 1) // CHUNK
    mask = pj <= pi
    sp_groups = []
    for g in range(GMLP_GROUPS):
        ws = jnp.where(mask, 0.5 * wsp_ref[g], 0.0).astype(BF16)
        cols = slice(g * GMLP_GROUP_DIM, (g + 1) * GMLP_GROUP_DIM)
        rhs = jnp.concatenate(
            [vb[n * GMLP_BLOCK:(n + 1) * GMLP_BLOCK, cols] for n in range(nb)], axis=1)
        sp_groups.append(_dot(ws, rhs))
    return sp_groups


def _mix_body(z, xc, gates, sp_groups, bgate_ref, lrua_ref, bsp_ref,
              glru_ref, ggmlp_ref, a_scr, b_scr, h_scr, state_ref):
    _, lru_gate, g_u, _ = z
    t = lru_gate.shape[0]
    slabs = lambda v: [v[:, c * LANES:(c + 1) * LANES] for c in range(v.shape[1] // LANES)]

    r_pre = jnp.concatenate([g[:, :GATE_TILE] for g in gates], axis=1)
    i_pre = jnp.concatenate([g[:, GATE_TILE:] for g in gates], axis=1)
    r_gate = jax.nn.sigmoid(r_pre + bgate_ref[0:1, :])
    i_gate = jax.nn.sigmoid(i_pre + bgate_ref[1:2, :])
    neg_a = -lrua_ref[...]
    softplus = jnp.maximum(neg_a, 0.0) + jnp.log1p(jnp.exp(-jnp.abs(neg_a)))
    log_a = r_gate * ((-LRU_C) * softplus)
    a = jnp.exp(log_a)
    th = jnp.tanh(log_a)
    q = (-2.0 * th) / (1.0 - th)
    mult = jnp.where(q > 0.0, q * lax.rsqrt(q), 0.0)
    bx = (mult * xc) * i_gate
    hs = _linear_scan(slabs(a), slabs(bx), state_ref, a_scr, b_scr, h_scr)
    y_lru = 0.5 * _gelu2_times(lru_gate, jnp.concatenate(hs, axis=1))
    yl = _rms(y_lru) * glru_ref[...]

    nb = t // GMLP_BLOCK
    bsp = 0.5 * bsp_ref[...]
    sp_rows = []
    for n in range(nb):
        blk = jnp.concatenate(
            [sg[:, n * GMLP_GROUP_DIM:(n + 1) * GMLP_GROUP_DIM] for sg in sp_groups], axis=1)
        sp_rows.append(blk + bsp)
    sp_half = jnp.concatenate(sp_rows, axis=0)
    y_gmlp = _gelu2_times(g_u, sp_half)
    yg = _rms(y_gmlp) * ggmlp_ref[...]

    return jnp.concatenate([yl, yg], axis=1).astype(BF16)


def _mixer(x, mod, g_pre, g_post, w_in, conv_w, conv_b, w_gate, b_gate, lru_a,
           v_norm_g, v_norm_b, w_spatial, b_spatial, g_lru_out, g_gmlp_out, w_out, later_weights,
           c, w_ada, b_ada, layer):
    bsz, s, d = x.shape
    t = MIX_T
    nb = MIX_NB
    assert bsz % nb == 0 and s % t == 0
    n_lg = LRU_WIDTH // LANES
    scan_rows = SUBLANES * (t // SUBLANES + 1)
    full = lambda a: pl.BlockSpec(a.shape, lambda b, i: (0,) * a.ndim)
    of_layer = lambda a: pl.BlockSpec((None,) + a.shape[1:],
                                      lambda b, i: (layer,) + (0,) * (a.ndim - 1))
    in_arrays = [g_pre, g_post, w_in, conv_w, conv_b, w_gate, b_gate, lru_a,
                 v_norm_g, v_norm_b, w_spatial, b_spatial, g_lru_out, g_gmlp_out, w_out]
    resident = lambda a: pl.BlockSpec((None,) + a.shape[1:], lambda b, i: (layer, 0, 0),
                                      pipeline_mode=pl.Buffered(1))
    in_array_specs = [resident(a) if a is w_in or a is w_out
                      else of_layer(a) if a is conv_w or a is w_spatial or a is b_spatial
                      else pl.BlockSpec(a.shape, lambda b, i: (0, 0, 0),
                                        pipeline_mode=pl.Buffered(1)) if a is w_gate
                      else full(a)
                      for a in in_arrays]
    packed = lambda shape: pltpu.VMEM(shape[:-2] + (shape[-2] // 2, shape[-1]), U32)
    nt = s // t
    n_steps = (bsz // nb) * nt
    step = lambda b, i: (b * nt + i, 0)
    later_in, later_out, later_shapes = [], [], []
    for w in later_weights:
        _, k, n = w.shape
        assert k % (n_steps * 2 * SUBLANES) == 0
        later_in.append(pl.BlockSpec((None, k // n_steps, n),
                                     lambda b, i: (layer, b * nt + i, 0)))
        later_out.append(pl.BlockSpec((k // n_steps // 2, n), step))
        later_shapes.append(jax.ShapeDtypeStruct((k // 2, n), U32))
    n_mod = w_ada.shape[2] // 2
    n_slices = w_ada.shape[1] // MOD_ROWS
    assert n_steps % n_slices == 0
    mod_hold = n_steps // n_slices
    assert n_mod % (mod_hold * LANES) == 0
    mod_in = [
        pl.BlockSpec((c.shape[0], MOD_ROWS), lambda b, i: (0, (b * nt + i) // mod_hold)),
        pl.BlockSpec((None, MOD_ROWS, n_mod // mod_hold),
                     lambda b, i: (layer, (b * nt + i) // mod_hold, mod_hold + (b * nt + i) % mod_hold)),
        pl.BlockSpec((1, n_mod), lambda b, i: (0, 1)),
    ]
    out = pl.pallas_call(
        functools.partial(_mixer_kernel, n_later=len(later_weights), mod_hold=mod_hold),
        grid=(bsz // nb, nt),
        in_specs=[
            pl.BlockSpec((nb, t, d), lambda b, i: (b, i, 0)),
            pl.BlockSpec(mod.shape, lambda b, i: (0, 0)),
        ] + in_array_specs + mod_in + later_in,
        out_specs=[pl.BlockSpec((nb, t, d), lambda b, i: (b, i, 0))] + later_out
                  + [pl.BlockSpec((c.shape[0], n_mod), lambda b, i: (0, 0))],
        out_shape=[jax.ShapeDtypeStruct(x.shape, x.dtype)] + later_shapes
                  + [jax.ShapeDtypeStruct((c.shape[0], n_mod), F32)],
        scratch_shapes=(
            [pltpu.VMEM((n_lg, SUBLANES + t, LANES), F32)] * nb
            + [pltpu.VMEM((n_lg, scan_rows, LANES), F32)] * (3 * nb)
            + [pltpu.VMEM((nb, SUBLANES, LRU_WIDTH), F32)] * 2
            + [packed(w_in.shape[1:]), packed(w_gate.shape), packed(w_out.shape[1:])]
            + [pltpu.VMEM((GMLP_BLOCK, GMLP_WIDTH), F32)]),
        compiler_params=pltpu.CompilerParams(
            dimension_semantics=("arbitrary", "arbitrary"), vmem_limit_bytes=VMEM_LIMIT),
        name="token_mixer",
    )(x, mod, *in_arrays, c, w_ada, b_ada, *later_weights)
    return out[0], out[1:-1], out[-1]


def _ffn_kernel(x_ref, mod_ref, gpre_ref, gpost_ref, wup_ref, cw_ref, cb_ref, wd_ref,
                o_ref, scr_ref, tail_ref):
    t_idx = pl.program_id(1)
    t, d = x_ref.shape
    d_ff = 2 * wd_ref.shape[0]
    fc = FFN_FC
    lg = fc // LANES

    @pl.when(t_idx == 0)
    def _():
        tail_ref[...] = jnp.zeros_like(tail_ref)

    mod = mod_ref[pl.ds(pl.program_id(0), 1), :]
    sh_f = mod[:, 0:d]
    sc_f = mod[:, d:2 * d]
    gt_f = mod[:, 2 * d:3 * d]
    x = x_ref[...]
    hb = (_rms(x) * (gpre_ref[...] * (1.0 + sc_f)) + sh_f).astype(BF16)

    n_chunks = d_ff // fc
    halves = ((0, 1.0), (d_ff, 0.5))
    chunk_cols = lambda base, j: slice(base + j * fc, base + (j + 1) * fc)
    up_proj = lambda j: [_dot(hb, _as_bf16(wup_ref[:, chunk_cols(base, j)])) for base, _ in halves]

    def gated(j, ups):
        conv = []
        for k, (base, scale) in enumerate(halves):
            cols = chunk_cols(base, j)
            tail = tail_ref[:, cols]
            tail_ref[:, cols] = ups[k][t - SUBLANES:]
            slot = (2 * j + k) % FFN_SLOTS
            conv.append(_conv_via_scratch(scr_ref, slot * lg, tail, ups[k],
                                          cw_ref[:, cols] * scale, cb_ref[:, cols] * scale))
        return jnp.concatenate(
            [_gelu2_times(cg, cv) for cg, cv in zip(*conv)], axis=1).astype(BF16)

    acc = None
    ups = up_proj(0)
    for j in range(n_chunks):
        ups_next = up_proj(j + 1) if j + 1 < n_chunks else None
        part = _dot(gated(j, ups), _as_bf16(wd_ref[j * fc // 2:(j + 1) * fc // 2, :]))
        acc = part if acc is None else acc + part
        ups = ups_next
    o_ref[...] = x + (gt_f * gpost_ref[...]) * _rms(acc)


def _ffn(x, mod, g_pre, g_post, w_up, conv_w, conv_b, w_down, layer):
    bsz, s, d = x.shape
    t = FFN_T
    resident = lambda a: pl.BlockSpec(a.shape, lambda b, i: (0,) * a.ndim,
                                      pipeline_mode=pl.Buffered(1))
    return pl.pallas_call(
        _ffn_kernel,
        grid=(bsz, s // t),
        in_specs=[
            pl.BlockSpec((None, t, d), lambda b, i: (b, i, 0)),
            pl.BlockSpec(mod.shape, lambda b, i: (0, 0)),
            resident(g_pre), resident(g_post), resident(w_up),
            pl.BlockSpec((None,) + conv_w.shape[1:], lambda b, i: (layer, 0, 0),
                         pipeline_mode=pl.Buffered(1)),
            resident(conv_b), resident(w_down),
        ],
        out_specs=pl.BlockSpec((None, t, d), lambda b, i: (b, i, 0)),
        out_shape=jax.ShapeDtypeStruct(x.shape, x.dtype),
        scratch_shapes=[
            pltpu.VMEM((FFN_SLOTS * (FFN_FC // LANES), SUBLANES + t, LANES), F32),
            pltpu.VMEM((SUBLANES, w_up.shape[1]), F32),
        ],
        compiler_params=pltpu.CompilerParams(
            dimension_semantics=("arbitrary", "arbitrary"), vmem_limit_bytes=VMEM_LIMIT),
        name="conv_ffn",
    )(x, mod, g_pre, g_post, w_up, conv_w, conv_b, w_down)


def _gate_weights(w_rgate, w_igate):
    per_tile = GATE_TILE // LRU_HEAD_DIM
    n_tiles = LRU_WIDTH // GATE_TILE

    def head_rows(w, head, slot):
        return jnp.pad(w[0, head], ((0, 0), (slot * LRU_HEAD_DIM, (per_tile - 1 - slot) * LRU_HEAD_DIM)))

    return jnp.stack([
        jnp.concatenate([
            jnp.concatenate([head_rows(w, t * per_tile + a, a) for w in (w_rgate, w_igate)], axis=1)
            for a in range(per_tile)], axis=0)
        for t in range(n_tiles)])


def kernel(x, c, w_ada, b_ada, g_mix_pre, g_mix_post, w_in, conv_w, conv_b, w_rgate, b_rgate, w_igate, b_igate, lru_a, v_norm_g, v_norm_b, w_spatial, b_spatial, g_lru_out, g_gmlp_out, w_out, g_ffn_pre, g_ffn_post, w_up, ffn_conv_w, ffn_conv_b, w_down):
    depth = w_ada.shape[0]
    bsz, s, d = x.shape
    for l in range(depth):
        mod = _modulation(c, w_ada, b_ada[l:l + 1], l)
        row = lambda a: a[l:l + 1]
        w_gate = _gate_weights(w_rgate[l:l + 1], w_igate[l:l + 1])
        b_gate = jnp.stack([b_rgate[l], b_igate[l]]).reshape(2, -1)
        x, (w_up_p, w_down_p), mod_ffn = _mixer(
                   x, mod, row(g_mix_pre), row(g_mix_post), w_in,
                   conv_w, row(conv_b), w_gate, b_gate, row(lru_a),
                   row(v_norm_g), row(v_norm_b), w_spatial, b_spatial,
                   row(g_lru_out), row(g_gmlp_out), w_out,
                   later_weights=[w_up, w_down], c=c, w_ada=w_ada, b_ada=b_ada[l:l + 1], layer=l)
        x = _ffn(x, mod_ffn, row(g_ffn_pre), row(g_ffn_post), w_up_p,
                 ffn_conv_w, row(ffn_conv_b), w_down_p, layer=l)
    return x
```

```python
import functools

import jax
import jax.numpy as jnp
from jax import lax
from jax.experimental import pallas as pl
from jax.experimental.pallas import tpu as pltpu

CHUNK = 64
LRU_WIDTH = 512
LRU_HEADS = 8
LRU_HEAD_DIM = LRU_WIDTH // LRU_HEADS
LRU_CONV_WIDTH = 4
LRU_C = 8.0
GMLP_WIDTH = 512
GMLP_GROUPS = 4
GMLP_GROUP_DIM = GMLP_WIDTH // GMLP_GROUPS
GMLP_BLOCK = 128
FFN_CONV_WIDTH = 3
N_MOD = 6
EPS = 1e-6

SUBLANES = 8
LANES = 128
GATE_TILE = 256

MOD_ROWS = 128
MOD_STREAMS = 2
MIX_T = 256
MIX_NB = 4
FFN_T = 512
FFN_FC = 1024
FFN_SLOTS = 4
VMEM_LIMIT = 60 * 1024 * 1024

F32 = jnp.float32
BF16 = jnp.bfloat16
U32 = jnp.uint32


def _dot(a, b):
    return jnp.dot(a, b, preferred_element_type=F32)


def _pack_rows(w):
    return pltpu.bitcast(w.astype(BF16), U32)


def _as_bf16(packed):
    return pltpu.bitcast(packed, BF16)


def _rms(x):
    return x * lax.rsqrt(jnp.mean(x * x, axis=-1, keepdims=True) + EPS)


def _gelu(x):
    c0 = 0.7978845608028654
    hx = 0.5 * x
    return hx + hx * jnp.tanh(x * (c0 + (c0 * 0.044715) * (x * x)))


def _gelu2_times(x, v):
    c0 = 0.7978845608028654
    z = x * (c0 + (c0 * 0.044715) * (x * x))
    return (x * v) * (1.0 + jnp.tanh(z))


def _rows(ref, lead, start, size):
    return ref[pl.ds(lead, 1, stride=2), pl.ds(start, size), :][0]


def _put_rows(ref, lead, start, val):
    ref[pl.ds(lead, 1, stride=2), pl.ds(start, val.shape[0]), :] = val[None]


def _linear_scan(a, b, state_ref, a_scr, b_scr, h_scr):
    t = a[0].shape[0]
    seg = t // SUBLANES
    pitch = seg + 1 - seg % 2
    n = len(a)
    for c in range(n):
        for s in range(SUBLANES):
            _put_rows(a_scr, c, s * pitch, a[c][s * seg:(s + 1) * seg])
            _put_rows(b_scr, c, s * pitch, b[c][s * seg:(s + 1) * seg])
    step = lambda ref, c, j: ref[c, pl.ds(j, SUBLANES, stride=pitch), :]
    row = lax.broadcasted_iota(jnp.int32, (SUBLANES, LANES), 0)
    shifted = lambda v, sh, fill: jnp.where(row >= sh, pltpu.roll(v, sh, axis=0), fill)

    prod = [None] * n
    end = [None] * n
    for j in range(seg):
        for c in range(n):
            aj, bj = step(a_scr, c, j), step(b_scr, c, j)
            prod[c] = aj if j == 0 else aj * prod[c]
            end[c] = bj if j == 0 else aj * end[c] + bj

    h = []
    for c in range(n):
        cols = slice(c * LANES, (c + 1) * LANES)
        h0 = state_ref[SUBLANES - 1:SUBLANES, cols]
        p = prod[c]
        e = end[c] + jnp.where(row == 0, p * h0, 0.0)
        for sh in (1, 2, 4):
            e = e + p * shifted(e, sh, 0.0)
            if sh < 4:
                p = p * shifted(p, sh, 1.0)
        state_ref[:, cols] = e
        h.append(shifted(e, 1, h0))
    for j in range(seg):
        for c in range(n):
            h[c] = step(a_scr, c, j) * h[c] + step(b_scr, c, j)
            h_scr[c, pl.ds(j, SUBLANES, stride=pitch), :] = h[c]
    return [jnp.concatenate([_rows(h_scr, c, s * pitch, seg) for s in range(SUBLANES)], axis=0)
            for c in range(n)]


def _conv_via_scratch(scr_ref, lead0, tail, x, w, b):
    k_width = w.shape[0]
    t, c = x.shape
    outs = []
    for g in range(c // LANES):
        cols = slice(g * LANES, (g + 1) * LANES)
        scr_ref[lead0 + g, 0:SUBLANES, :] = tail[:, cols]
        scr_ref[lead0 + g, SUBLANES:SUBLANES + t, :] = x[:, cols]
    for g in range(c // LANES):
        cols = slice(g * LANES, (g + 1) * LANES)
        out = x[:, cols] * w[k_width - 1:k_width, cols] + b[:, cols]
        for k in range(k_width - 1):
            shifted = _rows(scr_ref, lead0 + g, SUBLANES - (k_width - 1 - k), t)
            out = out + shifted * w[k:k + 1, cols]
        outs.append(out)
    return outs


def _mod_kernel(c_ref, *rest):
    w_refs, b_ref, o_ref = rest[:-2], rest[-2], rest[-1]
    c = c_ref[...]
    rows = c.shape[0]
    c_act = c * jax.nn.sigmoid(c)
    pad = (-rows) % SUBLANES
    if pad:
        c_act = jnp.concatenate([c_act, jnp.zeros((pad, c.shape[1]), F32)], axis=0)
    c_act = c_act.astype(BF16)
    bk = w_refs[0].shape[0]
    parts = [_dot(c_act[:, j * bk:(j + 1) * bk], w_ref[...].astype(BF16))[:rows]
             for j, w_ref in enumerate(w_refs)]

    @pl.when(pl.program_id(0) == 0)
    def _():
        o_ref[...] = functools.reduce(jnp.add, parts, b_ref[...])

    @pl.when(pl.program_id(0) > 0)
    def _():
        o_ref[...] = functools.reduce(jnp.add, parts, o_ref[...])


def _modulation(c, w_ada, b_ada, layer):
    rows, d = c.shape
    n = w_ada.shape[2]
    bk = MOD_ROWS
    ns = MOD_STREAMS
    assert d % (bk * ns) == 0
    row_block = lambda j: pl.BlockSpec((None, bk, n), lambda k: (layer, k * ns + j, 0))
    return pl.pallas_call(
        _mod_kernel,
        grid=(d // (bk * ns),),
        in_specs=[pl.BlockSpec((rows, bk * ns), lambda k: (0, k))]
                 + [row_block(j) for j in range(ns)]
                 + [pl.BlockSpec((1, n), lambda k: (0, 0))],
        out_specs=pl.BlockSpec((rows, n), lambda k: (0, 0)),
        out_shape=jax.ShapeDtypeStruct((rows, n), F32),
        compiler_params=pltpu.CompilerParams(
            dimension_semantics=("arbitrary",), vmem_limit_bytes=VMEM_LIMIT),
        name="adaln_mod",
    )(c, *[w_ada] * ns, b_ada)


def _mixer_kernel(x_ref, mod_ref, gpre_ref, gpost_ref, win_ref, cw_ref, cb_ref,
                  wgate_ref, bgate_ref, lrua_ref, vng_ref, vnb_ref, wsp_ref, bsp_ref,
                  glru_ref, ggmlp_ref, wout_ref, *rest, n_later):
    later_in, o_ref, later_out = rest[:n_later], rest[n_later], rest[n_later + 1:2 * n_later + 1]
    scratch = rest[2 * n_later + 1:]
    for w_ref, p_ref in zip(later_in, later_out):
        p_ref[...] = _pack_rows(w_ref[...])
    nb_tiles = x_ref.shape[0]
    (tail_ref, state_ref, win_packed_ref, wgate_packed_ref, wout_packed_ref,
     bsp_full_ref) = scratch[4 * nb_tiles:]

    @pl.when((pl.program_id(0) == 0) & (pl.program_id(1) == 0))
    def _():
        win_packed_ref[...] = _pack_rows(win_ref[...])
        for j in range(wgate_ref.shape[0]):
            wgate_packed_ref[j] = _pack_rows(wgate_ref[j])
        wout_packed_ref[...] = _pack_rows(wout_ref[...])
        diag = (lax.broadcasted_iota(jnp.int32, (GMLP_BLOCK, GMLP_BLOCK), 0)
                == lax.broadcasted_iota(jnp.int32, (GMLP_BLOCK, GMLP_BLOCK), 1))
        for g in range(GMLP_GROUPS):
            col = jnp.sum(jnp.where(diag, bsp_ref[g:g + 1, :], -0.0), axis=1, keepdims=True)
            bsp_full_ref[:, g * GMLP_GROUP_DIM:(g + 1) * GMLP_GROUP_DIM] = jnp.broadcast_to(
                col, (GMLP_BLOCK, GMLP_GROUP_DIM))

    @pl.when(pl.program_id(1) == 0)
    def _():
        tail_ref[...] = jnp.zeros_like(tail_ref)
        state_ref[...] = jnp.zeros_like(state_ref)

    mod_row = lambda k: mod_ref[pl.ds(pl.program_id(0) * nb_tiles + k, 1), :]
    project = lambda k: _mix_in(x_ref.at[k], mod_row(k), gpre_ref, win_packed_ref)
    z = project(0)
    for k in range(nb_tiles):
        conv_scr, a_scr, b_scr, h_scr = scratch[k:4 * nb_tiles:nb_tiles]
        sp_groups = _mix_positions(z[3], vng_ref, vnb_ref, wsp_ref)
        xc, gates = _mix_gates(z[0], cw_ref, cb_ref, wgate_packed_ref, conv_scr, tail_ref.at[k])
        z_next = project(k + 1) if k + 1 < nb_tiles else None
        y = _mix_body(z, xc, gates, sp_groups, bgate_ref, lrua_ref, bsp_full_ref,
                      glru_ref, ggmlp_ref, a_scr, b_scr, h_scr, state_ref.at[k])
        _mix_out(y, x_ref.at[k], mod_row(k), gpost_ref, wout_packed_ref, o_ref.at[k])
        z = z_next


def _mix_in(x_ref, mod, gpre_ref, win_ref):
    d = x_ref.shape[1]
    sh_m = mod[:, 0:d]
    sc_m = mod[:, d:2 * d]
    hb = (_rms(x_ref[...]) * (gpre_ref[...] * (1.0 + sc_m)) + sh_m).astype(BF16)
    widths = (LRU_WIDTH, LRU_WIDTH, GMLP_WIDTH, GMLP_WIDTH)
    starts = [sum(widths[:q]) for q in range(len(widths))]
    return [_dot(hb, _as_bf16(win_ref[:, c0:c0 + w])) for c0, w in zip(starts, widths)]


def _mix_gates(lru_x, cw_ref, cb_ref, wgate_ref, conv_scr, tail_ref):
    t = lru_x.shape[0]
    tail = tail_ref[...]
    tail_ref[...] = lru_x[t - SUBLANES:]
    xc = jnp.concatenate(
        _conv_via_scratch(conv_scr, 0, tail, lru_x, cw_ref[...], cb_ref[...]), axis=1)
    xcb = xc.astype(BF16)
    gates = [_dot(xcb[:, j * GATE_TILE:(j + 1) * GATE_TILE], _as_bf16(wgate_ref[j]))
             for j in range(LRU_WIDTH // GATE_TILE)]
    return xc, gates


def _mix_out(y, x_ref, mod, gpost_ref, wout_ref, o_ref):
    d = x_ref.shape[1]
    gt_m = mod[:, 2 * d:3 * d]
    y = _dot(y, _as_bf16(wout_ref[...]))
    o_ref[...] = x_ref[...] + (gt_m * gpost_ref[...]) * _rms(y)


def _mix_positions(g_v, vng_ref, vnb_ref, wsp_ref):
    t = g_v.shape[0]
    gv = _gelu(g_v)
    mu = jnp.mean(gv, axis=-1, keepdims=True)
    cen = gv - mu
    var = jnp.mean(cen * cen, axis=-1, keepdims=True)
    v = cen * lax.rsqrt(var + EPS) * vng_ref[...] + vnb_ref[...]
    vb = v.astype(BF16)
    nb = t // GMLP_BLOCK
    pi = lax.broadcasted_iota(jnp.int32, (GMLP_BLOCK, GMLP_BLOCK), 0) // CHUNK
    pj = lax.broadcasted_iota(jnp.int32, (GMLP_BLOCK, GMLP_BLOCK), 1) // CHUNK
    mask = pj <= pi
    sp_groups = []
    for g in range(GMLP_GROUPS):
        ws = jnp.where(mask, 0.5 * wsp_ref[g], 0.0).astype(BF16)
        cols = slice(g * GMLP_GROUP_DIM, (g + 1) * GMLP_GROUP_DIM)
        rhs = jnp.concatenate(
            [vb[n * GMLP_BLOCK:(n + 1) * GMLP_BLOCK, cols] for n in range(nb)], axis=1)
        sp_groups.append(_dot(ws, rhs))
    return sp_groups


def _mix_body(z, xc, gates, sp_groups, bgate_ref, lrua_ref, bsp_ref,
              glru_ref, ggmlp_ref, a_scr, b_scr, h_scr, state_ref):
    _, lru_gate, g_u, _ = z
    t = lru_gate.shape[0]
    slabs = lambda v: [v[:, c * LANES:(c + 1) * LANES] for c in range(v.shape[1] // LANES)]

    r_pre = jnp.concatenate([g[:, :GATE_TILE] for g in gates], axis=1)
    i_pre = jnp.concatenate([g[:, GATE_TILE:] for g in gates], axis=1)
    r_gate = jax.nn.sigmoid(r_pre + bgate_ref[0:1, :])
    i_gate = jax.nn.sigmoid(i_pre + bgate_ref[1:2, :])
    neg_a = -lrua_ref[...]
    softplus = jnp.maximum(neg_a, 0.0) + jnp.log1p(jnp.exp(-jnp.abs(neg_a)))
    log_a = r_gate * ((-LRU_C) * softplus)
    a = jnp.exp(log_a)
    th = jnp.tanh(log_a)
    q = (-2.0 * th) / (1.0 - th)
    mult = jnp.where(q > 0.0, q * lax.rsqrt(q), 0.0)
    bx = (mult * xc) * i_gate
    hs = _linear_scan(slabs(a), slabs(bx), state_ref, a_scr, b_scr, h_scr)
    y_lru = 0.5 * _gelu2_times(lru_gate, jnp.concatenate(hs, axis=1))
    yl = _rms(y_lru) * glru_ref[...]

    nb = t // GMLP_BLOCK
    bsp = 0.5 * bsp_ref[...]
    sp_rows = []
    for n in range(nb):
        blk = jnp.concatenate(
            [sg[:, n * GMLP_GROUP_DIM:(n + 1) * GMLP_GROUP_DIM] for sg in sp_groups], axis=1)
        sp_rows.append(blk + bsp)
    sp_half = jnp.concatenate(sp_rows, axis=0)
    y_gmlp = _gelu2_times(g_u, sp_half)
    yg = _rms(y_gmlp) * ggmlp_ref[...]

    return jnp.concatenate([yl, yg], axis=1).astype(BF16)


def _mixer(x, mod, g_pre, g_post, w_in, conv_w, conv_b, w_gate, b_gate, lru_a,
           v_norm_g, v_norm_b, w_spatial, b_spatial, g_lru_out, g_gmlp_out, w_out, later_weights,
           layer):
    bsz, s, d = x.shape
    t = MIX_T
    nb = MIX_NB
    assert bsz % nb == 0 and s % t == 0
    n_lg = LRU_WIDTH // LANES
    scan_rows = SUBLANES * (t // SUBLANES + 1)
    full = lambda a: pl.BlockSpec(a.shape, lambda b, i: (0,) * a.ndim)
    of_layer = lambda a: pl.BlockSpec((None,) + a.shape[1:],
                                      lambda b, i: (layer,) + (0,) * (a.ndim - 1))
    in_arrays = [g_pre, g_post, w_in, conv_w, conv_b, w_gate, b_gate, lru_a,
                 v_norm_g, v_norm_b, w_spatial, b_spatial, g_lru_out, g_gmlp_out, w_out]
    resident = lambda a: pl.BlockSpec((None,) + a.shape[1:], lambda b, i: (layer, 0, 0),
                                      pipeline_mode=pl.Buffered(1))
    in_array_specs = [resident(a) if a is w_in or a is w_out
                      else of_layer(a) if a is conv_w or a is w_spatial or a is b_spatial
                      else full(a)
                      for a in in_arrays]
    packed = lambda shape: pltpu.VMEM(shape[:-2] + (shape[-2] // 2, shape[-1]), U32)
    nt = s // t
    n_steps = (bsz // nb) * nt
    step = lambda b, i: (b * nt + i, 0)
    later_in, later_out, later_shapes = [], [], []
    for w in later_weights:
        _, k, n = w.shape
        assert k % (n_steps * 2 * SUBLANES) == 0
        later_in.append(pl.BlockSpec((None, k // n_steps, n),
                                     lambda b, i: (layer, b * nt + i, 0)))
        later_out.append(pl.BlockSpec((k // n_steps // 2, n), step))
        later_shapes.append(jax.ShapeDtypeStruct((k // 2, n), U32))
    out = pl.pallas_call(
        functools.partial(_mixer_kernel, n_later=len(later_weights)),
        grid=(bsz // nb, nt),
        in_specs=[
            pl.BlockSpec((nb, t, d), lambda b, i: (b, i, 0)),
            pl.BlockSpec(mod.shape, lambda b, i: (0, 0)),
        ] + in_array_specs + later_in,
        out_specs=[pl.BlockSpec((nb, t, d), lambda b, i: (b, i, 0))] + later_out,
        out_shape=[jax.ShapeDtypeStruct(x.shape, x.dtype)] + later_shapes,
        scratch_shapes=(
            [pltpu.VMEM((n_lg, SUBLANES + t, LANES), F32)] * nb
            + [pltpu.VMEM((n_lg, scan_rows, LANES), F32)] * (3 * nb)
            + [pltpu.VMEM((nb, SUBLANES, LRU_WIDTH), F32)] * 2
            + [packed(w_in.shape[1:]), packed(w_gate.shape), packed(w_out.shape[1:])]
            + [pltpu.VMEM((GMLP_BLOCK, GMLP_WIDTH), F32)]),
        compiler_params=pltpu.CompilerParams(
            dimension_semantics=("arbitrary", "arbitrary"), vmem_limit_bytes=VMEM_LIMIT),
        name="token_mixer",
    )(x, mod, *in_arrays, *later_weights)
    return out[0], out[1:]


def _ffn_kernel(x_ref, mod_ref, gpre_ref, gpost_ref, wup_ref, cw_ref, cb_ref, wd_ref,
                o_ref, scr_ref, tail_ref):
    t_idx = pl.program_id(1)
    t, d = x_ref.shape
    d_ff = 2 * wd_ref.shape[0]
    fc = FFN_FC
    lg = fc // LANES

    @pl.when(t_idx == 0)
    def _():
        tail_ref[...] = jnp.zeros_like(tail_ref)

    mod = mod_ref[pl.ds(pl.program_id(0), 1), :]
    sh_f = mod[:, 3 * d:4 * d]
    sc_f = mod[:, 4 * d:5 * d]
    gt_f = mod[:, 5 * d:6 * d]
    x = x_ref[...]
    hb = (_rms(x) * (gpre_ref[...] * (1.0 + sc_f)) + sh_f).astype(BF16)

    n_chunks = d_ff // fc
    halves = ((0, 1.0), (d_ff, 0.5))
    chunk_cols = lambda base, j: slice(base + j * fc, base + (j + 1) * fc)
    up_proj = lambda j: [_dot(hb, _as_bf16(wup_ref[:, chunk_cols(base, j)])) for base, _ in halves]

    def gated(j, ups):
        conv = []
        for k, (base, scale) in enumerate(halves):
            cols = chunk_cols(base, j)
            tail = tail_ref[:, cols]
            tail_ref[:, cols] = ups[k][t - SUBLANES:]
            slot = (2 * j + k) % FFN_SLOTS
            conv.append(_conv_via_scratch(scr_ref, slot * lg, tail, ups[k],
                                          cw_ref[:, cols] * scale, cb_ref[:, cols] * scale))
        return jnp.concatenate(
            [_gelu2_times(cg, cv) for cg, cv in zip(*conv)], axis=1).astype(BF16)

    acc = None
    ups = up_proj(0)
    for j in range(n_chunks):
        ups_next = up_proj(j + 1) if j + 1 < n_chunks else None
        part = _dot(gated(j, ups), _as_bf16(wd_ref[j * fc // 2:(j + 1) * fc // 2, :]))
        acc = part if acc is None else acc + part
        ups = ups_next
    o_ref[...] = x + (gt_f * gpost_ref[...]) * _rms(acc)


def _ffn(x, mod, g_pre, g_post, w_up, conv_w, conv_b, w_down, layer):
    bsz, s, d = x.shape
    t = FFN_T
    resident = lambda a: pl.BlockSpec(a.shape, lambda b, i: (0,) * a.ndim,
                                      pipeline_mode=pl.Buffered(1))
    return pl.pallas_call(
        _ffn_kernel,
        grid=(bsz, s // t),
        in_specs=[
            pl.BlockSpec((None, t, d), lambda b, i: (b, i, 0)),
            pl.BlockSpec(mod.shape, lambda b, i: (0, 0)),
            resident(g_pre), resident(g_post), resident(w_up),
            pl.BlockSpec((None,) + conv_w.shape[1:], lambda b, i: (layer, 0, 0),
                         pipeline_mode=pl.Buffered(1)),
            resident(conv_b), resident(w_down),
        ],
        out_specs=pl.BlockSpec((None, t, d), lambda b, i: (b, i, 0)),
        out_shape=jax.ShapeDtypeStruct(x.shape, x.dtype),
        scratch_shapes=[
            pltpu.VMEM((FFN_SLOTS * (FFN_FC // LANES), SUBLANES + t, LANES), F32),
            pltpu.VMEM((SUBLANES, w_up.shape[1]), F32),
        ],
        compiler_params=pltpu.CompilerParams(
            dimension_semantics=("arbitrary", "arbitrary"), vmem_limit_bytes=VMEM_LIMIT),
        name="conv_ffn",
    )(x, mod, g_pre, g_post, w_up, conv_w, conv_b, w_down)


def _gate_weights(w_rgate, w_igate):
    per_tile = GATE_TILE // LRU_HEAD_DIM
    n_tiles = LRU_WIDTH // GATE_TILE

    def head_rows(w, head, slot):
        return jnp.pad(w[0, head], ((0, 0), (slot * LRU_HEAD_DIM, (per_tile - 1 - slot) * LRU_HEAD_DIM)))

    return jnp.stack([
        jnp.concatenate([
            jnp.concatenate([head_rows(w, t * per_tile + a, a) for w in (w_rgate, w_igate)], axis=1)
            for a in range(per_tile)], axis=0)
        for t in range(n_tiles)])


def kernel(x, c, w_ada, b_ada, g_mix_pre, g_mix_post, w_in, conv_w, conv_b, w_rgate, b_rgate, w_igate, b_igate, lru_a, v_norm_g, v_norm_b, w_spatial, b_spatial, g_lru_out, g_gmlp_out, w_out, g_ffn_pre, g_ffn_post, w_up, ffn_conv_w, ffn_conv_b, w_down):
    depth = w_ada.shape[0]
    bsz, s, d = x.shape
    for l in range(depth):
        mod = _modulation(c, w_ada, b_ada[l:l + 1], l)
        row = lambda a: a[l:l + 1]
        w_gate = _gate_weights(w_rgate[l:l + 1], w_igate[l:l + 1])
        b_gate = jnp.stack([b_rgate[l], b_igate[l]]).reshape(2, -1)
        x, (w_up_p, w_down_p) = _mixer(
                   x, mod, row(g_mix_pre), row(g_mix_post), w_in,
                   conv_w, row(conv_b), w_gate, b_gate, row(lru_a),
                   row(v_norm_g), row(v_norm_b), w_spatial, b_spatial,
                   row(g_lru_out), row(g_gmlp_out), w_out,
                   later_weights=[w_up, w_down], layer=l)
        x = _ffn(x, mod, row(g_ffn_pre), row(g_ffn_post), w_up_p,
                 ffn_conv_w, row(ffn_conv_b), w_down_p, layer=l)
    return x
```

```python
import functools

import jax
import jax.numpy as jnp
from jax import lax
from jax.experimental import pallas as pl
from jax.experimental.pallas import tpu as pltpu

CHUNK = 64
LRU_WIDTH = 512
LRU_HEADS = 8
LRU_HEAD_DIM = LRU_WIDTH // LRU_HEADS
LRU_CONV_WIDTH = 4
LRU_C = 8.0
GMLP_WIDTH = 512
GMLP_GROUPS = 4
GMLP_GROUP_DIM = GMLP_WIDTH // GMLP_GROUPS
GMLP_BLOCK = 128
FFN_CONV_WIDTH = 3
N_MOD = 6
EPS = 1e-6

SUBLANES = 8
LANES = 128
GATE_TILE = 256

MOD_ROWS = 128
MOD_STREAMS = 2
MIX_T = 256
MIX_NB = 4
FFN_T = 512
FFN_FC = 768
FFN_SLOTS = 4
VMEM_LIMIT = 60 * 1024 * 1024

F32 = jnp.float32
BF16 = jnp.bfloat16
U32 = jnp.uint32


def _dot(a, b):
    return jnp.dot(a, b, preferred_element_type=F32)


def _pack_rows(w):
    return pltpu.bitcast(w.astype(BF16), U32)


def _as_bf16(packed):
    return pltpu.bitcast(packed, BF16)


def _rms(x):
    return x * lax.rsqrt(jnp.mean(x * x, axis=-1, keepdims=True) + EPS)


def _gelu(x):
    c0 = 0.7978845608028654
    hx = 0.5 * x
    return hx + hx * jnp.tanh(x * (c0 + (c0 * 0.044715) * (x * x)))


def _gelu2_times(x, v):
    c0 = 0.7978845608028654
    z = x * (c0 + (c0 * 0.044715) * (x * x))
    return (x * v) * (1.0 + jnp.tanh(z))


def _rows(ref, lead, start, size):
    return ref[pl.ds(lead, 1, stride=2), pl.ds(start, size), :][0]


def _put_rows(ref, lead, start, val):
    ref[pl.ds(lead, 1, stride=2), pl.ds(start, val.shape[0]), :] = val[None]


def _linear_scan(a, b, state_ref, a_scr, b_scr, h_scr):
    t = a[0].shape[0]
    seg = t // SUBLANES
    pitch = seg + 1 - seg % 2
    n = len(a)
    for c in range(n):
        for s in range(SUBLANES):
            _put_rows(a_scr, c, s * pitch, a[c][s * seg:(s + 1) * seg])
            _put_rows(b_scr, c, s * pitch, b[c][s * seg:(s + 1) * seg])
    step = lambda ref, c, j: ref[c, pl.ds(j, SUBLANES, stride=pitch), :]
    row = lax.broadcasted_iota(jnp.int32, (SUBLANES, LANES), 0)
    shifted = lambda v, sh, fill: jnp.where(row >= sh, pltpu.roll(v, sh, axis=0), fill)

    prod = [None] * n
    end = [None] * n
    for j in range(seg):
        for c in range(n):
            aj, bj = step(a_scr, c, j), step(b_scr, c, j)
            prod[c] = aj if j == 0 else aj * prod[c]
            end[c] = bj if j == 0 else aj * end[c] + bj

    h = []
    for c in range(n):
        cols = slice(c * LANES, (c + 1) * LANES)
        h0 = state_ref[SUBLANES - 1:SUBLANES, cols]
        p = prod[c]
        e = end[c] + jnp.where(row == 0, p * h0, 0.0)
        for sh in (1, 2, 4):
            e = e + p * shifted(e, sh, 0.0)
            if sh < 4:
                p = p * shifted(p, sh, 1.0)
        state_ref[:, cols] = e
        h.append(shifted(e, 1, h0))
    for j in range(seg):
        for c in range(n):
            h[c] = step(a_scr, c, j) * h[c] + step(b_scr, c, j)
            h_scr[c, pl.ds(j, SUBLANES, stride=pitch), :] = h[c]
    return [jnp.concatenate([_rows(h_scr, c, s * pitch, seg) for s in range(SUBLANES)], axis=0)
            for c in range(n)]


def _conv_via_scratch(scr_ref, lead0, tail, x, w, b):
    k_width = w.shape[0]
    t, c = x.shape
    outs = []
    for g in range(c // LANES):
        cols = slice(g * LANES, (g + 1) * LANES)
        scr_ref[lead0 + g, 0:SUBLANES, :] = tail[:, cols]
        scr_ref[lead0 + g, SUBLANES:SUBLANES + t, :] = x[:, cols]
    for g in range(c // LANES):
        cols = slice(g * LANES, (g + 1) * LANES)
        out = x[:, cols] * w[k_width - 1:k_width, cols] + b[:, cols]
        for k in range(k_width - 1):
            shifted = _rows(scr_ref, lead0 + g, SUBLANES - (k_width - 1 - k), t)
            out = out + shifted * w[k:k + 1, cols]
        outs.append(out)
    return outs


def _mod_kernel(c_ref, *rest):
    w_refs, b_ref, o_ref = rest[:-2], rest[-2], rest[-1]
    c = c_ref[...]
    rows = c.shape[0]
    c_act = c * jax.nn.sigmoid(c)
    pad = (-rows) % SUBLANES
    if pad:
        c_act = jnp.concatenate([c_act, jnp.zeros((pad, c.shape[1]), F32)], axis=0)
    c_act = c_act.astype(BF16)
    bk = w_refs[0].shape[0]
    parts = [_dot(c_act[:, j * bk:(j + 1) * bk], w_ref[...].astype(BF16))[:rows]
             for j, w_ref in enumerate(w_refs)]

    @pl.when(pl.program_id(0) == 0)
    def _():
        o_ref[...] = functools.reduce(jnp.add, parts, b_ref[...])

    @pl.when(pl.program_id(0) > 0)
    def _():
        o_ref[...] = functools.reduce(jnp.add, parts, o_ref[...])


def _modulation(c, w_ada, b_ada, layer):
    rows, d = c.shape
    n = w_ada.shape[2]
    bk = MOD_ROWS
    ns = MOD_STREAMS
    assert d % (bk * ns) == 0
    row_block = lambda j: pl.BlockSpec((None, bk, n), lambda k: (layer, k * ns + j, 0))
    return pl.pallas_call(
        _mod_kernel,
        grid=(d // (bk * ns),),
        in_specs=[pl.BlockSpec((rows, bk * ns), lambda k: (0, k))]
                 + [row_block(j) for j in range(ns)]
                 + [pl.BlockSpec((1, n), lambda k: (0, 0))],
        out_specs=pl.BlockSpec((rows, n), lambda k: (0, 0)),
        out_shape=jax.ShapeDtypeStruct((rows, n), F32),
        compiler_params=pltpu.CompilerParams(
            dimension_semantics=("arbitrary",), vmem_limit_bytes=VMEM_LIMIT),
        name="adaln_mod",
    )(c, *[w_ada] * ns, b_ada)


def _mixer_kernel(x_ref, mod_ref, gpre_ref, gpost_ref, win_ref, cw_ref, cb_ref,
                  wgate_ref, bgate_ref, lrua_ref, vng_ref, vnb_ref, wsp_ref, bsp_ref,
                  glru_ref, ggmlp_ref, wout_ref, *rest, n_later):
    later_in, o_ref, later_out = rest[:n_later], rest[n_later], rest[n_later + 1:2 * n_later + 1]
    scratch = rest[2 * n_later + 1:]
    for w_ref, p_ref in zip(later_in, later_out):
        p_ref[...] = _pack_rows(w_ref[...])
    nb_tiles = x_ref.shape[0]
    (tail_ref, state_ref, win_packed_ref, wgate_packed_ref, wout_packed_ref,
     bsp_full_ref) = scratch[4 * nb_tiles:]

    @pl.when((pl.program_id(0) == 0) & (pl.program_id(1) == 0))
    def _():
        win_packed_ref[...] = _pack_rows(win_ref[...])
        for j in range(wgate_ref.shape[0]):
            wgate_packed_ref[j] = _pack_rows(wgate_ref[j])
        wout_packed_ref[...] = _pack_rows(wout_ref[...])
        diag = (lax.broadcasted_iota(jnp.int32, (GMLP_BLOCK, GMLP_BLOCK), 0)
                == lax.broadcasted_iota(jnp.int32, (GMLP_BLOCK, GMLP_BLOCK), 1))
        for g in range(GMLP_GROUPS):
            col = jnp.sum(jnp.where(diag, bsp_ref[g:g + 1, :], -0.0), axis=1, keepdims=True)
            bsp_full_ref[:, g * GMLP_GROUP_DIM:(g + 1) * GMLP_GROUP_DIM] = jnp.broadcast_to(
                col, (GMLP_BLOCK, GMLP_GROUP_DIM))

    @pl.when(pl.program_id(1) == 0)
    def _():
        tail_ref[...] = jnp.zeros_like(tail_ref)
        state_ref[...] = jnp.zeros_like(state_ref)

    mod_row = lambda k: mod_ref[pl.ds(pl.program_id(0) * nb_tiles + k, 1), :]
    project = lambda k: _mix_in(x_ref.at[k], mod_row(k), gpre_ref, win_packed_ref)
    z = project(0)
    for k in range(nb_tiles):
        conv_scr, a_scr, b_scr, h_scr = scratch[k:4 * nb_tiles:nb_tiles]
        xc, gates = _mix_gates(z[0], cw_ref, cb_ref, wgate_packed_ref, conv_scr, tail_ref.at[k])
        sp_groups = _mix_positions(z[3], vng_ref, vnb_ref, wsp_ref)
        z_next = project(k + 1) if k + 1 < nb_tiles else None
        y = _mix_body(z, xc, gates, sp_groups, bgate_ref, lrua_ref, bsp_full_ref,
                      glru_ref, ggmlp_ref, a_scr, b_scr, h_scr, state_ref.at[k])
        _mix_out(y, x_ref.at[k], mod_row(k), gpost_ref, wout_packed_ref, o_ref.at[k])
        z = z_next


def _mix_in(x_ref, mod, gpre_ref, win_ref):
    d = x_ref.shape[1]
    sh_m = mod[:, 0:d]
    sc_m = mod[:, d:2 * d]
    hb = (_rms(x_ref[...]) * (gpre_ref[...] * (1.0 + sc_m)) + sh_m).astype(BF16)
    widths = (LRU_WIDTH, LRU_WIDTH, GMLP_WIDTH, GMLP_WIDTH)
    starts = [sum(widths[:q]) for q in range(len(widths))]
    return [_dot(hb, _as_bf16(win_ref[:, c0:c0 + w])) for c0, w in zip(starts, widths)]


def _mix_gates(lru_x, cw_ref, cb_ref, wgate_ref, conv_scr, tail_ref):
    t = lru_x.shape[0]
    tail = tail_ref[...]
    tail_ref[...] = lru_x[t - SUBLANES:]
    xc = jnp.concatenate(
        _conv_via_scratch(conv_scr, 0, tail, lru_x, cw_ref[...], cb_ref[...]), axis=1)
    xcb = xc.astype(BF16)
    gates = [_dot(xcb[:, j * GATE_TILE:(j + 1) * GATE_TILE], _as_bf16(wgate_ref[j]))
             for j in range(LRU_WIDTH // GATE_TILE)]
    return xc, gates


def _mix_out(y, x_ref, mod, gpost_ref, wout_ref, o_ref):
    d = x_ref.shape[1]
    gt_m = mod[:, 2 * d:3 * d]
    y = _dot(y, _as_bf16(wout_ref[...]))
    o_ref[...] = x_ref[...] + (gt_m * gpost_ref[...]) * _rms(y)


def _mix_positions(g_v, vng_ref, vnb_ref, wsp_ref):
    t = g_v.shape[0]
    gv = _gelu(g_v)
    mu = jnp.mean(gv, axis=-1, keepdims=True)
    cen = gv - mu
    var = jnp.mean(cen * cen, axis=-1, keepdims=True)
    v = cen * lax.rsqrt(var + EPS) * vng_ref[...] + vnb_ref[...]
    vb = v.astype(BF16)
    nb = t // GMLP_BLOCK
    pi = lax.broadcasted_iota(jnp.int32, (GMLP_BLOCK, GMLP_BLOCK), 0) // CHUNK
    pj = lax.broadcasted_iota(jnp.int32, (GMLP_BLOCK, GMLP_BLOCK), 1) // CHUNK
    mask = pj <= pi
    sp_groups = []
    for g in range(GMLP_GROUPS):
        ws = jnp.where(mask, 0.5 * wsp_ref[g], 0.0).astype(BF16)
        cols = slice(g * GMLP_GROUP_DIM, (g + 1) * GMLP_GROUP_DIM)
        rhs = jnp.concatenate(
            [vb[n * GMLP_BLOCK:(n + 1) * GMLP_BLOCK, cols] for n in range(nb)], axis=1)
        sp_groups.append(_dot(ws, rhs))
    return sp_groups


def _mix_body(z, xc, gates, sp_groups, bgate_ref, lrua_ref, bsp_ref,
              glru_ref, ggmlp_ref, a_scr, b_scr, h_scr, state_ref):
    _, lru_gate, g_u, _ = z
    t = lru_gate.shape[0]
    slabs = lambda v: [v[:, c * LANES:(c + 1) * LANES] for c in range(v.shape[1] // LANES)]

    r_pre = jnp.concatenate([g[:, :GATE_TILE] for g in gates], axis=1)
    i_pre = jnp.concatenate([g[:, GATE_TILE:] for g in gates], axis=1)
    r_gate = jax.nn.sigmoid(r_pre + bgate_ref[0:1, :])
    i_gate = jax.nn.sigmoid(i_pre + bgate_ref[1:2, :])
    neg_a = -lrua_ref[...]
    softplus = jnp.maximum(neg_a, 0.0) + jnp.log1p(jnp.exp(-jnp.abs(neg_a)))
    log_a = r_gate * ((-LRU_C) * softplus)
    a = jnp.exp(log_a)
    th = jnp.tanh(log_a)
    q = (-2.0 * th) / (1.0 - th)
    mult = jnp.where(q > 0.0, q * lax.rsqrt(q), 0.0)
    bx = (mult * xc) * i_gate
    hs = _linear_scan(slabs(a), slabs(bx), state_ref, a_scr, b_scr, h_scr)
    y_lru = 0.5 * _gelu2_times(lru_gate, jnp.concatenate(hs, axis=1))
    yl = _rms(y_lru) * glru_ref[...]

    nb = t // GMLP_BLOCK
    bsp = 0.5 * bsp_ref[...]
    sp_rows = []
    for n in range(nb):
        blk = jnp.concatenate(
            [sg[:, n * GMLP_GROUP_DIM:(n + 1) * GMLP_GROUP_DIM] for sg in sp_groups], axis=1)
        sp_rows.append(blk + bsp)
    sp_half = jnp.concatenate(sp_rows, axis=0)
    y_gmlp = _gelu2_times(g_u, sp_half)
    yg = _rms(y_gmlp) * ggmlp_ref[...]

    return jnp.concatenate([yl, yg], axis=1).astype(BF16)


def _mixer(x, mod, g_pre, g_post, w_in, conv_w, conv_b, w_gate, b_gate, lru_a,
           v_norm_g, v_norm_b, w_spatial, b_spatial, g_lru_out, g_gmlp_out, w_out, later_weights,
           layer):
    bsz, s, d = x.shape
    t = MIX_T
    nb = MIX_NB
    assert bsz % nb == 0 and s % t == 0
    n_lg = LRU_WIDTH // LANES
    scan_rows = SUBLANES * (t // SUBLANES + 1)
    full = lambda a: pl.BlockSpec(a.shape, lambda b, i: (0,) * a.ndim)
    of_layer = lambda a: pl.BlockSpec((None,) + a.shape[1:],
                                      lambda b, i: (layer,) + (0,) * (a.ndim - 1))
    in_arrays = [g_pre, g_post, w_in, conv_w, conv_b, w_gate, b_gate, lru_a,
                 v_norm_g, v_norm_b, w_spatial, b_spatial, g_lru_out, g_gmlp_out, w_out]
    resident = lambda a: pl.BlockSpec((None,) + a.shape[1:], lambda b, i: (layer, 0, 0),
                                      pipeline_mode=pl.Buffered(1))
    in_array_specs = [resident(a) if a is w_in or a is w_out
                      else of_layer(a) if a is conv_w or a is w_spatial or a is b_spatial
                      else full(a)
                      for a in in_arrays]
    packed = lambda shape: pltpu.VMEM(shape[:-2] + (shape[-2] // 2, shape[-1]), U32)
    nt = s // t
    n_steps = (bsz // nb) * nt
    step = lambda b, i: (b * nt + i, 0)
    later_in, later_out, later_shapes = [], [], []
    for w in later_weights:
        _, k, n = w.shape
        assert k % (n_steps * 2 * SUBLANES) == 0
        later_in.append(pl.BlockSpec((None, k // n_steps, n),
                                     lambda b, i: (layer, b * nt + i, 0)))
        later_out.append(pl.BlockSpec((k // n_steps // 2, n), step))
        later_shapes.append(jax.ShapeDtypeStruct((k // 2, n), U32))
    out = pl.pallas_call(
        functools.partial(_mixer_kernel, n_later=len(later_weights)),
        grid=(bsz // nb, nt),
        in_specs=[
            pl.BlockSpec((nb, t, d), lambda b, i: (b, i, 0)),
            pl.BlockSpec(mod.shape, lambda b, i: (0, 0)),
        ] + in_array_specs + later_in,
        out_specs=[pl.BlockSpec((nb, t, d), lambda b, i: (b, i, 0))] + later_out,
        out_shape=[jax.ShapeDtypeStruct(x.shape, x.dtype)] + later_shapes,
        scratch_shapes=(
            [pltpu.VMEM((n_lg, SUBLANES + t, LANES), F32)] * nb
            + [pltpu.VMEM((n_lg, scan_rows, LANES), F32)] * (3 * nb)
            + [pltpu.VMEM((nb, SUBLANES, LRU_WIDTH), F32)] * 2
            + [packed(w_in.shape[1:]), packed(w_gate.shape), packed(w_out.shape[1:])]
            + [pltpu.VMEM((GMLP_BLOCK, GMLP_WIDTH), F32)]),
        compiler_params=pltpu.CompilerParams(
            dimension_semantics=("arbitrary", "arbitrary"), vmem_limit_bytes=VMEM_LIMIT),
        name="token_mixer",
    )(x, mod, *in_arrays, *later_weights)
    return out[0], out[1:]


def _ffn_kernel(x_ref, mod_ref, gpre_ref, gpost_ref, wup_ref, cw_ref, cb_ref, wd_ref,
                o_ref, scr_ref, tail_ref):
    t_idx = pl.program_id(1)
    t, d = x_ref.shape
    d_ff = 2 * wd_ref.shape[0]
    fc = FFN_FC
    lg = fc // LANES

    @pl.when(t_idx == 0)
    def _():
        tail_ref[...] = jnp.zeros_like(tail_ref)

    mod = mod_ref[pl.ds(pl.program_id(0), 1), :]
    sh_f = mod[:, 3 * d:4 * d]
    sc_f = mod[:, 4 * d:5 * d]
    gt_f = mod[:, 5 * d:6 * d]
    x = x_ref[...]
    hb = (_rms(x) * (gpre_ref[...] * (1.0 + sc_f)) + sh_f).astype(BF16)

    n_chunks = d_ff // fc
    halves = ((0, 1.0), (d_ff, 0.5))
    chunk_cols = lambda base, j: slice(base + j * fc, base + (j + 1) * fc)
    up_proj = lambda j: [_dot(hb, _as_bf16(wup_ref[:, chunk_cols(base, j)])) for base, _ in halves]

    def gated(j, ups):
        conv = []
        for k, (base, scale) in enumerate(halves):
            cols = chunk_cols(base, j)
            tail = tail_ref[:, cols]
            tail_ref[:, cols] = ups[k][t - SUBLANES:]
            slot = (2 * j + k) % FFN_SLOTS
            conv.append(_conv_via_scratch(scr_ref, slot * lg, tail, ups[k],
                                          cw_ref[:, cols] * scale, cb_ref[:, cols] * scale))
        return jnp.concatenate(
            [_gelu2_times(cg, cv) for cg, cv in zip(*conv)], axis=1).astype(BF16)

    acc = None
    ups = up_proj(0)
    for j in range(n_chunks):
        ups_next = up_proj(j + 1) if j + 1 < n_chunks else None
        part = _dot(gated(j, ups), _as_bf16(wd_ref[j * fc // 2:(j + 1) * fc // 2, :]))
        acc = part if acc is None else acc + part
        ups = ups_next
    o_ref[...] = x + (gt_f * gpost_ref[...]) * _rms(acc)


def _ffn(x, mod, g_pre, g_post, w_up, conv_w, conv_b, w_down, layer):
    bsz, s, d = x.shape
    t = FFN_T
    resident = lambda a: pl.BlockSpec(a.shape, lambda b, i: (0,) * a.ndim,
                                      pipeline_mode=pl.Buffered(1))
    return pl.pallas_call(
        _ffn_kernel,
        grid=(bsz, s // t),
        in_specs=[
            pl.BlockSpec((None, t, d), lambda b, i: (b, i, 0)),
            pl.BlockSpec(mod.shape, lambda b, i: (0, 0)),
            resident(g_pre), resident(g_post), resident(w_up),
            pl.BlockSpec((None,) + conv_w.shape[1:], lambda b, i: (layer, 0, 0),
                         pipeline_mode=pl.Buffered(1)),
            resident(conv_b), resident(w_down),
        ],
        out_specs=pl.BlockSpec((None, t, d), lambda b, i: (b, i, 0)),
        out_shape=jax.ShapeDtypeStruct(x.shape, x.dtype),
        scratch_shapes=[
            pltpu.VMEM((FFN_SLOTS * (FFN_FC // LANES), SUBLANES + t, LANES), F32),
            pltpu.VMEM((SUBLANES, w_up.shape[1]), F32),
        ],
        compiler_params=pltpu.CompilerParams(
            dimension_semantics=("arbitrary", "arbitrary"), vmem_limit_bytes=VMEM_LIMIT),
        name="conv_ffn",
    )(x, mod, g_pre, g_post, w_up, conv_w, conv_b, w_down)


def _gate_weights(w_rgate, w_igate):
    per_tile = GATE_TILE // LRU_HEAD_DIM
    n_tiles = LRU_WIDTH // GATE_TILE

    def head_rows(w, head, slot):
        return jnp.pad(w[0, head], ((0, 0), (slot * LRU_HEAD_DIM, (per_tile - 1 - slot) * LRU_HEAD_DIM)))

    return jnp.stack([
        jnp.concatenate([
            jnp.concatenate([head_rows(w, t * per_tile + a, a) for w in (w_rgate, w_igate)], axis=1)
            for a in range(per_tile)], axis=0)
        for t in range(n_tiles)])


def kernel(x, c, w_ada, b_ada, g_mix_pre, g_mix_post, w_in, conv_w, conv_b, w_rgate, b_rgate, w_igate, b_igate, lru_a, v_norm_g, v_norm_b, w_spatial, b_spatial, g_lru_out, g_gmlp_out, w_out, g_ffn_pre, g_ffn_post, w_up, ffn_conv_w, ffn_conv_b, w_down):
    depth = w_ada.shape[0]
    bsz, s, d = x.shape
    for l in range(depth):
        mod = _modulation(c, w_ada, b_ada[l:l + 1], l)
        row = lambda a: a[l:l + 1]
        w_gate = _gate_weights(w_rgate[l:l + 1], w_igate[l:l + 1])
        b_gate = jnp.stack([b_rgate[l], b_igate[l]]).reshape(2, -1)
        x, (w_up_p, w_down_p) = _mixer(
                   x, mod, row(g_mix_pre), row(g_mix_post), w_in,
                   conv_w, row(conv_b), w_gate, b_gate, row(lru_a),
                   row(v_norm_g), row(v_norm_b), w_spatial, b_spatial,
                   row(g_lru_out), row(g_gmlp_out), w_out,
                   later_weights=[w_up, w_down], layer=l)
        x = _ffn(x, mod, row(g_ffn_pre), row(g_ffn_post), w_up_p,
                 ffn_conv_w, row(ffn_conv_b), w_down_p, layer=l)
    return x
```

```python
import functools

import jax
import jax.numpy as jnp
from jax import lax
from jax.experimental import pallas as pl
from jax.experimental.pallas import tpu as pltpu

CHUNK = 64
LRU_WIDTH = 512
LRU_HEADS = 8
LRU_HEAD_DIM = LRU_WIDTH // LRU_HEADS
LRU_CONV_WIDTH = 4
LRU_C = 8.0
GMLP_WIDTH = 512
GMLP_GROUPS = 4
GMLP_GROUP_DIM = GMLP_WIDTH // GMLP_GROUPS
GMLP_BLOCK = 128
FFN_CONV_WIDTH = 3
N_MOD = 6
EPS = 1e-6

SUBLANES = 8
LANES = 128
GATE_TILE = 256

MOD_ROWS = 128
MOD_STREAMS = 2
MIX_T = 256
MIX_NB = 4
FFN_T = 512
FFN_FC = 512
FFN_SLOTS = 4
VMEM_LIMIT = 60 * 1024 * 1024

F32 = jnp.float32
BF16 = jnp.bfloat16
U32 = jnp.uint32


def _dot(a, b):
    return jnp.dot(a, b, preferred_element_type=F32)


def _pack_rows(w):
    return pltpu.bitcast(w.astype(BF16), U32)


def _as_bf16(packed):
    return pltpu.bitcast(packed, BF16)


def _rms(x):
    return x * lax.rsqrt(jnp.mean(x * x, axis=-1, keepdims=True) + EPS)


def _gelu(x):
    c0 = 0.7978845608028654
    hx = 0.5 * x
    return hx + hx * jnp.tanh(x * (c0 + (c0 * 0.044715) * (x * x)))


def _gelu2_times(x, v):
    c0 = 0.7978845608028654
    z = x * (c0 + (c0 * 0.044715) * (x * x))
    return (x * v) * (1.0 + jnp.tanh(z))


def _rows(ref, lead, start, size):
    return ref[pl.ds(lead, 1, stride=2), pl.ds(start, size), :][0]


def _put_rows(ref, lead, start, val):
    ref[pl.ds(lead, 1, stride=2), pl.ds(start, val.shape[0]), :] = val[None]


def _linear_scan(a, b, state_ref, a_scr, b_scr, h_scr):
    t = a[0].shape[0]
    seg = t // SUBLANES
    pitch = seg + 1 - seg % 2
    n = len(a)
    for c in range(n):
        for s in range(SUBLANES):
            _put_rows(a_scr, c, s * pitch, a[c][s * seg:(s + 1) * seg])
            _put_rows(b_scr, c, s * pitch, b[c][s * seg:(s + 1) * seg])
    step = lambda ref, c, j: ref[c, pl.ds(j, SUBLANES, stride=pitch), :]
    row = lax.broadcasted_iota(jnp.int32, (SUBLANES, LANES), 0)
    shifted = lambda v, sh, fill: jnp.where(row >= sh, pltpu.roll(v, sh, axis=0), fill)

    prod = [None] * n
    end = [None] * n
    for j in range(seg):
        for c in range(n):
            aj, bj = step(a_scr, c, j), step(b_scr, c, j)
            prod[c] = aj if j == 0 else aj * prod[c]
            end[c] = bj if j == 0 else aj * end[c] + bj

    h = []
    for c in range(n):
        cols = slice(c * LANES, (c + 1) * LANES)
        h0 = state_ref[SUBLANES - 1:SUBLANES, cols]
        p = prod[c]
        e = end[c] + jnp.where(row == 0, p * h0, 0.0)
        for sh in (1, 2, 4):
            e = e + p * shifted(e, sh, 0.0)
            if sh < 4:
                p = p * shifted(p, sh, 1.0)
        state_ref[:, cols] = e
        h.append(shifted(e, 1, h0))
    for j in range(seg):
        for c in range(n):
            h[c] = step(a_scr, c, j) * h[c] + step(b_scr, c, j)
            h_scr[c, pl.ds(j, SUBLANES, stride=pitch), :] = h[c]
    return [jnp.concatenate([_rows(h_scr, c, s * pitch, seg) for s in range(SUBLANES)], axis=0)
            for c in range(n)]


def _conv_via_scratch(scr_ref, lead0, tail, x, w, b):
    k_width = w.shape[0]
    t, c = x.shape
    outs = []
    for g in range(c // LANES):
        cols = slice(g * LANES, (g + 1) * LANES)
        scr_ref[lead0 + g, 0:SUBLANES, :] = tail[:, cols]
        scr_ref[lead0 + g, SUBLANES:SUBLANES + t, :] = x[:, cols]
    for g in range(c // LANES):
        cols = slice(g * LANES, (g + 1) * LANES)
        out = x[:, cols] * w[k_width - 1:k_width, cols] + b[:, cols]
        for k in range(k_width - 1):
            shifted = _rows(scr_ref, lead0 + g, SUBLANES - (k_width - 1 - k), t)
            out = out + shifted * w[k:k + 1, cols]
        outs.append(out)
    return outs


def _mod_kernel(c_ref, *rest):
    w_refs, b_ref, o_ref = rest[:-2], rest[-2], rest[-1]
    c = c_ref[...]
    rows = c.shape[0]
    c_act = c * jax.nn.sigmoid(c)
    pad = (-rows) % SUBLANES
    if pad:
        c_act = jnp.concatenate([c_act, jnp.zeros((pad, c.shape[1]), F32)], axis=0)
    c_act = c_act.astype(BF16)
    bk = w_refs[0].shape[0]
    parts = [_dot(c_act[:, j * bk:(j + 1) * bk], w_ref[...].astype(BF16))[:rows]
             for j, w_ref in enumerate(w_refs)]

    @pl.when(pl.program_id(0) == 0)
    def _():
        o_ref[...] = functools.reduce(jnp.add, parts, b_ref[...])

    @pl.when(pl.program_id(0) > 0)
    def _():
        o_ref[...] = functools.reduce(jnp.add, parts, o_ref[...])


def _modulation(c, w_ada, b_ada, layer):
    rows, d = c.shape
    n = w_ada.shape[2]
    bk = MOD_ROWS
    ns = MOD_STREAMS
    assert d % (bk * ns) == 0
    row_block = lambda j: pl.BlockSpec((None, bk, n), lambda k: (layer, k * ns + j, 0))
    return pl.pallas_call(
        _mod_kernel,
        grid=(d // (bk * ns),),
        in_specs=[pl.BlockSpec((rows, bk * ns), lambda k: (0, k))]
                 + [row_block(j) for j in range(ns)]
                 + [pl.BlockSpec((1, n), lambda k: (0, 0))],
        out_specs=pl.BlockSpec((rows, n), lambda k: (0, 0)),
        out_shape=jax.ShapeDtypeStruct((rows, n), F32),
        compiler_params=pltpu.CompilerParams(
            dimension_semantics=("arbitrary",), vmem_limit_bytes=VMEM_LIMIT),
        name="adaln_mod",
    )(c, *[w_ada] * ns, b_ada)


def _mixer_kernel(x_ref, mod_ref, gpre_ref, gpost_ref, win_ref, cw_ref, cb_ref,
                  wgate_ref, bgate_ref, lrua_ref, vng_ref, vnb_ref, wsp_ref, bsp_ref,
                  glru_ref, ggmlp_ref, wout_ref, *rest, n_later):
    later_in, o_ref, later_out = rest[:n_later], rest[n_later], rest[n_later + 1:2 * n_later + 1]
    scratch = rest[2 * n_later + 1:]
    for w_ref, p_ref in zip(later_in, later_out):
        p_ref[...] = _pack_rows(w_ref[...])
    nb_tiles = x_ref.shape[0]
    (tail_ref, state_ref, win_packed_ref, wgate_packed_ref, wout_packed_ref,
     bsp_full_ref) = scratch[4 * nb_tiles:]

    @pl.when((pl.program_id(0) == 0) & (pl.program_id(1) == 0))
    def _():
        win_packed_ref[...] = _pack_rows(win_ref[...])
        for j in range(wgate_ref.shape[0]):
            wgate_packed_ref[j] = _pack_rows(wgate_ref[j])
        wout_packed_ref[...] = _pack_rows(wout_ref[...])
        diag = (lax.broadcasted_iota(jnp.int32, (GMLP_BLOCK, GMLP_BLOCK), 0)
                == lax.broadcasted_iota(jnp.int32, (GMLP_BLOCK, GMLP_BLOCK), 1))
        for g in range(GMLP_GROUPS):
            col = jnp.sum(jnp.where(diag, bsp_ref[g:g + 1, :], -0.0), axis=1, keepdims=True)
            bsp_full_ref[:, g * GMLP_GROUP_DIM:(g + 1) * GMLP_GROUP_DIM] = jnp.broadcast_to(
                col, (GMLP_BLOCK, GMLP_GROUP_DIM))

    @pl.when(pl.program_id(1) == 0)
    def _():
        tail_ref[...] = jnp.zeros_like(tail_ref)
        state_ref[...] = jnp.zeros_like(state_ref)

    mod_row = lambda k: mod_ref[pl.ds(pl.program_id(0) * nb_tiles + k, 1), :]
    project = lambda k: _mix_in(x_ref.at[k], mod_row(k), gpre_ref, win_packed_ref)
    z = project(0)
    for k in range(nb_tiles):
        conv_scr, a_scr, b_scr, h_scr = scratch[k:4 * nb_tiles:nb_tiles]
        xc, gates = _mix_gates(z[0], cw_ref, cb_ref, wgate_packed_ref, conv_scr, tail_ref.at[k])
        sp_groups = _mix_positions(z[3], vng_ref, vnb_ref, wsp_ref)
        z_next = project(k + 1) if k + 1 < nb_tiles else None
        y = _mix_body(z, xc, gates, sp_groups, bgate_ref, lrua_ref, bsp_full_ref,
                      glru_ref, ggmlp_ref, a_scr, b_scr, h_scr, state_ref.at[k])
        _mix_out(y, x_ref.at[k], mod_row(k), gpost_ref, wout_packed_ref, o_ref.at[k])
        z = z_next


def _mix_in(x_ref, mod, gpre_ref, win_ref):
    d = x_ref.shape[1]
    sh_m = mod[:, 0:d]
    sc_m = mod[:, d:2 * d]
    hb = (_rms(x_ref[...]) * (gpre_ref[...] * (1.0 + sc_m)) + sh_m).astype(BF16)
    widths = (LRU_WIDTH, LRU_WIDTH, GMLP_WIDTH, GMLP_WIDTH)
    starts = [sum(widths[:q]) for q in range(len(widths))]
    return [_dot(hb, _as_bf16(win_ref[:, c0:c0 + w])) for c0, w in zip(starts, widths)]


def _mix_gates(lru_x, cw_ref, cb_ref, wgate_ref, conv_scr, tail_ref):
    t = lru_x.shape[0]
    tail = tail_ref[...]
    tail_ref[...] = lru_x[t - SUBLANES:]
    xc = jnp.concatenate(
        _conv_via_scratch(conv_scr, 0, tail, lru_x, cw_ref[...], cb_ref[...]), axis=1)
    xcb = xc.astype(BF16)
    gates = [_dot(xcb[:, j * GATE_TILE:(j + 1) * GATE_TILE], _as_bf16(wgate_ref[j]))
             for j in range(LRU_WIDTH // GATE_TILE)]
    return xc, gates


def _mix_out(y, x_ref, mod, gpost_ref, wout_ref, o_ref):
    d = x_ref.shape[1]
    gt_m = mod[:, 2 * d:3 * d]
    y = _dot(y, _as_bf16(wout_ref[...]))
    o_ref[...] = x_ref[...] + (gt_m * gpost_ref[...]) * _rms(y)


def _mix_positions(g_v, vng_ref, vnb_ref, wsp_ref):
    t = g_v.shape[0]
    gv = _gelu(g_v)
    mu = jnp.mean(gv, axis=-1, keepdims=True)
    cen = gv - mu
    var = jnp.mean(cen * cen, axis=-1, keepdims=True)
    v = cen * lax.rsqrt(var + EPS) * vng_ref[...] + vnb_ref[...]
    vb = v.astype(BF16)
    nb = t // GMLP_BLOCK
    pi = lax.broadcasted_iota(jnp.int32, (GMLP_BLOCK, GMLP_BLOCK), 0) // CHUNK
    pj = lax.broadcasted_iota(jnp.int32, (GMLP_BLOCK, GMLP_BLOCK), 1) // CHUNK
    mask = pj <= pi
    sp_groups = []
    for g in range(GMLP_GROUPS):
        ws = jnp.where(mask, 0.5 * wsp_ref[g], 0.0).astype(BF16)
        cols = slice(g * GMLP_GROUP_DIM, (g + 1) * GMLP_GROUP_DIM)
        rhs = jnp.concatenate(
            [vb[n * GMLP_BLOCK:(n + 1) * GMLP_BLOCK, cols] for n in range(nb)], axis=1)
        sp_groups.append(_dot(ws, rhs))
    return sp_groups


def _mix_body(z, xc, gates, sp_groups, bgate_ref, lrua_ref, bsp_ref,
              glru_ref, ggmlp_ref, a_scr, b_scr, h_scr, state_ref):
    _, lru_gate, g_u, _ = z
    t = lru_gate.shape[0]
    slabs = lambda v: [v[:, c * LANES:(c + 1) * LANES] for c in range(v.shape[1] // LANES)]

    r_pre = jnp.concatenate([g[:, :GATE_TILE] for g in gates], axis=1)
    i_pre = jnp.concatenate([g[:, GATE_TILE:] for g in gates], axis=1)
    r_gate = jax.nn.sigmoid(r_pre + bgate_ref[0:1, :])
    i_gate = jax.nn.sigmoid(i_pre + bgate_ref[1:2, :])
    neg_a = -lrua_ref[...]
    softplus = jnp.maximum(neg_a, 0.0) + jnp.log1p(jnp.exp(-jnp.abs(neg_a)))
    log_a = r_gate * ((-LRU_C) * softplus)
    a = jnp.exp(log_a)
    th = jnp.tanh(log_a)
    q = (-2.0 * th) / (1.0 - th)
    mult = jnp.where(q > 0.0, q * lax.rsqrt(q), 0.0)
    bx = (mult * xc) * i_gate
    hs = _linear_scan(slabs(a), slabs(bx), state_ref, a_scr, b_scr, h_scr)
    y_lru = 0.5 * _gelu2_times(lru_gate, jnp.concatenate(hs, axis=1))
    yl = _rms(y_lru) * glru_ref[...]

    nb = t // GMLP_BLOCK
    bsp = 0.5 * bsp_ref[...]
    sp_rows = []
    for n in range(nb):
        blk = jnp.concatenate(
            [sg[:, n * GMLP_GROUP_DIM:(n + 1) * GMLP_GROUP_DIM] for sg in sp_groups], axis=1)
        sp_rows.append(blk + bsp)
    sp_half = jnp.concatenate(sp_rows, axis=0)
    y_gmlp = _gelu2_times(g_u, sp_half)
    yg = _rms(y_gmlp) * ggmlp_ref[...]

    return jnp.concatenate([yl, yg], axis=1).astype(BF16)


def _mixer(x, mod, g_pre, g_post, w_in, conv_w, conv_b, w_gate, b_gate, lru_a,
           v_norm_g, v_norm_b, w_spatial, b_spatial, g_lru_out, g_gmlp_out, w_out, later_weights,
           layer):
    bsz, s, d = x.shape
    t = MIX_T
    nb = MIX_NB
    assert bsz % nb == 0 and s % t == 0
    n_lg = LRU_WIDTH // LANES
    scan_rows = SUBLANES * (t // SUBLANES + 1)
    full = lambda a: pl.BlockSpec(a.shape, lambda b, i: (0,) * a.ndim)
    of_layer = lambda a: pl.BlockSpec((None,) + a.shape[1:],
                                      lambda b, i: (layer,) + (0,) * (a.ndim - 1))
    in_arrays = [g_pre, g_post, w_in, conv_w, conv_b, w_gate, b_gate, lru_a,
                 v_norm_g, v_norm_b, w_spatial, b_spatial, g_lru_out, g_gmlp_out, w_out]
    resident = lambda a: pl.BlockSpec((None,) + a.shape[1:], lambda b, i: (layer, 0, 0),
                                      pipeline_mode=pl.Buffered(1))
    in_array_specs = [resident(a) if a is w_in or a is w_out
                      else of_layer(a) if a is conv_w or a is w_spatial or a is b_spatial
                      else full(a)
                      for a in in_arrays]
    packed = lambda shape: pltpu.VMEM(shape[:-2] + (shape[-2] // 2, shape[-1]), U32)
    nt = s // t
    n_steps = (bsz // nb) * nt
    step = lambda b, i: (b * nt + i, 0)
    later_in, later_out, later_shapes = [], [], []
    for w in later_weights:
        _, k, n = w.shape
        assert k % (n_steps * 2 * SUBLANES) == 0
        later_in.append(pl.BlockSpec((None, k // n_steps, n),
                                     lambda b, i: (layer, b * nt + i, 0)))
        later_out.append(pl.BlockSpec((k // n_steps // 2, n), step))
        later_shapes.append(jax.ShapeDtypeStruct((k // 2, n), U32))
    out = pl.pallas_call(
        functools.partial(_mixer_kernel, n_later=len(later_weights)),
        grid=(bsz // nb, nt),
        in_specs=[
            pl.BlockSpec((nb, t, d), lambda b, i: (b, i, 0)),
            pl.BlockSpec(mod.shape, lambda b, i: (0, 0)),
        ] + in_array_specs + later_in,
        out_specs=[pl.BlockSpec((nb, t, d), lambda b, i: (b, i, 0))] + later_out,
        out_shape=[jax.ShapeDtypeStruct(x.shape, x.dtype)] + later_shapes,
        scratch_shapes=(
            [pltpu.VMEM((n_lg, SUBLANES + t, LANES), F32)] * nb
            + [pltpu.VMEM((n_lg, scan_rows, LANES), F32)] * (3 * nb)
            + [pltpu.VMEM((nb, SUBLANES, LRU_WIDTH), F32)] * 2
            + [packed(w_in.shape[1:]), packed(w_gate.shape), packed(w_out.shape[1:])]
            + [pltpu.VMEM((GMLP_BLOCK, GMLP_WIDTH), F32)]),
        compiler_params=pltpu.CompilerParams(
            dimension_semantics=("arbitrary", "arbitrary"), vmem_limit_bytes=VMEM_LIMIT),
        name="token_mixer",
    )(x, mod, *in_arrays, *later_weights)
    return out[0], out[1:]


def _ffn_kernel(x_ref, mod_ref, gpre_ref, gpost_ref, wup_ref, cw_ref, cb_ref, wd_ref,
                o_ref, scr_ref, tail_ref):
    t_idx = pl.program_id(1)
    t, d = x_ref.shape
    d_ff = 2 * wd_ref.shape[0]
    fc = FFN_FC
    lg = fc // LANES

    @pl.when(t_idx == 0)
    def _():
        tail_ref[...] = jnp.zeros_like(tail_ref)

    mod = mod_ref[pl.ds(pl.program_id(0), 1), :]
    sh_f = mod[:, 3 * d:4 * d]
    sc_f = mod[:, 4 * d:5 * d]
    gt_f = mod[:, 5 * d:6 * d]
    x = x_ref[...]
    hb = (_rms(x) * (gpre_ref[...] * (1.0 + sc_f)) + sh_f).astype(BF16)

    n_chunks = d_ff // fc
    halves = ((0, 1.0), (d_ff, 0.5))
    chunk_cols = lambda base, j: slice(base + j * fc, base + (j + 1) * fc)
    up_proj = lambda j: [_dot(hb, _as_bf16(wup_ref[:, chunk_cols(base, j)])) for base, _ in halves]

    def gated(j, ups):
        conv = []
        for k, (base, scale) in enumerate(halves):
            cols = chunk_cols(base, j)
            tail = tail_ref[:, cols]
            tail_ref[:, cols] = ups[k][t - SUBLANES:]
            slot = (2 * j + k) % FFN_SLOTS
            conv.append(_conv_via_scratch(scr_ref, slot * lg, tail, ups[k],
                                          cw_ref[:, cols] * scale, cb_ref[:, cols] * scale))
        return jnp.concatenate(
            [_gelu2_times(cg, cv) for cg, cv in zip(*conv)], axis=1).astype(BF16)

    acc = None
    ups = up_proj(0)
    for j in range(n_chunks):
        ups_next = up_proj(j + 1) if j + 1 < n_chunks else None
        part = _dot(gated(j, ups), _as_bf16(wd_ref[j * fc // 2:(j + 1) * fc // 2, :]))
        acc = part if acc is None else acc + part
        ups = ups_next
    o_ref[...] = x + (gt_f * gpost_ref[...]) * _rms(acc)


def _ffn(x, mod, g_pre, g_post, w_up, conv_w, conv_b, w_down, layer):
    bsz, s, d = x.shape
    t = FFN_T
    resident = lambda a: pl.BlockSpec(a.shape, lambda b, i: (0,) * a.ndim,
                                      pipeline_mode=pl.Buffered(1))
    return pl.pallas_call(
        _ffn_kernel,
        grid=(bsz, s // t),
        in_specs=[
            pl.BlockSpec((None, t, d), lambda b, i: (b, i, 0)),
            pl.BlockSpec(mod.shape, lambda b, i: (0, 0)),
            resident(g_pre), resident(g_post), resident(w_up),
            pl.BlockSpec((None,) + conv_w.shape[1:], lambda b, i: (layer, 0, 0),
                         pipeline_mode=pl.Buffered(1)),
            resident(conv_b), resident(w_down),
        ],
        out_specs=pl.BlockSpec((None, t, d), lambda b, i: (b, i, 0)),
        out_shape=jax.ShapeDtypeStruct(x.shape, x.dtype),
        scratch_shapes=[
            pltpu.VMEM((FFN_SLOTS * (FFN_FC // LANES), SUBLANES + t, LANES), F32),
            pltpu.VMEM((SUBLANES, w_up.shape[1]), F32),
        ],
        compiler_params=pltpu.CompilerParams(
            dimension_semantics=("arbitrary", "arbitrary"), vmem_limit_bytes=VMEM_LIMIT),
        name="conv_ffn",
    )(x, mod, g_pre, g_post, w_up, conv_w, conv_b, w_down)


def _gate_weights(w_rgate, w_igate):
    per_tile = GATE_TILE // LRU_HEAD_DIM
    n_tiles = LRU_WIDTH // GATE_TILE

    def head_rows(w, head, slot):
        return jnp.pad(w[0, head], ((0, 0), (slot * LRU_HEAD_DIM, (per_tile - 1 - slot) * LRU_HEAD_DIM)))

    return jnp.stack([
        jnp.concatenate([
            jnp.concatenate([head_rows(w, t * per_tile + a, a) for w in (w_rgate, w_igate)], axis=1)
            for a in range(per_tile)], axis=0)
        for t in range(n_tiles)])


def kernel(x, c, w_ada, b_ada, g_mix_pre, g_mix_post, w_in, conv_w, conv_b, w_rgate, b_rgate, w_igate, b_igate, lru_a, v_norm_g, v_norm_b, w_spatial, b_spatial, g_lru_out, g_gmlp_out, w_out, g_ffn_pre, g_ffn_post, w_up, ffn_conv_w, ffn_conv_b, w_down):
    depth = w_ada.shape[0]
    bsz, s, d = x.shape
    for l in range(depth):
        mod = _modulation(c, w_ada, b_ada[l:l + 1], l)
        row = lambda a: a[l:l + 1]
        w_gate = _gate_weights(w_rgate[l:l + 1], w_igate[l:l + 1])
        b_gate = jnp.stack([b_rgate[l], b_igate[l]]).reshape(2, -1)
        x, (w_up_p, w_down_p) = _mixer(
                   x, mod, row(g_mix_pre), row(g_mix_post), w_in,
                   conv_w, row(conv_b), w_gate, b_gate, row(lru_a),
                   row(v_norm_g), row(v_norm_b), w_spatial, b_spatial,
                   row(g_lru_out), row(g_gmlp_out), w_out,
                   later_weights=[w_up, w_down], layer=l)
        x = _ffn(x, mod, row(g_ffn_pre), row(g_ffn_post), w_up_p,
                 ffn_conv_w, row(ffn_conv_b), w_down_p, layer=l)
    return x
```

```python
import functools

import jax
import jax.numpy as jnp
from jax import lax
from jax.experimental import pallas as pl
from jax.experimental.pallas import tpu as pltpu

CHUNK = 64
LRU_WIDTH = 512
LRU_HEADS = 8
LRU_HEAD_DIM = LRU_WIDTH // LRU_HEADS
LRU_CONV_WIDTH = 4
LRU_C = 8.0
GMLP_WIDTH = 512
GMLP_GROUPS = 4
GMLP_GROUP_DIM = GMLP_WIDTH // GMLP_GROUPS
GMLP_BLOCK = 128
FFN_CONV_WIDTH = 3
N_MOD = 6
EPS = 1e-6

SUBLANES = 8
LANES = 128
GATE_TILE = 256

MOD_ROWS = 128
MOD_STREAMS = 2
MIX_T = 256
MIX_NB = 4
FFN_T = 512
FFN_FC = 768
FFN_SLOTS = 8
VMEM_LIMIT = 60 * 1024 * 1024

F32 = jnp.float32
BF16 = jnp.bfloat16
U32 = jnp.uint32


def _dot(a, b):
    return jnp.dot(a, b, preferred_element_type=F32)


def _pack_rows(w):
    return pltpu.bitcast(w.astype(BF16), U32)


def _as_bf16(packed):
    return pltpu.bitcast(packed, BF16)


def _rms(x):
    return x * lax.rsqrt(jnp.mean(x * x, axis=-1, keepdims=True) + EPS)


def _gelu(x):
    c0 = 0.7978845608028654
    hx = 0.5 * x
    return hx + hx * jnp.tanh(x * (c0 + (c0 * 0.044715) * (x * x)))


def _gelu2_times(x, v):
    c0 = 0.7978845608028654
    z = x * (c0 + (c0 * 0.044715) * (x * x))
    return (x * v) * (1.0 + jnp.tanh(z))


def _rows(ref, lead, start, size):
    return ref[pl.ds(lead, 1, stride=2), pl.ds(start, size), :][0]


def _put_rows(ref, lead, start, val):
    ref[pl.ds(lead, 1, stride=2), pl.ds(start, val.shape[0]), :] = val[None]


def _linear_scan(a, b, state_ref, a_scr, b_scr, h_scr):
    t = a[0].shape[0]
    seg = t // SUBLANES
    pitch = seg + 1 - seg % 2
    n = len(a)
    for c in range(n):
        for s in range(SUBLANES):
            _put_rows(a_scr, c, s * pitch, a[c][s * seg:(s + 1) * seg])
            _put_rows(b_scr, c, s * pitch, b[c][s * seg:(s + 1) * seg])
    step = lambda ref, c, j: ref[c, pl.ds(j, SUBLANES, stride=pitch), :]
    row = lax.broadcasted_iota(jnp.int32, (SUBLANES, LANES), 0)
    shifted = lambda v, sh, fill: jnp.where(row >= sh, pltpu.roll(v, sh, axis=0), fill)

    prod = [None] * n
    end = [None] * n
    for j in range(seg):
        for c in range(n):
            aj, bj = step(a_scr, c, j), step(b_scr, c, j)
            prod[c] = aj if j == 0 else aj * prod[c]
            end[c] = bj if j == 0 else aj * end[c] + bj

    h = []
    for c in range(n):
        cols = slice(c * LANES, (c + 1) * LANES)
        h0 = state_ref[SUBLANES - 1:SUBLANES, cols]
        p = prod[c]
        e = end[c] + jnp.where(row == 0, p * h0, 0.0)
        for sh in (1, 2, 4):
            e = e + p * shifted(e, sh, 0.0)
            if sh < 4:
                p = p * shifted(p, sh, 1.0)
        state_ref[:, cols] = e
        h.append(shifted(e, 1, h0))
    for j in range(seg):
        for c in range(n):
            h[c] = step(a_scr, c, j) * h[c] + step(b_scr, c, j)
            h_scr[c, pl.ds(j, SUBLANES, stride=pitch), :] = h[c]
    return [jnp.concatenate([_rows(h_scr, c, s * pitch, seg) for s in range(SUBLANES)], axis=0)
            for c in range(n)]


def _conv_via_scratch(scr_ref, lead0, tail, x, w, b):
    k_width = w.shape[0]
    t, c = x.shape
    outs = []
    for g in range(c // LANES):
        cols = slice(g * LANES, (g + 1) * LANES)
        scr_ref[lead0 + g, 0:SUBLANES, :] = tail[:, cols]
        scr_ref[lead0 + g, SUBLANES:SUBLANES + t, :] = x[:, cols]
    for g in range(c // LANES):
        cols = slice(g * LANES, (g + 1) * LANES)
        out = x[:, cols] * w[k_width - 1:k_width, cols] + b[:, cols]
        for k in range(k_width - 1):
            shifted = _rows(scr_ref, lead0 + g, SUBLANES - (k_width - 1 - k), t)
            out = out + shifted * w[k:k + 1, cols]
        outs.append(out)
    return outs


def _mod_kernel(c_ref, *rest):
    w_refs, b_ref, o_ref = rest[:-2], rest[-2], rest[-1]
    c = c_ref[...]
    rows = c.shape[0]
    c_act = c * jax.nn.sigmoid(c)
    pad = (-rows) % SUBLANES
    if pad:
        c_act = jnp.concatenate([c_act, jnp.zeros((pad, c.shape[1]), F32)], axis=0)
    c_act = c_act.astype(BF16)
    bk = w_refs[0].shape[0]
    parts = [_dot(c_act[:, j * bk:(j + 1) * bk], w_ref[...].astype(BF16))[:rows]
             for j, w_ref in enumerate(w_refs)]

    @pl.when(pl.program_id(0) == 0)
    def _():
        o_ref[...] = functools.reduce(jnp.add, parts, b_ref[...])

    @pl.when(pl.program_id(0) > 0)
    def _():
        o_ref[...] = functools.reduce(jnp.add, parts, o_ref[...])


def _modulation(c, w_ada, b_ada, layer):
    rows, d = c.shape
    n = w_ada.shape[2]
    bk = MOD_ROWS
    ns = MOD_STREAMS
    assert d % (bk * ns) == 0
    row_block = lambda j: pl.BlockSpec((None, bk, n), lambda k: (layer, k * ns + j, 0))
    return pl.pallas_call(
        _mod_kernel,
        grid=(d // (bk * ns),),
        in_specs=[pl.BlockSpec((rows, bk * ns), lambda k: (0, k))]
                 + [row_block(j) for j in range(ns)]
                 + [pl.BlockSpec((1, n), lambda k: (0, 0))],
        out_specs=pl.BlockSpec((rows, n), lambda k: (0, 0)),
        out_shape=jax.ShapeDtypeStruct((rows, n), F32),
        compiler_params=pltpu.CompilerParams(
            dimension_semantics=("arbitrary",), vmem_limit_bytes=VMEM_LIMIT),
        name="adaln_mod",
    )(c, *[w_ada] * ns, b_ada)


def _mixer_kernel(x_ref, mod_ref, gpre_ref, gpost_ref, win_ref, cw_ref, cb_ref,
                  wgate_ref, bgate_ref, lrua_ref, vng_ref, vnb_ref, wsp_ref, bsp_ref,
                  glru_ref, ggmlp_ref, wout_ref, *rest, n_later):
    later_in, o_ref, later_out = rest[:n_later], rest[n_later], rest[n_later + 1:2 * n_later + 1]
    scratch = rest[2 * n_later + 1:]
    for w_ref, p_ref in zip(later_in, later_out):
        p_ref[...] = _pack_rows(w_ref[...])
    nb_tiles = x_ref.shape[0]
    (tail_ref, state_ref, win_packed_ref, wgate_packed_ref, wout_packed_ref,
     bsp_full_ref) = scratch[4 * nb_tiles:]

    @pl.when((pl.program_id(0) == 0) & (pl.program_id(1) == 0))
    def _():
        win_packed_ref[...] = _pack_rows(win_ref[...])
        for j in range(wgate_ref.shape[0]):
            wgate_packed_ref[j] = _pack_rows(wgate_ref[j])
        wout_packed_ref[...] = _pack_rows(wout_ref[...])
        diag = (lax.broadcasted_iota(jnp.int32, (GMLP_BLOCK, GMLP_BLOCK), 0)
                == lax.broadcasted_iota(jnp.int32, (GMLP_BLOCK, GMLP_BLOCK), 1))
        for g in range(GMLP_GROUPS):
            col = jnp.sum(jnp.where(diag, bsp_ref[g:g + 1, :], -0.0), axis=1, keepdims=True)
            bsp_full_ref[:, g * GMLP_GROUP_DIM:(g + 1) * GMLP_GROUP_DIM] = jnp.broadcast_to(
                col, (GMLP_BLOCK, GMLP_GROUP_DIM))

    @pl.when(pl.program_id(1) == 0)
    def _():
        tail_ref[...] = jnp.zeros_like(tail_ref)
        state_ref[...] = jnp.zeros_like(state_ref)

    mod_row = lambda k: mod_ref[pl.ds(pl.program_id(0) * nb_tiles + k, 1), :]
    project = lambda k: _mix_in(x_ref.at[k], mod_row(k), gpre_ref, win_packed_ref)
    z = project(0)
    for k in range(nb_tiles):
        conv_scr, a_scr, b_scr, h_scr = scratch[k:4 * nb_tiles:nb_tiles]
        xc, gates = _mix_gates(z[0], cw_ref, cb_ref, wgate_packed_ref, conv_scr, tail_ref.at[k])
        sp_groups = _mix_positions(z[3], vng_ref, vnb_ref, wsp_ref)
        z_next = project(k + 1) if k + 1 < nb_tiles else None
        y = _mix_body(z, xc, gates, sp_groups, bgate_ref, lrua_ref, bsp_full_ref,
                      glru_ref, ggmlp_ref, a_scr, b_scr, h_scr, state_ref.at[k])
        _mix_out(y, x_ref.at[k], mod_row(k), gpost_ref, wout_packed_ref, o_ref.at[k])
        z = z_next


def _mix_in(x_ref, mod, gpre_ref, win_ref):
    d = x_ref.shape[1]
    sh_m = mod[:, 0:d]
    sc_m = mod[:, d:2 * d]
    hb = (_rms(x_ref[...]) * (gpre_ref[...] * (1.0 + sc_m)) + sh_m).astype(BF16)
    widths = (LRU_WIDTH, LRU_WIDTH, GMLP_WIDTH, GMLP_WIDTH)
    starts = [sum(widths[:q]) for q in range(len(widths))]
    return [_dot(hb, _as_bf16(win_ref[:, c0:c0 + w])) for c0, w in zip(starts, widths)]


def _mix_gates(lru_x, cw_ref, cb_ref, wgate_ref, conv_scr, tail_ref):
    t = lru_x.shape[0]
    tail = tail_ref[...]
    tail_ref[...] = lru_x[t - SUBLANES:]
    xc = jnp.concatenate(
        _conv_via_scratch(conv_scr, 0, tail, lru_x, cw_ref[...], cb_ref[...]), axis=1)
    xcb = xc.astype(BF16)
    gates = [_dot(xcb[:, j * GATE_TILE:(j + 1) * GATE_TILE], _as_bf16(wgate_ref[j]))
             for j in range(LRU_WIDTH // GATE_TILE)]
    return xc, gates


def _mix_out(y, x_ref, mod, gpost_ref, wout_ref, o_ref):
    d = x_ref.shape[1]
    gt_m = mod[:, 2 * d:3 * d]
    y = _dot(y, _as_bf16(wout_ref[...]))
    o_ref[...] = x_ref[...] + (gt_m * gpost_ref[...]) * _rms(y)


def _mix_positions(g_v, vng_ref, vnb_ref, wsp_ref):
    t = g_v.shape[0]
    gv = _gelu(g_v)
    mu = jnp.mean(gv, axis=-1, keepdims=True)
    cen = gv - mu
    var = jnp.mean(cen * cen, axis=-1, keepdims=True)
    v = cen * lax.rsqrt(var + EPS) * vng_ref[...] + vnb_ref[...]
    vb = v.astype(BF16)
    nb = t // GMLP_BLOCK
    pi = lax.broadcasted_iota(jnp.int32, (GMLP_BLOCK, GMLP_BLOCK), 0) // CHUNK
    pj = lax.broadcasted_iota(jnp.int32, (GMLP_BLOCK, GMLP_BLOCK), 1) // CHUNK
    mask = pj <= pi
    sp_groups = []
    for g in range(GMLP_GROUPS):
        ws = jnp.where(mask, 0.5 * wsp_ref[g], 0.0).astype(BF16)
        cols = slice(g * GMLP_GROUP_DIM, (g + 1) * GMLP_GROUP_DIM)
        rhs = jnp.concatenate(
            [vb[n * GMLP_BLOCK:(n + 1) * GMLP_BLOCK, cols] for n in range(nb)], axis=1)
        sp_groups.append(_dot(ws, rhs))
    return sp_groups


def _mix_body(z, xc, gates, sp_groups, bgate_ref, lrua_ref, bsp_ref,
              glru_ref, ggmlp_ref, a_scr, b_scr, h_scr, state_ref):
    _, lru_gate, g_u, _ = z
    t = lru_gate.shape[0]
    slabs = lambda v: [v[:, c * LANES:(c + 1) * LANES] for c in range(v.shape[1] // LANES)]

    r_pre = jnp.concatenate([g[:, :GATE_TILE] for g in gates], axis=1)
    i_pre = jnp.concatenate([g[:, GATE_TILE:] for g in gates], axis=1)
    r_gate = jax.nn.sigmoid(r_pre + bgate_ref[0:1, :])
    i_gate = jax.nn.sigmoid(i_pre + bgate_ref[1:2, :])
    neg_a = -lrua_ref[...]
    softplus = jnp.maximum(neg_a, 0.0) + jnp.log1p(jnp.exp(-jnp.abs(neg_a)))
    log_a = r_gate * ((-LRU_C) * softplus)
    a = jnp.exp(log_a)
    th = jnp.tanh(log_a)
    q = (-2.0 * th) / (1.0 - th)
    mult = jnp.where(q > 0.0, q * lax.rsqrt(q), 0.0)
    bx = (mult * xc) * i_gate
    hs = _linear_scan(slabs(a), slabs(bx), state_ref, a_scr, b_scr, h_scr)
    y_lru = 0.5 * _gelu2_times(lru_gate, jnp.concatenate(hs, axis=1))
    yl = _rms(y_lru) * glru_ref[...]

    nb = t // GMLP_BLOCK
    bsp = 0.5 * bsp_ref[...]
    sp_rows = []
    for n in range(nb):
        blk = jnp.concatenate(
            [sg[:, n * GMLP_GROUP_DIM:(n + 1) * GMLP_GROUP_DIM] for sg in sp_groups], axis=1)
        sp_rows.append(blk + bsp)
    sp_half = jnp.concatenate(sp_rows, axis=0)
    y_gmlp = _gelu2_times(g_u, sp_half)
    yg = _rms(y_gmlp) * ggmlp_ref[...]

    return jnp.concatenate([yl, yg], axis=1).astype(BF16)


def _mixer(x, mod, g_pre, g_post, w_in, conv_w, conv_b, w_gate, b_gate, lru_a,
           v_norm_g, v_norm_b, w_spatial, b_spatial, g_lru_out, g_gmlp_out, w_out, later_weights,
           layer):
    bsz, s, d = x.shape
    t = MIX_T
    nb = MIX_NB
    assert bsz % nb == 0 and s % t == 0
    n_lg = LRU_WIDTH // LANES
    scan_rows = SUBLANES * (t // SUBLANES + 1)
    full = lambda a: pl.BlockSpec(a.shape, lambda b, i: (0,) * a.ndim)
    of_layer = lambda a: pl.BlockSpec((None,) + a.shape[1:],
                                      lambda b, i: (layer,) + (0,) * (a.ndim - 1))
    in_arrays = [g_pre, g_post, w_in, conv_w, conv_b, w_gate, b_gate, lru_a,
                 v_norm_g, v_norm_b, w_spatial, b_spatial, g_lru_out, g_gmlp_out, w_out]
    resident = lambda a: pl.BlockSpec((None,) + a.shape[1:], lambda b, i: (layer, 0, 0),
                                      pipeline_mode=pl.Buffered(1))
    in_array_specs = [resident(a) if a is w_in or a is w_out
                      else of_layer(a) if a is conv_w or a is w_spatial or a is b_spatial
                      else full(a)
                      for a in in_arrays]
    packed = lambda shape: pltpu.VMEM(shape[:-2] + (shape[-2] // 2, shape[-1]), U32)
    nt = s // t
    n_steps = (bsz // nb) * nt
    step = lambda b, i: (b * nt + i, 0)
    later_in, later_out, later_shapes = [], [], []
    for w in later_weights:
        _, k, n = w.shape
        assert k % (n_steps * 2 * SUBLANES) == 0
        later_in.append(pl.BlockSpec((None, k // n_steps, n),
                                     lambda b, i: (layer, b * nt + i, 0)))
        later_out.append(pl.BlockSpec((k // n_steps // 2, n), step))
        later_shapes.append(jax.ShapeDtypeStruct((k // 2, n), U32))
    out = pl.pallas_call(
        functools.partial(_mixer_kernel, n_later=len(later_weights)),
        grid=(bsz // nb, nt),
        in_specs=[
            pl.BlockSpec((nb, t, d), lambda b, i: (b, i, 0)),
            pl.BlockSpec(mod.shape, lambda b, i: (0, 0)),
        ] + in_array_specs + later_in,
        out_specs=[pl.BlockSpec((nb, t, d), lambda b, i: (b, i, 0))] + later_out,
        out_shape=[jax.ShapeDtypeStruct(x.shape, x.dtype)] + later_shapes,
        scratch_shapes=(
            [pltpu.VMEM((n_lg, SUBLANES + t, LANES), F32)] * nb
            + [pltpu.VMEM((n_lg, scan_rows, LANES), F32)] * (3 * nb)
            + [pltpu.VMEM((nb, SUBLANES, LRU_WIDTH), F32)] * 2
            + [packed(w_in.shape[1:]), packed(w_gate.shape), packed(w_out.shape[1:])]
            + [pltpu.VMEM((GMLP_BLOCK, GMLP_WIDTH), F32)]),
        compiler_params=pltpu.CompilerParams(
            dimension_semantics=("arbitrary", "arbitrary"), vmem_limit_bytes=VMEM_LIMIT),
        name="token_mixer",
    )(x, mod, *in_arrays, *later_weights)
    return out[0], out[1:]


def _ffn_kernel(x_ref, mod_ref, gpre_ref, gpost_ref, wup_ref, cw_ref, cb_ref, wd_ref,
                o_ref, scr_ref, tail_ref):
    t_idx = pl.program_id(1)
    t, d = x_ref.shape
    d_ff = 2 * wd_ref.shape[0]
    fc = FFN_FC
    lg = fc // LANES

    @pl.when(t_idx == 0)
    def _():
        tail_ref[...] = jnp.zeros_like(tail_ref)

    mod = mod_ref[pl.ds(pl.program_id(0), 1), :]
    sh_f = mod[:, 3 * d:4 * d]
    sc_f = mod[:, 4 * d:5 * d]
    gt_f = mod[:, 5 * d:6 * d]
    x = x_ref[...]
    hb = (_rms(x) * (gpre_ref[...] * (1.0 + sc_f)) + sh_f).astype(BF16)

    n_chunks = d_ff // fc
    halves = ((0, 1.0), (d_ff, 0.5))
    chunk_cols = lambda base, j: slice(base + j * fc, base + (j + 1) * fc)
    up_proj = lambda j: [_dot(hb, _as_bf16(wup_ref[:, chunk_cols(base, j)])) for base, _ in halves]

    def gated(j, ups):
        conv = []
        for k, (base, scale) in enumerate(halves):
            cols = chunk_cols(base, j)
            tail = tail_ref[:, cols]
            tail_ref[:, cols] = ups[k][t - SUBLANES:]
            slot = (2 * j + k) % FFN_SLOTS
            conv.append(_conv_via_scratch(scr_ref, slot * lg, tail, ups[k],
                                          cw_ref[:, cols] * scale, cb_ref[:, cols] * scale))
        return jnp.concatenate(
            [_gelu2_times(cg, cv) for cg, cv in zip(*conv)], axis=1).astype(BF16)

    acc = None
    ups = up_proj(0)
    for j in range(n_chunks):
        ups_next = up_proj(j + 1) if j + 1 < n_chunks else None
        part = _dot(gated(j, ups), _as_bf16(wd_ref[j * fc // 2:(j + 1) * fc // 2, :]))
        acc = part if acc is None else acc + part
        ups = ups_next
    o_ref[...] = x + (gt_f * gpost_ref[...]) * _rms(acc)


def _ffn(x, mod, g_pre, g_post, w_up, conv_w, conv_b, w_down, layer):
    bsz, s, d = x.shape
    t = FFN_T
    resident = lambda a: pl.BlockSpec(a.shape, lambda b, i: (0,) * a.ndim,
                                      pipeline_mode=pl.Buffered(1))
    return pl.pallas_call(
        _ffn_kernel,
        grid=(bsz, s // t),
        in_specs=[
            pl.BlockSpec((None, t, d), lambda b, i: (b, i, 0)),
            pl.BlockSpec(mod.shape, lambda b, i: (0, 0)),
            resident(g_pre), resident(g_post), resident(w_up),
            pl.BlockSpec((None,) + conv_w.shape[1:], lambda b, i: (layer, 0, 0),
                         pipeline_mode=pl.Buffered(1)),
            resident(conv_b), resident(w_down),
        ],
        out_specs=pl.BlockSpec((None, t, d), lambda b, i: (b, i, 0)),
        out_shape=jax.ShapeDtypeStruct(x.shape, x.dtype),
        scratch_shapes=[
            pltpu.VMEM((FFN_SLOTS * (FFN_FC // LANES), SUBLANES + t, LANES), F32),
            pltpu.VMEM((SUBLANES, w_up.shape[1]), F32),
        ],
        compiler_params=pltpu.CompilerParams(
            dimension_semantics=("arbitrary", "arbitrary"), vmem_limit_bytes=VMEM_LIMIT),
        name="conv_ffn",
    )(x, mod, g_pre, g_post, w_up, conv_w, conv_b, w_down)


def _gate_weights(w_rgate, w_igate):
    per_tile = GATE_TILE // LRU_HEAD_DIM
    n_tiles = LRU_WIDTH // GATE_TILE

    def head_rows(w, head, slot):
        return jnp.pad(w[0, head], ((0, 0), (slot * LRU_HEAD_DIM, (per_tile - 1 - slot) * LRU_HEAD_DIM)))

    return jnp.stack([
        jnp.concatenate([
            jnp.concatenate([head_rows(w, t * per_tile + a, a) for w in (w_rgate, w_igate)], axis=1)
            for a in range(per_tile)], axis=0)
        for t in range(n_tiles)])


def kernel(x, c, w_ada, b_ada, g_mix_pre, g_mix_post, w_in, conv_w, conv_b, w_rgate, b_rgate, w_igate, b_igate, lru_a, v_norm_g, v_norm_b, w_spatial, b_spatial, g_lru_out, g_gmlp_out, w_out, g_ffn_pre, g_ffn_post, w_up, ffn_conv_w, ffn_conv_b, w_down):
    depth = w_ada.shape[0]
    bsz, s, d = x.shape
    for l in range(depth):
        mod = _modulation(c, w_ada, b_ada[l:l + 1], l)
        row = lambda a: a[l:l + 1]
        w_gate = _gate_weights(w_rgate[l:l + 1], w_igate[l:l + 1])
        b_gate = jnp.stack([b_rgate[l], b_igate[l]]).reshape(2, -1)
        x, (w_up_p, w_down_p) = _mixer(
                   x, mod, row(g_mix_pre), row(g_mix_post), w_in,
                   conv_w, row(conv_b), w_gate, b_gate, row(lru_a),
                   row(v_norm_g), row(v_norm_b), w_spatial, b_spatial,
                   row(g_lru_out), row(g_gmlp_out), w_out,
                   later_weights=[w_up, w_down], layer=l)
        x = _ffn(x, mod, row(g_ffn_pre), row(g_ffn_post), w_up_p,
                 ffn_conv_w, row(ffn_conv_b), w_down_p, layer=l)
    return x
```

```python
import functools

import jax
import jax.numpy as jnp
from jax import lax
from jax.experimental import pallas as pl
from jax.experimental.pallas import tpu as pltpu

CHUNK = 64
LRU_WIDTH = 512
LRU_HEADS = 8
LRU_HEAD_DIM = LRU_WIDTH // LRU_HEADS
LRU_CONV_WIDTH = 4
LRU_C = 8.0
GMLP_WIDTH = 512
GMLP_GROUPS = 4
GMLP_GROUP_DIM = GMLP_WIDTH // GMLP_GROUPS
GMLP_BLOCK = 128
FFN_CONV_WIDTH = 3
N_MOD = 6
EPS = 1e-6

SUBLANES = 8
LANES = 128
GATE_TILE = 256

MOD_ROWS = 128
MOD_STREAMS = 2
MIX_T = 256
MIX_NB = 4
FFN_T = 512
FFN_FC = 768
FFN_SLOTS = 4
VMEM_LIMIT = 60 * 1024 * 1024

F32 = jnp.float32
BF16 = jnp.bfloat16
U32 = jnp.uint32


def _dot(a, b):
    return jnp.dot(a, b, preferred_element_type=F32)


def _pack_rows(w):
    return pltpu.bitcast(w.astype(BF16), U32)


def _as_bf16(packed):
    return pltpu.bitcast(packed, BF16)


def _rms(x):
    return x * lax.rsqrt(jnp.mean(x * x, axis=-1, keepdims=True) + EPS)


def _gelu(x):
    c0 = 0.7978845608028654
    hx = 0.5 * x
    return hx + hx * jnp.tanh(x * (c0 + (c0 * 0.044715) * (x * x)))


def _gelu2_times(x, v):
    c0 = 0.7978845608028654
    z = x * (c0 + (c0 * 0.044715) * (x * x))
    return (x * v) * (1.0 + jnp.tanh(z))


def _rows(ref, lead, start, size):
    return ref[pl.ds(lead, 1, stride=2), pl.ds(start, size), :][0]


def _put_rows(ref, lead, start, val):
    ref[pl.ds(lead, 1, stride=2), pl.ds(start, val.shape[0]), :] = val[None]


def _linear_scan(a, b, state_ref, a_scr, b_scr, h_scr):
    t = a[0].shape[0]
    seg = t // SUBLANES
    pitch = seg + 1 - seg % 2
    n = len(a)
    for c in range(n):
        for s in range(SUBLANES):
            _put_rows(a_scr, c, s * pitch, a[c][s * seg:(s + 1) * seg])
            _put_rows(b_scr, c, s * pitch, b[c][s * seg:(s + 1) * seg])
    step = lambda ref, c, j: ref[c, pl.ds(j, SUBLANES, stride=pitch), :]
    row = lax.broadcasted_iota(jnp.int32, (SUBLANES, LANES), 0)
    shifted = lambda v, sh, fill: jnp.where(row >= sh, pltpu.roll(v, sh, axis=0), fill)

    prod = [None] * n
    end = [None] * n
    for j in range(seg):
        for c in range(n):
            aj, bj = step(a_scr, c, j), step(b_scr, c, j)
            prod[c] = aj if j == 0 else aj * prod[c]
            end[c] = bj if j == 0 else aj * end[c] + bj

    h = []
    for c in range(n):
        cols = slice(c * LANES, (c + 1) * LANES)
        h0 = state_ref[SUBLANES - 1:SUBLANES, cols]
        p = prod[c]
        e = end[c] + jnp.where(row == 0, p * h0, 0.0)
        for sh in (1, 2, 4):
            e = e + p * shifted(e, sh, 0.0)
            if sh < 4:
                p = p * shifted(p, sh, 1.0)
        state_ref[:, cols] = e
        h.append(shifted(e, 1, h0))
    for j in range(seg):
        for c in range(n):
            h[c] = step(a_scr, c, j) * h[c] + step(b_scr, c, j)
            h_scr[c, pl.ds(j, SUBLANES, stride=pitch), :] = h[c]
    return [jnp.concatenate([_rows(h_scr, c, s * pitch, seg) for s in range(SUBLANES)], axis=0)
            for c in range(n)]


def _conv_via_scratch(scr_ref, lead0, tail, x, w, b):
    k_width = w.shape[0]
    t, c = x.shape
    outs = []
    for g in range(c // LANES):
        cols = slice(g * LANES, (g + 1) * LANES)
        scr_ref[lead0 + g, 0:SUBLANES, :] = tail[:, cols]
        scr_ref[lead0 + g, SUBLANES:SUBLANES + t, :] = x[:, cols]
    for g in range(c // LANES):
        cols = slice(g * LANES, (g + 1) * LANES)
        out = x[:, cols] * w[k_width - 1:k_width, cols] + b[:, cols]
        for k in range(k_width - 1):
            shifted = _rows(scr_ref, lead0 + g, SUBLANES - (k_width - 1 - k), t)
            out = out + shifted * w[k:k + 1, cols]
        outs.append(out)
    return outs


def _mod_kernel(c_ref, *rest):
    w_refs, b_ref, o_ref = rest[:-2], rest[-2], rest[-1]
    c = c_ref[...]
    rows = c.shape[0]
    c_act = c * jax.nn.sigmoid(c)
    pad = (-rows) % SUBLANES
    if pad:
        c_act = jnp.concatenate([c_act, jnp.zeros((pad, c.shape[1]), F32)], axis=0)
    c_act = c_act.astype(BF16)
    bk = w_refs[0].shape[0]
    parts = [_dot(c_act[:, j * bk:(j + 1) * bk], w_ref[...].astype(BF16))[:rows]
             for j, w_ref in enumerate(w_refs)]

    @pl.when(pl.program_id(0) == 0)
    def _():
        o_ref[...] = functools.reduce(jnp.add, parts, b_ref[...])

    @pl.when(pl.program_id(0) > 0)
    def _():
        o_ref[...] = functools.reduce(jnp.add, parts, o_ref[...])


def _modulation(c, w_ada, b_ada, layer):
    rows, d = c.shape
    n = w_ada.shape[2]
    bk = MOD_ROWS
    ns = MOD_STREAMS
    assert d % (bk * ns) == 0
    row_block = lambda j: pl.BlockSpec((None, bk, n), lambda k: (layer, k * ns + j, 0))
    return pl.pallas_call(
        _mod_kernel,
        grid=(d // (bk * ns),),
        in_specs=[pl.BlockSpec((rows, bk * ns), lambda k: (0, k))]
                 + [row_block(j) for j in range(ns)]
                 + [pl.BlockSpec((1, n), lambda k: (0, 0))],
        out_specs=pl.BlockSpec((rows, n), lambda k: (0, 0)),
        out_shape=jax.ShapeDtypeStruct((rows, n), F32),
        compiler_params=pltpu.CompilerParams(
            dimension_semantics=("arbitrary",), vmem_limit_bytes=VMEM_LIMIT),
        name="adaln_mod",
    )(c, *[w_ada] * ns, b_ada)


def _mixer_kernel(x_ref, mod_ref, gpre_ref, gpost_ref, win_ref, cw_ref, cb_ref,
                  wgate_ref, bgate_ref, lrua_ref, vng_ref, vnb_ref, wsp_ref, bsp_ref,
                  glru_ref, ggmlp_ref, wout_ref, *rest, n_later):
    later_in, o_ref, later_out = rest[:n_later], rest[n_later], rest[n_later + 1:2 * n_later + 1]
    scratch = rest[2 * n_later + 1:]
    nb_tiles = x_ref.shape[0]
    (tail_ref, state_ref, win_packed_ref, wgate_packed_ref, wout_packed_ref,
     bsp_full_ref) = scratch[4 * nb_tiles:]

    @pl.when((pl.program_id(0) == 0) & (pl.program_id(1) == 0))
    def _():
        win_packed_ref[...] = _pack_rows(win_ref[...])
        for j in range(wgate_ref.shape[0]):
            wgate_packed_ref[j] = _pack_rows(wgate_ref[j])
        wout_packed_ref[...] = _pack_rows(wout_ref[...])
        diag = (lax.broadcasted_iota(jnp.int32, (GMLP_BLOCK, GMLP_BLOCK), 0)
                == lax.broadcasted_iota(jnp.int32, (GMLP_BLOCK, GMLP_BLOCK), 1))
        for g in range(GMLP_GROUPS):
            col = jnp.sum(jnp.where(diag, bsp_ref[g:g + 1, :], -0.0), axis=1, keepdims=True)
            bsp_full_ref[:, g * GMLP_GROUP_DIM:(g + 1) * GMLP_GROUP_DIM] = jnp.broadcast_to(
                col, (GMLP_BLOCK, GMLP_GROUP_DIM))

    @pl.when(pl.program_id(1) == 0)
    def _():
        tail_ref[...] = jnp.zeros_like(tail_ref)
        state_ref[...] = jnp.zeros_like(state_ref)

    mod_row = lambda k: mod_ref[pl.ds(pl.program_id(0) * nb_tiles + k, 1), :]
    project = lambda k: _mix_in(x_ref.at[k], mod_row(k), gpre_ref, win_packed_ref)
    z = project(0)
    for k in range(nb_tiles):
        conv_scr, a_scr, b_scr, h_scr = scratch[k:4 * nb_tiles:nb_tiles]
        xc, gates = _mix_gates(z[0], cw_ref, cb_ref, wgate_packed_ref, conv_scr, tail_ref.at[k])
        sp_groups = _mix_positions(z[3], vng_ref, vnb_ref, wsp_ref)
        z_next = project(k + 1) if k + 1 < nb_tiles else None
        y = _mix_body(z, xc, gates, sp_groups, bgate_ref, lrua_ref, bsp_full_ref,
                      glru_ref, ggmlp_ref, a_scr, b_scr, h_scr, state_ref.at[k])
        _mix_out(y, x_ref.at[k], mod_row(k), gpost_ref, wout_packed_ref, o_ref.at[k])
        z = z_next
    for w_ref, p_ref in zip(later_in, later_out):
        p_ref[...] = _pack_rows(w_ref[...])


def _mix_in(x_ref, mod, gpre_ref, win_ref):
    d = x_ref.shape[1]
    sh_m = mod[:, 0:d]
    sc_m = mod[:, d:2 * d]
    hb = (_rms(x_ref[...]) * (gpre_ref[...] * (1.0 + sc_m)) + sh_m).astype(BF16)
    widths = (LRU_WIDTH, LRU_WIDTH, GMLP_WIDTH, GMLP_WIDTH)
    starts = [sum(widths[:q]) for q in range(len(widths))]
    return [_dot(hb, _as_bf16(win_ref[:, c0:c0 + w])) for c0, w in zip(starts, widths)]


def _mix_gates(lru_x, cw_ref, cb_ref, wgate_ref, conv_scr, tail_ref):
    t = lru_x.shape[0]
    tail = tail_ref[...]
    tail_ref[...] = lru_x[t - SUBLANES:]
    xc = jnp.concatenate(
        _conv_via_scratch(conv_scr, 0, tail, lru_x, cw_ref[...], cb_ref[...]), axis=1)
    xcb = xc.astype(BF16)
    gates = [_dot(xcb[:, j * GATE_TILE:(j + 1) * GATE_TILE], _as_bf16(wgate_ref[j]))
             for j in range(LRU_WIDTH // GATE_TILE)]
    return xc, gates


def _mix_out(y, x_ref, mod, gpost_ref, wout_ref, o_ref):
    d = x_ref.shape[1]
    gt_m = mod[:, 2 * d:3 * d]
    y = _dot(y, _as_bf16(wout_ref[...]))
    o_ref[...] = x_ref[...] + (gt_m * gpost_ref[...]) * _rms(y)


def _mix_positions(g_v, vng_ref, vnb_ref, wsp_ref):
    t = g_v.shape[0]
    gv = _gelu(g_v)
    mu = jnp.mean(gv, axis=-1, keepdims=True)
    cen = gv - mu
    var = jnp.mean(cen * cen, axis=-1, keepdims=True)
    v = cen * lax.rsqrt(var + EPS) * vng_ref[...] + vnb_ref[...]
    vb = v.astype(BF16)
    nb = t // GMLP_BLOCK
    pi = lax.broadcasted_iota(jnp.int32, (GMLP_BLOCK, GMLP_BLOCK), 0) // CHUNK
    pj = lax.broadcasted_iota(jnp.int32, (GMLP_BLOCK, GMLP_BLOCK), 1) // CHUNK
    mask = pj <= pi
    sp_groups = []
    for g in range(GMLP_GROUPS):
        ws = jnp.where(mask, 0.5 * wsp_ref[g], 0.0).astype(BF16)
        cols = slice(g * GMLP_GROUP_DIM, (g + 1) * GMLP_GROUP_DIM)
        rhs = jnp.concatenate(
            [vb[n * GMLP_BLOCK:(n + 1) * GMLP_BLOCK, cols] for n in range(nb)], axis=1)
        sp_groups.append(_dot(ws, rhs))
    return sp_groups


def _mix_body(z, xc, gates, sp_groups, bgate_ref, lrua_ref, bsp_ref,
              glru_ref, ggmlp_ref, a_scr, b_scr, h_scr, state_ref):
    _, lru_gate, g_u, _ = z
    t = lru_gate.shape[0]
    slabs = lambda v: [v[:, c * LANES:(c + 1) * LANES] for c in range(v.shape[1] // LANES)]

    r_pre = jnp.concatenate([g[:, :GATE_TILE] for g in gates], axis=1)
    i_pre = jnp.concatenate([g[:, GATE_TILE:] for g in gates], axis=1)
    r_gate = jax.nn.sigmoid(r_pre + bgate_ref[0:1, :])
    i_gate = jax.nn.sigmoid(i_pre + bgate_ref[1:2, :])
    neg_a = -lrua_ref[...]
    softplus = jnp.maximum(neg_a, 0.0) + jnp.log1p(jnp.exp(-jnp.abs(neg_a)))
    log_a = r_gate * ((-LRU_C) * softplus)
    a = jnp.exp(log_a)
    th = jnp.tanh(log_a)
    q = (-2.0 * th) / (1.0 - th)
    mult = jnp.where(q > 0.0, q * lax.rsqrt(q), 0.0)
    bx = (mult * xc) * i_gate
    hs = _linear_scan(slabs(a), slabs(bx), state_ref, a_scr, b_scr, h_scr)
    y_lru = 0.5 * _gelu2_times(lru_gate, jnp.concatenate(hs, axis=1))
    yl = _rms(y_lru) * glru_ref[...]

    nb = t // GMLP_BLOCK
    bsp = 0.5 * bsp_ref[...]
    sp_rows = []
    for n in range(nb):
        blk = jnp.concatenate(
            [sg[:, n * GMLP_GROUP_DIM:(n + 1) * GMLP_GROUP_DIM] for sg in sp_groups], axis=1)
        sp_rows.append(blk + bsp)
    sp_half = jnp.concatenate(sp_rows, axis=0)
    y_gmlp = _gelu2_times(g_u, sp_half)
    yg = _rms(y_gmlp) * ggmlp_ref[...]

    return jnp.concatenate([yl, yg], axis=1).astype(BF16)


def _mixer(x, mod, g_pre, g_post, w_in, conv_w, conv_b, w_gate, b_gate, lru_a,
           v_norm_g, v_norm_b, w_spatial, b_spatial, g_lru_out, g_gmlp_out, w_out, later_weights,
           layer):
    bsz, s, d = x.shape
    t = MIX_T
    nb = MIX_NB
    assert bsz % nb == 0 and s % t == 0
    n_lg = LRU_WIDTH // LANES
    scan_rows = SUBLANES * (t // SUBLANES + 1)
    full = lambda a: pl.BlockSpec(a.shape, lambda b, i: (0,) * a.ndim)
    of_layer = lambda a: pl.BlockSpec((None,) + a.shape[1:],
                                      lambda b, i: (layer,) + (0,) * (a.ndim - 1))
    in_arrays = [g_pre, g_post, w_in, conv_w, conv_b, w_gate, b_gate, lru_a,
                 v_norm_g, v_norm_b, w_spatial, b_spatial, g_lru_out, g_gmlp_out, w_out]
    resident = lambda a: pl.BlockSpec((None,) + a.shape[1:], lambda b, i: (layer, 0, 0),
                                      pipeline_mode=pl.Buffered(1))
    in_array_specs = [resident(a) if a is w_in or a is w_out
                      else of_layer(a) if a is conv_w or a is w_spatial or a is b_spatial
                      else full(a)
                      for a in in_arrays]
    packed = lambda shape: pltpu.VMEM(shape[:-2] + (shape[-2] // 2, shape[-1]), U32)
    nt = s // t
    n_steps = (bsz // nb) * nt
    step = lambda b, i: (b * nt + i, 0)
    later_in, later_out, later_shapes = [], [], []
    for w in later_weights:
        _, k, n = w.shape
        assert k % (n_steps * 2 * SUBLANES) == 0
        later_in.append(pl.BlockSpec((None, k // n_steps, n),
                                     lambda b, i: (layer, b * nt + i, 0)))
        later_out.append(pl.BlockSpec((k // n_steps // 2, n), step))
        later_shapes.append(jax.ShapeDtypeStruct((k // 2, n), U32))
    out = pl.pallas_call(
        functools.partial(_mixer_kernel, n_later=len(later_weights)),
        grid=(bsz // nb, nt),
        in_specs=[
            pl.BlockSpec((nb, t, d), lambda b, i: (b, i, 0)),
            pl.BlockSpec(mod.shape, lambda b, i: (0, 0)),
        ] + in_array_specs + later_in,
        out_specs=[pl.BlockSpec((nb, t, d), lambda b, i: (b, i, 0))] + later_out,
        out_shape=[jax.ShapeDtypeStruct(x.shape, x.dtype)] + later_shapes,
        scratch_shapes=(
            [pltpu.VMEM((n_lg, SUBLANES + t, LANES), F32)] * nb
            + [pltpu.VMEM((n_lg, scan_rows, LANES), F32)] * (3 * nb)
            + [pltpu.VMEM((nb, SUBLANES, LRU_WIDTH), F32)] * 2
            + [packed(w_in.shape[1:]), packed(w_gate.shape), packed(w_out.shape[1:])]
            + [pltpu.VMEM((GMLP_BLOCK, GMLP_WIDTH), F32)]),
        compiler_params=pltpu.CompilerParams(
            dimension_semantics=("arbitrary", "arbitrary"), vmem_limit_bytes=VMEM_LIMIT),
        name="token_mixer",
    )(x, mod, *in_arrays, *later_weights)
    return out[0], out[1:]


def _ffn_kernel(x_ref, mod_ref, gpre_ref, gpost_ref, wup_ref, cw_ref, cb_ref, wd_ref,
                o_ref, scr_ref, tail_ref):
    t_idx = pl.program_id(1)
    t, d = x_ref.shape
    d_ff = 2 * wd_ref.shape[0]
    fc = FFN_FC
    lg = fc // LANES

    @pl.when(t_idx == 0)
    def _():
        tail_ref[...] = jnp.zeros_like(tail_ref)

    mod = mod_ref[pl.ds(pl.program_id(0), 1), :]
    sh_f = mod[:, 3 * d:4 * d]
    sc_f = mod[:, 4 * d:5 * d]
    gt_f = mod[:, 5 * d:6 * d]
    x = x_ref[...]
    hb = (_rms(x) * (gpre_ref[...] * (1.0 + sc_f)) + sh_f).astype(BF16)

    n_chunks = d_ff // fc
    halves = ((0, 1.0), (d_ff, 0.5))
    chunk_cols = lambda base, j: slice(base + j * fc, base + (j + 1) * fc)
    up_proj = lambda j: [_dot(hb, _as_bf16(wup_ref[:, chunk_cols(base, j)])) for base, _ in halves]

    def gated(j, ups):
        conv = []
        for k, (base, scale) in enumerate(halves):
            cols = chunk_cols(base, j)
            tail = tail_ref[:, cols]
            tail_ref[:, cols] = ups[k][t - SUBLANES:]
            slot = (2 * j + k) % FFN_SLOTS
            conv.append(_conv_via_scratch(scr_ref, slot * lg, tail, ups[k],
                                          cw_ref[:, cols] * scale, cb_ref[:, cols] * scale))
        return jnp.concatenate(
            [_gelu2_times(cg, cv) for cg, cv in zip(*conv)], axis=1).astype(BF16)

    acc = None
    ups = up_proj(0)
    for j in range(n_chunks):
        ups_next = up_proj(j + 1) if j + 1 < n_chunks else None
        part = _dot(gated(j, ups), _as_bf16(wd_ref[j * fc // 2:(j + 1) * fc // 2, :]))
        acc = part if acc is None else acc + part
        ups = ups_next
    o_ref[...] = x + (gt_f * gpost_ref[...]) * _rms(acc)


def _ffn(x, mod, g_pre, g_post, w_up, conv_w, conv_b, w_down, layer):
    bsz, s, d = x.shape
    t = FFN_T
    resident = lambda a: pl.BlockSpec(a.shape, lambda b, i: (0,) * a.ndim,
                                      pipeline_mode=pl.Buffered(1))
    return pl.pallas_call(
        _ffn_kernel,
        grid=(bsz, s // t),
        in_specs=[
            pl.BlockSpec((None, t, d), lambda b, i: (b, i, 0)),
            pl.BlockSpec(mod.shape, lambda b, i: (0, 0)),
            resident(g_pre), resident(g_post), resident(w_up),
            pl.BlockSpec((None,) + conv_w.shape[1:], lambda b, i: (layer, 0, 0),
                         pipeline_mode=pl.Buffered(1)),
            resident(conv_b), resident(w_down),
        ],
        out_specs=pl.BlockSpec((None, t, d), lambda b, i: (b, i, 0)),
        out_shape=jax.ShapeDtypeStruct(x.shape, x.dtype),
        scratch_shapes=[
            pltpu.VMEM((FFN_SLOTS * (FFN_FC // LANES), SUBLANES + t, LANES), F32),
            pltpu.VMEM((SUBLANES, w_up.shape[1]), F32),
        ],
        compiler_params=pltpu.CompilerParams(
            dimension_semantics=("arbitrary", "arbitrary"), vmem_limit_bytes=VMEM_LIMIT),
        name="conv_ffn",
    )(x, mod, g_pre, g_post, w_up, conv_w, conv_b, w_down)


def _gate_weights(w_rgate, w_igate):
    per_tile = GATE_TILE // LRU_HEAD_DIM
    n_tiles = LRU_WIDTH // GATE_TILE

    def head_rows(w, head, slot):
        return jnp.pad(w[0, head], ((0, 0), (slot * LRU_HEAD_DIM, (per_tile - 1 - slot) * LRU_HEAD_DIM)))

    return jnp.stack([
        jnp.concatenate([
            jnp.concatenate([head_rows(w, t * per_tile + a, a) for w in (w_rgate, w_igate)], axis=1)
            for a in range(per_tile)], axis=0)
        for t in range(n_tiles)])


def kernel(x, c, w_ada, b_ada, g_mix_pre, g_mix_post, w_in, conv_w, conv_b, w_rgate, b_rgate, w_igate, b_igate, lru_a, v_norm_g, v_norm_b, w_spatial, b_spatial, g_lru_out, g_gmlp_out, w_out, g_ffn_pre, g_ffn_post, w_up, ffn_conv_w, ffn_conv_b, w_down):
    depth = w_ada.shape[0]
    bsz, s, d = x.shape
    for l in range(depth):
        mod = _modulation(c, w_ada, b_ada[l:l + 1], l)
        row = lambda a: a[l:l + 1]
        w_gate = _gate_weights(w_rgate[l:l + 1], w_igate[l:l + 1])
        b_gate = jnp.stack([b_rgate[l], b_igate[l]]).reshape(2, -1)
        x, (w_up_p, w_down_p) = _mixer(
                   x, mod, row(g_mix_pre), row(g_mix_post), w_in,
                   conv_w, row(conv_b), w_gate, b_gate, row(lru_a),
                   row(v_norm_g), row(v_norm_b), w_spatial, b_spatial,
                   row(g_lru_out), row(g_gmlp_out), w_out,
                   later_weights=[w_up, w_down], layer=l)
        x = _ffn(x, mod, row(g_ffn_pre), row(g_ffn_post), w_up_p,
                 ffn_conv_w, row(ffn_conv_b), w_down_p, layer=l)
    return x
```

```python
import functools

import jax
import jax.numpy as jnp
from jax import lax
from jax.experimental import pallas as pl
from jax.experimental.pallas import tpu as pltpu

CHUNK = 64
LRU_WIDTH = 512
LRU_HEADS = 8
LRU_HEAD_DIM = LRU_WIDTH // LRU_HEADS
LRU_CONV_WIDTH = 4
LRU_C = 8.0
GMLP_WIDTH = 512
GMLP_GROUPS = 4
GMLP_GROUP_DIM = GMLP_WIDTH // GMLP_GROUPS
GMLP_BLOCK = 128
FFN_CONV_WIDTH = 3
N_MOD = 6
EPS = 1e-6

SUBLANES = 8
LANES = 128
GATE_TILE = 256

MOD_ROWS = 128
MOD_STREAMS = 2
MIX_T = 256
MIX_NB = 4
FFN_T = 512
FFN_FC = 768
FFN_SLOTS = 4
VMEM_LIMIT = 60 * 1024 * 1024

F32 = jnp.float32
BF16 = jnp.bfloat16
U32 = jnp.uint32


def _dot(a, b):
    return jnp.dot(a, b, preferred_element_type=F32)


def _pack_rows(w):
    return pltpu.bitcast(w.astype(BF16), U32)


def _as_bf16(packed):
    return pltpu.bitcast(packed, BF16)


def _rms(x):
    return x * lax.rsqrt(jnp.mean(x * x, axis=-1, keepdims=True) + EPS)


def _gelu(x):
    c0 = 0.7978845608028654
    hx = 0.5 * x
    return hx + hx * jnp.tanh(x * (c0 + (c0 * 0.044715) * (x * x)))


def _gelu2_times(x, v):
    c0 = 0.7978845608028654
    z = x * (c0 + (c0 * 0.044715) * (x * x))
    return (x * v) * (1.0 + jnp.tanh(z))


def _rows(ref, lead, start, size):
    return ref[pl.ds(lead, 1, stride=2), pl.ds(start, size), :][0]


def _put_rows(ref, lead, start, val):
    ref[pl.ds(lead, 1, stride=2), pl.ds(start, val.shape[0]), :] = val[None]


def _linear_scan(coeffs, t, n, state_ref, a_scr, b_scr, h_scr):
    seg = t // SUBLANES
    pitch = seg + 1 - seg % 2
    for s in range(SUBLANES):
        for c in range(n):
            a_blk, b_blk = coeffs(c, slice(s * seg, (s + 1) * seg))
            _put_rows(a_scr, c, s * pitch, a_blk)
            _put_rows(b_scr, c, s * pitch, b_blk)
    step = lambda ref, c, j: ref[c, pl.ds(j, SUBLANES, stride=pitch), :]
    row = lax.broadcasted_iota(jnp.int32, (SUBLANES, LANES), 0)
    shifted = lambda v, sh, fill: jnp.where(row >= sh, pltpu.roll(v, sh, axis=0), fill)

    prod = [None] * n
    end = [None] * n
    for j in range(seg):
        for c in range(n):
            aj, bj = step(a_scr, c, j), step(b_scr, c, j)
            prod[c] = aj if j == 0 else aj * prod[c]
            end[c] = bj if j == 0 else aj * end[c] + bj

    h = []
    for c in range(n):
        cols = slice(c * LANES, (c + 1) * LANES)
        h0 = state_ref[SUBLANES - 1:SUBLANES, cols]
        p = prod[c]
        e = end[c] + jnp.where(row == 0, p * h0, 0.0)
        for sh in (1, 2, 4):
            e = e + p * shifted(e, sh, 0.0)
            if sh < 4:
                p = p * shifted(p, sh, 1.0)
        state_ref[:, cols] = e
        h.append(shifted(e, 1, h0))
    for j in range(seg):
        for c in range(n):
            h[c] = step(a_scr, c, j) * h[c] + step(b_scr, c, j)
            h_scr[c, pl.ds(j, SUBLANES, stride=pitch), :] = h[c]
    return [jnp.concatenate([_rows(h_scr, c, s * pitch, seg) for s in range(SUBLANES)], axis=0)
            for c in range(n)]


def _conv_via_scratch(scr_ref, lead0, tail, x, w, b):
    k_width = w.shape[0]
    t, c = x.shape
    outs = []
    for g in range(c // LANES):
        cols = slice(g * LANES, (g + 1) * LANES)
        scr_ref[lead0 + g, 0:SUBLANES, :] = tail[:, cols]
        scr_ref[lead0 + g, SUBLANES:SUBLANES + t, :] = x[:, cols]
    for g in range(c // LANES):
        cols = slice(g * LANES, (g + 1) * LANES)
        out = x[:, cols] * w[k_width - 1:k_width, cols] + b[:, cols]
        for k in range(k_width - 1):
            shifted = _rows(scr_ref, lead0 + g, SUBLANES - (k_width - 1 - k), t)
            out = out + shifted * w[k:k + 1, cols]
        outs.append(out)
    return outs


def _mod_kernel(c_ref, *rest):
    w_refs, b_ref, o_ref = rest[:-2], rest[-2], rest[-1]
    c = c_ref[...]
    rows = c.shape[0]
    c_act = c * jax.nn.sigmoid(c)
    pad = (-rows) % SUBLANES
    if pad:
        c_act = jnp.concatenate([c_act, jnp.zeros((pad, c.shape[1]), F32)], axis=0)
    c_act = c_act.astype(BF16)
    bk = w_refs[0].shape[0]
    parts = [_dot(c_act[:, j * bk:(j + 1) * bk], w_ref[...].astype(BF16))[:rows]
             for j, w_ref in enumerate(w_refs)]

    @pl.when(pl.program_id(0) == 0)
    def _():
        o_ref[...] = functools.reduce(jnp.add, parts, b_ref[...])

    @pl.when(pl.program_id(0) > 0)
    def _():
        o_ref[...] = functools.reduce(jnp.add, parts, o_ref[...])


def _modulation(c, w_ada, b_ada, layer):
    rows, d = c.shape
    n = w_ada.shape[2]
    bk = MOD_ROWS
    ns = MOD_STREAMS
    assert d % (bk * ns) == 0
    row_block = lambda j: pl.BlockSpec((None, bk, n), lambda k: (layer, k * ns + j, 0))
    return pl.pallas_call(
        _mod_kernel,
        grid=(d // (bk * ns),),
        in_specs=[pl.BlockSpec((rows, bk * ns), lambda k: (0, k))]
                 + [row_block(j) for j in range(ns)]
                 + [pl.BlockSpec((1, n), lambda k: (0, 0))],
        out_specs=pl.BlockSpec((rows, n), lambda k: (0, 0)),
        out_shape=jax.ShapeDtypeStruct((rows, n), F32),
        compiler_params=pltpu.CompilerParams(
            dimension_semantics=("arbitrary",), vmem_limit_bytes=VMEM_LIMIT),
        name="adaln_mod",
    )(c, *[w_ada] * ns, b_ada)


def _mixer_kernel(x_ref, mod_ref, gpre_ref, gpost_ref, win_ref, cw_ref, cb_ref,
                  wgate_ref, bgate_ref, lrua_ref, vng_ref, vnb_ref, wsp_ref, bsp_ref,
                  glru_ref, ggmlp_ref, wout_ref, *rest, n_later):
    later_in, o_ref, later_out = rest[:n_later], rest[n_later], rest[n_later + 1:2 * n_later + 1]
    scratch = rest[2 * n_later + 1:]
    for w_ref, p_ref in zip(later_in, later_out):
        p_ref[...] = _pack_rows(w_ref[...])
    nb_tiles = x_ref.shape[0]
    (tail_ref, state_ref, win_packed_ref, wgate_packed_ref, wout_packed_ref,
     bsp_full_ref) = scratch[4 * nb_tiles:]

    @pl.when((pl.program_id(0) == 0) & (pl.program_id(1) == 0))
    def _():
        win_packed_ref[...] = _pack_rows(win_ref[...])
        for j in range(wgate_ref.shape[0]):
            wgate_packed_ref[j] = _pack_rows(wgate_ref[j])
        wout_packed_ref[...] = _pack_rows(wout_ref[...])
        diag = (lax.broadcasted_iota(jnp.int32, (GMLP_BLOCK, GMLP_BLOCK), 0)
                == lax.broadcasted_iota(jnp.int32, (GMLP_BLOCK, GMLP_BLOCK), 1))
        for g in range(GMLP_GROUPS):
            col = jnp.sum(jnp.where(diag, bsp_ref[g:g + 1, :], -0.0), axis=1, keepdims=True)
            bsp_full_ref[:, g * GMLP_GROUP_DIM:(g + 1) * GMLP_GROUP_DIM] = jnp.broadcast_to(
                col, (GMLP_BLOCK, GMLP_GROUP_DIM))

    @pl.when(pl.program_id(1) == 0)
    def _():
        tail_ref[...] = jnp.zeros_like(tail_ref)
        state_ref[...] = jnp.zeros_like(state_ref)

    mod_row = lambda k: mod_ref[pl.ds(pl.program_id(0) * nb_tiles + k, 1), :]
    project = lambda k: _mix_in(x_ref.at[k], mod_row(k), gpre_ref, win_packed_ref)
    z = project(0)
    for k in range(nb_tiles):
        conv_scr, a_scr, b_scr, h_scr = scratch[k:4 * nb_tiles:nb_tiles]
        xc, gates = _mix_gates(z[0], cw_ref, cb_ref, wgate_packed_ref, conv_scr, tail_ref.at[k])
        sp_groups = _mix_positions(z[3], vng_ref, vnb_ref, wsp_ref)
        z_next = project(k + 1) if k + 1 < nb_tiles else None
        y = _mix_body(z, xc, gates, sp_groups, bgate_ref, lrua_ref, bsp_full_ref,
                      glru_ref, ggmlp_ref, a_scr, b_scr, h_scr, state_ref.at[k])
        _mix_out(y, x_ref.at[k], mod_row(k), gpost_ref, wout_packed_ref, o_ref.at[k])
        z = z_next


def _mix_in(x_ref, mod, gpre_ref, win_ref):
    d = x_ref.shape[1]
    sh_m = mod[:, 0:d]
    sc_m = mod[:, d:2 * d]
    hb = (_rms(x_ref[...]) * (gpre_ref[...] * (1.0 + sc_m)) + sh_m).astype(BF16)
    widths = (LRU_WIDTH, LRU_WIDTH, GMLP_WIDTH, GMLP_WIDTH)
    starts = [sum(widths[:q]) for q in range(len(widths))]
    return [_dot(hb, _as_bf16(win_ref[:, c0:c0 + w])) for c0, w in zip(starts, widths)]


def _mix_gates(lru_x, cw_ref, cb_ref, wgate_ref, conv_scr, tail_ref):
    t = lru_x.shape[0]
    tail = tail_ref[...]
    tail_ref[...] = lru_x[t - SUBLANES:]
    xc = jnp.concatenate(
        _conv_via_scratch(conv_scr, 0, tail, lru_x, cw_ref[...], cb_ref[...]), axis=1)
    xcb = xc.astype(BF16)
    gates = [_dot(xcb[:, j * GATE_TILE:(j + 1) * GATE_TILE], _as_bf16(wgate_ref[j]))
             for j in range(LRU_WIDTH // GATE_TILE)]
    return xc, gates


def _mix_out(y, x_ref, mod, gpost_ref, wout_ref, o_ref):
    d = x_ref.shape[1]
    gt_m = mod[:, 2 * d:3 * d]
    y = _dot(y, _as_bf16(wout_ref[...]))
    o_ref[...] = x_ref[...] + (gt_m * gpost_ref[...]) * _rms(y)


def _mix_positions(g_v, vng_ref, vnb_ref, wsp_ref):
    t = g_v.shape[0]
    gv = _gelu(g_v)
    mu = jnp.mean(gv, axis=-1, keepdims=True)
    cen = gv - mu
    var = jnp.mean(cen * cen, axis=-1, keepdims=True)
    v = cen * lax.rsqrt(var + EPS) * vng_ref[...] + vnb_ref[...]
    vb = v.astype(BF16)
    nb = t // GMLP_BLOCK
    pi = lax.broadcasted_iota(jnp.int32, (GMLP_BLOCK, GMLP_BLOCK), 0) // CHUNK
    pj = lax.broadcasted_iota(jnp.int32, (GMLP_BLOCK, GMLP_BLOCK), 1) // CHUNK
    mask = pj <= pi
    sp_groups = []
    for g in range(GMLP_GROUPS):
        ws = jnp.where(mask, 0.5 * wsp_ref[g], 0.0).astype(BF16)
        cols = slice(g * GMLP_GROUP_DIM, (g + 1) * GMLP_GROUP_DIM)
        rhs = jnp.concatenate(
            [vb[n * GMLP_BLOCK:(n + 1) * GMLP_BLOCK, cols] for n in range(nb)], axis=1)
        sp_groups.append(_dot(ws, rhs))
    return sp_groups


def _mix_body(z, xc, gates, sp_groups, bgate_ref, lrua_ref, bsp_ref,
              glru_ref, ggmlp_ref, a_scr, b_scr, h_scr, state_ref):
    _, lru_gate, g_u, _ = z
    t = lru_gate.shape[0]
    slabs = lambda v: [v[:, c * LANES:(c + 1) * LANES] for c in range(v.shape[1] // LANES)]

    neg_a = -lrua_ref[...]
    softplus = jnp.maximum(neg_a, 0.0) + jnp.log1p(jnp.exp(-jnp.abs(neg_a)))
    decay = (-LRU_C) * softplus
    b_r, b_i = bgate_ref[0:1, :], bgate_ref[1:2, :]

    def coeffs(c, rows):
        cols = slice(c * LANES, (c + 1) * LANES)
        tile, off = divmod(c * LANES, GATE_TILE)
        r_pre = gates[tile][rows, off:off + LANES]
        i_pre = gates[tile][rows, GATE_TILE + off:GATE_TILE + off + LANES]
        r_gate = jax.nn.sigmoid(r_pre + b_r[:, cols])
        i_gate = jax.nn.sigmoid(i_pre + b_i[:, cols])
        log_a = r_gate * decay[:, cols]
        a = jnp.exp(log_a)
        th = jnp.tanh(log_a)
        q = (-2.0 * th) / (1.0 - th)
        mult = jnp.where(q > 0.0, q * lax.rsqrt(q), 0.0)
        return a, (mult * xc[rows, cols]) * i_gate

    hs = _linear_scan(coeffs, t, LRU_WIDTH // LANES, state_ref, a_scr, b_scr, h_scr)
    y_lru = 0.5 * _gelu2_times(lru_gate, jnp.concatenate(hs, axis=1))
    yl = _rms(y_lru) * glru_ref[...]

    nb = t // GMLP_BLOCK
    bsp = 0.5 * bsp_ref[...]
    sp_rows = []
    for n in range(nb):
        blk = jnp.concatenate(
            [sg[:, n * GMLP_GROUP_DIM:(n + 1) * GMLP_GROUP_DIM] for sg in sp_groups], axis=1)
        sp_rows.append(blk + bsp)
    sp_half = jnp.concatenate(sp_rows, axis=0)
    y_gmlp = _gelu2_times(g_u, sp_half)
    yg = _rms(y_gmlp) * ggmlp_ref[...]

    return jnp.concatenate([yl, yg], axis=1).astype(BF16)


def _mixer(x, mod, g_pre, g_post, w_in, conv_w, conv_b, w_gate, b_gate, lru_a,
           v_norm_g, v_norm_b, w_spatial, b_spatial, g_lru_out, g_gmlp_out, w_out, later_weights,
           layer):
    bsz, s, d = x.shape
    t = MIX_T
    nb = MIX_NB
    assert bsz % nb == 0 and s % t == 0
    n_lg = LRU_WIDTH // LANES
    scan_rows = SUBLANES * (t // SUBLANES + 1)
    full = lambda a: pl.BlockSpec(a.shape, lambda b, i: (0,) * a.ndim)
    of_layer = lambda a: pl.BlockSpec((None,) + a.shape[1:],
                                      lambda b, i: (layer,) + (0,) * (a.ndim - 1))
    in_arrays = [g_pre, g_post, w_in, conv_w, conv_b, w_gate, b_gate, lru_a,
                 v_norm_g, v_norm_b, w_spatial, b_spatial, g_lru_out, g_gmlp_out, w_out]
    resident = lambda a: pl.BlockSpec((None,) + a.shape[1:], lambda b, i: (layer, 0, 0),
                                      pipeline_mode=pl.Buffered(1))
    in_array_specs = [resident(a) if a is w_in or a is w_out
                      else of_layer(a) if a is conv_w or a is w_spatial or a is b_spatial
                      else full(a)
                      for a in in_arrays]
    packed = lambda shape: pltpu.VMEM(shape[:-2] + (shape[-2] // 2, shape[-1]), U32)
    nt = s // t
    n_steps = (bsz // nb) * nt
    step = lambda b, i: (b * nt + i, 0)
    later_in, later_out, later_shapes = [], [], []
    for w in later_weights:
        _, k, n = w.shape
        assert k % (n_steps * 2 * SUBLANES) == 0
        later_in.append(pl.BlockSpec((None, k // n_steps, n),
                                     lambda b, i: (layer, b * nt + i, 0)))
        later_out.append(pl.BlockSpec((k // n_steps // 2, n), step))
        later_shapes.append(jax.ShapeDtypeStruct((k // 2, n), U32))
    out = pl.pallas_call(
        functools.partial(_mixer_kernel, n_later=len(later_weights)),
        grid=(bsz // nb, nt),
        in_specs=[
            pl.BlockSpec((nb, t, d), lambda b, i: (b, i, 0)),
            pl.BlockSpec(mod.shape, lambda b, i: (0, 0)),
        ] + in_array_specs + later_in,
        out_specs=[pl.BlockSpec((nb, t, d), lambda b, i: (b, i, 0))] + later_out,
        out_shape=[jax.ShapeDtypeStruct(x.shape, x.dtype)] + later_shapes,
        scratch_shapes=(
            [pltpu.VMEM((n_lg, SUBLANES + t, LANES), F32)] * nb
            + [pltpu.VMEM((n_lg, scan_rows, LANES), F32)] * (3 * nb)
            + [pltpu.VMEM((nb, SUBLANES, LRU_WIDTH), F32)] * 2
            + [packed(w_in.shape[1:]), packed(w_gate.shape), packed(w_out.shape[1:])]
            + [pltpu.VMEM((GMLP_BLOCK, GMLP_WIDTH), F32)]),
        compiler_params=pltpu.CompilerParams(
            dimension_semantics=("arbitrary", "arbitrary"), vmem_limit_bytes=VMEM_LIMIT),
        name="token_mixer",
    )(x, mod, *in_arrays, *later_weights)
    return out[0], out[1:]


def _ffn_kernel(x_ref, mod_ref, gpre_ref, gpost_ref, wup_ref, cw_ref, cb_ref, wd_ref,
                o_ref, scr_ref, tail_ref):
    t_idx = pl.program_id(1)
    t, d = x_ref.shape
    d_ff = 2 * wd_ref.shape[0]
    fc = FFN_FC
    lg = fc // LANES

    @pl.when(t_idx == 0)
    def _():
        tail_ref[...] = jnp.zeros_like(tail_ref)

    mod = mod_ref[pl.ds(pl.program_id(0), 1), :]
    sh_f = mod[:, 3 * d:4 * d]
    sc_f = mod[:, 4 * d:5 * d]
    gt_f = mod[:, 5 * d:6 * d]
    x = x_ref[...]
    hb = (_rms(x) * (gpre_ref[...] * (1.0 + sc_f)) + sh_f).astype(BF16)

    n_chunks = d_ff // fc
    halves = ((0, 1.0), (d_ff, 0.5))
    chunk_cols = lambda base, j: slice(base + j * fc, base + (j + 1) * fc)
    up_proj = lambda j: [_dot(hb, _as_bf16(wup_ref[:, chunk_cols(base, j)])) for base, _ in halves]

    def gated(j, ups):
        conv = []
        for k, (base, scale) in enumerate(halves):
            cols = chunk_cols(base, j)
            tail = tail_ref[:, cols]
            tail_ref[:, cols] = ups[k][t - SUBLANES:]
            slot = (2 * j + k) % FFN_SLOTS
            conv.append(_conv_via_scratch(scr_ref, slot * lg, tail, ups[k],
                                          cw_ref[:, cols] * scale, cb_ref[:, cols] * scale))
        return jnp.concatenate(
            [_gelu2_times(cg, cv) for cg, cv in zip(*conv)], axis=1).astype(BF16)

    acc = None
    ups = up_proj(0)
    for j in range(n_chunks):
        ups_next = up_proj(j + 1) if j + 1 < n_chunks else None
        part = _dot(gated(j, ups), _as_bf16(wd_ref[j * fc // 2:(j + 1) * fc // 2, :]))
        acc = part if acc is None else acc + part
        ups = ups_next
    o_ref[...] = x + (gt_f * gpost_ref[...]) * _rms(acc)


def _ffn(x, mod, g_pre, g_post, w_up, conv_w, conv_b, w_down, layer):
    bsz, s, d = x.shape
    t = FFN_T
    resident = lambda a: pl.BlockSpec(a.shape, lambda b, i: (0,) * a.ndim,
                                      pipeline_mode=pl.Buffered(1))
    return pl.pallas_call(
        _ffn_kernel,
        grid=(bsz, s // t),
        in_specs=[
            pl.BlockSpec((None, t, d), lambda b, i: (b, i, 0)),
            pl.BlockSpec(mod.shape, lambda b, i: (0, 0)),
            resident(g_pre), resident(g_post), resident(w_up),
            pl.BlockSpec((None,) + conv_w.shape[1:], lambda b, i: (layer, 0, 0),
                         pipeline_mode=pl.Buffered(1)),
            resident(conv_b), resident(w_down),
        ],
        out_specs=pl.BlockSpec((None, t, d), lambda b, i: (b, i, 0)),
        out_shape=jax.ShapeDtypeStruct(x.shape, x.dtype),
        scratch_shapes=[
            pltpu.VMEM((FFN_SLOTS * (FFN_FC // LANES), SUBLANES + t, LANES), F32),
            pltpu.VMEM((SUBLANES, w_up.shape[1]), F32),
        ],
        compiler_params=pltpu.CompilerParams(
            dimension_semantics=("arbitrary", "arbitrary"), vmem_limit_bytes=VMEM_LIMIT),
        name="conv_ffn",
    )(x, mod, g_pre, g_post, w_up, conv_w, conv_b, w_down)


def _gate_weights(w_rgate, w_igate):
    per_tile = GATE_TILE // LRU_HEAD_DIM
    n_tiles = LRU_WIDTH // GATE_TILE

    def head_rows(w, head, slot):
        return jnp.pad(w[0, head], ((0, 0), (slot * LRU_HEAD_DIM, (per_tile - 1 - slot) * LRU_HEAD_DIM)))

    return jnp.stack([
        jnp.concatenate([
            jnp.concatenate([head_rows(w, t * per_tile + a, a) for w in (w_rgate, w_igate)], axis=1)
            for a in range(per_tile)], axis=0)
        for t in range(n_tiles)])


def kernel(x, c, w_ada, b_ada, g_mix_pre, g_mix_post, w_in, conv_w, conv_b, w_rgate, b_rgate, w_igate, b_igate, lru_a, v_norm_g, v_norm_b, w_spatial, b_spatial, g_lru_out, g_gmlp_out, w_out, g_ffn_pre, g_ffn_post, w_up, ffn_conv_w, ffn_conv_b, w_down):
    depth = w_ada.shape[0]
    bsz, s, d = x.shape
    for l in range(depth):
        mod = _modulation(c, w_ada, b_ada[l:l + 1], l)
        row = lambda a: a[l:l + 1]
        w_gate = _gate_weights(w_rgate[l:l + 1], w_igate[l:l + 1])
        b_gate = jnp.stack([b_rgate[l], b_igate[l]]).reshape(2, -1)
        x, (w_up_p, w_down_p) = _mixer(
                   x, mod, row(g_mix_pre), row(g_mix_post), w_in,
                   conv_w, row(conv_b), w_gate, b_gate, row(lru_a),
                   row(v_norm_g), row(v_norm_b), w_spatial, b_spatial,
                   row(g_lru_out), row(g_gmlp_out), w_out,
                   later_weights=[w_up, w_down], layer=l)
        x = _ffn(x, mod, row(g_ffn_pre), row(g_ffn_post), w_up_p,
                 ffn_conv_w, row(ffn_conv_b), w_down_p, layer=l)
    return x
```
